```python
import jax
import jax.numpy as jnp
from jax import lax
import numpy as np

D_MODEL = 1024
BATCH = 16
SEQ = 256
DEPTH = 4
DEC_BATCH = 8
DEC_SEQ = 1024
PAST_LEN = 512

GRID_W = 64
Q_BLOCK = 128
ROPE_THETA = 10000.0
NORM_EPS = 1e-6
NEG_INF = -1e30
N_MIXERS = 4
N_MOD = 6
D_FF = 4 * D_MODEL

ATTN_HEADS = 8
ATTN_KV_HEADS = 2
ATTN_HEAD_DIM = D_MODEL // ATTN_HEADS
MLA_HEADS = 16
MLA_Q_LORA = 384
MLA_KV_LORA = 256
MLA_NOPE = 64
MLA_ROPE = 32
MLA_V_DIM = 64
MLA_SCALE = (MLA_NOPE + MLA_ROPE) ** -0.5
SWA_HEADS = 16
SWA_KV_HEADS = 4
SWA_HEAD_DIM = D_MODEL // SWA_HEADS
SWA_WINDOW = 128
NAT_HEADS = 16
NAT_HEAD_DIM = D_MODEL // NAT_HEADS
NAT_WIN_R = 8
NAT_WIN_C = 16

kernel_name = 'hybrid_dit_prefix_context_step'


def rmsnorm(x, g):
    xf = x.astype(jnp.float32)
    y = xf * lax.rsqrt(jnp.mean(xf * xf, axis=-1, keepdims=True) + NORM_EPS)
    return (y * g.astype(jnp.float32)).astype(x.dtype)


def axial_rope(x):
    S, dim = x.shape[1], x.shape[-1]
    quarter = dim // 4
    t = jnp.arange(S)
    pos = jnp.stack([t // GRID_W, t % GRID_W], axis=-1).astype(jnp.float32)
    inv = ROPE_THETA ** (-jnp.arange(quarter, dtype=jnp.float32) / quarter)
    ang = pos[:, :, None] * inv
    bshape = (S,) + (1,) * (x.ndim - 3) + (2, quarter)
    cos = jnp.cos(ang).reshape(bshape).astype(x.dtype)
    sin = jnp.sin(ang).reshape(bshape).astype(x.dtype)
    xr = x.reshape(x.shape[:-1] + (2, 2, quarter))
    x1, x2 = xr[..., 0, :], xr[..., 1, :]
    out = jnp.stack([x1 * cos - x2 * sin, x2 * cos + x1 * sin], axis=-2)
    return out.reshape(x.shape)


def attend(q, segments, scale, sink=None):
    logits = []
    for k, _, bias in segments:
        eq = 'bqhgd,bqkhd->bhgqk' if k.ndim == 5 else 'bqhgd,bkhd->bhgqk'
        s = jnp.einsum(eq, q, k, preferred_element_type=jnp.float32) * scale
        if bias is not None:
            s = s + bias
        logits.append(s)
    sizes = [s.shape[-1] for s in logits]
    if sink is not None:
        B, Q = q.shape[:2]
        logits.append(jnp.broadcast_to(sink.astype(jnp.float32)[None, :, :, None, None],
                                       (B,) + sink.shape + (Q, 1)))
    p = jax.nn.softmax(jnp.concatenate(logits, axis=-1), axis=-1)
    out = None
    off = 0
    for (_, v, _), n in zip(segments, sizes):
        pv = p[..., off:off + n].astype(v.dtype)
        eq = 'bhgqk,bqkhd->bqhgd' if v.ndim == 5 else 'bhgqk,bkhd->bqhgd'
        o = jnp.einsum(eq, pv, v)
        out = o if out is None else out + o
        off += n
    return out


def map_query_blocks(fn, q, *block_inputs):
    B, S = q.shape[:2]
    nb = S // Q_BLOCK
    qb = jnp.moveaxis(q.reshape((B, nb, Q_BLOCK) + q.shape[2:]), 1, 0)
    out = lax.map(lambda xs: fn(*xs), (jnp.arange(nb), qb) + tuple(block_inputs))
    return jnp.moveaxis(out, 0, 1).reshape((B, S) + out.shape[3:])


def gqa_split(z, n_heads, n_kv, hd):
    B, T = z.shape[:2]
    q, k, v = jnp.split(z, [n_heads * hd, (n_heads + n_kv) * hd], axis=-1)
    return (q.reshape(B, T, n_kv, n_heads // n_kv, hd),
            k.reshape(B, T, n_kv, hd), v.reshape(B, T, n_kv, hd))


def attn_context(h, w_qkv, q_norm, k_norm, w_o):
    B, L = h.shape[:2]
    q, k, v = gqa_split(h @ w_qkv, ATTN_HEADS, ATTN_KV_HEADS, ATTN_HEAD_DIM)
    q, k = rmsnorm(q, q_norm), rmsnorm(k, k_norm)
    o = attend(q, [(k, v, None)], ATTN_HEAD_DIM ** -0.5)
    return o.reshape(B, L, -1) @ w_o, (k, v)


def attn_latent(h, k_ctx, v_ctx, w_qkv, q_norm, k_norm, w_o):
    B, S = h.shape[:2]
    q, k, v = gqa_split(h @ w_qkv, ATTN_HEADS, ATTN_KV_HEADS, ATTN_HEAD_DIM)
    q, k = axial_rope(rmsnorm(q, q_norm)), axial_rope(rmsnorm(k, k_norm))

    def block(b, qb):
        return attend(qb, [(k, v, None), (k_ctx, v_ctx, None)], ATTN_HEAD_DIM ** -0.5)

    o = map_query_blocks(block, q)
    return o.reshape(B, S, -1) @ w_o


def mla_project(h, w_in, q_norm, kv_norm, w_uq):
    B, T = h.shape[:2]
    cq, ckv, k_pe = jnp.split(h @ w_in, [MLA_Q_LORA, MLA_Q_LORA + MLA_KV_LORA], axis=-1)
    q = (rmsnorm(cq, q_norm) @ w_uq).reshape(B, T, MLA_HEADS, 1, MLA_NOPE + MLA_ROPE)
    return q, rmsnorm(ckv, kv_norm), k_pe


def mla_expand(ckv, k_pe, w_ukv):
    B, T = ckv.shape[:2]
    kv = (ckv @ w_ukv).reshape(B, T, MLA_HEADS, MLA_NOPE + MLA_V_DIM)
    k_nope, v = jnp.split(kv, [MLA_NOPE], axis=-1)
    k_rope = jnp.broadcast_to(k_pe[:, :, None, :], (B, T, MLA_HEADS, MLA_ROPE))
    return jnp.concatenate([k_nope, k_rope], axis=-1), v


def mla_context(h, w_in, q_norm, kv_norm, w_uq, w_ukv, w_o):
    B, L = h.shape[:2]
    q, ckv, k_pe = mla_project(h, w_in, q_norm, kv_norm, w_uq)
    k, v = mla_expand(ckv, k_pe, w_ukv)
    o = attend(q, [(k, v, None)], MLA_SCALE)
    return o.reshape(B, L, -1) @ w_o, (ckv, k_pe)


def mla_latent(h, ckv_ctx, kpe_ctx, w_in, q_norm, kv_norm, w_uq, w_ukv, w_o):
    B, S = h.shape[:2]
    q, ckv, k_pe = mla_project(h, w_in, q_norm, kv_norm, w_uq)
    q = jnp.concatenate([q[..., :MLA_NOPE], axial_rope(q[..., MLA_NOPE:])], axis=-1)
    k, v = mla_expand(ckv, axial_rope(k_pe[:, :, None, :])[:, :, 0, :], w_ukv)
    k_ctx, v_ctx = mla_expand(ckv_ctx, kpe_ctx, w_ukv)

    def block(b, qb):
        return attend(qb, [(k, v, None), (k_ctx, v_ctx, None)], MLA_SCALE)

    o = map_query_blocks(block, q)
    return o.reshape(B, S, -1) @ w_o


def swa_context(h, w_qkv, sink, w_o):
    B, L = h.shape[:2]
    q, k, v = gqa_split(h @ w_qkv, SWA_HEADS, SWA_KV_HEADS, SWA_HEAD_DIM)
    o = attend(q, [(k, v, None)], SWA_HEAD_DIM ** -0.5, sink.reshape(SWA_KV_HEADS, -1))
    return o.reshape(B, L, -1) @ w_o, (k, v)


def swa_latent(h, k_ctx, v_ctx, w_qkv, sink, w_o):
    B, S = h.shape[:2]
    q, k, v = gqa_split(h @ w_qkv, SWA_HEADS, SWA_KV_HEADS, SWA_HEAD_DIM)
    q, k = axial_rope(q), axial_rope(k)
    pad = ((0, 0), (SWA_WINDOW, SWA_WINDOW), (0, 0), (0, 0))
    kp, vp = jnp.pad(k, pad), jnp.pad(v, pad)
    band = Q_BLOCK + 2 * SWA_WINDOW
    sink_g = sink.reshape(SWA_KV_HEADS, -1)

    def block(b, qb):
        start = b * Q_BLOCK
        kb = lax.dynamic_slice_in_dim(kp, start, band, axis=1)
        vb = lax.dynamic_slice_in_dim(vp, start, band, axis=1)
        qpos = start + jnp.arange(Q_BLOCK)
        kpos = start - SWA_WINDOW + jnp.arange(band)
        valid = (kpos[None, :] >= 0) & (kpos[None, :] < S) & (jnp.abs(qpos[:, None] - kpos[None, :]) <= SWA_WINDOW)
        bias = jnp.where(valid, 0.0, NEG_INF).astype(jnp.float32)
        return attend(qb, [(kb, vb, bias), (k_ctx, v_ctx, None)], SWA_HEAD_DIM ** -0.5, sink_g)

    o = map_query_blocks(block, q)
    return o.reshape(B, S, -1) @ w_o


def neighbourhood_tables(S, rpb):
    rows = S // GRID_W
    wr = min(NAT_WIN_R, rows)
    t = jnp.arange(S)
    r, c = t // GRID_W, t % GRID_W
    r0 = jnp.clip(r - wr // 2, 0, rows - wr)
    c0 = jnp.clip(c - NAT_WIN_C // 2, 0, GRID_W - NAT_WIN_C)
    kr = jnp.broadcast_to(r0[:, None, None] + jnp.arange(wr)[None, :, None], (S, wr, NAT_WIN_C)).reshape(S, -1)
    kc = jnp.broadcast_to(c0[:, None, None] + jnp.arange(NAT_WIN_C)[None, None, :], (S, wr, NAT_WIN_C)).reshape(S, -1)
    idx = kr * GRID_W + kc
    bias = rpb[:, kr - r[:, None] + NAT_WIN_R - 1, kc - c[:, None] + NAT_WIN_C - 1]
    return idx, bias


def nat_context(h, w_qkv, w_o):
    B, L = h.shape[:2]
    q, k, v = gqa_split(h @ w_qkv, NAT_HEADS, NAT_HEADS, NAT_HEAD_DIM)
    o = attend(q, [(k, v, None)], NAT_HEAD_DIM ** -0.5)
    return o.reshape(B, L, -1) @ w_o, (k, v)


def nat_latent(h, k_ctx, v_ctx, w_qkv, rpb, w_o):
    B, S = h.shape[:2]
    q, k, v = gqa_split(h @ w_qkv, NAT_HEADS, NAT_HEADS, NAT_HEAD_DIM)
    idx, bias = neighbourhood_tables(S, rpb)
    nb = S // Q_BLOCK
    idx_blocks = idx.reshape(nb, Q_BLOCK, -1)
    bias_blocks = jnp.moveaxis(bias.reshape(NAT_HEADS, nb, Q_BLOCK, -1), 1, 0)

    def block(b, qb, idx_b, bias_b):
        kg = jnp.take(k, idx_b, axis=1)
        vg = jnp.take(v, idx_b, axis=1)
        return attend(qb, [(kg, vg, bias_b[None, :, None].astype(jnp.float32)), (k_ctx, v_ctx, None)],
                      NAT_HEAD_DIM ** -0.5)

    o = map_query_blocks(block, q, idx_blocks, bias_blocks)
    return o.reshape(B, S, -1) @ w_o


def sandwich_layer(x, cond, ada_w, ada_b, norm_g, w1, w2, mix):
    m = jax.nn.silu(cond) @ ada_w + ada_b
    shift1, scale1, gate1, shift2, scale2, gate2 = jnp.split(m[:, None, :], N_MOD, axis=-1)
    out, extra = mix(rmsnorm(x, norm_g[0]) * (1 + scale1) + shift1)
    x = x + gate1 * rmsnorm(out, norm_g[1])
    h = rmsnorm(x, norm_g[2]) * (1 + scale2) + shift2
    ff = jnp.square(jax.nn.relu(h @ w1)) @ w2
    x = x + gate2 * rmsnorm(ff, norm_g[3])
    return x, extra


def setup_inputs(seed: int = 0) -> dict:
    key = jax.random.key(seed)
    ks = iter(jax.random.split(key, 40))

    def nrm(shape, scale=1.0):
        return scale * jax.random.normal(next(ks), shape, jnp.float32)

    D = D_MODEL
    return {
        'x_prompt': nrm((BATCH, SEQ, D)),
        'x_sample': nrm((DEC_BATCH, DEC_SEQ, D)),
        'cache_l0_k': nrm((DEC_BATCH, PAST_LEN, ATTN_KV_HEADS, ATTN_HEAD_DIM)),
        'cache_l0_v': nrm((DEC_BATCH, PAST_LEN, ATTN_KV_HEADS, ATTN_HEAD_DIM)),
        'cache_l1_ckv': nrm((DEC_BATCH, PAST_LEN, MLA_KV_LORA)),
        'cache_l1_kpe': nrm((DEC_BATCH, PAST_LEN, MLA_ROPE)),
        'cache_l2_k': nrm((DEC_BATCH, PAST_LEN, SWA_KV_HEADS, SWA_HEAD_DIM)),
        'cache_l2_v': nrm((DEC_BATCH, PAST_LEN, SWA_KV_HEADS, SWA_HEAD_DIM)),
        'cache_l3_k': nrm((DEC_BATCH, PAST_LEN, NAT_HEADS, NAT_HEAD_DIM)),
        'cache_l3_v': nrm((DEC_BATCH, PAST_LEN, NAT_HEADS, NAT_HEAD_DIM)),
        'c': nrm((DEC_BATCH, D)),
        'c_ctx': nrm((D,)),
        'ada_w': nrm((DEPTH, D, N_MOD * D), 0.5 * D ** -0.5),
        'ada_b': nrm((DEPTH, N_MOD * D), 0.01),
        'norm_g': 1.0 + nrm((DEPTH, 4, D), 0.05),
        'mlp_w1': nrm((DEPTH, D, D_FF), D ** -0.5),
        'mlp_w2': nrm((DEPTH, D_FF, D), D_FF ** -0.5),
        'attn_w_qkv': nrm((D, (ATTN_HEADS + 2 * ATTN_KV_HEADS) * ATTN_HEAD_DIM), D ** -0.5),
        'attn_q_norm': 1.0 + nrm((ATTN_HEAD_DIM,), 0.05),
        'attn_k_norm': 1.0 + nrm((ATTN_HEAD_DIM,), 0.05),
        'attn_w_o': nrm((ATTN_HEADS * ATTN_HEAD_DIM, D), (ATTN_HEADS * ATTN_HEAD_DIM) ** -0.5),
        'mla_w_in': nrm((D, MLA_Q_LORA + MLA_KV_LORA + MLA_ROPE), D ** -0.5),
        'mla_q_norm': 1.0 + nrm((MLA_Q_LORA,), 0.05),
        'mla_kv_norm': 1.0 + nrm((MLA_KV_LORA,), 0.05),
        'mla_w_uq': nrm((MLA_Q_LORA, MLA_HEADS * (MLA_NOPE + MLA_ROPE)), MLA_Q_LORA ** -0.5),
        'mla_w_ukv': nrm((MLA_KV_LORA, MLA_HEADS * (MLA_NOPE + MLA_V_DIM)), MLA_KV_LORA ** -0.5),
        'mla_w_o': nrm((MLA_HEADS * MLA_V_DIM, D), (MLA_HEADS * MLA_V_DIM) ** -0.5),
        'swa_w_qkv': nrm((D, (SWA_HEADS + 2 * SWA_KV_HEADS) * SWA_HEAD_DIM), D ** -0.5),
        'swa_sink': nrm((SWA_HEADS,), 0.5),
        'swa_w_o': nrm((SWA_HEADS * SWA_HEAD_DIM, D), (SWA_HEADS * SWA_HEAD_DIM) ** -0.5),
        'nat_w_qkv': nrm((D, 3 * NAT_HEADS * NAT_HEAD_DIM), D ** -0.5),
        'nat_rpb': nrm((NAT_HEADS, 2 * NAT_WIN_R - 1, 2 * NAT_WIN_C - 1), 0.1),
        'nat_w_o': nrm((NAT_HEADS * NAT_HEAD_DIM, D), (NAT_HEADS * NAT_HEAD_DIM) ** -0.5),
    }


def reference(x_prompt, x_sample, cache_l0_k, cache_l0_v, cache_l1_ckv, cache_l1_kpe,
              cache_l2_k, cache_l2_v, cache_l3_k, cache_l3_v, c, c_ctx,
              ada_w, ada_b, norm_g, mlp_w1, mlp_w2,
              attn_w_qkv, attn_q_norm, attn_k_norm, attn_w_o,
              mla_w_in, mla_q_norm, mla_kv_norm, mla_w_uq, mla_w_ukv, mla_w_o,
              swa_w_qkv, swa_sink, swa_w_o,
              nat_w_qkv, nat_rpb, nat_w_o):
    context_mixers = [
        lambda h: attn_context(h, attn_w_qkv, attn_q_norm, attn_k_norm, attn_w_o),
        lambda h: mla_context(h, mla_w_in, mla_q_norm, mla_kv_norm, mla_w_uq, mla_w_ukv, mla_w_o),
        lambda h: swa_context(h, swa_w_qkv, swa_sink, swa_w_o),
        lambda h: nat_context(h, nat_w_qkv, nat_w_o),
    ]
    latent_mixers = [
        lambda h, st: attn_latent(h, st[0], st[1], attn_w_qkv, attn_q_norm, attn_k_norm, attn_w_o),
        lambda h, st: mla_latent(h, st[0], st[1], mla_w_in, mla_q_norm, mla_kv_norm, mla_w_uq, mla_w_ukv, mla_w_o),
        lambda h, st: swa_latent(h, st[0], st[1], swa_w_qkv, swa_sink, swa_w_o),
        lambda h, st: nat_latent(h, st[0], st[1], nat_w_qkv, nat_rpb, nat_w_o),
    ]
    cached = [(cache_l0_k, cache_l0_v), (cache_l1_ckv, cache_l1_kpe),
              (cache_l2_k, cache_l2_v), (cache_l3_k, cache_l3_v)]
    xp, xs = x_prompt, x_sample
    new_state = []
    for i in range(DEPTH):
        kind = i % N_MIXERS
        layer_w = (ada_w[i], ada_b[i], norm_g[i], mlp_w1[i], mlp_w2[i])
        xp, st = sandwich_layer(xp, c_ctx[None, :], *layer_w, context_mixers[kind])
        new_state.append(st)
        xs, _ = sandwich_layer(xs, c, *layer_w, lambda h: (latent_mixers[kind](h, cached[i]), None))
    (l0_k, l0_v), (l1_ckv, l1_kpe), (l2_k, l2_v), (l3_k, l3_v) = new_state
    return (xp, xs, l0_k, l0_v, l1_ckv, l1_kpe, l2_k, l2_v, l3_k, l3_v)
```

```python
import functools

import jax
import jax.numpy as jnp
from jax import lax
from jax.experimental import pallas as pl
from jax.experimental.pallas import tpu as pltpu

F32 = jnp.float32
BF16 = jnp.bfloat16

D_MODEL = 1024
DEPTH = 4
N_MOD = 6
D_FF = 4 * D_MODEL
GRID_W = 64
ROPE_THETA = 10000.0
NORM_EPS = 1e-6
NEG_INF = -1e30

ATTN_HEADS, ATTN_KV_HEADS, ATTN_HEAD_DIM = 8, 2, 128
MLA_HEADS, MLA_Q_LORA, MLA_KV_LORA = 16, 384, 256
MLA_NOPE, MLA_ROPE, MLA_V_DIM = 64, 32, 64
MLA_SCALE = (MLA_NOPE + MLA_ROPE) ** -0.5
SWA_HEADS, SWA_KV_HEADS, SWA_HEAD_DIM, SWA_WINDOW = 16, 4, 64, 128
NAT_HEADS, NAT_HEAD_DIM, NAT_WIN_R, NAT_WIN_C = 16, 64, 8, 16

LANES = 128
COND_ROWS = 16
CTX_MOD_ROW = 8
VMEM_LIMIT = 48 * 1024 * 1024

TOK_TILE = 512
MLP_TOK_TILE = 1024
MLP_FF_TILE = 512


def _params(*sem):
    return pltpu.CompilerParams(dimension_semantics=sem, vmem_limit_bytes=VMEM_LIMIT)


def _full(shape):
    nd = len(shape)
    return pl.BlockSpec(shape, lambda *_: (0,) * nd)


def _rms(x, g):
    return x * lax.rsqrt(jnp.mean(x * x, axis=-1, keepdims=True) + NORM_EPS) * g


def _mod(m, i):
    return m[:, i * D_MODEL:(i + 1) * D_MODEL]


def _premod(x, g, m, sub):
    return _rms(x, g) * (1.0 + _mod(m, 3 * sub + 1)) + _mod(m, 3 * sub)


def _rope(x, cos, sin_signed, quarter):
    n = x.shape[-1]
    lane = lax.broadcasted_iota(jnp.int32, x.shape, 1)
    first = ((lane // quarter) % 2) == 0
    partner = jnp.where(first, pltpu.roll(x, n - quarter, 1), pltpu.roll(x, quarter, 1))
    return x * cos + partner * sin_signed


def _dot(a, b):
    return jnp.dot(a, b, preferred_element_type=F32)


def _dot_nt(a, b):
    return lax.dot_general(a, b, (((1,), (1,)), ((), ())), preferred_element_type=F32)


def _attend_stack(q_stack, n, tq, segs, sinks=None):
    logits = [_dot_nt(q_stack, k) for k, _, _ in segs]
    outs = []
    for i in range(n):
        ss = []
        for s, (_, _, bias_fn) in zip(logits, segs):
            si = s[i * tq:(i + 1) * tq]
            if bias_fn is not None:
                si = si + bias_fn(i)
            ss.append(si)
        m = ss[0].max(axis=-1, keepdims=True)
        for si in ss[1:]:
            m = jnp.maximum(m, si.max(axis=-1, keepdims=True))
        if sinks is not None:
            m = jnp.maximum(m, sinks[i])
        den = None
        acc = None
        for si, (_, v, _) in zip(ss, segs):
            p = jnp.exp(si - m)
            ps = p.sum(axis=-1, keepdims=True)
            pv = _dot(p.astype(BF16), v)
            den = ps if den is None else den + ps
            acc = pv if acc is None else acc + pv
        if sinks is not None:
            den = den + jnp.exp(sinks[i] - m)
        outs.append(acc / den)
    return outs


def _low_half(shape):
    return lax.broadcasted_iota(jnp.int32, shape, 1) < (LANES // 2)


def _mods_kernel(cond_ref, w_ref, b_ref, o_ref):
    cnd = cond_ref[...]
    act = cnd * jax.nn.sigmoid(cnd)
    o_ref[...] = _dot(act.astype(BF16), w_ref[...].astype(BF16)) + b_ref[...]


def _modulation(cond, ada_w, ada_b):
    tn = 1536
    n = N_MOD * D_MODEL
    return pl.pallas_call(
        _mods_kernel,
        grid=(DEPTH, n // tn),
        in_specs=[
            _full((COND_ROWS, D_MODEL)),
            pl.BlockSpec((None, D_MODEL, tn), lambda l, j: (l, 0, j)),
            pl.BlockSpec((None, 1, tn), lambda l, j: (l, 0, j)),
        ],
        out_specs=pl.BlockSpec((None, COND_ROWS, tn), lambda l, j: (l, 0, j)),
        out_shape=jax.ShapeDtypeStruct((DEPTH, COND_ROWS, n), F32),
        compiler_params=_params("arbitrary", "arbitrary"),
        name="adaln_mods",
    )(cond, ada_w, ada_b.reshape(DEPTH, 1, n))


def _proj_a_kernel(latent, x_ref, m_ref, g_ref, w_ref, qn_ref, kn_ref, *rest):
    if latent:
        cos_ref, sin_ref, q_ref, k_ref, v_ref = rest
    else:
        q_ref, k_ref, v_ref, ks_ref, vs_ref = rest
    hd = ATTN_HEAD_DIM
    h = _premod(x_ref[...], g_ref[0:1, :], m_ref[...], 0)
    z = _dot(h.astype(BF16), w_ref[...])
    for i in range(ATTN_HEADS + ATTN_KV_HEADS):
        is_q = i < ATTN_HEADS
        y = _rms(z[:, i * hd:(i + 1) * hd], qn_ref[...] if is_q else kn_ref[...])
        if not latent and not is_q:
            j = i - ATTN_HEADS
            ks_ref[:, j * hd:(j + 1) * hd] = y
        if latent:
            y = _rope(y, cos_ref[...], sin_ref[...], hd // 4)
        if is_q:
            q_ref[:, i * hd:(i + 1) * hd] = (y * (hd ** -0.5)).astype(BF16)
        else:
            j = i - ATTN_HEADS
            k_ref[:, j * hd:(j + 1) * hd] = y.astype(BF16)
    v = z[:, (ATTN_HEADS + ATTN_KV_HEADS) * hd:]
    v_ref[...] = v.astype(BF16)
    if not latent:
        vs_ref[...] = v


def _proj_b_kernel(latent, x_ref, m_ref, g_ref, win_ref, qn_ref, kvn_ref, wuq_ref, wukv_ref, *rest):
    if latent:
        cos_ref, sin_ref, qn_o, qr_o, kn_o, v_o, kp_o = rest
    else:
        qn_o, qr_o, kn_o, v_o, kp_o, ckv_s, kpe_s = rest
    nq, nkv = MLA_Q_LORA, MLA_KV_LORA
    nope_w = MLA_HEADS * MLA_NOPE
    h = _premod(x_ref[...], g_ref[0:1, :], m_ref[...], 0)
    z = _dot(h.astype(BF16), win_ref[...])
    cq = _rms(z[:, :nq], qn_ref[...])
    ckv = _rms(z[:, nq:nq + nkv], kvn_ref[...])
    kp = z[:, nq + nkv:]
    q = _dot(cq.astype(BF16), wuq_ref[...])
    kv = _dot(ckv.astype(BF16), wukv_ref[...])
    if not latent:
        ckv_s[...] = ckv
        kpe_s[...] = kp[:, :MLA_ROPE]
    qn_o[...] = (q[:, :nope_w] * MLA_SCALE).astype(BF16)
    for i in range(MLA_HEADS * MLA_ROPE // LANES):
        qr = q[:, nope_w + i * LANES:nope_w + (i + 1) * LANES]
        if latent:
            qr = _rope(qr, cos_ref[...], sin_ref[...], MLA_ROPE // 4)
        qr_o[:, i * LANES:(i + 1) * LANES] = (qr * MLA_SCALE).astype(BF16)
    if latent:
        kp = _rope(kp, cos_ref[...], sin_ref[...], MLA_ROPE // 4)
    kp_o[...] = kp.astype(BF16)
    kn_o[...] = kv[:, :nope_w].astype(BF16)
    v_o[...] = kv[:, nope_w:].astype(BF16)


def _proj_c_kernel(latent, x_ref, m_ref, g_ref, w_ref, *rest):
    if latent:
        cos_ref, sin_ref, q_o, kd_o, vd_o = rest
    else:
        q_o, kd_o, vd_o, ks_o, vs_o = rest
    qw = SWA_HEADS * SWA_HEAD_DIM
    kw = SWA_KV_HEADS * SWA_HEAD_DIM
    h = _premod(x_ref[...], g_ref[0:1, :], m_ref[...], 0)
    z = _dot(h.astype(BF16), w_ref[...])
    off = qw
    if not latent:
        ks_o[...] = z[:, qw:qw + kw]
        vs_o[...] = z[:, qw + kw:qw + 2 * kw]
        off = qw + 2 * kw
    scale = SWA_HEAD_DIM ** -0.5
    for i in range(qw // LANES):
        y = z[:, i * LANES:(i + 1) * LANES]
        if latent:
            y = _rope(y, cos_ref[...], sin_ref[...], SWA_HEAD_DIM // 4)
        q_o[:, i * LANES:(i + 1) * LANES] = (y * scale).astype(BF16)
    for i in range(2 * kw // LANES):
        y = z[:, off + i * LANES:off + (i + 1) * LANES]
        if latent:
            y = _rope(y, cos_ref[...], sin_ref[...], SWA_HEAD_DIM // 4)
        kd_o[:, i * LANES:(i + 1) * LANES] = y.astype(BF16)
    vd_o[...] = z[:, off + 2 * kw:].astype(BF16)


def _proj_d_kernel(latent, x_ref, m_ref, g_ref, w_ref, *rest):
    if latent:
        q_o, k_o, v_o = rest
    else:
        q_o, k_o, v_o, ks_o, vs_o = rest
    hw = NAT_HEADS * NAT_HEAD_DIM
    h = _premod(x_ref[...], g_ref[0:1, :], m_ref[...], 0)
    z = _dot(h.astype(BF16), w_ref[...])
    q_o[...] = (z[:, :hw] * (NAT_HEAD_DIM ** -0.5)).astype(BF16)
    k_o[...] = z[:, hw:2 * hw].astype(BF16)
    v_o[...] = z[:, 2 * hw:].astype(BF16)
    if not latent:
        ks_o[...] = z[:, hw:2 * hw]
        vs_o[...] = z[:, 2 * hw:]


def _run_proj(kernel, latent, x, mods_l, g, consts, tables, outs, name):
    bx, s, _ = x.shape
    tm = TOK_TILE
    row = (lambda b, j: (b, 0, 0)) if latent else (lambda b, j: (CTX_MOD_ROW, 0, 0))
    in_specs = [
        pl.BlockSpec((None, tm, D_MODEL), lambda b, j: (b, j, 0)),
        pl.BlockSpec((None, 1, N_MOD * D_MODEL), row),
        _full(g.shape),
    ] + [_full(c.shape) for c in consts]
    in_specs += [pl.BlockSpec((tm, LANES), lambda b, j: (j, 0)) for _ in tables]
    out_specs = [pl.BlockSpec((None, tm, w), lambda b, j: (b, j, 0)) for w, _ in outs]
    out_shape = [jax.ShapeDtypeStruct((bx, s, w), dt) for w, dt in outs]
    return pl.pallas_call(
        functools.partial(kernel, latent),
        grid=(bx, s // tm),
        in_specs=in_specs,
        out_specs=out_specs,
        out_shape=out_shape,
        compiler_params=_params("arbitrary", "arbitrary"),
        name=name,
    )(x, mods_l, g, *consts, *tables)


def _attn_a_kernel(has_ctx, tq, q_ref, k_ref, v_ref, *rest):
    if has_ctx:
        kc_ref, vc_ref, o_ref = rest
    else:
        (o_ref,) = rest
    group = ATTN_HEADS // ATTN_KV_HEADS
    q = q_ref[...]
    q_stack = jnp.concatenate([q[:, i * LANES:(i + 1) * LANES] for i in range(group)], axis=0)
    segs = [(k_ref[...], v_ref[...], None)]
    if has_ctx:
        segs.append((kc_ref[...], vc_ref[...], None))
    outs = _attend_stack(q_stack, group, tq, segs)
    for i in range(group):
        o_ref[:, i * LANES:(i + 1) * LANES] = outs[i].astype(BF16)


def _attention_a(q, k, v, kc=None, vc=None, *, tq):
    b, s, _ = q.shape
    kl = k.shape[1]
    gw = (ATTN_HEADS // ATTN_KV_HEADS) * ATTN_HEAD_DIM
    has_ctx = kc is not None
    in_specs = [
        pl.BlockSpec((None, tq, gw), lambda bi, g, t: (bi, t, g)),
        pl.BlockSpec((None, kl, LANES), lambda bi, g, t: (bi, 0, g)),
        pl.BlockSpec((None, kl, LANES), lambda bi, g, t: (bi, 0, g)),
    ]
    args = [q, k, v]
    if has_ctx:
        cl = kc.shape[1]
        in_specs += [pl.BlockSpec((None, cl, LANES), lambda bi, g, t: (bi, 0, g))] * 2
        args += [kc, vc]
    return pl.pallas_call(
        functools.partial(_attn_a_kernel, has_ctx, tq),
        grid=(b, ATTN_KV_HEADS, s // tq),
        in_specs=in_specs,
        out_specs=pl.BlockSpec((None, tq, gw), lambda bi, g, t: (bi, t, g)),
        out_shape=jax.ShapeDtypeStruct(q.shape, BF16),
        compiler_params=_params("arbitrary", "arbitrary", "arbitrary"),
        name="attn_gqa_latent" if has_ctx else "attn_gqa_context",
    )(*args)


def _attn_b_kernel(has_ctx, tq, qn_ref, qr_ref, kn_ref, kp_ref, v_ref, *rest):
    if has_ctx:
        knc_ref, kpc_ref, vc_ref, o_ref = rest
    else:
        (o_ref,) = rest
    pair = pl.program_id(1)
    qn = qn_ref[...]
    qr = qr_ref[...]
    lane = lax.broadcasted_iota(jnp.int32, qn.shape, 1)
    low = lane < (LANES // 2)
    quarter = lane // MLA_ROPE
    base = 2 * (pair % 2)
    zero = jnp.zeros_like(qn)
    lhs0 = jnp.concatenate([jnp.where(low, qn, zero), jnp.where(quarter == base, qr, zero)], axis=1)
    lhs1 = jnp.concatenate([jnp.where(low, zero, qn), jnp.where(quarter == base + 1, qr, zero)], axis=1)
    q_stack = jnp.concatenate([lhs0, lhs1], axis=0)
    segs = [(jnp.concatenate([kn_ref[...], kp_ref[...]], axis=1), v_ref[...], None)]
    if has_ctx:
        segs.append((jnp.concatenate([knc_ref[...], kpc_ref[...]], axis=1), vc_ref[...], None))
    outs = _attend_stack(q_stack, 2, tq, segs)
    o_ref[...] = jnp.where(low, outs[0], outs[1]).astype(BF16)


def _attention_b(qn, qr, kn, kp, v, knc=None, kpc=None, vc=None, *, tq):
    b, s, _ = qn.shape
    kl = kn.shape[1]
    has_ctx = knc is not None
    pairs = MLA_HEADS // 2
    blk = lambda rows, fn: pl.BlockSpec((None, rows, LANES), fn)
    in_specs = [
        blk(tq, lambda bi, g, t: (bi, t, g)),
        blk(tq, lambda bi, g, t: (bi, t, g // 2)),
        blk(kl, lambda bi, g, t: (bi, 0, g)),
        blk(kl, lambda bi, g, t: (bi, 0, 0)),
        blk(kl, lambda bi, g, t: (bi, 0, g)),
    ]
    args = [qn, qr, kn, kp, v]
    if has_ctx:
        cl = knc.shape[1]
        in_specs += [
            blk(cl, lambda bi, g, t: (bi, 0, g)),
            blk(cl, lambda bi, g, t: (bi, 0, 0)),
            blk(cl, lambda bi, g, t: (bi, 0, g)),
        ]
        args += [knc, kpc, vc]
    return pl.pallas_call(
        functools.partial(_attn_b_kernel, has_ctx, tq),
        grid=(b, pairs, s // tq),
        in_specs=in_specs,
        out_specs=blk(tq, lambda bi, g, t: (bi, t, g)),
        out_shape=jax.ShapeDtypeStruct(qn.shape, BF16),
        compiler_params=_params("arbitrary", "arbitrary", "arbitrary"),
        name="attn_mla_latent" if has_ctx else "attn_mla_context",
    )(*args)


def _attn_c_kernel(latent, tq, sink_ref, q_ref, kd_ref, vd_ref, *rest):
    if latent:
        kc_ref, vc_ref, o_ref = rest
    else:
        (o_ref,) = rest
    group = SWA_HEADS // SWA_KV_HEADS
    kvh = pl.program_id(1)
    q = q_ref[...]
    low = _low_half((tq, LANES))
    zero = jnp.zeros((tq, LANES), BF16)
    stack = []
    for g in range(group // 2):
        qg = q[:, g * LANES:(g + 1) * LANES]
        stack += [jnp.where(low, qg, zero), jnp.where(low, zero, qg)]
    q_stack = jnp.concatenate(stack, axis=0)
    if latent:
        s_len = kd_ref.shape[0]
        win = tq + 2 * SWA_WINDOW
        t0 = pl.program_id(2) * tq
        ws = pl.multiple_of(jnp.clip(t0 - SWA_WINDOW, 0, s_len - win), LANES)
        qpos = t0 + lax.broadcasted_iota(jnp.int32, (tq, win), 0)
        kpos = ws + lax.broadcasted_iota(jnp.int32, (tq, win), 1)
        bias = jnp.where(jnp.abs(qpos - kpos) <= SWA_WINDOW, 0.0, NEG_INF).astype(F32)
        segs = [(kd_ref[pl.ds(ws, win), :], vd_ref[pl.ds(ws, win), :], lambda i: bias),
                (kc_ref[...], vc_ref[...], None)]
    else:
        segs = [(kd_ref[...], vd_ref[...], None)]
    sinks = [sink_ref[group * kvh + i] for i in range(group)]
    outs = _attend_stack(q_stack, group, tq, segs, sinks)
    for g in range(group // 2):
        o_ref[:, g * LANES:(g + 1) * LANES] = jnp.where(low, outs[2 * g], outs[2 * g + 1]).astype(BF16)


def _attention_c(sink, q, kd, vd, kc=None, vc=None, *, tq):
    b, s, _ = q.shape
    kl = kd.shape[1]
    latent = kc is not None
    gw = (SWA_HEADS // SWA_KV_HEADS) * SWA_HEAD_DIM
    in_specs = [
        pl.BlockSpec(memory_space=pltpu.SMEM),
        pl.BlockSpec((None, tq, gw), lambda bi, g, t: (bi, t, g)),
        pl.BlockSpec((None, kl, LANES), lambda bi, g, t: (bi, 0, g)),
        pl.BlockSpec((None, kl, LANES), lambda bi, g, t: (bi, 0, g)),
    ]
    args = [sink, q, kd, vd]
    if latent:
        cl = kc.shape[1]
        in_specs += [pl.BlockSpec((None, cl, LANES), lambda bi, g, t: (bi, 0, g))] * 2
        args += [kc, vc]
    return pl.pallas_call(
        functools.partial(_attn_c_kernel, latent, tq),
        grid=(b, SWA_KV_HEADS, s // tq),
        in_specs=in_specs,
        out_specs=pl.BlockSpec((None, tq, gw), lambda bi, g, t: (bi, t, g)),
        out_shape=jax.ShapeDtypeStruct(q.shape, BF16),
        compiler_params=_params("arbitrary", "arbitrary", "arbitrary"),
        name="attn_swa_latent" if latent else "attn_swa_context",
    )(*args)


def _attn_d_ctx_kernel(tq, q_ref, k_ref, v_ref, o_ref):
    q = q_ref[...]
    low = _low_half(q.shape)
    zero = jnp.zeros_like(q)
    q_stack = jnp.concatenate([jnp.where(low, q, zero), jnp.where(low, zero, q)], axis=0)
    outs = _attend_stack(q_stack, 2, tq, [(k_ref[...], v_ref[...], None)])
    o_ref[...] = jnp.where(low, outs[0], outs[1]).astype(BF16)


def _attention_d_ctx(q, k, v):
    b, s, _ = q.shape
    blk = pl.BlockSpec((None, s, LANES), lambda bi, g: (bi, 0, g))
    return pl.pallas_call(
        functools.partial(_attn_d_ctx_kernel, s),
        grid=(b, NAT_HEADS // 2),
        in_specs=[blk, blk, blk],
        out_specs=blk,
        out_shape=jax.ShapeDtypeStruct(q.shape, BF16),
        compiler_params=_params("arbitrary", "arbitrary"),
        name="attn_nat_context",
    )(q, k, v)


NAT_ROWS_PER_STEP = 4


def _attn_d_lat_kernel(q_ref, k_ref, v_ref, kc_ref, vc_ref, bias_ref, o_ref):
    rows = k_ref.shape[0] // GRID_W
    win = NAT_WIN_R * GRID_W
    low = _low_half((GRID_W, LANES))
    zero = jnp.zeros((GRID_W, LANES), BF16)
    kc = kc_ref[...]
    vc = vc_ref[...]
    for i in range(NAT_ROWS_PER_STEP):
        r = pl.program_id(2) * NAT_ROWS_PER_STEP + i
        r0 = jnp.clip(r - NAT_WIN_R // 2, 0, rows - NAT_WIN_R)
        delta = r - r0
        start = pl.multiple_of(r0 * GRID_W, GRID_W)
        q = q_ref[i * GRID_W:(i + 1) * GRID_W, :]
        q_stack = jnp.concatenate([jnp.where(low, q, zero), jnp.where(low, zero, q)], axis=0)
        segs = [(k_ref[pl.ds(start, win), :], v_ref[pl.ds(start, win), :],
                 lambda h, delta=delta: bias_ref[h, delta]),
                (kc, vc, None)]
        outs = _attend_stack(q_stack, 2, GRID_W, segs)
        o_ref[i * GRID_W:(i + 1) * GRID_W, :] = jnp.where(low, outs[0], outs[1]).astype(BF16)


def _attention_d_lat(q, k, v, kc, vc, bias_win):
    b, s, _ = q.shape
    cl = kc.shape[1]
    tq = NAT_ROWS_PER_STEP * GRID_W
    win = NAT_WIN_R * GRID_W
    return pl.pallas_call(
        _attn_d_lat_kernel,
        grid=(NAT_HEADS // 2, b, s // tq),
        in_specs=[
            pl.BlockSpec((None, tq, LANES), lambda g, bi, t: (bi, t, g)),
            pl.BlockSpec((None, s, LANES), lambda g, bi, t: (bi, 0, g)),
            pl.BlockSpec((None, s, LANES), lambda g, bi, t: (bi, 0, g)),
            pl.BlockSpec((None, cl, LANES), lambda g, bi, t: (bi, 0, g)),
            pl.BlockSpec((None, cl, LANES), lambda g, bi, t: (bi, 0, g)),
            pl.BlockSpec((2, NAT_WIN_R, GRID_W, win), lambda g, bi, t: (g, 0, 0, 0)),
        ],
        out_specs=pl.BlockSpec((None, tq, LANES), lambda g, bi, t: (bi, t, g)),
        out_shape=jax.ShapeDtypeStruct(q.shape, BF16),
        compiler_params=_params("arbitrary", "arbitrary", "arbitrary"),
        name="attn_nat_latent",
    )(q, k, v, kc, vc, bias_win)


def _mla_expand_kernel(c_ref, w_ref, kn_ref, v_ref):
    kv = _dot(c_ref[...].astype(BF16), w_ref[...])
    half = kv.shape[1] // 2
    kn_ref[...] = kv[:, :half].astype(BF16)
    v_ref[...] = kv[:, half:].astype(BF16)


def _mla_expand(ckv, w_ukv):
    b, l, c = ckv.shape
    n = w_ukv.shape[1] // 2
    out = pl.BlockSpec((None, l, n), lambda bi: (bi, 0, 0))
    return pl.pallas_call(
        _mla_expand_kernel,
        grid=(b,),
        in_specs=[pl.BlockSpec((None, l, c), lambda bi: (bi, 0, 0)), _full(w_ukv.shape)],
        out_specs=[out, out],
        out_shape=[jax.ShapeDtypeStruct((b, l, n), BF16)] * 2,
        compiler_params=_params("arbitrary"),
        name="mla_expand_cache",
    )(ckv, w_ukv)


def _post_attn_kernel(o_ref, x_ref, m_ref, g_ref, wo_ref, x1_ref, h2_ref):
    m = m_ref[...]
    a = _dot(o_ref[...], wo_ref[...])
    x1 = x_ref[...] + _mod(m, 2) * _rms(a, g_ref[1:2, :])
    x1_ref[...] = x1
    h2_ref[...] = _premod(x1, g_ref[2:3, :], m, 1).astype(BF16)


def _post_attn(o, x, mods_l, g, wo, latent):
    bx, s, _ = x.shape
    tm = TOK_TILE
    row = (lambda b, j: (b, 0, 0)) if latent else (lambda b, j: (CTX_MOD_ROW, 0, 0))
    tok = lambda w: pl.BlockSpec((None, tm, w), lambda b, j: (b, j, 0))
    return pl.pallas_call(
        _post_attn_kernel,
        grid=(bx, s // tm),
        in_specs=[tok(o.shape[-1]), tok(D_MODEL),
                  pl.BlockSpec((None, 1, N_MOD * D_MODEL), row), _full(g.shape), _full(wo.shape)],
        out_specs=[tok(D_MODEL), tok(D_MODEL)],
        out_shape=[jax.ShapeDtypeStruct(x.shape, F32), jax.ShapeDtypeStruct(x.shape, BF16)],
        compiler_params=_params("arbitrary", "arbitrary"),
        name="out_proj_residual",
    )(o, x, mods_l, g, wo)


def _mlp_kernel(h_ref, x1_ref, m_ref, g_ref, w1_ref, w2_ref, o_ref, acc_ref):
    f = pl.program_id(2)

    @pl.when(f == 0)
    def _():
        acc_ref[...] = jnp.zeros_like(acc_ref)

    u = _dot(h_ref[...], w1_ref[...])
    u = jnp.square(jnp.maximum(u, 0.0))
    acc_ref[...] += _dot(u.astype(BF16), w2_ref[...])

    @pl.when(f == pl.num_programs(2) - 1)
    def _():
        o_ref[...] = x1_ref[...] + _mod(m_ref[...], 5) * _rms(acc_ref[...], g_ref[3:4, :])


def _mlp(h2, x1, mods_l, g, w1, w2, latent):
    bx, s, _ = x1.shape
    tm, tf = MLP_TOK_TILE, MLP_FF_TILE
    row = (lambda b, j, f: (b, 0, 0)) if latent else (lambda b, j, f: (CTX_MOD_ROW, 0, 0))
    tok = pl.BlockSpec((None, tm, D_MODEL), lambda b, j, f: (b, j, 0))
    return pl.pallas_call(
        _mlp_kernel,
        grid=(bx, s // tm, D_FF // tf),
        in_specs=[tok, tok, pl.BlockSpec((None, 1, N_MOD * D_MODEL), row), _full(g.shape),
                  pl.BlockSpec((D_MODEL, tf), lambda b, j, f: (0, f)),
                  pl.BlockSpec((tf, D_MODEL), lambda b, j, f: (f, 0))],
        out_specs=tok,
        out_shape=jax.ShapeDtypeStruct(x1.shape, F32),
        scratch_shapes=[pltpu.VMEM((tm, D_MODEL), F32)],
        compiler_params=_params("arbitrary", "arbitrary", "arbitrary"),
        name="mlp_relu2",
    )(h2, x1, mods_l, g, w1, w2)


def _rope_tables(s, dim):
    quarter = dim // 4
    t = jnp.arange(s)
    pos = jnp.stack([t // GRID_W, t % GRID_W], axis=-1).astype(F32)
    inv = ROPE_THETA ** (-jnp.arange(quarter, dtype=F32) / quarter)
    ang = pos[:, :, None] * inv
    cos = jnp.broadcast_to(jnp.cos(ang)[:, :, None, :], (s, 2, 2, quarter)).reshape(s, dim)
    sign = jnp.array([-1.0, 1.0], F32)[None, None, :, None]
    sin = (jnp.sin(ang)[:, :, None, :] * sign).reshape(s, dim)
    reps = LANES // dim
    return jnp.tile(cos, (1, reps)), jnp.tile(sin, (1, reps))


def _dup_heads(w, heads, dim):
    lead = w.shape[:-1]
    w = w.reshape(lead + (heads, 1, dim))
    return jnp.broadcast_to(w, lead + (heads, 2, dim)).reshape(lead + (heads * 2 * dim,))


def _nat_bias_windows(rpb):
    c = jnp.arange(GRID_W)
    c0 = jnp.clip(c - NAT_WIN_C // 2, 0, GRID_W - NAT_WIN_C)
    kc = jnp.arange(GRID_W)
    in_c = (kc[None, :] >= c0[:, None]) & (kc[None, :] < c0[:, None] + NAT_WIN_C)
    dc = jnp.clip(kc[None, :] - c[:, None] + NAT_WIN_C - 1, 0, 2 * NAT_WIN_C - 2)
    t = jnp.where(in_c[None, None], rpb[:, :, dc], NEG_INF)
    delta = jnp.arange(NAT_WIN_R)
    j = jnp.arange(NAT_WIN_R)
    a = j[None, :] + (NAT_WIN_R - 1) - delta[:, None]
    bw = t[:, a]
    bw = bw.transpose(0, 1, 3, 2, 4)
    return bw.reshape(rpb.shape[0], NAT_WIN_R, GRID_W, NAT_WIN_R * GRID_W).astype(F32)


def _finish_layer(o, x, mods_l, g, wo, w1, w2, latent):
    x1, h2 = _post_attn(o, x, mods_l, g, wo, latent)
    return _mlp(h2, x1, mods_l, g, w1, w2, latent)


def kernel(x_prompt, x_sample, cache_l0_k, cache_l0_v, cache_l1_ckv, cache_l1_kpe, cache_l2_k, cache_l2_v, cache_l3_k, cache_l3_v, c, c_ctx, ada_w, ada_b, norm_g, mlp_w1, mlp_w2, attn_w_qkv, attn_q_norm, attn_k_norm, attn_w_o, mla_w_in, mla_q_norm, mla_kv_norm, mla_w_uq, mla_w_ukv, mla_w_o, swa_w_qkv, swa_sink, swa_w_o, nat_w_qkv, nat_rpb, nat_w_o):
    nb, seq, d = x_prompt.shape
    db, dseq, _ = x_sample.shape
    past = cache_l0_k.shape[1]
    ctx_b = nb * seq // dseq
    xp = x_prompt.reshape(ctx_b, dseq, d)
    xs = x_sample

    cond = jnp.zeros((COND_ROWS, d), F32).at[:db].set(c).at[CTX_MOD_ROW].set(c_ctx)
    mods = _modulation(cond, ada_w, ada_b).reshape(DEPTH, COND_ROWS, 1, N_MOD * d)

    w1 = mlp_w1.astype(BF16)
    w2 = mlp_w2.astype(BF16)
    row = lambda v: v.reshape(1, -1)

    def as_ctx(a):
        return a.reshape(nb, seq, a.shape[-1])

    def as_slab(a):
        return a.reshape(ctx_b, dseq, a.shape[-1])

    g = norm_g[0]
    w = attn_w_qkv.astype(BF16)
    wo = attn_w_o.astype(BF16)
    consts = [w, row(attn_q_norm), row(attn_k_norm)]
    kvw = ATTN_KV_HEADS * ATTN_HEAD_DIM
    q, k, v, l0_k, l0_v = _run_proj(
        _proj_a_kernel, False, xp, mods[0], g, consts, [],
        [(d, BF16), (kvw, BF16), (kvw, BF16), (kvw, F32), (kvw, F32)], "proj_gqa_context")
    o = _attention_a(as_ctx(q), as_ctx(k), as_ctx(v), tq=seq)
    xp = _finish_layer(as_slab(o), xp, mods[0], g, wo, w1[0], w2[0], False)
    tables = list(_rope_tables(dseq, ATTN_HEAD_DIM))
    q, k, v = _run_proj(_proj_a_kernel, True, xs, mods[0], g, consts, tables,
                        [(d, BF16), (kvw, BF16), (kvw, BF16)], "proj_gqa_latent")
    o = _attention_a(q, k, v, cache_l0_k.reshape(db, past, kvw).astype(BF16),
                     cache_l0_v.reshape(db, past, kvw).astype(BF16), tq=128)
    xs = _finish_layer(o, xs, mods[0], g, wo, w1[0], w2[0], True)
    new_l0 = (l0_k.reshape(nb, seq, ATTN_KV_HEADS, ATTN_HEAD_DIM),
              l0_v.reshape(nb, seq, ATTN_KV_HEADS, ATTN_HEAD_DIM))

    g = norm_g[1]
    nq, nkv = MLA_Q_LORA, MLA_KV_LORA
    w_in = jnp.concatenate([mla_w_in[:, :nq + nkv]] + [mla_w_in[:, nq + nkv:]] * (LANES // MLA_ROPE),
                           axis=1).astype(BF16)
    wuq = mla_w_uq.reshape(nq, MLA_HEADS, MLA_NOPE + MLA_ROPE)
    wuq = jnp.concatenate([wuq[:, :, :MLA_NOPE].reshape(nq, -1), wuq[:, :, MLA_NOPE:].reshape(nq, -1)],
                          axis=1).astype(BF16)
    wukv = mla_w_ukv.reshape(nkv, MLA_HEADS, MLA_NOPE + MLA_V_DIM)
    wukv = jnp.concatenate([wukv[:, :, :MLA_NOPE].reshape(nkv, -1), wukv[:, :, MLA_NOPE:].reshape(nkv, -1)],
                           axis=1).astype(BF16)
    wo = mla_w_o.astype(BF16)
    consts = [w_in, row(mla_q_norm), row(mla_kv_norm), wuq, wukv]
    hw = MLA_HEADS * MLA_NOPE
    rw = MLA_HEADS * MLA_ROPE
    outs = [(hw, BF16), (rw, BF16), (hw, BF16), (hw, BF16), (LANES, BF16)]
    qn, qr, kn, v, kp, l1_ckv, l1_kpe = _run_proj(
        _proj_b_kernel, False, xp, mods[1], g, consts, [],
        outs + [(nkv, F32), (MLA_ROPE, F32)], "proj_mla_context")
    o = _attention_b(as_ctx(qn), as_ctx(qr), as_ctx(kn), as_ctx(kp), as_ctx(v), tq=seq)
    xp = _finish_layer(as_slab(o), xp, mods[1], g, wo, w1[1], w2[1], False)
    tables = list(_rope_tables(dseq, MLA_ROPE))
    qn, qr, kn, v, kp = _run_proj(_proj_b_kernel, True, xs, mods[1], g, consts, tables, outs,
                                  "proj_mla_latent")
    knc, vc = _mla_expand(cache_l1_ckv, wukv)
    kpc = jnp.tile(cache_l1_kpe, (1, 1, LANES // MLA_ROPE)).astype(BF16)
    o = _attention_b(qn, qr, kn, kp, v, knc, kpc, vc, tq=256)
    xs = _finish_layer(o, xs, mods[1], g, wo, w1[1], w2[1], True)
    new_l1 = (l1_ckv.reshape(nb, seq, nkv), l1_kpe.reshape(nb, seq, MLA_ROPE))

    g = norm_g[2]
    qw = SWA_HEADS * SWA_HEAD_DIM
    kw = SWA_KV_HEADS * SWA_HEAD_DIM
    wq, wk, wv = swa_w_qkv[:, :qw], swa_w_qkv[:, qw:qw + kw], swa_w_qkv[:, qw + kw:]
    wkd = _dup_heads(wk, SWA_KV_HEADS, SWA_HEAD_DIM)
    wvd = _dup_heads(wv, SWA_KV_HEADS, SWA_HEAD_DIM)
    w_ctx = jnp.concatenate([wq, wk, wv, wkd, wvd], axis=1).astype(BF16)
    w_lat = jnp.concatenate([wq, wkd, wvd], axis=1).astype(BF16)
    wo = swa_w_o.astype(BF16)
    q, kd, vd, l2_k, l2_v = _run_proj(
        _proj_c_kernel, False, xp, mods[2], g, [w_ctx], [],
        [(qw, BF16), (2 * kw, BF16), (2 * kw, BF16), (kw, F32), (kw, F32)], "proj_swa_context")
    o = _attention_c(swa_sink, as_ctx(q), as_ctx(kd), as_ctx(vd), tq=seq)
    xp = _finish_layer(as_slab(o), xp, mods[2], g, wo, w1[2], w2[2], False)
    tables = list(_rope_tables(dseq, SWA_HEAD_DIM))
    q, kd, vd = _run_proj(_proj_c_kernel, True, xs, mods[2], g, [w_lat], tables,
                          [(qw, BF16), (2 * kw, BF16), (2 * kw, BF16)], "proj_swa_latent")
    kc = _dup_heads(cache_l2_k.reshape(db, past, kw), SWA_KV_HEADS, SWA_HEAD_DIM).astype(BF16)
    vc = _dup_heads(cache_l2_v.reshape(db, past, kw), SWA_KV_HEADS, SWA_HEAD_DIM).astype(BF16)
    o = _attention_c(swa_sink, q, kd, vd, kc, vc, tq=128)
    xs = _finish_layer(o, xs, mods[2], g, wo, w1[2], w2[2], True)
    new_l2 = (l2_k.reshape(nb, seq, SWA_KV_HEADS, SWA_HEAD_DIM),
              l2_v.reshape(nb, seq, SWA_KV_HEADS, SWA_HEAD_DIM))

    g = norm_g[3]
    hw = NAT_HEADS * NAT_HEAD_DIM
    w = nat_w_qkv.astype(BF16)
    wo = nat_w_o.astype(BF16)
    q, k, v, l3_k, l3_v = _run_proj(
        _proj_d_kernel, False, xp, mods[3], g, [w], [],
        [(hw, BF16), (hw, BF16), (hw, BF16), (hw, F32), (hw, F32)], "proj_nat_context")
    o = _attention_d_ctx(as_ctx(q), as_ctx(k), as_ctx(v))
    xp = _finish_layer(as_slab(o), xp, mods[3], g, wo, w1[3], w2[3], False)
    q, k, v = _run_proj(_proj_d_kernel, True, xs, mods[3], g, [w], [],
                        [(hw, BF16), (hw, BF16), (hw, BF16)], "proj_nat_latent")
    o = _attention_d_lat(q, k, v, cache_l3_k.reshape(db, past, hw).astype(BF16),
                         cache_l3_v.reshape(db, past, hw).astype(BF16), _nat_bias_windows(nat_rpb))
    xs = _finish_layer(o, xs, mods[3], g, wo, w1[3], w2[3], True)
    new_l3 = (l3_k.reshape(nb, seq, NAT_HEADS, NAT_HEAD_DIM),
              l3_v.reshape(nb, seq, NAT_HEADS, NAT_HEAD_DIM))

    return (xp.reshape(nb, seq, d), xs) + new_l0 + new_l1 + new_l2 + new_l3
```

```python
import functools

import numpy as np

import jax
import jax.numpy as jnp
from jax import lax
from jax.experimental import pallas as pl
from jax.experimental.pallas import tpu as pltpu

F32 = jnp.float32
BF16 = jnp.bfloat16

D_MODEL = 1024
DEPTH = 4
N_MOD = 6
D_FF = 4 * D_MODEL
GRID_W = 64
ROPE_THETA = 10000.0
NORM_EPS = 1e-6
NEG_INF = -1e30
LOG2E = 1.4426950408889634

ATTN_HEADS, ATTN_KV_HEADS, ATTN_HEAD_DIM = 8, 2, 128
MLA_HEADS, MLA_Q_LORA, MLA_KV_LORA = 16, 384, 256
MLA_NOPE, MLA_ROPE, MLA_V_DIM = 64, 32, 64
MLA_SCALE = (MLA_NOPE + MLA_ROPE) ** -0.5
SWA_HEADS, SWA_KV_HEADS, SWA_HEAD_DIM, SWA_WINDOW = 16, 4, 64, 128
NAT_HEADS, NAT_HEAD_DIM, NAT_WIN_R, NAT_WIN_C = 16, 64, 8, 16

LANES = 128
COND_ROWS = 16
CTX_MOD_ROW = 8
VMEM_LIMIT = 48 * 1024 * 1024

TOK_TILE = 512
MLP_TOK_TILE = 1024
MLP_FF_TILE = 512
Q_TILE = 256
HEADS_PER_STEP = 8
NAT_TILE_ROWS = Q_TILE // GRID_W
NAT_KEY_ROWS = NAT_WIN_R + NAT_TILE_ROWS


def _params(*sem):
    return pltpu.CompilerParams(dimension_semantics=sem, vmem_limit_bytes=VMEM_LIMIT)


def _full(shape):
    nd = len(shape)
    return pl.BlockSpec(shape, lambda *_: (0,) * nd)


def _rms(x, g):
    return x * lax.rsqrt(jnp.mean(x * x, axis=-1, keepdims=True) + NORM_EPS) * g


def _mod(m, i):
    return m[:, i * D_MODEL:(i + 1) * D_MODEL]


def _premod(x, g, m, sub):
    return _rms(x, g) * (1.0 + _mod(m, 3 * sub + 1)) + _mod(m, 3 * sub)


def _rope(x, cos, sin_signed, quarter):
    n = x.shape[-1]
    lane = lax.broadcasted_iota(jnp.int32, x.shape, 1)
    first = ((lane // quarter) % 2) == 0
    partner = jnp.where(first, pltpu.roll(x, n - quarter, 1), pltpu.roll(x, quarter, 1))
    return x * cos + partner * sin_signed


def _dot(a, b):
    return jnp.dot(a, b, preferred_element_type=F32)


def _dot_nt(a, b):
    return lax.dot_general(a, b, (((1,), (1,)), ((), ())), preferred_element_type=F32)


def _with_ones(v):
    return jnp.concatenate([v, jnp.ones_like(v)], axis=1)


def _attend(q, segs, sink=None):
    logits = []
    for k, _, bias in segs:
        s = _dot_nt(q, k)
        logits.append(s if bias is None else s + bias)
    m = logits[0].max(axis=-1, keepdims=True)
    for s in logits[1:]:
        m = jnp.maximum(m, s.max(axis=-1, keepdims=True))
    if sink is not None:
        m = jnp.maximum(m, sink)
    acc = None
    for s, (_, v1, _) in zip(logits, segs):
        pv = _dot(jnp.exp2(s - m).astype(BF16), v1)
        acc = pv if acc is None else acc + pv
    den = acc[:, LANES:LANES + 1]
    if sink is not None:
        den = den + jnp.exp2(sink - m)
    return acc[:, :LANES] / den


def _attend_pair(q, segs, sinks=None):
    low = lax.broadcasted_iota(jnp.int32, q.shape, 1) < (LANES // 2)
    zero = jnp.zeros_like(q)
    outs = []
    for j in range(2):
        segs_j = [(k, v1, None if bias is None else bias[j]) for k, v1, bias in segs]
        outs.append(_attend(jnp.where(low if j == 0 else ~low, q, zero), segs_j,
                            None if sinks is None else sinks[j]))
    return jnp.where(low, outs[0], outs[1]).astype(BF16)


def _mods_kernel(cond_ref, w_ref, b_ref, o_ref):
    cnd = cond_ref[...]
    act = cnd * jax.nn.sigmoid(cnd)
    o_ref[...] = _dot(act.astype(BF16), w_ref[...].astype(BF16)) + b_ref[...]


def _modulation(cond, ada_w, ada_b):
    tn = 1536
    n = N_MOD * D_MODEL
    return pl.pallas_call(
        _mods_kernel,
        grid=(DEPTH, n // tn),
        in_specs=[
            _full((COND_ROWS, D_MODEL)),
            pl.BlockSpec((None, D_MODEL, tn), lambda l, j: (l, 0, j)),
            pl.BlockSpec((None, 1, tn), lambda l, j: (l, 0, j)),
        ],
        out_specs=pl.BlockSpec((None, COND_ROWS, tn), lambda l, j: (l, 0, j)),
        out_shape=jax.ShapeDtypeStruct((DEPTH, COND_ROWS, n), F32),
        compiler_params=_params("arbitrary", "arbitrary"),
        name="adaln_mods",
    )(cond, ada_w, ada_b.reshape(DEPTH, 1, n))


def _proj_a_kernel(latent, x_ref, m_ref, g_ref, w_ref, qn_ref, kn_ref, *rest):
    if latent:
        cos_ref, sin_ref, q_ref, k_ref, v_ref = rest
    else:
        q_ref, k_ref, v_ref, ks_ref, vs_ref = rest
    hd = ATTN_HEAD_DIM
    h = _premod(x_ref[...], g_ref[0:1, :], m_ref[...], 0)
    z = _dot(h.astype(BF16), w_ref[...])
    for i in range(ATTN_HEADS + ATTN_KV_HEADS):
        is_q = i < ATTN_HEADS
        y = _rms(z[:, i * hd:(i + 1) * hd], qn_ref[...] if is_q else kn_ref[...])
        if not latent and not is_q:
            j = i - ATTN_HEADS
            ks_ref[:, j * hd:(j + 1) * hd] = y
        if latent:
            y = _rope(y, cos_ref[...], sin_ref[...], hd // 4)
        if is_q:
            q_ref[:, i * hd:(i + 1) * hd] = (y * (hd ** -0.5 * LOG2E)).astype(BF16)
        else:
            j = i - ATTN_HEADS
            k_ref[:, j * hd:(j + 1) * hd] = y.astype(BF16)
    v = z[:, (ATTN_HEADS + ATTN_KV_HEADS) * hd:]
    v_ref[...] = v.astype(BF16)
    if not latent:
        vs_ref[...] = v


def _proj_b_kernel(latent, x_ref, m_ref, g_ref, win_ref, qn_ref, kvn_ref, wuq_ref, wukv_ref, *rest):
    if latent:
        cos_ref, sin_ref, qn_o, qr_o, kn_o, v_o, kp_o = rest
    else:
        qn_o, qr_o, kn_o, v_o, kp_o, ckv_s, kpe_s = rest
    nq, nkv = MLA_Q_LORA, MLA_KV_LORA
    nope_w = MLA_HEADS * MLA_NOPE
    scale = MLA_SCALE * LOG2E
    h = _premod(x_ref[...], g_ref[0:1, :], m_ref[...], 0)
    z = _dot(h.astype(BF16), win_ref[...])
    cq = _rms(z[:, :nq], qn_ref[...])
    ckv = _rms(z[:, nq:nq + nkv], kvn_ref[...])
    kp = z[:, nq + nkv:]
    q = _dot(cq.astype(BF16), wuq_ref[...])
    kv = _dot(ckv.astype(BF16), wukv_ref[...])
    if not latent:
        ckv_s[...] = ckv
        kpe_s[...] = kp[:, :MLA_ROPE]
    qn_o[...] = (q[:, :nope_w] * scale).astype(BF16)
    for i in range(MLA_HEADS * MLA_ROPE // LANES):
        qr = q[:, nope_w + i * LANES:nope_w + (i + 1) * LANES]
        if latent:
            qr = _rope(qr, cos_ref[...], sin_ref[...], MLA_ROPE // 4)
        qr_o[:, i * LANES:(i + 1) * LANES] = (qr * scale).astype(BF16)
    if latent:
        kp = _rope(kp, cos_ref[...], sin_ref[...], MLA_ROPE // 4)
    kp_o[...] = kp.astype(BF16)
    kn_o[...] = kv[:, :nope_w].astype(BF16)
    v_o[...] = kv[:, nope_w:].astype(BF16)


def _proj_c_kernel(latent, x_ref, m_ref, g_ref, w_ref, *rest):
    if latent:
        cos_ref, sin_ref, q_o, kd_o, vd_o = rest
    else:
        q_o, kd_o, vd_o, ks_o, vs_o = rest
    qw = SWA_HEADS * SWA_HEAD_DIM
    kw = SWA_KV_HEADS * SWA_HEAD_DIM
    h = _premod(x_ref[...], g_ref[0:1, :], m_ref[...], 0)
    z = _dot(h.astype(BF16), w_ref[...])
    off = qw
    if not latent:
        ks_o[...] = z[:, qw:qw + kw]
        vs_o[...] = z[:, qw + kw:qw + 2 * kw]
        off = qw + 2 * kw
    scale = SWA_HEAD_DIM ** -0.5 * LOG2E
    for i in range(qw // LANES):
        y = z[:, i * LANES:(i + 1) * LANES]
        if latent:
            y = _rope(y, cos_ref[...], sin_ref[...], SWA_HEAD_DIM // 4)
        q_o[:, i * LANES:(i + 1) * LANES] = (y * scale).astype(BF16)
    for i in range(2 * kw // LANES):
        y = z[:, off + i * LANES:off + (i + 1) * LANES]
        if latent:
            y = _rope(y, cos_ref[...], sin_ref[...], SWA_HEAD_DIM // 4)
        kd_o[:, i * LANES:(i + 1) * LANES] = y.astype(BF16)
    vd_o[...] = z[:, off + 2 * kw:].astype(BF16)


def _proj_d_kernel(latent, x_ref, m_ref, g_ref, w_ref, *rest):
    if latent:
        q_o, k_o, v_o = rest
    else:
        q_o, k_o, v_o, ks_o, vs_o = rest
    hw = NAT_HEADS * NAT_HEAD_DIM
    h = _premod(x_ref[...], g_ref[0:1, :], m_ref[...], 0)
    z = _dot(h.astype(BF16), w_ref[...])
    q_o[...] = (z[:, :hw] * (NAT_HEAD_DIM ** -0.5 * LOG2E)).astype(BF16)
    k_o[...] = z[:, hw:2 * hw].astype(BF16)
    v_o[...] = z[:, 2 * hw:].astype(BF16)
    if not latent:
        ks_o[...] = z[:, hw:2 * hw]
        vs_o[...] = z[:, 2 * hw:]


def _run_proj(kernel, latent, x, mods_l, g, consts, tables, outs, name):
    bx, s, _ = x.shape
    tm = TOK_TILE
    row = (lambda b, j: (b, 0, 0)) if latent else (lambda b, j: (CTX_MOD_ROW, 0, 0))
    in_specs = [
        pl.BlockSpec((None, tm, D_MODEL), lambda b, j: (b, j, 0)),
        pl.BlockSpec((None, 1, N_MOD * D_MODEL), row),
        _full(g.shape),
    ] + [_full(c.shape) for c in consts]
    in_specs += [pl.BlockSpec((tm, LANES), lambda b, j: (j, 0)) for _ in tables]
    out_specs = [pl.BlockSpec((None, tm, w), lambda b, j: (b, j, 0)) for w, _ in outs]
    out_shape = [jax.ShapeDtypeStruct((bx, s, w), dt) for w, dt in outs]
    return pl.pallas_call(
        functools.partial(kernel, latent),
        grid=(bx, s // tm),
        in_specs=in_specs,
        out_specs=out_specs,
        out_shape=out_shape,
        compiler_params=_params("arbitrary", "arbitrary"),
        name=name,
    )(x, mods_l, g, *consts, *tables)


def _attn_a_kernel(has_ctx, q_ref, k_ref, v_ref, *rest):
    if has_ctx:
        kc_ref, vc_ref, o_ref = rest
    else:
        (o_ref,) = rest
    group = ATTN_HEADS // ATTN_KV_HEADS
    for j in range(ATTN_KV_HEADS):
        kcols = slice(j * LANES, (j + 1) * LANES)
        segs = [(k_ref[:, kcols], _with_ones(v_ref[:, kcols]), None)]
        if has_ctx:
            segs.append((kc_ref[:, kcols], _with_ones(vc_ref[:, kcols]), None))
        for i in range(group):
            cols = slice((j * group + i) * LANES, (j * group + i + 1) * LANES)
            o_ref[:, cols] = _attend(q_ref[:, cols], segs).astype(BF16)


def _attention_a(q, k, v, kc=None, vc=None):
    b, s, qw = q.shape
    kl, kw = k.shape[1:]
    tq = min(Q_TILE, s)
    has_ctx = kc is not None
    in_specs = [
        pl.BlockSpec((None, tq, qw), lambda bi, t: (bi, t, 0)),
        pl.BlockSpec((None, kl, kw), lambda bi, t: (bi, 0, 0)),
        pl.BlockSpec((None, kl, kw), lambda bi, t: (bi, 0, 0)),
    ]
    args = [q, k, v]
    if has_ctx:
        cl = kc.shape[1]
        in_specs += [pl.BlockSpec((None, cl, kw), lambda bi, t: (bi, 0, 0))] * 2
        args += [kc, vc]
    return pl.pallas_call(
        functools.partial(_attn_a_kernel, has_ctx),
        grid=(b, s // tq),
        in_specs=in_specs,
        out_specs=pl.BlockSpec((None, tq, qw), lambda bi, t: (bi, t, 0)),
        out_shape=jax.ShapeDtypeStruct(q.shape, BF16),
        compiler_params=_params("arbitrary", "arbitrary"),
        name="attn_gqa_latent" if has_ctx else "attn_gqa_context",
    )(*args)


def _attn_b_kernel(has_ctx, qn_ref, qr_ref, kn_ref, kp_ref, v_ref, *rest):
    if has_ctx:
        knc_ref, kpc_ref, vc_ref, o_ref = rest
    else:
        (o_ref,) = rest
    tq = qn_ref.shape[0]
    lane = lax.broadcasted_iota(jnp.int32, (tq, LANES), 1)
    low = lane < (LANES // 2)
    quarter = lane // MLA_ROPE
    zero = jnp.zeros((tq, LANES), BF16)
    kp = kp_ref[...]
    kpc = kpc_ref[...] if has_ctx else None
    pairs_per_rope = LANES // MLA_ROPE // 2
    for p in range(HEADS_PER_STEP // 2):
        cols = slice(p * LANES, (p + 1) * LANES)
        qn = qn_ref[:, cols]
        rg = p // pairs_per_rope
        qr = qr_ref[:, rg * LANES:(rg + 1) * LANES]
        segs = [(jnp.concatenate([kn_ref[:, cols], kp], axis=1), _with_ones(v_ref[:, cols]), None)]
        if has_ctx:
            segs.append((jnp.concatenate([knc_ref[:, cols], kpc], axis=1), _with_ones(vc_ref[:, cols]), None))
        outs = []
        for j in range(2):
            lhs = jnp.concatenate([jnp.where(low if j == 0 else ~low, qn, zero),
                                   jnp.where(quarter == 2 * (p % pairs_per_rope) + j, qr, zero)], axis=1)
            outs.append(_attend(lhs, segs))
        o_ref[:, cols] = jnp.where(low, outs[0], outs[1]).astype(BF16)


def _attention_b(qn, qr, kn, kp, v, knc=None, kpc=None, vc=None):
    b, s, _ = qn.shape
    kl = kn.shape[1]
    tq = min(Q_TILE, s)
    has_ctx = knc is not None
    gw = HEADS_PER_STEP * MLA_NOPE
    blk = lambda rows, w, fn: pl.BlockSpec((None, rows, w), fn)
    in_specs = [
        blk(tq, gw, lambda bi, g, t: (bi, t, g)),
        blk(tq, HEADS_PER_STEP * MLA_ROPE, lambda bi, g, t: (bi, t, g)),
        blk(kl, gw, lambda bi, g, t: (bi, 0, g)),
        blk(kl, LANES, lambda bi, g, t: (bi, 0, 0)),
        blk(kl, gw, lambda bi, g, t: (bi, 0, g)),
    ]
    args = [qn, qr, kn, kp, v]
    if has_ctx:
        cl = knc.shape[1]
        in_specs += [
            blk(cl, gw, lambda bi, g, t: (bi, 0, g)),
            blk(cl, LANES, lambda bi, g, t: (bi, 0, 0)),
            blk(cl, gw, lambda bi, g, t: (bi, 0, g)),
        ]
        args += [knc, kpc, vc]
    return pl.pallas_call(
        functools.partial(_attn_b_kernel, has_ctx),
        grid=(b, MLA_HEADS // HEADS_PER_STEP, s // tq),
        in_specs=in_specs,
        out_specs=blk(tq, gw, lambda bi, g, t: (bi, t, g)),
        out_shape=jax.ShapeDtypeStruct(qn.shape, BF16),
        compiler_params=_params("arbitrary", "arbitrary", "arbitrary"),
        name="attn_mla_latent" if has_ctx else "attn_mla_context",
    )(*args)


def _attn_c_kernel(latent, sink_ref, q_ref, kd_ref, vd_ref, *rest):
    if latent:
        kc_ref, vc_ref, o_ref = rest
    else:
        (o_ref,) = rest
    tq = q_ref.shape[0]
    group = SWA_HEADS // SWA_KV_HEADS
    kv_per_step = HEADS_PER_STEP // group
    head0 = pl.program_id(1) * HEADS_PER_STEP
    if latent:
        s_len = kd_ref.shape[0]
        win = tq + 2 * SWA_WINDOW
        t0 = pl.program_id(2) * tq
        ws = pl.multiple_of(jnp.clip(t0 - SWA_WINDOW, 0, s_len - win), LANES)
        qpos = t0 + lax.broadcasted_iota(jnp.int32, (tq, win), 0)
        kpos = ws + lax.broadcasted_iota(jnp.int32, (tq, win), 1)
        bias = jnp.where(jnp.abs(qpos - kpos) <= SWA_WINDOW, 0.0, NEG_INF).astype(F32)
    for j in range(kv_per_step):
        kcols = slice(j * LANES, (j + 1) * LANES)
        if latent:
            segs = [(kd_ref[pl.ds(ws, win), kcols], _with_ones(vd_ref[pl.ds(ws, win), kcols]), (bias, bias)),
                    (kc_ref[:, kcols], _with_ones(vc_ref[:, kcols]), None)]
        else:
            segs = [(kd_ref[:, kcols], _with_ones(vd_ref[:, kcols]), None)]
        for i in range(group // 2):
            pair = j * (group // 2) + i
            cols = slice(pair * LANES, (pair + 1) * LANES)
            sinks = [sink_ref[head0 + 2 * pair + e] * LOG2E for e in range(2)]
            o_ref[:, cols] = _attend_pair(q_ref[:, cols], segs, sinks)


def _attention_c(sink, q, kd, vd, kc=None, vc=None):
    b, s, _ = q.shape
    kl = kd.shape[1]
    tq = min(Q_TILE, s)
    latent = kc is not None
    gw = HEADS_PER_STEP * SWA_HEAD_DIM
    kvw = 2 * SWA_HEAD_DIM * HEADS_PER_STEP // (SWA_HEADS // SWA_KV_HEADS)
    in_specs = [
        pl.BlockSpec(memory_space=pltpu.SMEM),
        pl.BlockSpec((None, tq, gw), lambda bi, g, t: (bi, t, g)),
        pl.BlockSpec((None, kl, kvw), lambda bi, g, t: (bi, 0, g)),
        pl.BlockSpec((None, kl, kvw), lambda bi, g, t: (bi, 0, g)),
    ]
    args = [sink, q, kd, vd]
    if latent:
        cl = kc.shape[1]
        in_specs += [pl.BlockSpec((None, cl, kvw), lambda bi, g, t: (bi, 0, g))] * 2
        args += [kc, vc]
    return pl.pallas_call(
        functools.partial(_attn_c_kernel, latent),
        grid=(b, SWA_HEADS // HEADS_PER_STEP, s // tq),
        in_specs=in_specs,
        out_specs=pl.BlockSpec((None, tq, gw), lambda bi, g, t: (bi, t, g)),
        out_shape=jax.ShapeDtypeStruct(q.shape, BF16),
        compiler_params=_params("arbitrary", "arbitrary", "arbitrary"),
        name="attn_swa_latent" if latent else "attn_swa_context",
    )(*args)


def _attn_d_ctx_kernel(q_ref, k_ref, v_ref, o_ref):
    for p in range(HEADS_PER_STEP // 2):
        cols = slice(p * LANES, (p + 1) * LANES)
        o_ref[:, cols] = _attend_pair(q_ref[:, cols], [(k_ref[:, cols], _with_ones(v_ref[:, cols]), None)])


def _attention_d_ctx(q, k, v):
    b, s, _ = q.shape
    gw = HEADS_PER_STEP * NAT_HEAD_DIM
    blk = pl.BlockSpec((None, s, gw), lambda bi, g: (bi, 0, g))
    return pl.pallas_call(
        _attn_d_ctx_kernel,
        grid=(b, NAT_HEADS // HEADS_PER_STEP),
        in_specs=[blk, blk, blk],
        out_specs=blk,
        out_shape=jax.ShapeDtypeStruct(q.shape, BF16),
        compiler_params=_params("arbitrary", "arbitrary"),
        name="attn_nat_context",
    )(q, k, v)


def _nat_window_start(first_row, rows):
    r0 = jnp.clip(first_row - NAT_WIN_R // 2, 0, rows - NAT_WIN_R)
    return jnp.minimum(r0, rows - NAT_KEY_ROWS)


def _attn_d_lat_kernel(q_ref, k_ref, v_ref, kc_ref, vc_ref, bias_ref, o_ref):
    rows = k_ref.shape[0] // GRID_W
    slab = NAT_KEY_ROWS * GRID_W
    start = _nat_window_start(pl.program_id(0) * NAT_TILE_ROWS, rows) * GRID_W
    start = pl.multiple_of(start, GRID_W)
    for p in range(HEADS_PER_STEP // 2):
        cols = slice(p * LANES, (p + 1) * LANES)
        segs = [(k_ref[pl.ds(start, slab), cols], _with_ones(v_ref[pl.ds(start, slab), cols]),
                 (bias_ref[2 * p], bias_ref[2 * p + 1])),
                (kc_ref[:, cols], _with_ones(vc_ref[:, cols]), None)]
        o_ref[:, cols] = _attend_pair(q_ref[:, cols], segs)


def _attention_d_lat(q, k, v, kc, vc, bias):
    b, s, _ = q.shape
    cl = kc.shape[1]
    tq = NAT_TILE_ROWS * GRID_W
    slab = NAT_KEY_ROWS * GRID_W
    gw = HEADS_PER_STEP * NAT_HEAD_DIM
    return pl.pallas_call(
        _attn_d_lat_kernel,
        grid=(s // tq, NAT_HEADS // HEADS_PER_STEP, b),
        in_specs=[
            pl.BlockSpec((None, tq, gw), lambda t, g, bi: (bi, t, g)),
            pl.BlockSpec((None, s, gw), lambda t, g, bi: (bi, 0, g)),
            pl.BlockSpec((None, s, gw), lambda t, g, bi: (bi, 0, g)),
            pl.BlockSpec((None, cl, gw), lambda t, g, bi: (bi, 0, g)),
            pl.BlockSpec((None, cl, gw), lambda t, g, bi: (bi, 0, g)),
            pl.BlockSpec((HEADS_PER_STEP, None, tq, slab), lambda t, g, bi: (g, t, 0, 0)),
        ],
        out_specs=pl.BlockSpec((None, tq, gw), lambda t, g, bi: (bi, t, g)),
        out_shape=jax.ShapeDtypeStruct(q.shape, BF16),
        compiler_params=_params("arbitrary", "arbitrary", "arbitrary"),
        name="attn_nat_latent",
    )(q, k, v, kc, vc, bias)


def _mla_expand_kernel(c_ref, w_ref, kn_ref, v_ref):
    kv = _dot(c_ref[...].astype(BF16), w_ref[...])
    half = kv.shape[1] // 2
    kn_ref[...] = kv[:, :half].astype(BF16)
    v_ref[...] = kv[:, half:].astype(BF16)


def _mla_expand(ckv, w_ukv):
    b, l, c = ckv.shape
    n = w_ukv.shape[1] // 2
    out = pl.BlockSpec((None, l, n), lambda bi: (bi, 0, 0))
    return pl.pallas_call(
        _mla_expand_kernel,
        grid=(b,),
        in_specs=[pl.BlockSpec((None, l, c), lambda bi: (bi, 0, 0)), _full(w_ukv.shape)],
        out_specs=[out, out],
        out_shape=[jax.ShapeDtypeStruct((b, l, n), BF16)] * 2,
        compiler_params=_params("arbitrary"),
        name="mla_expand_cache",
    )(ckv, w_ukv)


def _post_attn_kernel(o_ref, x_ref, m_ref, g_ref, wo_ref, x1_ref, h2_ref):
    m = m_ref[...]
    a = _dot(o_ref[...], wo_ref[...])
    x1 = x_ref[...] + _mod(m, 2) * _rms(a, g_ref[1:2, :])
    x1_ref[...] = x1
    h2_ref[...] = _premod(x1, g_ref[2:3, :], m, 1).astype(BF16)


def _post_attn(o, x, mods_l, g, wo, latent):
    bx, s, _ = x.shape
    tm = TOK_TILE
    row = (lambda b, j: (b, 0, 0)) if latent else (lambda b, j: (CTX_MOD_ROW, 0, 0))
    tok = lambda w: pl.BlockSpec((None, tm, w), lambda b, j: (b, j, 0))
    return pl.pallas_call(
        _post_attn_kernel,
        grid=(bx, s // tm),
        in_specs=[tok(o.shape[-1]), tok(D_MODEL),
                  pl.BlockSpec((None, 1, N_MOD * D_MODEL), row), _full(g.shape), _full(wo.shape)],
        out_specs=[tok(D_MODEL), tok(D_MODEL)],
        out_shape=[jax.ShapeDtypeStruct(x.shape, F32), jax.ShapeDtypeStruct(x.shape, BF16)],
        compiler_params=_params("arbitrary", "arbitrary"),
        name="out_proj_residual",
    )(o, x, mods_l, g, wo)


def _mlp_kernel(h_ref, x1_ref, m_ref, g_ref, w1_ref, w2_ref, o_ref, acc_ref):
    f = pl.program_id(2)

    @pl.when(f == 0)
    def _():
        acc_ref[...] = jnp.zeros_like(acc_ref)

    u = _dot(h_ref[...], w1_ref[...].astype(BF16))
    u = jnp.square(jnp.maximum(u, 0.0))
    acc_ref[...] += _dot(u.astype(BF16), w2_ref[...].astype(BF16))

    @pl.when(f == pl.num_programs(2) - 1)
    def _():
        o_ref[...] = x1_ref[...] + _mod(m_ref[...], 5) * _rms(acc_ref[...], g_ref[3:4, :])


def _mlp(h2, x1, mods_l, g, w1, w2, layer, latent):
    bx, s, _ = x1.shape
    tm, tf = MLP_TOK_TILE, MLP_FF_TILE
    row = (lambda b, j, f: (b, 0, 0)) if latent else (lambda b, j, f: (CTX_MOD_ROW, 0, 0))
    tok = pl.BlockSpec((None, tm, D_MODEL), lambda b, j, f: (b, j, 0))
    return pl.pallas_call(
        _mlp_kernel,
        grid=(bx, s // tm, D_FF // tf),
        in_specs=[tok, tok, pl.BlockSpec((None, 1, N_MOD * D_MODEL), row), _full(g.shape),
                  pl.BlockSpec((None, D_MODEL, tf), lambda b, j, f: (layer, 0, f)),
                  pl.BlockSpec((None, tf, D_MODEL), lambda b, j, f: (layer, f, 0))],
        out_specs=tok,
        out_shape=jax.ShapeDtypeStruct(x1.shape, F32),
        scratch_shapes=[pltpu.VMEM((tm, D_MODEL), F32)],
        compiler_params=_params("arbitrary", "arbitrary", "arbitrary"),
        name="mlp_relu2",
    )(h2, x1, mods_l, g, w1, w2)


def _rope_tables(s, dim):
    quarter = dim // 4
    t = jnp.arange(s)
    pos = jnp.stack([t // GRID_W, t % GRID_W], axis=-1).astype(F32)
    inv = ROPE_THETA ** (-jnp.arange(quarter, dtype=F32) / quarter)
    ang = pos[:, :, None] * inv
    cos = jnp.broadcast_to(jnp.cos(ang)[:, :, None, :], (s, 2, 2, quarter)).reshape(s, dim)
    sign = jnp.array([-1.0, 1.0], F32)[None, None, :, None]
    sin = (jnp.sin(ang)[:, :, None, :] * sign).reshape(s, dim)
    reps = LANES // dim
    return jnp.tile(cos, (1, reps)), jnp.tile(sin, (1, reps))


def _dup_heads(w, heads, dim):
    lead = w.shape[:-1]
    w = w.reshape(lead + (heads, 1, dim))
    return jnp.broadcast_to(w, lead + (heads, 2, dim)).reshape(lead + (heads * 2 * dim,))


def _nat_dense_bias(rpb, rows):
    heads = rpb.shape[0]
    c = np.arange(GRID_W)
    c0 = np.clip(c - NAT_WIN_C // 2, 0, GRID_W - NAT_WIN_C)
    in_c = (c[None, :] >= c0[:, None]) & (c[None, :] < c0[:, None] + NAT_WIN_C)
    dc = c[None, :] - c[:, None] + NAT_WIN_C - 1
    onehot = (dc[None] == np.arange(2 * NAT_WIN_C - 1)[:, None, None]) & in_c[None]
    toe = jnp.einsum("had,dck->hack", rpb * LOG2E, jnp.asarray(onehot, F32),
                     precision=lax.Precision.HIGHEST)
    toe = jnp.where(jnp.asarray(in_c)[None, None], toe, NEG_INF)
    tiles = rows // NAT_TILE_ROWS
    r = np.arange(rows)
    r0 = np.clip(r - NAT_WIN_R // 2, 0, rows - NAT_WIN_R)
    ws = np.minimum(r0[::NAT_TILE_ROWS], rows - NAT_KEY_ROWS)
    kr = ws[:, None, None] + np.arange(NAT_KEY_ROWS)[None, None, :]
    rr = r.reshape(tiles, NAT_TILE_ROWS)[:, :, None]
    r0r = r0.reshape(tiles, NAT_TILE_ROWS)[:, :, None]
    valid = (kr >= r0r) & (kr < r0r + NAT_WIN_R)
    a = np.clip(kr - rr + NAT_WIN_R - 1, 0, 2 * NAT_WIN_R - 2)
    dense = jnp.where(jnp.asarray(valid)[None, :, :, :, None, None], toe[:, a], NEG_INF)
    dense = dense.transpose(0, 1, 2, 4, 3, 5)
    return dense.reshape(heads, tiles, NAT_TILE_ROWS * GRID_W, NAT_KEY_ROWS * GRID_W)


def kernel(x_prompt, x_sample, cache_l0_k, cache_l0_v, cache_l1_ckv, cache_l1_kpe, cache_l2_k, cache_l2_v, cache_l3_k, cache_l3_v, c, c_ctx, ada_w, ada_b, norm_g, mlp_w1, mlp_w2, attn_w_qkv, attn_q_norm, attn_k_norm, attn_w_o, mla_w_in, mla_q_norm, mla_kv_norm, mla_w_uq, mla_w_ukv, mla_w_o, swa_w_qkv, swa_sink, swa_w_o, nat_w_qkv, nat_rpb, nat_w_o):
    nb, seq, d = x_prompt.shape
    db, dseq, _ = x_sample.shape
    past = cache_l0_k.shape[1]
    ctx_b = nb * seq // dseq
    xp = x_prompt.reshape(ctx_b, dseq, d)
    xs = x_sample

    cond = jnp.zeros((COND_ROWS, d), F32).at[:db].set(c).at[CTX_MOD_ROW].set(c_ctx)
    mods = _modulation(cond, ada_w, ada_b).reshape(DEPTH, COND_ROWS, 1, N_MOD * d)

    row = lambda v: v.reshape(1, -1)

    def as_ctx(a):
        return a.reshape(nb, seq, a.shape[-1])

    def as_slab(a):
        return a.reshape(ctx_b, dseq, a.shape[-1])

    def finish(o, x, layer, wo, latent):
        x1, h2 = _post_attn(o, x, mods[layer], norm_g[layer], wo, latent)
        return _mlp(h2, x1, mods[layer], norm_g[layer], mlp_w1, mlp_w2, layer, latent)

    g = norm_g[0]
    w = attn_w_qkv.astype(BF16)
    wo = attn_w_o.astype(BF16)
    consts = [w, row(attn_q_norm), row(attn_k_norm)]
    kvw = ATTN_KV_HEADS * ATTN_HEAD_DIM
    q, k, v, l0_k, l0_v = _run_proj(
        _proj_a_kernel, False, xp, mods[0], g, consts, [],
        [(d, BF16), (kvw, BF16), (kvw, BF16), (kvw, F32), (kvw, F32)], "proj_gqa_context")
    o = _attention_a(as_ctx(q), as_ctx(k), as_ctx(v))
    xp = finish(as_slab(o), xp, 0, wo, False)
    tables = list(_rope_tables(dseq, ATTN_HEAD_DIM))
    q, k, v = _run_proj(_proj_a_kernel, True, xs, mods[0], g, consts, tables,
                        [(d, BF16), (kvw, BF16), (kvw, BF16)], "proj_gqa_latent")
    o = _attention_a(q, k, v, cache_l0_k.reshape(db, past, kvw).astype(BF16),
                     cache_l0_v.reshape(db, past, kvw).astype(BF16))
    xs = finish(o, xs, 0, wo, True)
    new_l0 = (l0_k.reshape(nb, seq, ATTN_KV_HEADS, ATTN_HEAD_DIM),
              l0_v.reshape(nb, seq, ATTN_KV_HEADS, ATTN_HEAD_DIM))

    g = norm_g[1]
    nq, nkv = MLA_Q_LORA, MLA_KV_LORA
    w_in = jnp.concatenate([mla_w_in[:, :nq + nkv]] + [mla_w_in[:, nq + nkv:]] * (LANES // MLA_ROPE),
                           axis=1).astype(BF16)
    wuq = mla_w_uq.reshape(nq, MLA_HEADS, MLA_NOPE + MLA_ROPE)
    wuq = jnp.concatenate([wuq[:, :, :MLA_NOPE].reshape(nq, -1), wuq[:, :, MLA_NOPE:].reshape(nq, -1)],
                          axis=1).astype(BF16)
    wukv = mla_w_ukv.reshape(nkv, MLA_HEADS, MLA_NOPE + MLA_V_DIM)
    wukv = jnp.concatenate([wukv[:, :, :MLA_NOPE].reshape(nkv, -1), wukv[:, :, MLA_NOPE:].reshape(nkv, -1)],
                           axis=1).astype(BF16)
    wo = mla_w_o.astype(BF16)
    consts = [w_in, row(mla_q_norm), row(mla_kv_norm), wuq, wukv]
    hw = MLA_HEADS * MLA_NOPE
    rw = MLA_HEADS * MLA_ROPE
    outs = [(hw, BF16), (rw, BF16), (hw, BF16), (hw, BF16), (LANES, BF16)]
    qn, qr, kn, v, kp, l1_ckv, l1_kpe = _run_proj(
        _proj_b_kernel, False, xp, mods[1], g, consts, [],
        outs + [(nkv, F32), (MLA_ROPE, F32)], "proj_mla_context")
    o = _attention_b(as_ctx(qn), as_ctx(qr), as_ctx(kn), as_ctx(kp), as_ctx(v))
    xp = finish(as_slab(o), xp, 1, wo, False)
    tables = list(_rope_tables(dseq, MLA_ROPE))
    qn, qr, kn, v, kp = _run_proj(_proj_b_kernel, True, xs, mods[1], g, consts, tables, outs,
                                  "proj_mla_latent")
    knc, vc = _mla_expand(cache_l1_ckv, wukv)
    kpc = jnp.tile(cache_l1_kpe, (1, 1, LANES // MLA_ROPE)).astype(BF16)
    o = _attention_b(qn, qr, kn, kp, v, knc, kpc, vc)
    xs = finish(o, xs, 1, wo, True)
    new_l1 = (l1_ckv.reshape(nb, seq, nkv), l1_kpe.reshape(nb, seq, MLA_ROPE))

    g = norm_g[2]
    qw = SWA_HEADS * SWA_HEAD_DIM
    kw = SWA_KV_HEADS * SWA_HEAD_DIM
    wq, wk, wv = swa_w_qkv[:, :qw], swa_w_qkv[:, qw:qw + kw], swa_w_qkv[:, qw + kw:]
    wkd = _dup_heads(wk, SWA_KV_HEADS, SWA_HEAD_DIM)
    wvd = _dup_heads(wv, SWA_KV_HEADS, SWA_HEAD_DIM)
    w_ctx = jnp.concatenate([wq, wk, wv, wkd, wvd], axis=1).astype(BF16)
    w_lat = jnp.concatenate([wq, wkd, wvd], axis=1).astype(BF16)
    wo = swa_w_o.astype(BF16)
    q, kd, vd, l2_k, l2_v = _run_proj(
        _proj_c_kernel, False, xp, mods[2], g, [w_ctx], [],
        [(qw, BF16), (2 * kw, BF16), (2 * kw, BF16), (kw, F32), (kw, F32)], "proj_swa_context")
    o = _attention_c(swa_sink, as_ctx(q), as_ctx(kd), as_ctx(vd))
    xp = finish(as_slab(o), xp, 2, wo, False)
    tables = list(_rope_tables(dseq, SWA_HEAD_DIM))
    q, kd, vd = _run_proj(_proj_c_kernel, True, xs, mods[2], g, [w_lat], tables,
                          [(qw, BF16), (2 * kw, BF16), (2 * kw, BF16)], "proj_swa_latent")
    kc = _dup_heads(cache_l2_k.reshape(db, past, kw), SWA_KV_HEADS, SWA_HEAD_DIM).astype(BF16)
    vc = _dup_heads(cache_l2_v.reshape(db, past, kw), SWA_KV_HEADS, SWA_HEAD_DIM).astype(BF16)
    o = _attention_c(swa_sink, q, kd, vd, kc, vc)
    xs = finish(o, xs, 2, wo, True)
    new_l2 = (l2_k.reshape(nb, seq, SWA_KV_HEADS, SWA_HEAD_DIM),
              l2_v.reshape(nb, seq, SWA_KV_HEADS, SWA_HEAD_DIM))

    g = norm_g[3]
    hw = NAT_HEADS * NAT_HEAD_DIM
    w = nat_w_qkv.astype(BF16)
    wo = nat_w_o.astype(BF16)
    q, k, v, l3_k, l3_v = _run_proj(
        _proj_d_kernel, False, xp, mods[3], g, [w], [],
        [(hw, BF16), (hw, BF16), (hw, BF16), (hw, F32), (hw, F32)], "proj_nat_context")
    o = _attention_d_ctx(as_ctx(q), as_ctx(k), as_ctx(v))
    xp = finish(as_slab(o), xp, 3, wo, False)
    q, k, v = _run_proj(_proj_d_kernel, True, xs, mods[3], g, [w], [],
                        [(hw, BF16), (hw, BF16), (hw, BF16)], "proj_nat_latent")
    o = _attention_d_lat(q, k, v, cache_l3_k.reshape(db, past, hw).astype(BF16),
                         cache_l3_v.reshape(db, past, hw).astype(BF16),
                         _nat_dense_bias(nat_rpb, dseq // GRID_W))
    xs = finish(o, xs, 3, wo, True)
    new_l3 = (l3_k.reshape(nb, seq, NAT_HEADS, NAT_HEAD_DIM),
              l3_v.reshape(nb, seq, NAT_HEADS, NAT_HEAD_DIM))

    return (xp.reshape(nb, seq, d), xs) + new_l0 + new_l1 + new_l2 + new_l3
```

```python
import functools

import numpy as np

import jax
import jax.numpy as jnp
from jax import lax
from jax.experimental import pallas as pl
from jax.experimental.pallas import tpu as pltpu

F32 = jnp.float32
BF16 = jnp.bfloat16

D_MODEL = 1024
DEPTH = 4
N_MOD = 6
D_FF = 4 * D_MODEL
GRID_W = 64
ROPE_THETA = 10000.0
NORM_EPS = 1e-6
NEG_INF = -1e30
LOG2E = 1.4426950408889634

ATTN_HEADS, ATTN_KV_HEADS, ATTN_HEAD_DIM = 8, 2, 128
MLA_HEADS, MLA_Q_LORA, MLA_KV_LORA = 16, 384, 256
MLA_NOPE, MLA_ROPE, MLA_V_DIM = 64, 32, 64
MLA_SCALE = (MLA_NOPE + MLA_ROPE) ** -0.5
SWA_HEADS, SWA_KV_HEADS, SWA_HEAD_DIM, SWA_WINDOW = 16, 4, 64, 128
NAT_HEADS, NAT_HEAD_DIM, NAT_WIN_R, NAT_WIN_C = 16, 64, 8, 16

LANES = 128
COND_ROWS = 16
CTX_MOD_ROW = 8
VMEM_LIMIT = 48 * 1024 * 1024

TOK_TILE = 512
TOK_CHAIN = 128
MLP_TOK_TILE = 1024
MLP_FF_TILE = 512
Q_TILE = 256
HEADS_PER_STEP = 8
NAT_TILE_ROWS = Q_TILE // GRID_W
NAT_KEY_ROWS = NAT_WIN_R + NAT_TILE_ROWS


def _params(*sem):
    return pltpu.CompilerParams(dimension_semantics=sem, vmem_limit_bytes=VMEM_LIMIT)


def _full(shape):
    nd = len(shape)
    return pl.BlockSpec(shape, lambda *_: (0,) * nd)


def _row_chains(body, row_refs, tile, sub):
    def kernel(*refs):
        for c in range(tile // sub):
            rows = slice(c * sub, (c + 1) * sub)
            body(*[r.at[rows] if is_row else r for r, is_row in zip(refs, row_refs)])
    return kernel


def _rms(x, g):
    return x * lax.rsqrt(jnp.mean(x * x, axis=-1, keepdims=True) + NORM_EPS) * g


def _mod(m, i):
    return m[:, i * D_MODEL:(i + 1) * D_MODEL]


def _premod(x, g, m, sub):
    return _rms(x, g) * (1.0 + _mod(m, 3 * sub + 1)) + _mod(m, 3 * sub)


def _rope(x, cos, sin_signed, quarter):
    n = x.shape[-1]
    lane = lax.broadcasted_iota(jnp.int32, x.shape, 1)
    first = ((lane // quarter) % 2) == 0
    partner = jnp.where(first, pltpu.roll(x, n - quarter, 1), pltpu.roll(x, quarter, 1))
    return x * cos + partner * sin_signed


def _dot(a, b):
    return jnp.dot(a, b, preferred_element_type=F32)


def _dot_nt(a, b):
    return lax.dot_general(a, b, (((1,), (1,)), ((), ())), preferred_element_type=F32)


def _with_ones(v):
    return jnp.concatenate([v, jnp.ones_like(v)], axis=1)


def _attend(q, segs, sink=None):
    logits = []
    for k, _, bias in segs:
        s = _dot_nt(q, k)
        logits.append(s if bias is None else s + bias)
    m = logits[0].max(axis=-1, keepdims=True)
    for s in logits[1:]:
        m = jnp.maximum(m, s.max(axis=-1, keepdims=True))
    if sink is not None:
        m = jnp.maximum(m, sink)
    acc = None
    for s, (_, v1, _) in zip(logits, segs):
        pv = _dot(jnp.exp2(s - m).astype(BF16), v1)
        acc = pv if acc is None else acc + pv
    den = acc[:, LANES:LANES + 1]
    if sink is not None:
        den = den + jnp.exp2(sink - m)
    return acc[:, :LANES] / den


def _attend_group(qs, ks, v1s, sink_logits=None):
    tq = qs[0].shape[0]
    s = jnp.concatenate([_dot_nt(q, k) for q, k in zip(qs, ks)], axis=0)
    if sink_logits is not None:
        s = jnp.concatenate([s, sink_logits], axis=1)
        zeros = jnp.zeros((LANES, LANES), BF16)
        tail = jnp.concatenate([zeros, jnp.ones_like(zeros)], axis=1)
        v1s = [jnp.concatenate([v1, tail], axis=0) for v1 in v1s]
    p = jnp.exp2(s - s.max(axis=-1, keepdims=True)).astype(BF16)
    acc = jnp.concatenate([_dot(p[i * tq:(i + 1) * tq], v1) for i, v1 in enumerate(v1s)], axis=0)
    return acc[:, :LANES] / acc[:, LANES:LANES + 1]


def _split_pair(q):
    low = lax.broadcasted_iota(jnp.int32, q.shape, 1) < (LANES // 2)
    zero = jnp.zeros_like(q)
    return [jnp.where(low, q, zero), jnp.where(low, zero, q)]


def _merge_pair(o0, o1):
    low = lax.broadcasted_iota(jnp.int32, o0.shape, 1) < (LANES // 2)
    return jnp.where(low, o0, o1).astype(BF16)


def _attend_pair(q, segs, sinks=None):
    outs = []
    for j, qj in enumerate(_split_pair(q)):
        segs_j = [(k, v1, None if bias is None else bias[j]) for k, v1, bias in segs]
        outs.append(_attend(qj, segs_j, None if sinks is None else sinks[j]))
    return _merge_pair(*outs)


def _mods_kernel(cond_ref, w_ref, b_ref, o_ref):
    cnd = cond_ref[...]
    act = cnd * jax.nn.sigmoid(cnd)
    o_ref[...] = _dot(act.astype(BF16), w_ref[...].astype(BF16)) + b_ref[...]


def _modulation(cond, ada_w, ada_b):
    tn = 1536
    n = N_MOD * D_MODEL
    return pl.pallas_call(
        _mods_kernel,
        grid=(DEPTH, n // tn),
        in_specs=[
            _full((COND_ROWS, D_MODEL)),
            pl.BlockSpec((None, D_MODEL, tn), lambda l, j: (l, 0, j)),
            pl.BlockSpec((None, 1, tn), lambda l, j: (l, 0, j)),
        ],
        out_specs=pl.BlockSpec((None, COND_ROWS, tn), lambda l, j: (l, 0, j)),
        out_shape=jax.ShapeDtypeStruct((DEPTH, COND_ROWS, n), F32),
        compiler_params=_params("arbitrary", "arbitrary"),
        name="adaln_mods",
    )(cond, ada_w, ada_b.reshape(DEPTH, 1, n))


def _proj_a_kernel(latent, x_ref, m_ref, g_ref, w_ref, qn_ref, kn_ref, *rest):
    if latent:
        cos_ref, sin_ref, q_ref, k_ref, v_ref = rest
    else:
        q_ref, k_ref, v_ref, ks_ref, vs_ref = rest
    hd = ATTN_HEAD_DIM
    h = _premod(x_ref[...], g_ref[0:1, :], m_ref[...], 0)
    z = _dot(h.astype(BF16), w_ref[...])
    for i in range(ATTN_HEADS + ATTN_KV_HEADS):
        is_q = i < ATTN_HEADS
        y = _rms(z[:, i * hd:(i + 1) * hd], qn_ref[...] if is_q else kn_ref[...])
        if not latent and not is_q:
            j = i - ATTN_HEADS
            ks_ref[:, j * hd:(j + 1) * hd] = y
        if latent:
            y = _rope(y, cos_ref[...], sin_ref[...], hd // 4)
        if is_q:
            q_ref[:, i * hd:(i + 1) * hd] = (y * (hd ** -0.5 * LOG2E)).astype(BF16)
        else:
            j = i - ATTN_HEADS
            k_ref[:, j * hd:(j + 1) * hd] = y.astype(BF16)
    v = z[:, (ATTN_HEADS + ATTN_KV_HEADS) * hd:]
    v_ref[...] = v.astype(BF16)
    if not latent:
        vs_ref[...] = v


def _proj_b_kernel(latent, x_ref, m_ref, g_ref, win_ref, qn_ref, kvn_ref, wuq_ref, wukv_ref, *rest):
    if latent:
        cos_ref, sin_ref, qn_o, qr_o, kn_o, v_o, kp_o = rest
    else:
        qn_o, qr_o, kn_o, v_o, kp_o, ckv_s, kpe_s = rest
    nq, nkv = MLA_Q_LORA, MLA_KV_LORA
    nope_w = MLA_HEADS * MLA_NOPE
    scale = MLA_SCALE * LOG2E
    h = _premod(x_ref[...], g_ref[0:1, :], m_ref[...], 0)
    z = _dot(h.astype(BF16), win_ref[...])
    cq = _rms(z[:, :nq], qn_ref[...])
    ckv = _rms(z[:, nq:nq + nkv], kvn_ref[...])
    kp = z[:, nq + nkv:]
    q = _dot(cq.astype(BF16), wuq_ref[...])
    kv = _dot(ckv.astype(BF16), wukv_ref[...])
    if not latent:
        ckv_s[...] = ckv
        kpe_s[...] = kp[:, :MLA_ROPE]
    qn_o[...] = (q[:, :nope_w] * scale).astype(BF16)
    for i in range(MLA_HEADS * MLA_ROPE // LANES):
        qr = q[:, nope_w + i * LANES:nope_w + (i + 1) * LANES]
        if latent:
            qr = _rope(qr, cos_ref[...], sin_ref[...], MLA_ROPE // 4)
        qr_o[:, i * LANES:(i + 1) * LANES] = (qr * scale).astype(BF16)
    if latent:
        kp = _rope(kp, cos_ref[...], sin_ref[...], MLA_ROPE // 4)
    kp_o[...] = kp.astype(BF16)
    kn_o[...] = kv[:, :nope_w].astype(BF16)
    v_o[...] = kv[:, nope_w:].astype(BF16)


def _proj_c_kernel(latent, x_ref, m_ref, g_ref, w_ref, *rest):
    if latent:
        cos_ref, sin_ref, q_o, kd_o, vd_o = rest
    else:
        q_o, kd_o, vd_o, ks_o, vs_o = rest
    qw = SWA_HEADS * SWA_HEAD_DIM
    kw = SWA_KV_HEADS * SWA_HEAD_DIM
    h = _premod(x_ref[...], g_ref[0:1, :], m_ref[...], 0)
    z = _dot(h.astype(BF16), w_ref[...])
    off = qw
    if not latent:
        ks_o[...] = z[:, qw:qw + kw]
        vs_o[...] = z[:, qw + kw:qw + 2 * kw]
        off = qw + 2 * kw
    scale = SWA_HEAD_DIM ** -0.5 * LOG2E
    for i in range(qw // LANES):
        y = z[:, i * LANES:(i + 1) * LANES]
        if latent:
            y = _rope(y, cos_ref[...], sin_ref[...], SWA_HEAD_DIM // 4)
        q_o[:, i * LANES:(i + 1) * LANES] = (y * scale).astype(BF16)
    for i in range(2 * kw // LANES):
        y = z[:, off + i * LANES:off + (i + 1) * LANES]
        if latent:
            y = _rope(y, cos_ref[...], sin_ref[...], SWA_HEAD_DIM // 4)
        kd_o[:, i * LANES:(i + 1) * LANES] = y.astype(BF16)
    vd_o[...] = z[:, off + 2 * kw:].astype(BF16)


def _proj_d_kernel(latent, x_ref, m_ref, g_ref, w_ref, *rest):
    if latent:
        q_o, k_o, v_o = rest
    else:
        q_o, k_o, v_o, ks_o, vs_o = rest
    hw = NAT_HEADS * NAT_HEAD_DIM
    h = _premod(x_ref[...], g_ref[0:1, :], m_ref[...], 0)
    z = _dot(h.astype(BF16), w_ref[...])
    q_o[...] = (z[:, :hw] * (NAT_HEAD_DIM ** -0.5 * LOG2E)).astype(BF16)
    k_o[...] = z[:, hw:2 * hw].astype(BF16)
    v_o[...] = z[:, 2 * hw:].astype(BF16)
    if not latent:
        ks_o[...] = z[:, hw:2 * hw]
        vs_o[...] = z[:, 2 * hw:]


def _run_proj(kernel, latent, x, mods_l, g, consts, tables, outs, name):
    bx, s, _ = x.shape
    tm = TOK_TILE
    row = (lambda b, j: (b, 0, 0)) if latent else (lambda b, j: (CTX_MOD_ROW, 0, 0))
    in_specs = [
        pl.BlockSpec((None, tm, D_MODEL), lambda b, j: (b, j, 0)),
        pl.BlockSpec((None, 1, N_MOD * D_MODEL), row),
        _full(g.shape),
    ] + [_full(c.shape) for c in consts]
    in_specs += [pl.BlockSpec((tm, LANES), lambda b, j: (j, 0)) for _ in tables]
    out_specs = [pl.BlockSpec((None, tm, w), lambda b, j: (b, j, 0)) for w, _ in outs]
    out_shape = [jax.ShapeDtypeStruct((bx, s, w), dt) for w, dt in outs]
    return pl.pallas_call(
        functools.partial(kernel, latent),
        grid=(bx, s // tm),
        in_specs=in_specs,
        out_specs=out_specs,
        out_shape=out_shape,
        compiler_params=_params("arbitrary", "arbitrary"),
        name=name,
    )(x, mods_l, g, *consts, *tables)


def _attn_a_kernel(has_ctx, q_ref, k_ref, v_ref, *rest):
    if has_ctx:
        kc_ref, vc_ref, o_ref = rest
    else:
        (o_ref,) = rest
    group = ATTN_HEADS // ATTN_KV_HEADS
    head_cols = [slice(h * LANES, (h + 1) * LANES) for h in range(ATTN_HEADS)]
    kv_cols = [slice(j * LANES, (j + 1) * LANES) for j in range(ATTN_KV_HEADS)]
    if not has_ctx:
        tq = q_ref.shape[0]
        qs = [jnp.concatenate([q_ref[:, c] for c in head_cols[j * group:(j + 1) * group]], axis=0)
              for j in range(ATTN_KV_HEADS)]
        out = _attend_group(qs, [k_ref[:, c] for c in kv_cols], [_with_ones(v_ref[:, c]) for c in kv_cols])
        for h, c in enumerate(head_cols):
            o_ref[:, c] = out[h * tq:(h + 1) * tq].astype(BF16)
        return
    for j, kc in enumerate(kv_cols):
        segs = [(k_ref[:, kc], _with_ones(v_ref[:, kc]), None),
                (kc_ref[:, kc], _with_ones(vc_ref[:, kc]), None)]
        for c in head_cols[j * group:(j + 1) * group]:
            o_ref[:, c] = _attend(q_ref[:, c], segs).astype(BF16)


def _attention_a(q, k, v, kc=None, vc=None):
    b, s, qw = q.shape
    kl, kw = k.shape[1:]
    tq = min(Q_TILE, s)
    has_ctx = kc is not None
    in_specs = [
        pl.BlockSpec((None, tq, qw), lambda bi, t: (bi, t, 0)),
        pl.BlockSpec((None, kl, kw), lambda bi, t: (bi, 0, 0)),
        pl.BlockSpec((None, kl, kw), lambda bi, t: (bi, 0, 0)),
    ]
    args = [q, k, v]
    if has_ctx:
        cl = kc.shape[1]
        in_specs += [pl.BlockSpec((None, cl, kw), lambda bi, t: (bi, 0, 0))] * 2
        args += [kc, vc]
    return pl.pallas_call(
        functools.partial(_attn_a_kernel, has_ctx),
        grid=(b, s // tq),
        in_specs=in_specs,
        out_specs=pl.BlockSpec((None, tq, qw), lambda bi, t: (bi, t, 0)),
        out_shape=jax.ShapeDtypeStruct(q.shape, BF16),
        compiler_params=_params("arbitrary", "arbitrary"),
        name="attn_gqa_latent" if has_ctx else "attn_gqa_context",
    )(*args)


def _attn_b_kernel(has_ctx, qn_ref, qr_ref, kn_ref, kp_ref, v_ref, *rest):
    if has_ctx:
        knc_ref, kpc_ref, vc_ref, o_ref = rest
    else:
        (o_ref,) = rest
    tq = qn_ref.shape[0]
    lane = lax.broadcasted_iota(jnp.int32, (tq, LANES), 1)
    low = lane < (LANES // 2)
    quarter = lane // MLA_ROPE
    zero = jnp.zeros((tq, LANES), BF16)
    kp = kp_ref[...]
    kpc = kpc_ref[...] if has_ctx else None
    pairs_per_rope = LANES // MLA_ROPE // 2
    n_pairs = HEADS_PER_STEP // 2
    pair_cols = [slice(p * LANES, (p + 1) * LANES) for p in range(n_pairs)]

    def pair_lhs(p):
        qn = qn_ref[:, pair_cols[p]]
        rg = p // pairs_per_rope
        qr = qr_ref[:, rg * LANES:(rg + 1) * LANES]
        return [jnp.concatenate([jnp.where(low if j == 0 else ~low, qn, zero),
                                 jnp.where(quarter == 2 * (p % pairs_per_rope) + j, qr, zero)], axis=1)
                for j in range(2)]

    if not has_ctx:
        qs = [jnp.concatenate(pair_lhs(p), axis=0) for p in range(n_pairs)]
        out = _attend_group(qs, [jnp.concatenate([kn_ref[:, c], kp], axis=1) for c in pair_cols],
                            [_with_ones(v_ref[:, c]) for c in pair_cols])
        for p, c in enumerate(pair_cols):
            o_ref[:, c] = _merge_pair(out[2 * p * tq:(2 * p + 1) * tq], out[(2 * p + 1) * tq:(2 * p + 2) * tq])
        return
    for p, cols in enumerate(pair_cols):
        segs = [(jnp.concatenate([kn_ref[:, cols], kp], axis=1), _with_ones(v_ref[:, cols]), None),
                (jnp.concatenate([knc_ref[:, cols], kpc], axis=1), _with_ones(vc_ref[:, cols]), None)]
        o_ref[:, cols] = _merge_pair(*[_attend(lhs, segs) for lhs in pair_lhs(p)])


def _attention_b(qn, qr, kn, kp, v, knc=None, kpc=None, vc=None):
    b, s, _ = qn.shape
    kl = kn.shape[1]
    tq = min(Q_TILE, s)
    has_ctx = knc is not None
    gw = HEADS_PER_STEP * MLA_NOPE
    blk = lambda rows, w, fn: pl.BlockSpec((None, rows, w), fn)
    in_specs = [
        blk(tq, gw, lambda bi, g, t: (bi, t, g)),
        blk(tq, HEADS_PER_STEP * MLA_ROPE, lambda bi, g, t: (bi, t, g)),
        blk(kl, gw, lambda bi, g, t: (bi, 0, g)),
        blk(kl, LANES, lambda bi, g, t: (bi, 0, 0)),
        blk(kl, gw, lambda bi, g, t: (bi, 0, g)),
    ]
    args = [qn, qr, kn, kp, v]
    if has_ctx:
        cl = knc.shape[1]
        in_specs += [
            blk(cl, gw, lambda bi, g, t: (bi, 0, g)),
            blk(cl, LANES, lambda bi, g, t: (bi, 0, 0)),
            blk(cl, gw, lambda bi, g, t: (bi, 0, g)),
        ]
        args += [knc, kpc, vc]
    return pl.pallas_call(
        functools.partial(_attn_b_kernel, has_ctx),
        grid=(b, MLA_HEADS // HEADS_PER_STEP, s // tq),
        in_specs=in_specs,
        out_specs=blk(tq, gw, lambda bi, g, t: (bi, t, g)),
        out_shape=jax.ShapeDtypeStruct(qn.shape, BF16),
        compiler_params=_params("arbitrary", "arbitrary", "arbitrary"),
        name="attn_mla_latent" if has_ctx else "attn_mla_context",
    )(*args)


def _attn_c_kernel(latent, sink_ref, q_ref, kd_ref, vd_ref, *rest):
    if latent:
        kc_ref, vc_ref, o_ref = rest
    else:
        (o_ref,) = rest
    tq = q_ref.shape[0]
    group = SWA_HEADS // SWA_KV_HEADS
    kv_per_step = HEADS_PER_STEP // group
    head0 = pl.program_id(1) * HEADS_PER_STEP
    if latent:
        s_len = kd_ref.shape[0]
        win = tq + 2 * SWA_WINDOW
        t0 = pl.program_id(2) * tq
        ws = pl.multiple_of(jnp.clip(t0 - SWA_WINDOW, 0, s_len - win), LANES)
        qpos = t0 + lax.broadcasted_iota(jnp.int32, (tq, win), 0)
        kpos = ws + lax.broadcasted_iota(jnp.int32, (tq, win), 1)
        bias = jnp.where(jnp.abs(qpos - kpos) <= SWA_WINDOW, 0.0, NEG_INF).astype(F32)
    pairs_per_kv = group // 2
    pair_cols = [slice(p * LANES, (p + 1) * LANES) for p in range(HEADS_PER_STEP // 2)]
    kv_cols = [slice(j * LANES, (j + 1) * LANES) for j in range(kv_per_step)]
    if not latent:
        qs = [jnp.concatenate([h for c in pair_cols[j * pairs_per_kv:(j + 1) * pairs_per_kv]
                               for h in _split_pair(q_ref[:, c])], axis=0) for j in range(kv_per_step)]
        out = _attend_group(qs, [kd_ref[:, c] for c in kv_cols], [_with_ones(vd_ref[:, c]) for c in kv_cols],
                            sink_ref[...])
        for p, c in enumerate(pair_cols):
            o_ref[:, c] = _merge_pair(out[2 * p * tq:(2 * p + 1) * tq], out[(2 * p + 1) * tq:(2 * p + 2) * tq])
        return
    sinks = [sink_ref[head0 + h] * LOG2E for h in range(HEADS_PER_STEP)]
    for j, kc in enumerate(kv_cols):
        segs = [(kd_ref[pl.ds(ws, win), kc], _with_ones(vd_ref[pl.ds(ws, win), kc]), (bias, bias)),
                (kc_ref[:, kc], _with_ones(vc_ref[:, kc]), None)]
        for p in range(j * pairs_per_kv, (j + 1) * pairs_per_kv):
            o_ref[:, pair_cols[p]] = _attend_pair(q_ref[:, pair_cols[p]], segs, sinks[2 * p:2 * p + 2])


def _attention_c(sink, q, kd, vd, kc=None, vc=None):
    b, s, _ = q.shape
    kl = kd.shape[1]
    tq = min(Q_TILE, s)
    latent = kc is not None
    gw = HEADS_PER_STEP * SWA_HEAD_DIM
    kvw = 2 * SWA_HEAD_DIM * HEADS_PER_STEP // (SWA_HEADS // SWA_KV_HEADS)
    if latent:
        sink_spec = pl.BlockSpec(memory_space=pltpu.SMEM)
    else:
        rows = jnp.repeat(sink * LOG2E, tq).reshape(SWA_HEADS // HEADS_PER_STEP, HEADS_PER_STEP * tq, 1)
        lane0 = jnp.arange(LANES)[None, None, :] == 0
        sink = jnp.where(lane0, rows, NEG_INF)
        sink_spec = pl.BlockSpec((None, HEADS_PER_STEP * tq, LANES), lambda bi, g, t: (g, 0, 0))
    in_specs = [
        sink_spec,
        pl.BlockSpec((None, tq, gw), lambda bi, g, t: (bi, t, g)),
        pl.BlockSpec((None, kl, kvw), lambda bi, g, t: (bi, 0, g)),
        pl.BlockSpec((None, kl, kvw), lambda bi, g, t: (bi, 0, g)),
    ]
    args = [sink, q, kd, vd]
    if latent:
        cl = kc.shape[1]
        in_specs += [pl.BlockSpec((None, cl, kvw), lambda bi, g, t: (bi, 0, g))] * 2
        args += [kc, vc]
    return pl.pallas_call(
        functools.partial(_attn_c_kernel, latent),
        grid=(b, SWA_HEADS // HEADS_PER_STEP, s // tq),
        in_specs=in_specs,
        out_specs=pl.BlockSpec((None, tq, gw), lambda bi, g, t: (bi, t, g)),
        out_shape=jax.ShapeDtypeStruct(q.shape, BF16),
        compiler_params=_params("arbitrary", "arbitrary", "arbitrary"),
        name="attn_swa_latent" if latent else "attn_swa_context",
    )(*args)


def _attn_d_ctx_kernel(q_ref, k_ref, v_ref, o_ref):
    tq = q_ref.shape[0]
    pair_cols = [slice(p * LANES, (p + 1) * LANES) for p in range(HEADS_PER_STEP // 2)]
    qs = [jnp.concatenate(_split_pair(q_ref[:, c]), axis=0) for c in pair_cols]
    out = _attend_group(qs, [k_ref[:, c] for c in pair_cols], [_with_ones(v_ref[:, c]) for c in pair_cols])
    for p, c in enumerate(pair_cols):
        o_ref[:, c] = _merge_pair(out[2 * p * tq:(2 * p + 1) * tq], out[(2 * p + 1) * tq:(2 * p + 2) * tq])


def _attention_d_ctx(q, k, v):
    b, s, _ = q.shape
    gw = HEADS_PER_STEP * NAT_HEAD_DIM
    blk = pl.BlockSpec((None, s, gw), lambda bi, g: (bi, 0, g))
    return pl.pallas_call(
        _attn_d_ctx_kernel,
        grid=(b, NAT_HEADS // HEADS_PER_STEP),
        in_specs=[blk, blk, blk],
        out_specs=blk,
        out_shape=jax.ShapeDtypeStruct(q.shape, BF16),
        compiler_params=_params("arbitrary", "arbitrary"),
        name="attn_nat_context",
    )(q, k, v)


def _nat_window_start(first_row, rows):
    r0 = jnp.clip(first_row - NAT_WIN_R // 2, 0, rows - NAT_WIN_R)
    return jnp.minimum(r0, rows - NAT_KEY_ROWS)


def _attn_d_lat_kernel(q_ref, k_ref, v_ref, kc_ref, vc_ref, bias_ref, o_ref):
    rows = k_ref.shape[0] // GRID_W
    slab = NAT_KEY_ROWS * GRID_W
    start = _nat_window_start(pl.program_id(0) * NAT_TILE_ROWS, rows) * GRID_W
    start = pl.multiple_of(start, GRID_W)
    for p in range(HEADS_PER_STEP // 2):
        cols = slice(p * LANES, (p + 1) * LANES)
        segs = [(k_ref[pl.ds(start, slab), cols], _with_ones(v_ref[pl.ds(start, slab), cols]),
                 (bias_ref[2 * p], bias_ref[2 * p + 1])),
                (kc_ref[:, cols], _with_ones(vc_ref[:, cols]), None)]
        o_ref[:, cols] = _attend_pair(q_ref[:, cols], segs)


def _attention_d_lat(q, k, v, kc, vc, bias):
    b, s, _ = q.shape
    cl = kc.shape[1]
    tq = NAT_TILE_ROWS * GRID_W
    slab = NAT_KEY_ROWS * GRID_W
    gw = HEADS_PER_STEP * NAT_HEAD_DIM
    return pl.pallas_call(
        _attn_d_lat_kernel,
        grid=(s // tq, NAT_HEADS // HEADS_PER_STEP, b),
        in_specs=[
            pl.BlockSpec((None, tq, gw), lambda t, g, bi: (bi, t, g)),
            pl.BlockSpec((None, s, gw), lambda t, g, bi: (bi, 0, g)),
            pl.BlockSpec((None, s, gw), lambda t, g, bi: (bi, 0, g)),
            pl.BlockSpec((None, cl, gw), lambda t, g, bi: (bi, 0, g)),
            pl.BlockSpec((None, cl, gw), lambda t, g, bi: (bi, 0, g)),
            pl.BlockSpec((HEADS_PER_STEP, None, tq, slab), lambda t, g, bi: (g, t, 0, 0)),
        ],
        out_specs=pl.BlockSpec((None, tq, gw), lambda t, g, bi: (bi, t, g)),
        out_shape=jax.ShapeDtypeStruct(q.shape, BF16),
        compiler_params=_params("arbitrary", "arbitrary", "arbitrary"),
        name="attn_nat_latent",
    )(q, k, v, kc, vc, bias)


def _mla_expand_kernel(c_ref, w_ref, kn_ref, v_ref):
    kv = _dot(c_ref[...].astype(BF16), w_ref[...])
    half = kv.shape[1] // 2
    kn_ref[...] = kv[:, :half].astype(BF16)
    v_ref[...] = kv[:, half:].astype(BF16)


def _mla_expand(ckv, w_ukv):
    b, l, c = ckv.shape
    n = w_ukv.shape[1] // 2
    out = pl.BlockSpec((None, l, n), lambda bi: (bi, 0, 0))
    return pl.pallas_call(
        _mla_expand_kernel,
        grid=(b,),
        in_specs=[pl.BlockSpec((None, l, c), lambda bi: (bi, 0, 0)), _full(w_ukv.shape)],
        out_specs=[out, out],
        out_shape=[jax.ShapeDtypeStruct((b, l, n), BF16)] * 2,
        compiler_params=_params("arbitrary"),
        name="mla_expand_cache",
    )(ckv, w_ukv)


def _post_attn_kernel(o_ref, x_ref, m_ref, g_ref, wo_ref, x1_ref, h2_ref):
    m = m_ref[...]
    a = _dot(o_ref[...], wo_ref[...])
    x1 = x_ref[...] + _mod(m, 2) * _rms(a, g_ref[1:2, :])
    x1_ref[...] = x1
    h2_ref[...] = _premod(x1, g_ref[2:3, :], m, 1).astype(BF16)


def _post_attn(o, x, mods_l, g, wo, latent):
    bx, s, _ = x.shape
    tm = TOK_TILE
    row = (lambda b, j: (b, 0, 0)) if latent else (lambda b, j: (CTX_MOD_ROW, 0, 0))
    tok = lambda w: pl.BlockSpec((None, tm, w), lambda b, j: (b, j, 0))
    return pl.pallas_call(
        _row_chains(_post_attn_kernel, [True, True, False, False, False, True, True], tm, TOK_CHAIN),
        grid=(bx, s // tm),
        in_specs=[tok(o.shape[-1]), tok(D_MODEL),
                  pl.BlockSpec((None, 1, N_MOD * D_MODEL), row), _full(g.shape), _full(wo.shape)],
        out_specs=[tok(D_MODEL), tok(D_MODEL)],
        out_shape=[jax.ShapeDtypeStruct(x.shape, F32), jax.ShapeDtypeStruct(x.shape, BF16)],
        compiler_params=_params("arbitrary", "arbitrary"),
        name="out_proj_residual",
    )(o, x, mods_l, g, wo)


def _mlp_kernel(h_ref, x1_ref, m_ref, g_ref, w1_ref, w2_ref, o_ref, acc_ref):
    f = pl.program_id(2)

    @pl.when(f == 0)
    def _():
        acc_ref[...] = jnp.zeros_like(acc_ref)

    u = _dot(h_ref[...], w1_ref[...].astype(BF16))
    u = jnp.square(jnp.maximum(u, 0.0))
    acc_ref[...] += _dot(u.astype(BF16), w2_ref[...].astype(BF16))

    @pl.when(f == pl.num_programs(2) - 1)
    def _():
        o_ref[...] = x1_ref[...] + _mod(m_ref[...], 5) * _rms(acc_ref[...], g_ref[3:4, :])


def _mlp(h2, x1, mods_l, g, w1, w2, layer, latent):
    bx, s, _ = x1.shape
    tm, tf = MLP_TOK_TILE, MLP_FF_TILE
    row = (lambda b, j, f: (b, 0, 0)) if latent else (lambda b, j, f: (CTX_MOD_ROW, 0, 0))
    tok = pl.BlockSpec((None, tm, D_MODEL), lambda b, j, f: (b, j, 0))
    return pl.pallas_call(
        _mlp_kernel,
        grid=(bx, s // tm, D_FF // tf),
        in_specs=[tok, tok, pl.BlockSpec((None, 1, N_MOD * D_MODEL), row), _full(g.shape),
                  pl.BlockSpec((None, D_MODEL, tf), lambda b, j, f: (layer, 0, f)),
                  pl.BlockSpec((None, tf, D_MODEL), lambda b, j, f: (layer, f, 0))],
        out_specs=tok,
        out_shape=jax.ShapeDtypeStruct(x1.shape, F32),
        scratch_shapes=[pltpu.VMEM((tm, D_MODEL), F32)],
        compiler_params=_params("arbitrary", "arbitrary", "arbitrary"),
        name="mlp_relu2",
    )(h2, x1, mods_l, g, w1, w2)


def _rope_tables(s, dim):
    quarter = dim // 4
    t = jnp.arange(s)
    pos = jnp.stack([t // GRID_W, t % GRID_W], axis=-1).astype(F32)
    inv = ROPE_THETA ** (-jnp.arange(quarter, dtype=F32) / quarter)
    ang = pos[:, :, None] * inv
    cos = jnp.broadcast_to(jnp.cos(ang)[:, :, None, :], (s, 2, 2, quarter)).reshape(s, dim)
    sign = jnp.array([-1.0, 1.0], F32)[None, None, :, None]
    sin = (jnp.sin(ang)[:, :, None, :] * sign).reshape(s, dim)
    reps = LANES // dim
    return jnp.tile(cos, (1, reps)), jnp.tile(sin, (1, reps))


def _dup_heads(w, heads, dim):
    lead = w.shape[:-1]
    w = w.reshape(lead + (heads, 1, dim))
    return jnp.broadcast_to(w, lead + (heads, 2, dim)).reshape(lead + (heads * 2 * dim,))


def _nat_dense_bias(rpb, rows):
    heads = rpb.shape[0]
    c = np.arange(GRID_W)
    c0 = np.clip(c - NAT_WIN_C // 2, 0, GRID_W - NAT_WIN_C)
    in_c = (c[None, :] >= c0[:, None]) & (c[None, :] < c0[:, None] + NAT_WIN_C)
    dc = c[None, :] - c[:, None] + NAT_WIN_C - 1
    onehot = (dc[None] == np.arange(2 * NAT_WIN_C - 1)[:, None, None]) & in_c[None]
    toe = jnp.einsum("had,dck->hack", rpb * LOG2E, jnp.asarray(onehot, F32),
                     precision=lax.Precision.HIGHEST)
    toe = jnp.where(jnp.asarray(in_c)[None, None], toe, NEG_INF)
    pad = jnp.full((heads, 1, GRID_W, GRID_W), NEG_INF, F32)
    ext = jnp.concatenate([pad, toe, pad], axis=1)
    pairs = jnp.concatenate([ext[:, :-1], ext[:, 1:]], axis=-1)
    tiles = rows // NAT_TILE_ROWS
    tq, slab = NAT_TILE_ROWS * GRID_W, NAT_KEY_ROWS * GRID_W
    n_off = 2 * NAT_WIN_R
    return pl.pallas_call(
        functools.partial(_nat_bias_kernel, rows),
        grid=(heads, tiles),
        in_specs=[pl.BlockSpec((None, n_off, GRID_W, LANES), lambda h, t: (h, 0, 0, 0))],
        out_specs=pl.BlockSpec((None, None, tq, slab), lambda h, t: (h, t, 0, 0)),
        out_shape=jax.ShapeDtypeStruct((heads, tiles, tq, slab), F32),
        compiler_params=_params("arbitrary", "arbitrary"),
        name="nat_bias_expand",
    )(pairs)


def _nat_bias_kernel(rows, pairs_ref, o_ref):
    first = pl.program_id(1) * NAT_TILE_ROWS
    ws = _nat_window_start(first, rows)
    lane = lax.broadcasted_iota(jnp.int32, (GRID_W, LANES), 1)
    for i in range(NAT_TILE_ROWS):
        r = first + i
        r0 = jnp.clip(r - NAT_WIN_R // 2, 0, rows - NAT_WIN_R)
        for jb in range(NAT_KEY_ROWS // 2):
            kr = ws + 2 * jb
            ok_lo = ((kr >= r0) & (kr < r0 + NAT_WIN_R)).astype(jnp.int32)
            ok_hi = ((kr + 1 >= r0) & (kr + 1 < r0 + NAT_WIN_R)).astype(jnp.int32)
            a = jnp.clip(kr - r + NAT_WIN_R, 0, 2 * NAT_WIN_R - 1)
            ok = jnp.where(lane < GRID_W, ok_lo, ok_hi) > 0
            o_ref[i * GRID_W:(i + 1) * GRID_W, jb * LANES:(jb + 1) * LANES] = jnp.where(ok, pairs_ref[a], NEG_INF)


def kernel(x_prompt, x_sample, cache_l0_k, cache_l0_v, cache_l1_ckv, cache_l1_kpe, cache_l2_k, cache_l2_v, cache_l3_k, cache_l3_v, c, c_ctx, ada_w, ada_b, norm_g, mlp_w1, mlp_w2, attn_w_qkv, attn_q_norm, attn_k_norm, attn_w_o, mla_w_in, mla_q_norm, mla_kv_norm, mla_w_uq, mla_w_ukv, mla_w_o, swa_w_qkv, swa_sink, swa_w_o, nat_w_qkv, nat_rpb, nat_w_o):
    nb, seq, d = x_prompt.shape
    db, dseq, _ = x_sample.shape
    past = cache_l0_k.shape[1]
    ctx_b = nb * seq // dseq
    xp = x_prompt.reshape(ctx_b, dseq, d)
    xs = x_sample

    cond = jnp.zeros((COND_ROWS, d), F32).at[:db].set(c).at[CTX_MOD_ROW].set(c_ctx)
    mods = _modulation(cond, ada_w, ada_b).reshape(DEPTH, COND_ROWS, 1, N_MOD * d)

    row = lambda v: v.reshape(1, -1)

    def as_ctx(a):
        return a.reshape(nb, seq, a.shape[-1])

    def as_slab(a):
        return a.reshape(ctx_b, dseq, a.shape[-1])

    def finish(o, x, layer, wo, latent):
        x1, h2 = _post_attn(o, x, mods[layer], norm_g[layer], wo, latent)
        return _mlp(h2, x1, mods[layer], norm_g[layer], mlp_w1, mlp_w2, layer, latent)

    g = norm_g[0]
    w = attn_w_qkv.astype(BF16)
    wo = attn_w_o.astype(BF16)
    consts = [w, row(attn_q_norm), row(attn_k_norm)]
    kvw = ATTN_KV_HEADS * ATTN_HEAD_DIM
    q, k, v, l0_k, l0_v = _run_proj(
        _proj_a_kernel, False, xp, mods[0], g, consts, [],
        [(d, BF16), (kvw, BF16), (kvw, BF16), (kvw, F32), (kvw, F32)], "proj_gqa_context")
    o = _attention_a(as_ctx(q), as_ctx(k), as_ctx(v))
    xp = finish(as_slab(o), xp, 0, wo, False)
    tables = list(_rope_tables(dseq, ATTN_HEAD_DIM))
    q, k, v = _run_proj(_proj_a_kernel, True, xs, mods[0], g, consts, tables,
                        [(d, BF16), (kvw, BF16), (kvw, BF16)], "proj_gqa_latent")
    o = _attention_a(q, k, v, cache_l0_k.reshape(db, past, kvw).astype(BF16),
                     cache_l0_v.reshape(db, past, kvw).astype(BF16))
    xs = finish(o, xs, 0, wo, True)
    new_l0 = (l0_k.reshape(nb, seq, ATTN_KV_HEADS, ATTN_HEAD_DIM),
              l0_v.reshape(nb, seq, ATTN_KV_HEADS, ATTN_HEAD_DIM))

    g = norm_g[1]
    nq, nkv = MLA_Q_LORA, MLA_KV_LORA
    w_in = jnp.concatenate([mla_w_in[:, :nq + nkv]] + [mla_w_in[:, nq + nkv:]] * (LANES // MLA_ROPE),
                           axis=1).astype(BF16)
    wuq = mla_w_uq.reshape(nq, MLA_HEADS, MLA_NOPE + MLA_ROPE)
    wuq = jnp.concatenate([wuq[:, :, :MLA_NOPE].reshape(nq, -1), wuq[:, :, MLA_NOPE:].reshape(nq, -1)],
                          axis=1).astype(BF16)
    wukv = mla_w_ukv.reshape(nkv, MLA_HEADS, MLA_NOPE + MLA_V_DIM)
    wukv = jnp.concatenate([wukv[:, :, :MLA_NOPE].reshape(nkv, -1), wukv[:, :, MLA_NOPE:].reshape(nkv, -1)],
                           axis=1).astype(BF16)
    wo = mla_w_o.astype(BF16)
    consts = [w_in, row(mla_q_norm), row(mla_kv_norm), wuq, wukv]
    hw = MLA_HEADS * MLA_NOPE
    rw = MLA_HEADS * MLA_ROPE
    outs = [(hw, BF16), (rw, BF16), (hw, BF16), (hw, BF16), (LANES, BF16)]
    qn, qr, kn, v, kp, l1_ckv, l1_kpe = _run_proj(
        _proj_b_kernel, False, xp, mods[1], g, consts, [],
        outs + [(nkv, F32), (MLA_ROPE, F32)], "proj_mla_context")
    o = _attention_b(as_ctx(qn), as_ctx(qr), as_ctx(kn), as_ctx(kp), as_ctx(v))
    xp = finish(as_slab(o), xp, 1, wo, False)
    tables = list(_rope_tables(dseq, MLA_ROPE))
    qn, qr, kn, v, kp = _run_proj(_proj_b_kernel, True, xs, mods[1], g, consts, tables, outs,
                                  "proj_mla_latent")
    knc, vc = _mla_expand(cache_l1_ckv, wukv)
    kpc = jnp.tile(cache_l1_kpe, (1, 1, LANES // MLA_ROPE)).astype(BF16)
    o = _attention_b(qn, qr, kn, kp, v, knc, kpc, vc)
    xs = finish(o, xs, 1, wo, True)
    new_l1 = (l1_ckv.reshape(nb, seq, nkv), l1_kpe.reshape(nb, seq, MLA_ROPE))

    g = norm_g[2]
    qw = SWA_HEADS * SWA_HEAD_DIM
    kw = SWA_KV_HEADS * SWA_HEAD_DIM
    wq, wk, wv = swa_w_qkv[:, :qw], swa_w_qkv[:, qw:qw + kw], swa_w_qkv[:, qw + kw:]
    wkd = _dup_heads(wk, SWA_KV_HEADS, SWA_HEAD_DIM)
    wvd = _dup_heads(wv, SWA_KV_HEADS, SWA_HEAD_DIM)
    w_ctx = jnp.concatenate([wq, wk, wv, wkd, wvd], axis=1).astype(BF16)
    w_lat = jnp.concatenate([wq, wkd, wvd], axis=1).astype(BF16)
    wo = swa_w_o.astype(BF16)
    q, kd, vd, l2_k, l2_v = _run_proj(
        _proj_c_kernel, False, xp, mods[2], g, [w_ctx], [],
        [(qw, BF16), (2 * kw, BF16), (2 * kw, BF16), (kw, F32), (kw, F32)], "proj_swa_context")
    o = _attention_c(swa_sink, as_ctx(q), as_ctx(kd), as_ctx(vd))
    xp = finish(as_slab(o), xp, 2, wo, False)
    tables = list(_rope_tables(dseq, SWA_HEAD_DIM))
    q, kd, vd = _run_proj(_proj_c_kernel, True, xs, mods[2], g, [w_lat], tables,
                          [(qw, BF16), (2 * kw, BF16), (2 * kw, BF16)], "proj_swa_latent")
    kc = _dup_heads(cache_l2_k.reshape(db, past, kw), SWA_KV_HEADS, SWA_HEAD_DIM).astype(BF16)
    vc = _dup_heads(cache_l2_v.reshape(db, past, kw), SWA_KV_HEADS, SWA_HEAD_DIM).astype(BF16)
    o = _attention_c(swa_sink, q, kd, vd, kc, vc)
    xs = finish(o, xs, 2, wo, True)
    new_l2 = (l2_k.reshape(nb, seq, SWA_KV_HEADS, SWA_HEAD_DIM),
              l2_v.reshape(nb, seq, SWA_KV_HEADS, SWA_HEAD_DIM))

    g = norm_g[3]
    hw = NAT_HEADS * NAT_HEAD_DIM
    w = nat_w_qkv.astype(BF16)
    wo = nat_w_o.astype(BF16)
    q, k, v, l3_k, l3_v = _run_proj(
        _proj_d_kernel, False, xp, mods[3], g, [w], [],
        [(hw, BF16), (hw, BF16), (hw, BF16), (hw, F32), (hw, F32)], "proj_nat_context")
    o = _attention_d_ctx(as_ctx(q), as_ctx(k), as_ctx(v))
    xp = finish(as_slab(o), xp, 3, wo, False)
    q, k, v = _run_proj(_proj_d_kernel, True, xs, mods[3], g, [w], [],
                        [(hw, BF16), (hw, BF16), (hw, BF16)], "proj_nat_latent")
    o = _attention_d_lat(q, k, v, cache_l3_k.reshape(db, past, hw).astype(BF16),
                         cache_l3_v.reshape(db, past, hw).astype(BF16),
                         _nat_dense_bias(nat_rpb, dseq // GRID_W))
    xs = finish(o, xs, 3, wo, True)
    new_l3 = (l3_k.reshape(nb, seq, NAT_HEADS, NAT_HEAD_DIM),
              l3_v.reshape(nb, seq, NAT_HEADS, NAT_HEAD_DIM))

    return (xp.reshape(nb, seq, d), xs) + new_l0 + new_l1 + new_l2 + new_l3
```

```python
import functools

import numpy as np

import jax
import jax.numpy as jnp
from jax import lax
from jax.experimental import pallas as pl
from jax.experimental.pallas import tpu as pltpu

F32 = jnp.float32
BF16 = jnp.bfloat16

D_MODEL = 1024
DEPTH = 4
N_MOD = 6
D_FF = 4 * D_MODEL
GRID_W = 64
ROPE_THETA = 10000.0
NORM_EPS = 1e-6
NEG_INF = -1e30
LOG2E = 1.4426950408889634

ATTN_HEADS, ATTN_KV_HEADS, ATTN_HEAD_DIM = 8, 2, 128
MLA_HEADS, MLA_Q_LORA, MLA_KV_LORA = 16, 384, 256
MLA_NOPE, MLA_ROPE, MLA_V_DIM = 64, 32, 64
MLA_SCALE = (MLA_NOPE + MLA_ROPE) ** -0.5
SWA_HEADS, SWA_KV_HEADS, SWA_HEAD_DIM, SWA_WINDOW = 16, 4, 64, 128
NAT_HEADS, NAT_HEAD_DIM, NAT_WIN_R, NAT_WIN_C = 16, 64, 8, 16

LANES = 128
COND_ROWS = 16
CTX_MOD_ROW = 8
VMEM_LIMIT = 48 * 1024 * 1024

TOK_TILE = 512
TOK_CHAIN = 128
MLP_TOK_TILE = 1024
MLP_FF_TILE = 1024
MLP_FF_CHAIN = 512
MLP_VMEM_LIMIT = 56 * 1024 * 1024
Q_TILE = 256
HEADS_PER_STEP = 16
NAT_TILE_ROWS = Q_TILE // GRID_W
NAT_KEY_ROWS = NAT_WIN_R + NAT_TILE_ROWS


def _params(*sem):
    return pltpu.CompilerParams(dimension_semantics=sem, vmem_limit_bytes=VMEM_LIMIT)


def _full(shape):
    nd = len(shape)
    return pl.BlockSpec(shape, lambda *_: (0,) * nd)


def _row_chains(body, row_refs, tile, sub):
    def kernel(*refs):
        for c in range(tile // sub):
            rows = slice(c * sub, (c + 1) * sub)
            body(*[r.at[rows] if is_row else r for r, is_row in zip(refs, row_refs)])
    return kernel


def _rms(x, g):
    return x * lax.rsqrt(jnp.mean(x * x, axis=-1, keepdims=True) + NORM_EPS) * g


def _mod(m, i):
    return m[:, i * D_MODEL:(i + 1) * D_MODEL]


def _premod(x, g, m, sub):
    return _rms(x, g) * (1.0 + _mod(m, 3 * sub + 1)) + _mod(m, 3 * sub)


def _rope(x, cos, sin_signed, quarter):
    n = x.shape[-1]
    lane = lax.broadcasted_iota(jnp.int32, x.shape, 1)
    first = ((lane // quarter) % 2) == 0
    partner = jnp.where(first, pltpu.roll(x, n - quarter, 1), pltpu.roll(x, quarter, 1))
    return x * cos + partner * sin_signed


def _dot(a, b):
    return jnp.dot(a, b, preferred_element_type=F32)


def _dot_nt(a, b):
    return lax.dot_general(a, b, (((1,), (1,)), ((), ())), preferred_element_type=F32)


def _with_ones(v):
    return jnp.concatenate([v, jnp.ones_like(v)], axis=1)


def _attend(q, segs, sink=None):
    logits = []
    for k, _, bias in segs:
        s = _dot_nt(q, k)
        logits.append(s if bias is None else s + bias)
    m = logits[0].max(axis=-1, keepdims=True)
    for s in logits[1:]:
        m = jnp.maximum(m, s.max(axis=-1, keepdims=True))
    if sink is not None:
        m = jnp.maximum(m, sink)
    acc = None
    for s, (_, v1, _) in zip(logits, segs):
        pv = _dot(jnp.exp2(s - m).astype(BF16), v1)
        acc = pv if acc is None else acc + pv
    den = acc[:, LANES:LANES + 1]
    if sink is not None:
        den = den + jnp.exp2(sink - m)
    return acc[:, :LANES] / den


def _attend_group(qs, ks, v1s, sink_logits=None):
    tq = qs[0].shape[0]
    s = jnp.concatenate([_dot_nt(q, k) for q, k in zip(qs, ks)], axis=0)
    if sink_logits is not None:
        s = jnp.concatenate([s, sink_logits], axis=1)
        zeros = jnp.zeros((LANES, LANES), BF16)
        tail = jnp.concatenate([zeros, jnp.ones_like(zeros)], axis=1)
        v1s = [jnp.concatenate([v1, tail], axis=0) for v1 in v1s]
    p = jnp.exp2(s - s.max(axis=-1, keepdims=True)).astype(BF16)
    acc = jnp.concatenate([_dot(p[i * tq:(i + 1) * tq], v1) for i, v1 in enumerate(v1s)], axis=0)
    return acc[:, :LANES] / acc[:, LANES:LANES + 1]


def _split_pair(q):
    low = lax.broadcasted_iota(jnp.int32, q.shape, 1) < (LANES // 2)
    zero = jnp.zeros_like(q)
    return [jnp.where(low, q, zero), jnp.where(low, zero, q)]


def _merge_pair(o0, o1):
    low = lax.broadcasted_iota(jnp.int32, o0.shape, 1) < (LANES // 2)
    return jnp.where(low, o0, o1).astype(BF16)


def _attend_pair(q, segs, sinks=None):
    outs = []
    for j, qj in enumerate(_split_pair(q)):
        segs_j = [(k, v1, None if bias is None else bias[j]) for k, v1, bias in segs]
        outs.append(_attend(qj, segs_j, None if sinks is None else sinks[j]))
    return _merge_pair(*outs)


def _mods_kernel(cond_ref, w_ref, b_ref, o_ref):
    cnd = cond_ref[...]
    act = cnd * jax.nn.sigmoid(cnd)
    o_ref[...] = _dot(act.astype(BF16), w_ref[...].astype(BF16)) + b_ref[...]


def _modulation(cond, ada_w, ada_b):
    tn = 1536
    n = N_MOD * D_MODEL
    return pl.pallas_call(
        _mods_kernel,
        grid=(DEPTH, n // tn),
        in_specs=[
            _full((COND_ROWS, D_MODEL)),
            pl.BlockSpec((None, D_MODEL, tn), lambda l, j: (l, 0, j)),
            pl.BlockSpec((None, 1, tn), lambda l, j: (l, 0, j)),
        ],
        out_specs=pl.BlockSpec((None, COND_ROWS, tn), lambda l, j: (l, 0, j)),
        out_shape=jax.ShapeDtypeStruct((DEPTH, COND_ROWS, n), F32),
        compiler_params=_params("arbitrary", "arbitrary"),
        name="adaln_mods",
    )(cond, ada_w, ada_b.reshape(DEPTH, 1, n))


def _proj_a_kernel(latent, x_ref, m_ref, g_ref, w_ref, qn_ref, kn_ref, *rest):
    if latent:
        cos_ref, sin_ref, q_ref, k_ref, v_ref = rest
    else:
        q_ref, k_ref, v_ref, ks_ref, vs_ref = rest
    hd = ATTN_HEAD_DIM
    h = _premod(x_ref[...], g_ref[0:1, :], m_ref[...], 0)
    z = _dot(h.astype(BF16), w_ref[...])
    for i in range(ATTN_HEADS + ATTN_KV_HEADS):
        is_q = i < ATTN_HEADS
        y = _rms(z[:, i * hd:(i + 1) * hd], qn_ref[...] if is_q else kn_ref[...])
        if not latent and not is_q:
            j = i - ATTN_HEADS
            ks_ref[:, j * hd:(j + 1) * hd] = y
        if latent:
            y = _rope(y, cos_ref[...], sin_ref[...], hd // 4)
        if is_q:
            q_ref[:, i * hd:(i + 1) * hd] = (y * (hd ** -0.5 * LOG2E)).astype(BF16)
        else:
            j = i - ATTN_HEADS
            k_ref[:, j * hd:(j + 1) * hd] = y.astype(BF16)
    v = z[:, (ATTN_HEADS + ATTN_KV_HEADS) * hd:]
    v_ref[...] = v.astype(BF16)
    if not latent:
        vs_ref[...] = v


def _proj_b_kernel(latent, x_ref, m_ref, g_ref, win_ref, qn_ref, kvn_ref, wuq_ref, wukv_ref, *rest):
    if latent:
        cos_ref, sin_ref, qn_o, qr_o, kn_o, v_o, kp_o = rest
    else:
        qn_o, qr_o, kn_o, v_o, kp_o, ckv_s, kpe_s = rest
    nq, nkv = MLA_Q_LORA, MLA_KV_LORA
    nope_w = MLA_HEADS * MLA_NOPE
    scale = MLA_SCALE * LOG2E
    h = _premod(x_ref[...], g_ref[0:1, :], m_ref[...], 0)
    z = _dot(h.astype(BF16), win_ref[...])
    cq = _rms(z[:, :nq], qn_ref[...])
    ckv = _rms(z[:, nq:nq + nkv], kvn_ref[...])
    kp = z[:, nq + nkv:]
    q = _dot(cq.astype(BF16), wuq_ref[...])
    kv = _dot(ckv.astype(BF16), wukv_ref[...])
    if not latent:
        ckv_s[...] = ckv
        kpe_s[...] = kp[:, :MLA_ROPE]
    qn_o[...] = (q[:, :nope_w] * scale).astype(BF16)
    for i in range(MLA_HEADS * MLA_ROPE // LANES):
        qr = q[:, nope_w + i * LANES:nope_w + (i + 1) * LANES]
        if latent:
            qr = _rope(qr, cos_ref[...], sin_ref[...], MLA_ROPE // 4)
        qr_o[:, i * LANES:(i + 1) * LANES] = (qr * scale).astype(BF16)
    if latent:
        kp = _rope(kp, cos_ref[...], sin_ref[...], MLA_ROPE // 4)
    kp_o[...] = kp.astype(BF16)
    kn_o[...] = kv[:, :nope_w].astype(BF16)
    v_o[...] = kv[:, nope_w:].astype(BF16)


def _proj_c_kernel(latent, x_ref, m_ref, g_ref, w_ref, *rest):
    if latent:
        cos_ref, sin_ref, q_o, kd_o, vd_o = rest
    else:
        q_o, kd_o, vd_o, ks_o, vs_o = rest
    qw = SWA_HEADS * SWA_HEAD_DIM
    kw = SWA_KV_HEADS * SWA_HEAD_DIM
    h = _premod(x_ref[...], g_ref[0:1, :], m_ref[...], 0)
    z = _dot(h.astype(BF16), w_ref[...])
    off = qw
    if not latent:
        ks_o[...] = z[:, qw:qw + kw]
        vs_o[...] = z[:, qw + kw:qw + 2 * kw]
        off = qw + 2 * kw
    scale = SWA_HEAD_DIM ** -0.5 * LOG2E
    for i in range(qw // LANES):
        y = z[:, i * LANES:(i + 1) * LANES]
        if latent:
            y = _rope(y, cos_ref[...], sin_ref[...], SWA_HEAD_DIM // 4)
        q_o[:, i * LANES:(i + 1) * LANES] = (y * scale).astype(BF16)
    for i in range(2 * kw // LANES):
        y = z[:, off + i * LANES:off + (i + 1) * LANES]
        if latent:
            y = _rope(y, cos_ref[...], sin_ref[...], SWA_HEAD_DIM // 4)
        kd_o[:, i * LANES:(i + 1) * LANES] = y.astype(BF16)
    vd_o[...] = z[:, off + 2 * kw:].astype(BF16)


def _proj_d_kernel(latent, x_ref, m_ref, g_ref, w_ref, *rest):
    if latent:
        q_o, k_o, v_o = rest
    else:
        q_o, k_o, v_o, ks_o, vs_o = rest
    hw = NAT_HEADS * NAT_HEAD_DIM
    h = _premod(x_ref[...], g_ref[0:1, :], m_ref[...], 0)
    z = _dot(h.astype(BF16), w_ref[...])
    q_o[...] = (z[:, :hw] * (NAT_HEAD_DIM ** -0.5 * LOG2E)).astype(BF16)
    k_o[...] = z[:, hw:2 * hw].astype(BF16)
    v_o[...] = z[:, 2 * hw:].astype(BF16)
    if not latent:
        ks_o[...] = z[:, hw:2 * hw]
        vs_o[...] = z[:, 2 * hw:]


def _run_proj(kernel, latent, x, mods_l, g, consts, tables, outs, name):
    bx, s, _ = x.shape
    tm = TOK_TILE
    row = (lambda b, j: (b, 0, 0)) if latent else (lambda b, j: (CTX_MOD_ROW, 0, 0))
    in_specs = [
        pl.BlockSpec((None, tm, D_MODEL), lambda b, j: (b, j, 0)),
        pl.BlockSpec((None, 1, N_MOD * D_MODEL), row),
        _full(g.shape),
    ] + [_full(c.shape) for c in consts]
    in_specs += [pl.BlockSpec((tm, LANES), lambda b, j: (j, 0)) for _ in tables]
    out_specs = [pl.BlockSpec((None, tm, w), lambda b, j: (b, j, 0)) for w, _ in outs]
    out_shape = [jax.ShapeDtypeStruct((bx, s, w), dt) for w, dt in outs]
    return pl.pallas_call(
        functools.partial(kernel, latent),
        grid=(bx, s // tm),
        in_specs=in_specs,
        out_specs=out_specs,
        out_shape=out_shape,
        compiler_params=_params("arbitrary", "arbitrary"),
        name=name,
    )(x, mods_l, g, *consts, *tables)


def _attn_a_kernel(has_ctx, q_ref, k_ref, v_ref, *rest):
    if has_ctx:
        kc_ref, vc_ref, o_ref = rest
    else:
        (o_ref,) = rest
    group = ATTN_HEADS // ATTN_KV_HEADS
    head_cols = [slice(h * LANES, (h + 1) * LANES) for h in range(ATTN_HEADS)]
    kv_cols = [slice(j * LANES, (j + 1) * LANES) for j in range(ATTN_KV_HEADS)]
    if not has_ctx:
        tq = q_ref.shape[0]
        qs = [jnp.concatenate([q_ref[:, c] for c in head_cols[j * group:(j + 1) * group]], axis=0)
              for j in range(ATTN_KV_HEADS)]
        out = _attend_group(qs, [k_ref[:, c] for c in kv_cols], [_with_ones(v_ref[:, c]) for c in kv_cols])
        for h, c in enumerate(head_cols):
            o_ref[:, c] = out[h * tq:(h + 1) * tq].astype(BF16)
        return
    for j, kc in enumerate(kv_cols):
        segs = [(k_ref[:, kc], _with_ones(v_ref[:, kc]), None),
                (kc_ref[:, kc], _with_ones(vc_ref[:, kc]), None)]
        for c in head_cols[j * group:(j + 1) * group]:
            o_ref[:, c] = _attend(q_ref[:, c], segs).astype(BF16)


def _attention_a(q, k, v, kc=None, vc=None):
    b, s, qw = q.shape
    kl, kw = k.shape[1:]
    tq = min(Q_TILE, s)
    has_ctx = kc is not None
    in_specs = [
        pl.BlockSpec((None, tq, qw), lambda bi, t: (bi, t, 0)),
        pl.BlockSpec((None, kl, kw), lambda bi, t: (bi, 0, 0)),
        pl.BlockSpec((None, kl, kw), lambda bi, t: (bi, 0, 0)),
    ]
    args = [q, k, v]
    if has_ctx:
        cl = kc.shape[1]
        in_specs += [pl.BlockSpec((None, cl, kw), lambda bi, t: (bi, 0, 0))] * 2
        args += [kc, vc]
    return pl.pallas_call(
        functools.partial(_attn_a_kernel, has_ctx),
        grid=(b, s // tq),
        in_specs=in_specs,
        out_specs=pl.BlockSpec((None, tq, qw), lambda bi, t: (bi, t, 0)),
        out_shape=jax.ShapeDtypeStruct(q.shape, BF16),
        compiler_params=_params("arbitrary", "arbitrary"),
        name="attn_gqa_latent" if has_ctx else "attn_gqa_context",
    )(*args)


def _attn_b_kernel(has_ctx, qn_ref, qr_ref, kn_ref, kp_ref, v_ref, *rest):
    if has_ctx:
        knc_ref, kpc_ref, vc_ref, o_ref = rest
    else:
        (o_ref,) = rest
    tq = qn_ref.shape[0]
    lane = lax.broadcasted_iota(jnp.int32, (tq, LANES), 1)
    low = lane < (LANES // 2)
    quarter = lane // MLA_ROPE
    zero = jnp.zeros((tq, LANES), BF16)
    kp = kp_ref[...]
    kpc = kpc_ref[...] if has_ctx else None
    pairs_per_rope = LANES // MLA_ROPE // 2
    n_pairs = HEADS_PER_STEP // 2
    pair_cols = [slice(p * LANES, (p + 1) * LANES) for p in range(n_pairs)]

    def pair_lhs(p):
        qn = qn_ref[:, pair_cols[p]]
        rg = p // pairs_per_rope
        qr = qr_ref[:, rg * LANES:(rg + 1) * LANES]
        return [jnp.concatenate([jnp.where(low if j == 0 else ~low, qn, zero),
                                 jnp.where(quarter == 2 * (p % pairs_per_rope) + j, qr, zero)], axis=1)
                for j in range(2)]

    if not has_ctx:
        qs = [jnp.concatenate(pair_lhs(p), axis=0) for p in range(n_pairs)]
        out = _attend_group(qs, [jnp.concatenate([kn_ref[:, c], kp], axis=1) for c in pair_cols],
                            [_with_ones(v_ref[:, c]) for c in pair_cols])
        for p, c in enumerate(pair_cols):
            o_ref[:, c] = _merge_pair(out[2 * p * tq:(2 * p + 1) * tq], out[(2 * p + 1) * tq:(2 * p + 2) * tq])
        return
    for p, cols in enumerate(pair_cols):
        segs = [(jnp.concatenate([kn_ref[:, cols], kp], axis=1), _with_ones(v_ref[:, cols]), None),
                (jnp.concatenate([knc_ref[:, cols], kpc], axis=1), _with_ones(vc_ref[:, cols]), None)]
        o_ref[:, cols] = _merge_pair(*[_attend(lhs, segs) for lhs in pair_lhs(p)])


def _attention_b(qn, qr, kn, kp, v, knc=None, kpc=None, vc=None):
    b, s, _ = qn.shape
    kl = kn.shape[1]
    tq = min(Q_TILE, s)
    has_ctx = knc is not None
    gw = HEADS_PER_STEP * MLA_NOPE
    blk = lambda rows, w, fn: pl.BlockSpec((None, rows, w), fn)
    in_specs = [
        blk(tq, gw, lambda bi, g, t: (bi, t, g)),
        blk(tq, HEADS_PER_STEP * MLA_ROPE, lambda bi, g, t: (bi, t, g)),
        blk(kl, gw, lambda bi, g, t: (bi, 0, g)),
        blk(kl, LANES, lambda bi, g, t: (bi, 0, 0)),
        blk(kl, gw, lambda bi, g, t: (bi, 0, g)),
    ]
    args = [qn, qr, kn, kp, v]
    if has_ctx:
        cl = knc.shape[1]
        in_specs += [
            blk(cl, gw, lambda bi, g, t: (bi, 0, g)),
            blk(cl, LANES, lambda bi, g, t: (bi, 0, 0)),
            blk(cl, gw, lambda bi, g, t: (bi, 0, g)),
        ]
        args += [knc, kpc, vc]
    return pl.pallas_call(
        functools.partial(_attn_b_kernel, has_ctx),
        grid=(b, MLA_HEADS // HEADS_PER_STEP, s // tq),
        in_specs=in_specs,
        out_specs=blk(tq, gw, lambda bi, g, t: (bi, t, g)),
        out_shape=jax.ShapeDtypeStruct(qn.shape, BF16),
        compiler_params=_params("arbitrary", "arbitrary", "arbitrary"),
        name="attn_mla_latent" if has_ctx else "attn_mla_context",
    )(*args)


def _attn_c_kernel(latent, sink_ref, q_ref, kd_ref, vd_ref, *rest):
    if latent:
        kc_ref, vc_ref, o_ref = rest
    else:
        (o_ref,) = rest
    tq = q_ref.shape[0]
    group = SWA_HEADS // SWA_KV_HEADS
    kv_per_step = HEADS_PER_STEP // group
    head0 = pl.program_id(1) * HEADS_PER_STEP
    if latent:
        s_len = kd_ref.shape[0]
        win = tq + 2 * SWA_WINDOW
        t0 = pl.program_id(2) * tq
        ws = pl.multiple_of(jnp.clip(t0 - SWA_WINDOW, 0, s_len - win), LANES)
        qpos = t0 + lax.broadcasted_iota(jnp.int32, (tq, win), 0)
        kpos = ws + lax.broadcasted_iota(jnp.int32, (tq, win), 1)
        bias = jnp.where(jnp.abs(qpos - kpos) <= SWA_WINDOW, 0.0, NEG_INF).astype(F32)
    pairs_per_kv = group // 2
    pair_cols = [slice(p * LANES, (p + 1) * LANES) for p in range(HEADS_PER_STEP // 2)]
    kv_cols = [slice(j * LANES, (j + 1) * LANES) for j in range(kv_per_step)]
    if not latent:
        qs = [jnp.concatenate([h for c in pair_cols[j * pairs_per_kv:(j + 1) * pairs_per_kv]
                               for h in _split_pair(q_ref[:, c])], axis=0) for j in range(kv_per_step)]
        out = _attend_group(qs, [kd_ref[:, c] for c in kv_cols], [_with_ones(vd_ref[:, c]) for c in kv_cols],
                            sink_ref[...])
        for p, c in enumerate(pair_cols):
            o_ref[:, c] = _merge_pair(out[2 * p * tq:(2 * p + 1) * tq], out[(2 * p + 1) * tq:(2 * p + 2) * tq])
        return
    sinks = [sink_ref[head0 + h] * LOG2E for h in range(HEADS_PER_STEP)]
    for j, kc in enumerate(kv_cols):
        segs = [(kd_ref[pl.ds(ws, win), kc], _with_ones(vd_ref[pl.ds(ws, win), kc]), (bias, bias)),
                (kc_ref[:, kc], _with_ones(vc_ref[:, kc]), None)]
        for p in range(j * pairs_per_kv, (j + 1) * pairs_per_kv):
            o_ref[:, pair_cols[p]] = _attend_pair(q_ref[:, pair_cols[p]], segs, sinks[2 * p:2 * p + 2])


def _attention_c(sink, q, kd, vd, kc=None, vc=None):
    b, s, _ = q.shape
    kl = kd.shape[1]
    tq = min(Q_TILE, s)
    latent = kc is not None
    gw = HEADS_PER_STEP * SWA_HEAD_DIM
    kvw = 2 * SWA_HEAD_DIM * HEADS_PER_STEP // (SWA_HEADS // SWA_KV_HEADS)
    if latent:
        sink_spec = pl.BlockSpec(memory_space=pltpu.SMEM)
    else:
        rows = jnp.repeat(sink * LOG2E, tq).reshape(SWA_HEADS // HEADS_PER_STEP, HEADS_PER_STEP * tq, 1)
        lane0 = jnp.arange(LANES)[None, None, :] == 0
        sink = jnp.where(lane0, rows, NEG_INF)
        sink_spec = pl.BlockSpec((None, HEADS_PER_STEP * tq, LANES), lambda bi, g, t: (g, 0, 0))
    in_specs = [
        sink_spec,
        pl.BlockSpec((None, tq, gw), lambda bi, g, t: (bi, t, g)),
        pl.BlockSpec((None, kl, kvw), lambda bi, g, t: (bi, 0, g)),
        pl.BlockSpec((None, kl, kvw), lambda bi, g, t: (bi, 0, g)),
    ]
    args = [sink, q, kd, vd]
    if latent:
        cl = kc.shape[1]
        in_specs += [pl.BlockSpec((None, cl, kvw), lambda bi, g, t: (bi, 0, g))] * 2
        args += [kc, vc]
    return pl.pallas_call(
        functools.partial(_attn_c_kernel, latent),
        grid=(b, SWA_HEADS // HEADS_PER_STEP, s // tq),
        in_specs=in_specs,
        out_specs=pl.BlockSpec((None, tq, gw), lambda bi, g, t: (bi, t, g)),
        out_shape=jax.ShapeDtypeStruct(q.shape, BF16),
        compiler_params=_params("arbitrary", "arbitrary", "arbitrary"),
        name="attn_swa_latent" if latent else "attn_swa_context",
    )(*args)


def _attn_d_ctx_kernel(q_ref, k_ref, v_ref, o_ref):
    tq = q_ref.shape[0]
    pair_cols = [slice(p * LANES, (p + 1) * LANES) for p in range(HEADS_PER_STEP // 2)]
    qs = [jnp.concatenate(_split_pair(q_ref[:, c]), axis=0) for c in pair_cols]
    out = _attend_group(qs, [k_ref[:, c] for c in pair_cols], [_with_ones(v_ref[:, c]) for c in pair_cols])
    for p, c in enumerate(pair_cols):
        o_ref[:, c] = _merge_pair(out[2 * p * tq:(2 * p + 1) * tq], out[(2 * p + 1) * tq:(2 * p + 2) * tq])


def _attention_d_ctx(q, k, v):
    b, s, _ = q.shape
    gw = HEADS_PER_STEP * NAT_HEAD_DIM
    blk = pl.BlockSpec((None, s, gw), lambda bi, g: (bi, 0, g))
    return pl.pallas_call(
        _attn_d_ctx_kernel,
        grid=(b, NAT_HEADS // HEADS_PER_STEP),
        in_specs=[blk, blk, blk],
        out_specs=blk,
        out_shape=jax.ShapeDtypeStruct(q.shape, BF16),
        compiler_params=_params("arbitrary", "arbitrary"),
        name="attn_nat_context",
    )(q, k, v)


def _nat_window_start(first_row, rows):
    r0 = jnp.clip(first_row - NAT_WIN_R // 2, 0, rows - NAT_WIN_R)
    return jnp.minimum(r0, rows - NAT_KEY_ROWS)


def _attn_d_lat_kernel(q_ref, k_ref, v_ref, kc_ref, vc_ref, bias_ref, o_ref):
    rows = k_ref.shape[0] // GRID_W
    slab = NAT_KEY_ROWS * GRID_W
    start = _nat_window_start(pl.program_id(0) * NAT_TILE_ROWS, rows) * GRID_W
    start = pl.multiple_of(start, GRID_W)
    for p in range(HEADS_PER_STEP // 2):
        cols = slice(p * LANES, (p + 1) * LANES)
        segs = [(k_ref[pl.ds(start, slab), cols], _with_ones(v_ref[pl.ds(start, slab), cols]),
                 (bias_ref[2 * p], bias_ref[2 * p + 1])),
                (kc_ref[:, cols], _with_ones(vc_ref[:, cols]), None)]
        o_ref[:, cols] = _attend_pair(q_ref[:, cols], segs)


def _attention_d_lat(q, k, v, kc, vc, bias):
    b, s, _ = q.shape
    cl = kc.shape[1]
    tq = NAT_TILE_ROWS * GRID_W
    slab = NAT_KEY_ROWS * GRID_W
    gw = HEADS_PER_STEP * NAT_HEAD_DIM
    return pl.pallas_call(
        _attn_d_lat_kernel,
        grid=(s // tq, NAT_HEADS // HEADS_PER_STEP, b),
        in_specs=[
            pl.BlockSpec((None, tq, gw), lambda t, g, bi: (bi, t, g)),
            pl.BlockSpec((None, s, gw), lambda t, g, bi: (bi, 0, g)),
            pl.BlockSpec((None, s, gw), lambda t, g, bi: (bi, 0, g)),
            pl.BlockSpec((None, cl, gw), lambda t, g, bi: (bi, 0, g)),
            pl.BlockSpec((None, cl, gw), lambda t, g, bi: (bi, 0, g)),
            pl.BlockSpec((HEADS_PER_STEP, None, tq, slab), lambda t, g, bi: (g, t, 0, 0)),
        ],
        out_specs=pl.BlockSpec((None, tq, gw), lambda t, g, bi: (bi, t, g)),
        out_shape=jax.ShapeDtypeStruct(q.shape, BF16),
        compiler_params=_params("arbitrary", "arbitrary", "arbitrary"),
        name="attn_nat_latent",
    )(q, k, v, kc, vc, bias)


def _mla_expand_kernel(c_ref, w_ref, kn_ref, v_ref):
    kv = _dot(c_ref[...].astype(BF16), w_ref[...])
    half = kv.shape[1] // 2
    kn_ref[...] = kv[:, :half].astype(BF16)
    v_ref[...] = kv[:, half:].astype(BF16)


def _mla_expand(ckv, w_ukv):
    b, l, c = ckv.shape
    n = w_ukv.shape[1] // 2
    out = pl.BlockSpec((None, l, n), lambda bi: (bi, 0, 0))
    return pl.pallas_call(
        _mla_expand_kernel,
        grid=(b,),
        in_specs=[pl.BlockSpec((None, l, c), lambda bi: (bi, 0, 0)), _full(w_ukv.shape)],
        out_specs=[out, out],
        out_shape=[jax.ShapeDtypeStruct((b, l, n), BF16)] * 2,
        compiler_params=_params("arbitrary"),
        name="mla_expand_cache",
    )(ckv, w_ukv)


def _post_attn_kernel(o_ref, x_ref, m_ref, g_ref, wo_ref, x1_ref, h2_ref):
    m = m_ref[...]
    a = _dot(o_ref[...], wo_ref[...])
    x1 = x_ref[...] + _mod(m, 2) * _rms(a, g_ref[1:2, :])
    x1_ref[...] = x1
    h2_ref[...] = _premod(x1, g_ref[2:3, :], m, 1).astype(BF16)


def _post_attn(o, x, mods_l, g, wo, latent):
    bx, s, _ = x.shape
    tm = TOK_TILE
    row = (lambda b, j: (b, 0, 0)) if latent else (lambda b, j: (CTX_MOD_ROW, 0, 0))
    tok = lambda w: pl.BlockSpec((None, tm, w), lambda b, j: (b, j, 0))
    return pl.pallas_call(
        _row_chains(_post_attn_kernel, [True, True, False, False, False, True, True], tm, TOK_CHAIN),
        grid=(bx, s // tm),
        in_specs=[tok(o.shape[-1]), tok(D_MODEL),
                  pl.BlockSpec((None, 1, N_MOD * D_MODEL), row), _full(g.shape), _full(wo.shape)],
        out_specs=[tok(D_MODEL), tok(D_MODEL)],
        out_shape=[jax.ShapeDtypeStruct(x.shape, F32), jax.ShapeDtypeStruct(x.shape, BF16)],
        compiler_params=_params("arbitrary", "arbitrary"),
        name="out_proj_residual",
    )(o, x, mods_l, g, wo)


def _mlp_kernel(h_ref, x1_ref, m_ref, g_ref, w1_ref, w2_ref, o_ref):
    f = pl.program_id(2)

    @pl.when(f == 0)
    def _():
        o_ref[...] = jnp.zeros_like(o_ref)

    h = h_ref[...]
    for c in range(w1_ref.shape[1] // MLP_FF_CHAIN):
        cols = slice(c * MLP_FF_CHAIN, (c + 1) * MLP_FF_CHAIN)
        u = _dot(h, w1_ref[:, cols].astype(BF16))
        u = jnp.square(jnp.maximum(u, 0.0)).astype(BF16)
        o_ref[...] += _dot(u, w2_ref[cols, :].astype(BF16))

    @pl.when(f == pl.num_programs(2) - 1)
    def _():
        o_ref[...] = x1_ref[...] + _mod(m_ref[...], 5) * _rms(o_ref[...], g_ref[3:4, :])


def _mlp(h2, x1, mods_l, g, w1, w2, layer, latent):
    bx, s, _ = x1.shape
    tm, tf = MLP_TOK_TILE, MLP_FF_TILE
    row = (lambda b, j, f: (b, 0, 0)) if latent else (lambda b, j, f: (CTX_MOD_ROW, 0, 0))
    tok = pl.BlockSpec((None, tm, D_MODEL), lambda b, j, f: (b, j, 0))
    return pl.pallas_call(
        _mlp_kernel,
        grid=(bx, s // tm, D_FF // tf),
        in_specs=[tok, tok, pl.BlockSpec((None, 1, N_MOD * D_MODEL), row), _full(g.shape),
                  pl.BlockSpec((None, D_MODEL, tf), lambda b, j, f: (layer, 0, f)),
                  pl.BlockSpec((None, tf, D_MODEL), lambda b, j, f: (layer, f, 0))],
        out_specs=tok,
        out_shape=jax.ShapeDtypeStruct(x1.shape, F32),
        compiler_params=pltpu.CompilerParams(dimension_semantics=("arbitrary",) * 3,
                                             vmem_limit_bytes=MLP_VMEM_LIMIT),
        name="mlp_relu2",
    )(h2, x1, mods_l, g, w1, w2)


def _rope_tables(s, dim):
    quarter = dim // 4
    t = jnp.arange(s)
    pos = jnp.stack([t // GRID_W, t % GRID_W], axis=-1).astype(F32)
    inv = ROPE_THETA ** (-jnp.arange(quarter, dtype=F32) / quarter)
    ang = pos[:, :, None] * inv
    cos = jnp.broadcast_to(jnp.cos(ang)[:, :, None, :], (s, 2, 2, quarter)).reshape(s, dim)
    sign = jnp.array([-1.0, 1.0], F32)[None, None, :, None]
    sin = (jnp.sin(ang)[:, :, None, :] * sign).reshape(s, dim)
    reps = LANES // dim
    return jnp.tile(cos, (1, reps)), jnp.tile(sin, (1, reps))


def _dup_heads(w, heads, dim):
    lead = w.shape[:-1]
    w = w.reshape(lead + (heads, 1, dim))
    return jnp.broadcast_to(w, lead + (heads, 2, dim)).reshape(lead + (heads * 2 * dim,))


def _nat_dense_bias(rpb, rows):
    heads = rpb.shape[0]
    c = np.arange(GRID_W)
    c0 = np.clip(c - NAT_WIN_C // 2, 0, GRID_W - NAT_WIN_C)
    in_c = (c[None, :] >= c0[:, None]) & (c[None, :] < c0[:, None] + NAT_WIN_C)
    dc = c[None, :] - c[:, None] + NAT_WIN_C - 1
    onehot = (dc[None] == np.arange(2 * NAT_WIN_C - 1)[:, None, None]) & in_c[None]
    toe = jnp.einsum("had,dck->hack", rpb * LOG2E, jnp.asarray(onehot, F32),
                     precision=lax.Precision.HIGHEST)
    toe = jnp.where(jnp.asarray(in_c)[None, None], toe, NEG_INF)
    pad = jnp.full((heads, 1, GRID_W, GRID_W), NEG_INF, F32)
    ext = jnp.concatenate([pad, toe, pad], axis=1)
    pairs = jnp.concatenate([ext[:, :-1], ext[:, 1:]], axis=-1)
    tiles = rows // NAT_TILE_ROWS
    tq, slab = NAT_TILE_ROWS * GRID_W, NAT_KEY_ROWS * GRID_W
    n_off = 2 * NAT_WIN_R
    return pl.pallas_call(
        functools.partial(_nat_bias_kernel, rows),
        grid=(heads, tiles),
        in_specs=[pl.BlockSpec((None, n_off, GRID_W, LANES), lambda h, t: (h, 0, 0, 0))],
        out_specs=pl.BlockSpec((None, None, tq, slab), lambda h, t: (h, t, 0, 0)),
        out_shape=jax.ShapeDtypeStruct((heads, tiles, tq, slab), F32),
        compiler_params=_params("arbitrary", "arbitrary"),
        name="nat_bias_expand",
    )(pairs)


def _nat_bias_kernel(rows, pairs_ref, o_ref):
    first = pl.program_id(1) * NAT_TILE_ROWS
    ws = _nat_window_start(first, rows)
    lane = lax.broadcasted_iota(jnp.int32, (GRID_W, LANES), 1)
    for i in range(NAT_TILE_ROWS):
        r = first + i
        r0 = jnp.clip(r - NAT_WIN_R // 2, 0, rows - NAT_WIN_R)
        for jb in range(NAT_KEY_ROWS // 2):
            kr = ws + 2 * jb
            ok_lo = ((kr >= r0) & (kr < r0 + NAT_WIN_R)).astype(jnp.int32)
            ok_hi = ((kr + 1 >= r0) & (kr + 1 < r0 + NAT_WIN_R)).astype(jnp.int32)
            a = jnp.clip(kr - r + NAT_WIN_R, 0, 2 * NAT_WIN_R - 1)
            ok = jnp.where(lane < GRID_W, ok_lo, ok_hi) > 0
            o_ref[i * GRID_W:(i + 1) * GRID_W, jb * LANES:(jb + 1) * LANES] = jnp.where(ok, pairs_ref[a], NEG_INF)


def kernel(x_prompt, x_sample, cache_l0_k, cache_l0_v, cache_l1_ckv, cache_l1_kpe, cache_l2_k, cache_l2_v, cache_l3_k, cache_l3_v, c, c_ctx, ada_w, ada_b, norm_g, mlp_w1, mlp_w2, attn_w_qkv, attn_q_norm, attn_k_norm, attn_w_o, mla_w_in, mla_q_norm, mla_kv_norm, mla_w_uq, mla_w_ukv, mla_w_o, swa_w_qkv, swa_sink, swa_w_o, nat_w_qkv, nat_rpb, nat_w_o):
    nb, seq, d = x_prompt.shape
    db, dseq, _ = x_sample.shape
    past = cache_l0_k.shape[1]
    ctx_b = nb * seq // dseq
    xp = x_prompt.reshape(ctx_b, dseq, d)
    xs = x_sample

    cond = jnp.zeros((COND_ROWS, d), F32).at[:db].set(c).at[CTX_MOD_ROW].set(c_ctx)
    mods = _modulation(cond, ada_w, ada_b).reshape(DEPTH, COND_ROWS, 1, N_MOD * d)

    row = lambda v: v.reshape(1, -1)

    def as_ctx(a):
        return a.reshape(nb, seq, a.shape[-1])

    def as_slab(a):
        return a.reshape(ctx_b, dseq, a.shape[-1])

    def finish(o, x, layer, wo, latent):
        x1, h2 = _post_attn(o, x, mods[layer], norm_g[layer], wo, latent)
        return _mlp(h2, x1, mods[layer], norm_g[layer], mlp_w1, mlp_w2, layer, latent)

    g = norm_g[0]
    w = attn_w_qkv.astype(BF16)
    wo = attn_w_o.astype(BF16)
    consts = [w, row(attn_q_norm), row(attn_k_norm)]
    kvw = ATTN_KV_HEADS * ATTN_HEAD_DIM
    q, k, v, l0_k, l0_v = _run_proj(
        _proj_a_kernel, False, xp, mods[0], g, consts, [],
        [(d, BF16), (kvw, BF16), (kvw, BF16), (kvw, F32), (kvw, F32)], "proj_gqa_context")
    o = _attention_a(as_ctx(q), as_ctx(k), as_ctx(v))
    xp = finish(as_slab(o), xp, 0, wo, False)
    tables = list(_rope_tables(dseq, ATTN_HEAD_DIM))
    q, k, v = _run_proj(_proj_a_kernel, True, xs, mods[0], g, consts, tables,
                        [(d, BF16), (kvw, BF16), (kvw, BF16)], "proj_gqa_latent")
    o = _attention_a(q, k, v, cache_l0_k.reshape(db, past, kvw).astype(BF16),
                     cache_l0_v.reshape(db, past, kvw).astype(BF16))
    xs = finish(o, xs, 0, wo, True)
    new_l0 = (l0_k.reshape(nb, seq, ATTN_KV_HEADS, ATTN_HEAD_DIM),
              l0_v.reshape(nb, seq, ATTN_KV_HEADS, ATTN_HEAD_DIM))

    g = norm_g[1]
    nq, nkv = MLA_Q_LORA, MLA_KV_LORA
    w_in = jnp.concatenate([mla_w_in[:, :nq + nkv]] + [mla_w_in[:, nq + nkv:]] * (LANES // MLA_ROPE),
                           axis=1).astype(BF16)
    wuq = mla_w_uq.reshape(nq, MLA_HEADS, MLA_NOPE + MLA_ROPE)
    wuq = jnp.concatenate([wuq[:, :, :MLA_NOPE].reshape(nq, -1), wuq[:, :, MLA_NOPE:].reshape(nq, -1)],
                          axis=1).astype(BF16)
    wukv = mla_w_ukv.reshape(nkv, MLA_HEADS, MLA_NOPE + MLA_V_DIM)
    wukv = jnp.concatenate([wukv[:, :, :MLA_NOPE].reshape(nkv, -1), wukv[:, :, MLA_NOPE:].reshape(nkv, -1)],
                           axis=1).astype(BF16)
    wo = mla_w_o.astype(BF16)
    consts = [w_in, row(mla_q_norm), row(mla_kv_norm), wuq, wukv]
    hw = MLA_HEADS * MLA_NOPE
    rw = MLA_HEADS * MLA_ROPE
    outs = [(hw, BF16), (rw, BF16), (hw, BF16), (hw, BF16), (LANES, BF16)]
    qn, qr, kn, v, kp, l1_ckv, l1_kpe = _run_proj(
        _proj_b_kernel, False, xp, mods[1], g, consts, [],
        outs + [(nkv, F32), (MLA_ROPE, F32)], "proj_mla_context")
    o = _attention_b(as_ctx(qn), as_ctx(qr), as_ctx(kn), as_ctx(kp), as_ctx(v))
    xp = finish(as_slab(o), xp, 1, wo, False)
    tables = list(_rope_tables(dseq, MLA_ROPE))
    qn, qr, kn, v, kp = _run_proj(_proj_b_kernel, True, xs, mods[1], g, consts, tables, outs,
                                  "proj_mla_latent")
    knc, vc = _mla_expand(cache_l1_ckv, wukv)
    kpc = jnp.tile(cache_l1_kpe, (1, 1, LANES // MLA_ROPE)).astype(BF16)
    o = _attention_b(qn, qr, kn, kp, v, knc, kpc, vc)
    xs = finish(o, xs, 1, wo, True)
    new_l1 = (l1_ckv.reshape(nb, seq, nkv), l1_kpe.reshape(nb, seq, MLA_ROPE))

    g = norm_g[2]
    qw = SWA_HEADS * SWA_HEAD_DIM
    kw = SWA_KV_HEADS * SWA_HEAD_DIM
    wq, wk, wv = swa_w_qkv[:, :qw], swa_w_qkv[:, qw:qw + kw], swa_w_qkv[:, qw + kw:]
    wkd = _dup_heads(wk, SWA_KV_HEADS, SWA_HEAD_DIM)
    wvd = _dup_heads(wv, SWA_KV_HEADS, SWA_HEAD_DIM)
    w_ctx = jnp.concatenate([wq, wk, wv, wkd, wvd], axis=1).astype(BF16)
    w_lat = jnp.concatenate([wq, wkd, wvd], axis=1).astype(BF16)
    wo = swa_w_o.astype(BF16)
    q, kd, vd, l2_k, l2_v = _run_proj(
        _proj_c_kernel, False, xp, mods[2], g, [w_ctx], [],
        [(qw, BF16), (2 * kw, BF16), (2 * kw, BF16), (kw, F32), (kw, F32)], "proj_swa_context")
    o = _attention_c(swa_sink, as_ctx(q), as_ctx(kd), as_ctx(vd))
    xp = finish(as_slab(o), xp, 2, wo, False)
    tables = list(_rope_tables(dseq, SWA_HEAD_DIM))
    q, kd, vd = _run_proj(_proj_c_kernel, True, xs, mods[2], g, [w_lat], tables,
                          [(qw, BF16), (2 * kw, BF16), (2 * kw, BF16)], "proj_swa_latent")
    kc = _dup_heads(cache_l2_k.reshape(db, past, kw), SWA_KV_HEADS, SWA_HEAD_DIM).astype(BF16)
    vc = _dup_heads(cache_l2_v.reshape(db, past, kw), SWA_KV_HEADS, SWA_HEAD_DIM).astype(BF16)
    o = _attention_c(swa_sink, q, kd, vd, kc, vc)
    xs = finish(o, xs, 2, wo, True)
    new_l2 = (l2_k.reshape(nb, seq, SWA_KV_HEADS, SWA_HEAD_DIM),
              l2_v.reshape(nb, seq, SWA_KV_HEADS, SWA_HEAD_DIM))

    g = norm_g[3]
    hw = NAT_HEADS * NAT_HEAD_DIM
    w = nat_w_qkv.astype(BF16)
    wo = nat_w_o.astype(BF16)
    q, k, v, l3_k, l3_v = _run_proj(
        _proj_d_kernel, False, xp, mods[3], g, [w], [],
        [(hw, BF16), (hw, BF16), (hw, BF16), (hw, F32), (hw, F32)], "proj_nat_context")
    o = _attention_d_ctx(as_ctx(q), as_ctx(k), as_ctx(v))
    xp = finish(as_slab(o), xp, 3, wo, False)
    q, k, v = _run_proj(_proj_d_kernel, True, xs, mods[3], g, [w], [],
                        [(hw, BF16), (hw, BF16), (hw, BF16)], "proj_nat_latent")
    o = _attention_d_lat(q, k, v, cache_l3_k.reshape(db, past, hw).astype(BF16),
                         cache_l3_v.reshape(db, past, hw).astype(BF16),
                         _nat_dense_bias(nat_rpb, dseq // GRID_W))
    xs = finish(o, xs, 3, wo, True)
    new_l3 = (l3_k.reshape(nb, seq, NAT_HEADS, NAT_HEAD_DIM),
              l3_v.reshape(nb, seq, NAT_HEADS, NAT_HEAD_DIM))

    return (xp.reshape(nb, seq, d), xs) + new_l0 + new_l1 + new_l2 + new_l3
```

```python
import functools

import numpy as np

import jax
import jax.numpy as jnp
from jax import lax
from jax.experimental import pallas as pl
from jax.experimental.pallas import tpu as pltpu

F32 = jnp.float32
BF16 = jnp.bfloat16

D_MODEL = 1024
DEPTH = 4
N_MOD = 6
D_FF = 4 * D_MODEL
GRID_W = 64
ROPE_THETA = 10000.0
NORM_EPS = 1e-6
NEG_INF = -1e30
LOG2E = 1.4426950408889634

ATTN_HEADS, ATTN_KV_HEADS, ATTN_HEAD_DIM = 8, 2, 128
MLA_HEADS, MLA_Q_LORA, MLA_KV_LORA = 16, 384, 256
MLA_NOPE, MLA_ROPE, MLA_V_DIM = 64, 32, 64
MLA_SCALE = (MLA_NOPE + MLA_ROPE) ** -0.5
SWA_HEADS, SWA_KV_HEADS, SWA_HEAD_DIM, SWA_WINDOW = 16, 4, 64, 128
NAT_HEADS, NAT_HEAD_DIM, NAT_WIN_R, NAT_WIN_C = 16, 64, 8, 16

LANES = 128
COND_ROWS = 16
CTX_MOD_ROW = 8
VMEM_LIMIT = 48 * 1024 * 1024

TOK_TILE = 512
TOK_CHAIN = 256
MLP_TOK_TILE = 1024
MLP_FF_TILE = 1024
MLP_FF_CHAIN = 512
MLP_VMEM_LIMIT = 56 * 1024 * 1024
Q_TILE = 256
HEADS_PER_STEP = 16
NAT_TILE_ROWS = Q_TILE // GRID_W
NAT_KEY_ROWS = NAT_WIN_R + NAT_TILE_ROWS


def _params(*sem):
    return pltpu.CompilerParams(dimension_semantics=sem, vmem_limit_bytes=VMEM_LIMIT)


def _full(shape):
    nd = len(shape)
    return pl.BlockSpec(shape, lambda *_: (0,) * nd)


def _row_chains(body, row_refs, tile, sub):
    def kernel(*refs):
        for c in range(tile // sub):
            rows = slice(c * sub, (c + 1) * sub)
            body(*[r.at[rows] if is_row else r for r, is_row in zip(refs, row_refs)])
    return kernel


def _rms(x, g):
    return x * lax.rsqrt(jnp.mean(x * x, axis=-1, keepdims=True) + NORM_EPS) * g


def _mod(m, i):
    return m[:, i * D_MODEL:(i + 1) * D_MODEL]


def _premod(x, g, m, sub):
    return _rms(x, g * (1.0 + _mod(m, 3 * sub + 1))) + _mod(m, 3 * sub)


def _rope(x, cos, sin_signed, quarter):
    n = x.shape[-1]
    lane = lax.broadcasted_iota(jnp.int32, x.shape, 1)
    first = ((lane // quarter) % 2) == 0
    partner = jnp.where(first, pltpu.roll(x, n - quarter, 1), pltpu.roll(x, quarter, 1))
    return x * cos + partner * sin_signed


def _dot(a, b):
    return jnp.dot(a, b, preferred_element_type=F32)


def _dot_nt(a, b):
    return lax.dot_general(a, b, (((1,), (1,)), ((), ())), preferred_element_type=F32)


def _with_ones(v):
    return jnp.concatenate([v, jnp.ones_like(v)], axis=1)


def _attend(q, segs, sink=None):
    logits = []
    for seg in segs:
        k, bias, transposed = seg[0], seg[2], len(seg) > 3 and seg[3]
        s = _dot(q, k) if transposed else _dot_nt(q, k)
        logits.append(s if bias is None else s + bias)
    m = logits[0].max(axis=-1, keepdims=True)
    for s in logits[1:]:
        m = jnp.maximum(m, s.max(axis=-1, keepdims=True))
    if sink is not None:
        m = jnp.maximum(m, sink)
    acc = None
    for s, seg in zip(logits, segs):
        p = jnp.exp2(s - m).astype(BF16)
        pv = _dot_nt(p, seg[1]) if len(seg) > 3 and seg[3] else _dot(p, seg[1])
        acc = pv if acc is None else acc + pv
    den = acc[:, LANES:LANES + 1]
    if sink is not None:
        den = den + jnp.exp2(sink - m)
    return acc[:, :LANES] / den


def _attend_group(qs, ks, v1s, sink_logits=None):
    tq = qs[0].shape[0]
    s = jnp.concatenate([_dot_nt(q, k) for q, k in zip(qs, ks)], axis=0)
    if sink_logits is not None:
        s = jnp.concatenate([s, sink_logits], axis=1)
        zeros = jnp.zeros((LANES, LANES), BF16)
        tail = jnp.concatenate([zeros, jnp.ones_like(zeros)], axis=1)
        v1s = [jnp.concatenate([v1, tail], axis=0) for v1 in v1s]
    p = jnp.exp2(s - s.max(axis=-1, keepdims=True)).astype(BF16)
    acc = jnp.concatenate([_dot(p[i * tq:(i + 1) * tq], v1) for i, v1 in enumerate(v1s)], axis=0)
    return acc[:, :LANES] / acc[:, LANES:LANES + 1]


def _split_pair(q):
    low = lax.broadcasted_iota(jnp.int32, q.shape, 1) < (LANES // 2)
    zero = jnp.zeros_like(q)
    return [jnp.where(low, q, zero), jnp.where(low, zero, q)]


def _merge_pair(o0, o1):
    low = lax.broadcasted_iota(jnp.int32, o0.shape, 1) < (LANES // 2)
    return jnp.where(low, o0, o1).astype(BF16)


def _attend_pair(q, segs, sinks=None):
    outs = []
    for j, qj in enumerate(_split_pair(q)):
        segs_j = [(s[0], s[1], None if s[2] is None else s[2][j]) + tuple(s[3:]) for s in segs]
        outs.append(_attend(qj, segs_j, None if sinks is None else sinks[j]))
    return _merge_pair(*outs)


def _mods_kernel(cond_ref, w_ref, b_ref, o_ref):
    cnd = cond_ref[...]
    act = cnd * jax.nn.sigmoid(cnd)
    o_ref[...] = _dot(act.astype(BF16), w_ref[...].astype(BF16)) + b_ref[...]


def _modulation(cond, ada_w, ada_b):
    tn = 1536
    n = N_MOD * D_MODEL
    return pl.pallas_call(
        _mods_kernel,
        grid=(DEPTH, n // tn),
        in_specs=[
            _full((COND_ROWS, D_MODEL)),
            pl.BlockSpec((None, D_MODEL, tn), lambda l, j: (l, 0, j)),
            pl.BlockSpec((None, 1, tn), lambda l, j: (l, 0, j)),
        ],
        out_specs=pl.BlockSpec((None, COND_ROWS, tn), lambda l, j: (l, 0, j)),
        out_shape=jax.ShapeDtypeStruct((DEPTH, COND_ROWS, n), F32),
        compiler_params=_params("arbitrary", "arbitrary"),
        name="adaln_mods",
    )(cond, ada_w, ada_b.reshape(DEPTH, 1, n))


def _proj_a_kernel(latent, x_ref, m_ref, g_ref, w_ref, qn_ref, kn_ref, *rest):
    if latent:
        cos_ref, sin_ref, q_ref, k_ref, v_ref = rest
    else:
        q_ref, k_ref, v_ref, ks_ref, vs_ref = rest
    hd = ATTN_HEAD_DIM
    h = _premod(x_ref[...], g_ref[0:1, :], m_ref[...], 0)
    z = _dot(h.astype(BF16), w_ref[...])
    for i in range(ATTN_HEADS + ATTN_KV_HEADS):
        is_q = i < ATTN_HEADS
        y = _rms(z[:, i * hd:(i + 1) * hd], qn_ref[...] if is_q else kn_ref[...])
        if not latent and not is_q:
            j = i - ATTN_HEADS
            ks_ref[:, j * hd:(j + 1) * hd] = y
        if latent:
            y = _rope(y, cos_ref[...], sin_ref[...], hd // 4)
        if is_q:
            q_ref[:, i * hd:(i + 1) * hd] = (y * (hd ** -0.5 * LOG2E)).astype(BF16)
        else:
            j = i - ATTN_HEADS
            k_ref[:, j * hd:(j + 1) * hd] = y.astype(BF16)
    v = z[:, (ATTN_HEADS + ATTN_KV_HEADS) * hd:]
    v_ref[...] = v.astype(BF16)
    if not latent:
        vs_ref[...] = v


def _proj_b_kernel(latent, x_ref, m_ref, g_ref, win_ref, qn_ref, kvn_ref, wuq_ref, wukv_ref, *rest):
    if latent:
        cos_ref, sin_ref, qn_o, qr_o, kn_o, v_o, kp_o = rest
    else:
        qn_o, qr_o, kn_o, v_o, kp_o, ckv_s, kpe_s = rest
    nq, nkv = MLA_Q_LORA, MLA_KV_LORA
    nope_w = MLA_HEADS * MLA_NOPE
    scale = MLA_SCALE * LOG2E
    h = _premod(x_ref[...], g_ref[0:1, :], m_ref[...], 0)
    z = _dot(h.astype(BF16), win_ref[...])
    cq = _rms(z[:, :nq], qn_ref[...])
    ckv = _rms(z[:, nq:nq + nkv], kvn_ref[...])
    kp = z[:, nq + nkv:]
    q = _dot(cq.astype(BF16), wuq_ref[...])
    kv = _dot(ckv.astype(BF16), wukv_ref[...])
    if not latent:
        ckv_s[...] = ckv
        _store_transposed(kpe_s, kp, MLA_ROPE)
    qn_o[...] = (q[:, :nope_w] * scale).astype(BF16)
    for i in range(MLA_HEADS * MLA_ROPE // LANES):
        qr = q[:, nope_w + i * LANES:nope_w + (i + 1) * LANES]
        if latent:
            qr = _rope(qr, cos_ref[...], sin_ref[...], MLA_ROPE // 4)
        qr_o[:, i * LANES:(i + 1) * LANES] = (qr * scale).astype(BF16)
    if latent:
        kp = _rope(kp, cos_ref[...], sin_ref[...], MLA_ROPE // 4)
    kp_o[...] = kp.astype(BF16)
    kn_o[...] = kv[:, :nope_w].astype(BF16)
    v_o[...] = kv[:, nope_w:].astype(BF16)


def _proj_c_kernel(latent, x_ref, m_ref, g_ref, w_ref, *rest):
    if latent:
        cos_ref, sin_ref, q_o, kd_o, vd_o = rest
    else:
        q_o, kd_o, vd_o, ks_o, vs_o = rest
    qw = SWA_HEADS * SWA_HEAD_DIM
    kw = SWA_KV_HEADS * SWA_HEAD_DIM
    h = _premod(x_ref[...], g_ref[0:1, :], m_ref[...], 0)
    z = _dot(h.astype(BF16), w_ref[...])
    off = qw
    if not latent:
        _store_transposed(ks_o, z[:, qw:qw + kw], kw)
        _store_transposed(vs_o, z[:, qw + kw:qw + 2 * kw], kw)
        off = qw + 2 * kw
    scale = SWA_HEAD_DIM ** -0.5 * LOG2E
    for i in range(qw // LANES):
        y = z[:, i * LANES:(i + 1) * LANES]
        if latent:
            y = _rope(y, cos_ref[...], sin_ref[...], SWA_HEAD_DIM // 4)
        q_o[:, i * LANES:(i + 1) * LANES] = (y * scale).astype(BF16)
    for i in range(2 * kw // LANES):
        y = z[:, off + i * LANES:off + (i + 1) * LANES]
        if latent:
            y = _rope(y, cos_ref[...], sin_ref[...], SWA_HEAD_DIM // 4)
        kd_o[:, i * LANES:(i + 1) * LANES] = y.astype(BF16)
    vd_o[...] = z[:, off + 2 * kw:].astype(BF16)


def _proj_d_kernel(latent, x_ref, m_ref, g_ref, w_ref, *rest):
    if latent:
        q_o, k_o, v_o = rest
    else:
        q_o, k_o, v_o, ks_o, vs_o = rest
    hw = NAT_HEADS * NAT_HEAD_DIM
    h = _premod(x_ref[...], g_ref[0:1, :], m_ref[...], 0)
    z = _dot(h.astype(BF16), w_ref[...])
    q_o[...] = (z[:, :hw] * (NAT_HEAD_DIM ** -0.5 * LOG2E)).astype(BF16)
    k_o[...] = z[:, hw:2 * hw].astype(BF16)
    v_o[...] = z[:, 2 * hw:].astype(BF16)
    if not latent:
        _store_transposed(ks_o, z[:, hw:2 * hw], hw)
        _store_transposed(vs_o, z[:, 2 * hw:], hw)


def _store_transposed(ref, val, width):
    n, _, seq = ref.shape
    for i in range(n):
        ref[i] = val[i * seq:(i + 1) * seq, :].T[:width, :]


def _run_proj(kernel, latent, x, mods_l, g, consts, tables, outs, name, state_seq=None):
    bx, s, _ = x.shape
    tm = TOK_TILE
    row = (lambda b, j: (b, 0, 0)) if latent else (lambda b, j: (CTX_MOD_ROW, 0, 0))
    in_specs = [
        pl.BlockSpec((None, tm, D_MODEL), lambda b, j: (b, j, 0)),
        pl.BlockSpec((None, 1, N_MOD * D_MODEL), row),
        _full(g.shape),
    ] + [_full(c.shape) for c in consts]
    in_specs += [pl.BlockSpec((tm, LANES), lambda b, j: (j, 0)) for _ in tables]
    out_specs, out_shape = [], []
    for o in outs:
        if len(o) == 2:
            out_specs.append(pl.BlockSpec((None, tm, o[0]), lambda b, j: (b, j, 0)))
            out_shape.append(jax.ShapeDtypeStruct((bx, s, o[0]), o[1]))
        else:
            per_tile = tm // state_seq
            out_specs.append(pl.BlockSpec((per_tile, o[0], state_seq),
                                          lambda b, j: (b * (s // tm) + j, 0, 0)))
            out_shape.append(jax.ShapeDtypeStruct((bx * s // state_seq, o[0], state_seq), o[1]))
    return pl.pallas_call(
        functools.partial(kernel, latent),
        grid=(bx, s // tm),
        in_specs=in_specs,
        out_specs=out_specs,
        out_shape=out_shape,
        compiler_params=_params("arbitrary", "arbitrary"),
        name=name,
    )(x, mods_l, g, *consts, *tables)


def _attn_a_kernel(has_ctx, q_ref, k_ref, v_ref, *rest):
    if has_ctx:
        kc_ref, vc_ref, o_ref = rest
    else:
        (o_ref,) = rest
    group = ATTN_HEADS // ATTN_KV_HEADS
    head_cols = [slice(h * LANES, (h + 1) * LANES) for h in range(ATTN_HEADS)]
    kv_cols = [slice(j * LANES, (j + 1) * LANES) for j in range(ATTN_KV_HEADS)]
    if not has_ctx:
        tq = q_ref.shape[0]
        qs = [jnp.concatenate([q_ref[:, c] for c in head_cols[j * group:(j + 1) * group]], axis=0)
              for j in range(ATTN_KV_HEADS)]
        out = _attend_group(qs, [k_ref[:, c] for c in kv_cols], [_with_ones(v_ref[:, c]) for c in kv_cols])
        for h, c in enumerate(head_cols):
            o_ref[:, c] = out[h * tq:(h + 1) * tq].astype(BF16)
        return
    for j, kc in enumerate(kv_cols):
        segs = [(k_ref[:, kc], _with_ones(v_ref[:, kc]), None),
                (kc_ref[:, kc], _with_ones(vc_ref[:, kc]), None)]
        for c in head_cols[j * group:(j + 1) * group]:
            o_ref[:, c] = _attend(q_ref[:, c], segs).astype(BF16)


def _attention_a(q, k, v, kc=None, vc=None):
    b, s, qw = q.shape
    kl, kw = k.shape[1:]
    tq = min(Q_TILE, s)
    has_ctx = kc is not None
    in_specs = [
        pl.BlockSpec((None, tq, qw), lambda bi, t: (bi, t, 0)),
        pl.BlockSpec((None, kl, kw), lambda bi, t: (bi, 0, 0)),
        pl.BlockSpec((None, kl, kw), lambda bi, t: (bi, 0, 0)),
    ]
    args = [q, k, v]
    if has_ctx:
        cl = kc.shape[1]
        in_specs += [pl.BlockSpec((None, cl, kw), lambda bi, t: (bi, 0, 0))] * 2
        args += [kc, vc]
    return pl.pallas_call(
        functools.partial(_attn_a_kernel, has_ctx),
        grid=(b, s // tq),
        in_specs=in_specs,
        out_specs=pl.BlockSpec((None, tq, qw), lambda bi, t: (bi, t, 0)),
        out_shape=jax.ShapeDtypeStruct(q.shape, BF16),
        compiler_params=_params("arbitrary", "arbitrary"),
        name="attn_gqa_latent" if has_ctx else "attn_gqa_context",
    )(*args)


def _attn_b_kernel(has_ctx, qn_ref, qr_ref, kn_ref, kp_ref, v_ref, *rest):
    if has_ctx:
        knc_ref, kpc_ref, vc_ref, o_ref = rest
    else:
        (o_ref,) = rest
    tq = qn_ref.shape[0]
    lane = lax.broadcasted_iota(jnp.int32, (tq, LANES), 1)
    low = lane < (LANES // 2)
    quarter = lane // MLA_ROPE
    zero = jnp.zeros((tq, LANES), BF16)
    kp = kp_ref[...]
    kpc = kpc_ref[...] if has_ctx else None
    pairs_per_rope = LANES // MLA_ROPE // 2
    n_pairs = HEADS_PER_STEP // 2
    pair_cols = [slice(p * LANES, (p + 1) * LANES) for p in range(n_pairs)]

    def pair_lhs(p):
        qn = qn_ref[:, pair_cols[p]]
        rg = p // pairs_per_rope
        qr = qr_ref[:, rg * LANES:(rg + 1) * LANES]
        return [jnp.concatenate([jnp.where(low if j == 0 else ~low, qn, zero),
                                 jnp.where(quarter == 2 * (p % pairs_per_rope) + j, qr, zero)], axis=1)
                for j in range(2)]

    if not has_ctx:
        qs = [jnp.concatenate(pair_lhs(p), axis=0) for p in range(n_pairs)]
        out = _attend_group(qs, [jnp.concatenate([kn_ref[:, c], kp], axis=1) for c in pair_cols],
                            [_with_ones(v_ref[:, c]) for c in pair_cols])
        for p, c in enumerate(pair_cols):
            o_ref[:, c] = _merge_pair(out[2 * p * tq:(2 * p + 1) * tq], out[(2 * p + 1) * tq:(2 * p + 2) * tq])
        return
    for p, cols in enumerate(pair_cols):
        segs = [(jnp.concatenate([kn_ref[:, cols], kp], axis=1), _with_ones(v_ref[:, cols]), None),
                (jnp.concatenate([knc_ref[:, cols], kpc], axis=1), _with_ones(vc_ref[:, cols]), None)]
        o_ref[:, cols] = _merge_pair(*[_attend(lhs, segs) for lhs in pair_lhs(p)])


def _attention_b(qn, qr, kn, kp, v, knc=None, kpc=None, vc=None):
    b, s, _ = qn.shape
    kl = kn.shape[1]
    tq = min(Q_TILE, s)
    has_ctx = knc is not None
    gw = HEADS_PER_STEP * MLA_NOPE
    blk = lambda rows, w, fn: pl.BlockSpec((None, rows, w), fn)
    in_specs = [
        blk(tq, gw, lambda bi, g, t: (bi, t, g)),
        blk(tq, HEADS_PER_STEP * MLA_ROPE, lambda bi, g, t: (bi, t, g)),
        blk(kl, gw, lambda bi, g, t: (bi, 0, g)),
        blk(kl, LANES, lambda bi, g, t: (bi, 0, 0)),
        blk(kl, gw, lambda bi, g, t: (bi, 0, g)),
    ]
    args = [qn, qr, kn, kp, v]
    if has_ctx:
        cl = knc.shape[1]
        in_specs += [
            blk(cl, gw, lambda bi, g, t: (bi, 0, g)),
            blk(cl, LANES, lambda bi, g, t: (bi, 0, 0)),
            blk(cl, gw, lambda bi, g, t: (bi, 0, g)),
        ]
        args += [knc, kpc, vc]
    return pl.pallas_call(
        functools.partial(_attn_b_kernel, has_ctx),
        grid=(b, MLA_HEADS // HEADS_PER_STEP, s // tq),
        in_specs=in_specs,
        out_specs=blk(tq, gw, lambda bi, g, t: (bi, t, g)),
        out_shape=jax.ShapeDtypeStruct(qn.shape, BF16),
        compiler_params=_params("arbitrary", "arbitrary", "arbitrary"),
        name="attn_mla_latent" if has_ctx else "attn_mla_context",
    )(*args)


def _attn_c_kernel(latent, sink_ref, q_ref, kd_ref, vd_ref, *rest):
    if latent:
        kc_ref, vc_ref, o_ref = rest
    else:
        (o_ref,) = rest
    tq = q_ref.shape[0]
    group = SWA_HEADS // SWA_KV_HEADS
    kv_per_step = HEADS_PER_STEP // group
    head0 = pl.program_id(1) * HEADS_PER_STEP
    if latent:
        s_len = kd_ref.shape[0]
        win = tq + 2 * SWA_WINDOW
        t0 = pl.program_id(2) * tq
        ws = pl.multiple_of(jnp.clip(t0 - SWA_WINDOW, 0, s_len - win), LANES)
        qpos = t0 + lax.broadcasted_iota(jnp.int32, (tq, win), 0)
        kpos = ws + lax.broadcasted_iota(jnp.int32, (tq, win), 1)
        bias = jnp.where(jnp.abs(qpos - kpos) <= SWA_WINDOW, 0.0, NEG_INF).astype(F32)
    pairs_per_kv = group // 2
    pair_cols = [slice(p * LANES, (p + 1) * LANES) for p in range(HEADS_PER_STEP // 2)]
    kv_cols = [slice(j * LANES, (j + 1) * LANES) for j in range(kv_per_step)]
    if not latent:
        qs = [jnp.concatenate([h for c in pair_cols[j * pairs_per_kv:(j + 1) * pairs_per_kv]
                               for h in _split_pair(q_ref[:, c])], axis=0) for j in range(kv_per_step)]
        out = _attend_group(qs, [kd_ref[:, c] for c in kv_cols], [_with_ones(vd_ref[:, c]) for c in kv_cols],
                            sink_ref[...])
        for p, c in enumerate(pair_cols):
            o_ref[:, c] = _merge_pair(out[2 * p * tq:(2 * p + 1) * tq], out[(2 * p + 1) * tq:(2 * p + 2) * tq])
        return
    sinks = [sink_ref[head0 + h] * LOG2E for h in range(HEADS_PER_STEP)]
    hd = SWA_HEAD_DIM
    for j, kc in enumerate(kv_cols):
        kct = kc_ref[j * hd:(j + 1) * hd, :].astype(BF16)
        vct = vc_ref[j * hd:(j + 1) * hd, :].astype(BF16)
        segs = [(kd_ref[pl.ds(ws, win), kc], _with_ones(vd_ref[pl.ds(ws, win), kc]), (bias, bias)),
                (jnp.concatenate([kct, kct], axis=0),
                 jnp.concatenate([vct, vct, jnp.ones((LANES, vct.shape[1]), BF16)], axis=0), None, True)]
        for p in range(j * pairs_per_kv, (j + 1) * pairs_per_kv):
            o_ref[:, pair_cols[p]] = _attend_pair(q_ref[:, pair_cols[p]], segs, sinks[2 * p:2 * p + 2])


def _attention_c(sink, q, kd, vd, kc=None, vc=None):
    b, s, _ = q.shape
    kl = kd.shape[1]
    tq = min(Q_TILE, s)
    latent = kc is not None
    gw = HEADS_PER_STEP * SWA_HEAD_DIM
    kvw = 2 * SWA_HEAD_DIM * HEADS_PER_STEP // (SWA_HEADS // SWA_KV_HEADS)
    if latent:
        sink_spec = pl.BlockSpec(memory_space=pltpu.SMEM)
    else:
        rows = jnp.repeat(sink * LOG2E, tq).reshape(SWA_HEADS // HEADS_PER_STEP, HEADS_PER_STEP * tq, 1)
        lane0 = jnp.arange(LANES)[None, None, :] == 0
        sink = jnp.where(lane0, rows, NEG_INF)
        sink_spec = pl.BlockSpec((None, HEADS_PER_STEP * tq, LANES), lambda bi, g, t: (g, 0, 0))
    in_specs = [
        sink_spec,
        pl.BlockSpec((None, tq, gw), lambda bi, g, t: (bi, t, g)),
        pl.BlockSpec((None, kl, kvw), lambda bi, g, t: (bi, 0, g)),
        pl.BlockSpec((None, kl, kvw), lambda bi, g, t: (bi, 0, g)),
    ]
    args = [sink, q, kd, vd]
    if latent:
        cw, cl = kc.shape[1] * HEADS_PER_STEP // SWA_HEADS, kc.shape[2]
        in_specs += [pl.BlockSpec((None, cw, cl), lambda bi, g, t: (bi, g, 0))] * 2
        args += [kc, vc]
    return pl.pallas_call(
        functools.partial(_attn_c_kernel, latent),
        grid=(b, SWA_HEADS // HEADS_PER_STEP, s // tq),
        in_specs=in_specs,
        out_specs=pl.BlockSpec((None, tq, gw), lambda bi, g, t: (bi, t, g)),
        out_shape=jax.ShapeDtypeStruct(q.shape, BF16),
        compiler_params=_params("arbitrary", "arbitrary", "arbitrary"),
        name="attn_swa_latent" if latent else "attn_swa_context",
    )(*args)


def _attn_d_ctx_kernel(q_ref, k_ref, v_ref, o_ref):
    tq = q_ref.shape[0]
    pair_cols = [slice(p * LANES, (p + 1) * LANES) for p in range(HEADS_PER_STEP // 2)]
    qs = [jnp.concatenate(_split_pair(q_ref[:, c]), axis=0) for c in pair_cols]
    out = _attend_group(qs, [k_ref[:, c] for c in pair_cols], [_with_ones(v_ref[:, c]) for c in pair_cols])
    for p, c in enumerate(pair_cols):
        o_ref[:, c] = _merge_pair(out[2 * p * tq:(2 * p + 1) * tq], out[(2 * p + 1) * tq:(2 * p + 2) * tq])


def _attention_d_ctx(q, k, v):
    b, s, _ = q.shape
    gw = HEADS_PER_STEP * NAT_HEAD_DIM
    blk = pl.BlockSpec((None, s, gw), lambda bi, g: (bi, 0, g))
    return pl.pallas_call(
        _attn_d_ctx_kernel,
        grid=(b, NAT_HEADS // HEADS_PER_STEP),
        in_specs=[blk, blk, blk],
        out_specs=blk,
        out_shape=jax.ShapeDtypeStruct(q.shape, BF16),
        compiler_params=_params("arbitrary", "arbitrary"),
        name="attn_nat_context",
    )(q, k, v)


def _nat_window_start(first_row, rows):
    r0 = jnp.clip(first_row - NAT_WIN_R // 2, 0, rows - NAT_WIN_R)
    return jnp.minimum(r0, rows - NAT_KEY_ROWS)


def _attn_d_lat_kernel(q_ref, k_ref, v_ref, kc_ref, vc_ref, bias_ref, o_ref):
    rows = k_ref.shape[0] // GRID_W
    slab = NAT_KEY_ROWS * GRID_W
    start = _nat_window_start(pl.program_id(0) * NAT_TILE_ROWS, rows) * GRID_W
    start = pl.multiple_of(start, GRID_W)
    ones = jnp.ones((LANES, kc_ref.shape[1]), BF16)
    for p in range(HEADS_PER_STEP // 2):
        cols = slice(p * LANES, (p + 1) * LANES)
        segs = [(k_ref[pl.ds(start, slab), cols], _with_ones(v_ref[pl.ds(start, slab), cols]),
                 (bias_ref[2 * p], bias_ref[2 * p + 1])),
                (kc_ref[cols, :].astype(BF16),
                 jnp.concatenate([vc_ref[cols, :].astype(BF16), ones], axis=0), None, True)]
        o_ref[:, cols] = _attend_pair(q_ref[:, cols], segs)


def _attention_d_lat(q, k, v, kc, vc, bias):
    b, s, _ = q.shape
    cl = kc.shape[2]
    tq = NAT_TILE_ROWS * GRID_W
    slab = NAT_KEY_ROWS * GRID_W
    gw = HEADS_PER_STEP * NAT_HEAD_DIM
    return pl.pallas_call(
        _attn_d_lat_kernel,
        grid=(s // tq, NAT_HEADS // HEADS_PER_STEP, b),
        in_specs=[
            pl.BlockSpec((None, tq, gw), lambda t, g, bi: (bi, t, g)),
            pl.BlockSpec((None, s, gw), lambda t, g, bi: (bi, 0, g)),
            pl.BlockSpec((None, s, gw), lambda t, g, bi: (bi, 0, g)),
            pl.BlockSpec((None, gw, cl), lambda t, g, bi: (bi, g, 0)),
            pl.BlockSpec((None, gw, cl), lambda t, g, bi: (bi, g, 0)),
            pl.BlockSpec((HEADS_PER_STEP, None, tq, slab), lambda t, g, bi: (g, t, 0, 0)),
        ],
        out_specs=pl.BlockSpec((None, tq, gw), lambda t, g, bi: (bi, t, g)),
        out_shape=jax.ShapeDtypeStruct(q.shape, BF16),
        compiler_params=_params("arbitrary", "arbitrary", "arbitrary"),
        name="attn_nat_latent",
    )(q, k, v, kc, vc, bias)


def _mla_expand_kernel(c_ref, w_ref, kn_ref, v_ref):
    kv = _dot(c_ref[...].astype(BF16), w_ref[...])
    half = kv.shape[1] // 2
    kn_ref[...] = kv[:, :half].astype(BF16)
    v_ref[...] = kv[:, half:].astype(BF16)


def _mla_expand(ckv, w_ukv):
    b, l, c = ckv.shape
    n = w_ukv.shape[1] // 2
    out = pl.BlockSpec((None, l, n), lambda bi: (bi, 0, 0))
    return pl.pallas_call(
        _mla_expand_kernel,
        grid=(b,),
        in_specs=[pl.BlockSpec((None, l, c), lambda bi: (bi, 0, 0)), _full(w_ukv.shape)],
        out_specs=[out, out],
        out_shape=[jax.ShapeDtypeStruct((b, l, n), BF16)] * 2,
        compiler_params=_params("arbitrary"),
        name="mla_expand_cache",
    )(ckv, w_ukv)


def _post_attn_kernel(o_ref, x_ref, m_ref, g_ref, wo_ref, x1_ref, h2_ref):
    m = m_ref[...]
    a = _dot(o_ref[...], wo_ref[...])
    x1 = x_ref[...] + _rms(a, _mod(m, 2) * g_ref[1:2, :])
    x1_ref[...] = x1
    h2_ref[...] = _premod(x1, g_ref[2:3, :], m, 1).astype(BF16)


def _post_attn(o, x, mods_l, g, wo, latent):
    bx, s, _ = x.shape
    tm = TOK_TILE
    row = (lambda b, j: (b, 0, 0)) if latent else (lambda b, j: (CTX_MOD_ROW, 0, 0))
    tok = lambda w: pl.BlockSpec((None, tm, w), lambda b, j: (b, j, 0))
    return pl.pallas_call(
        _row_chains(_post_attn_kernel, [True, True, False, False, False, True, True], tm, TOK_CHAIN),
        grid=(bx, s // tm),
        in_specs=[tok(o.shape[-1]), tok(D_MODEL),
                  pl.BlockSpec((None, 1, N_MOD * D_MODEL), row), _full(g.shape), _full(wo.shape)],
        out_specs=[tok(D_MODEL), tok(D_MODEL)],
        out_shape=[jax.ShapeDtypeStruct(x.shape, F32), jax.ShapeDtypeStruct(x.shape, BF16)],
        compiler_params=_params("arbitrary", "arbitrary"),
        name="out_proj_residual",
    )(o, x, mods_l, g, wo)


def _mlp_kernel(h_ref, x1_ref, m_ref, g_ref, w1_ref, w2_ref, o_ref):
    f = pl.program_id(2)

    @pl.when(f == 0)
    def _():
        o_ref[...] = jnp.zeros_like(o_ref)

    h = h_ref[...]
    for c in range(w1_ref.shape[1] // MLP_FF_CHAIN):
        cols = slice(c * MLP_FF_CHAIN, (c + 1) * MLP_FF_CHAIN)
        u = _dot(h, w1_ref[:, cols].astype(BF16))
        u = jnp.square(jnp.maximum(u, 0.0)).astype(BF16)
        o_ref[...] += _dot(u, w2_ref[cols, :].astype(BF16))

    @pl.when(f == pl.num_programs(2) - 1)
    def _():
        o_ref[...] = x1_ref[...] + _rms(o_ref[...], _mod(m_ref[...], 5) * g_ref[3:4, :])


def _mlp(h2, x1, mods_l, g, w1, w2, layer, latent):
    bx, s, _ = x1.shape
    tm, tf = MLP_TOK_TILE, MLP_FF_TILE
    row = (lambda b, j, f: (b, 0, 0)) if latent else (lambda b, j, f: (CTX_MOD_ROW, 0, 0))
    tok = pl.BlockSpec((None, tm, D_MODEL), lambda b, j, f: (b, j, 0))
    return pl.pallas_call(
        _mlp_kernel,
        grid=(bx, s // tm, D_FF // tf),
        in_specs=[tok, tok, pl.BlockSpec((None, 1, N_MOD * D_MODEL), row), _full(g.shape),
                  pl.BlockSpec((None, D_MODEL, tf), lambda b, j, f: (layer, 0, f)),
                  pl.BlockSpec((None, tf, D_MODEL), lambda b, j, f: (layer, f, 0))],
        out_specs=tok,
        out_shape=jax.ShapeDtypeStruct(x1.shape, F32),
        compiler_params=pltpu.CompilerParams(dimension_semantics=("arbitrary",) * 3,
                                             vmem_limit_bytes=MLP_VMEM_LIMIT),
        name="mlp_relu2",
    )(h2, x1, mods_l, g, w1, w2)


def _rope_tables(s, dim):
    quarter = dim // 4
    t = jnp.arange(s)
    pos = jnp.stack([t // GRID_W, t % GRID_W], axis=-1).astype(F32)
    inv = ROPE_THETA ** (-jnp.arange(quarter, dtype=F32) / quarter)
    ang = pos[:, :, None] * inv
    cos = jnp.broadcast_to(jnp.cos(ang)[:, :, None, :], (s, 2, 2, quarter)).reshape(s, dim)
    sign = jnp.array([-1.0, 1.0], F32)[None, None, :, None]
    sin = (jnp.sin(ang)[:, :, None, :] * sign).reshape(s, dim)
    reps = LANES // dim
    return jnp.tile(cos, (1, reps)), jnp.tile(sin, (1, reps))


def _dup_heads(w, heads, dim):
    lead = w.shape[:-1]
    w = w.reshape(lead + (heads, 1, dim))
    return jnp.broadcast_to(w, lead + (heads, 2, dim)).reshape(lead + (heads * 2 * dim,))


def _nat_dense_bias(rpb, rows):
    heads = rpb.shape[0]
    c = np.arange(GRID_W)
    c0 = np.clip(c - NAT_WIN_C // 2, 0, GRID_W - NAT_WIN_C)
    in_c = (c[None, :] >= c0[:, None]) & (c[None, :] < c0[:, None] + NAT_WIN_C)
    dc = c[None, :] - c[:, None] + NAT_WIN_C - 1
    onehot = (dc[None] == np.arange(2 * NAT_WIN_C - 1)[:, None, None]) & in_c[None]
    toe = jnp.einsum("had,dck->hack", rpb * LOG2E, jnp.asarray(onehot, F32),
                     precision=lax.Precision.HIGHEST)
    toe = jnp.where(jnp.asarray(in_c)[None, None], toe, NEG_INF)
    pad = jnp.full((heads, 1, GRID_W, GRID_W), NEG_INF, F32)
    ext = jnp.concatenate([pad, toe, pad], axis=1)
    pairs = jnp.concatenate([ext[:, :-1], ext[:, 1:]], axis=-1)
    tiles = rows // NAT_TILE_ROWS
    tq, slab = NAT_TILE_ROWS * GRID_W, NAT_KEY_ROWS * GRID_W
    n_off = 2 * NAT_WIN_R
    return pl.pallas_call(
        functools.partial(_nat_bias_kernel, rows),
        grid=(heads, tiles),
        in_specs=[pl.BlockSpec((None, n_off, GRID_W, LANES), lambda h, t: (h, 0, 0, 0))],
        out_specs=pl.BlockSpec((None, None, tq, slab), lambda h, t: (h, t, 0, 0)),
        out_shape=jax.ShapeDtypeStruct((heads, tiles, tq, slab), F32),
        compiler_params=_params("arbitrary", "arbitrary"),
        name="nat_bias_expand",
    )(pairs)


def _nat_bias_kernel(rows, pairs_ref, o_ref):
    first = pl.program_id(1) * NAT_TILE_ROWS
    ws = _nat_window_start(first, rows)
    lane = lax.broadcasted_iota(jnp.int32, (GRID_W, LANES), 1)
    for i in range(NAT_TILE_ROWS):
        r = first + i
        r0 = jnp.clip(r - NAT_WIN_R // 2, 0, rows - NAT_WIN_R)
        for jb in range(NAT_KEY_ROWS // 2):
            kr = ws + 2 * jb
            ok_lo = ((kr >= r0) & (kr < r0 + NAT_WIN_R)).astype(jnp.int32)
            ok_hi = ((kr + 1 >= r0) & (kr + 1 < r0 + NAT_WIN_R)).astype(jnp.int32)
            a = jnp.clip(kr - r + NAT_WIN_R, 0, 2 * NAT_WIN_R - 1)
            ok = jnp.where(lane < GRID_W, ok_lo, ok_hi) > 0
            o_ref[i * GRID_W:(i + 1) * GRID_W, jb * LANES:(jb + 1) * LANES] = jnp.where(ok, pairs_ref[a], NEG_INF)


def kernel(x_prompt, x_sample, cache_l0_k, cache_l0_v, cache_l1_ckv, cache_l1_kpe, cache_l2_k, cache_l2_v, cache_l3_k, cache_l3_v, c, c_ctx, ada_w, ada_b, norm_g, mlp_w1, mlp_w2, attn_w_qkv, attn_q_norm, attn_k_norm, attn_w_o, mla_w_in, mla_q_norm, mla_kv_norm, mla_w_uq, mla_w_ukv, mla_w_o, swa_w_qkv, swa_sink, swa_w_o, nat_w_qkv, nat_rpb, nat_w_o):
    nb, seq, d = x_prompt.shape
    db, dseq, _ = x_sample.shape
    past = cache_l0_k.shape[1]
    ctx_b = nb * seq // dseq
    xp = x_prompt.reshape(ctx_b, dseq, d)
    xs = x_sample

    cond = jnp.zeros((COND_ROWS, d), F32).at[:db].set(c).at[CTX_MOD_ROW].set(c_ctx)
    mods = _modulation(cond, ada_w, ada_b).reshape(DEPTH, COND_ROWS, 1, N_MOD * d)

    row = lambda v: v.reshape(1, -1)

    def as_ctx(a):
        return a.reshape(nb, seq, a.shape[-1])

    def as_slab(a):
        return a.reshape(ctx_b, dseq, a.shape[-1])

    def head_major(cache):
        b_, l_, h_, dh = cache.shape
        return cache.transpose(0, 2, 3, 1).reshape(b_, h_ * dh, l_)

    def token_major(state, heads):
        b_, w_, l_ = state.shape
        return state.reshape(b_, heads, w_ // heads, l_).transpose(0, 3, 1, 2)

    def finish(o, x, layer, wo, latent):
        x1, h2 = _post_attn(o, x, mods[layer], norm_g[layer], wo, latent)
        return _mlp(h2, x1, mods[layer], norm_g[layer], mlp_w1, mlp_w2, layer, latent)

    g = norm_g[0]
    w = attn_w_qkv.astype(BF16)
    wo = attn_w_o.astype(BF16)
    consts = [w, row(attn_q_norm), row(attn_k_norm)]
    kvw = ATTN_KV_HEADS * ATTN_HEAD_DIM
    q, k, v, l0_k, l0_v = _run_proj(
        _proj_a_kernel, False, xp, mods[0], g, consts, [],
        [(d, BF16), (kvw, BF16), (kvw, BF16), (kvw, F32), (kvw, F32)], "proj_gqa_context")
    o = _attention_a(as_ctx(q), as_ctx(k), as_ctx(v))
    xp = finish(as_slab(o), xp, 0, wo, False)
    tables = list(_rope_tables(dseq, ATTN_HEAD_DIM))
    q, k, v = _run_proj(_proj_a_kernel, True, xs, mods[0], g, consts, tables,
                        [(d, BF16), (kvw, BF16), (kvw, BF16)], "proj_gqa_latent")
    o = _attention_a(q, k, v, cache_l0_k.reshape(db, past, kvw).astype(BF16),
                     cache_l0_v.reshape(db, past, kvw).astype(BF16))
    xs = finish(o, xs, 0, wo, True)
    new_l0 = (l0_k.reshape(nb, seq, ATTN_KV_HEADS, ATTN_HEAD_DIM),
              l0_v.reshape(nb, seq, ATTN_KV_HEADS, ATTN_HEAD_DIM))

    g = norm_g[1]
    nq, nkv = MLA_Q_LORA, MLA_KV_LORA
    w_in = jnp.concatenate([mla_w_in[:, :nq + nkv]] + [mla_w_in[:, nq + nkv:]] * (LANES // MLA_ROPE),
                           axis=1).astype(BF16)
    wuq = mla_w_uq.reshape(nq, MLA_HEADS, MLA_NOPE + MLA_ROPE)
    wuq = jnp.concatenate([wuq[:, :, :MLA_NOPE].reshape(nq, -1), wuq[:, :, MLA_NOPE:].reshape(nq, -1)],
                          axis=1).astype(BF16)
    wukv = mla_w_ukv.reshape(nkv, MLA_HEADS, MLA_NOPE + MLA_V_DIM)
    wukv = jnp.concatenate([wukv[:, :, :MLA_NOPE].reshape(nkv, -1), wukv[:, :, MLA_NOPE:].reshape(nkv, -1)],
                           axis=1).astype(BF16)
    wo = mla_w_o.astype(BF16)
    consts = [w_in, row(mla_q_norm), row(mla_kv_norm), wuq, wukv]
    hw = MLA_HEADS * MLA_NOPE
    rw = MLA_HEADS * MLA_ROPE
    outs = [(hw, BF16), (rw, BF16), (hw, BF16), (hw, BF16), (LANES, BF16)]
    qn, qr, kn, v, kp, l1_ckv, l1_kpe = _run_proj(
        _proj_b_kernel, False, xp, mods[1], g, consts, [],
        outs + [(nkv, F32), (MLA_ROPE, F32, "T")], "proj_mla_context", seq)
    o = _attention_b(as_ctx(qn), as_ctx(qr), as_ctx(kn), as_ctx(kp), as_ctx(v))
    xp = finish(as_slab(o), xp, 1, wo, False)
    tables = list(_rope_tables(dseq, MLA_ROPE))
    qn, qr, kn, v, kp = _run_proj(_proj_b_kernel, True, xs, mods[1], g, consts, tables, outs,
                                  "proj_mla_latent")
    knc, vc = _mla_expand(cache_l1_ckv, wukv)
    kpc = jnp.tile(cache_l1_kpe, (1, 1, LANES // MLA_ROPE)).astype(BF16)
    o = _attention_b(qn, qr, kn, kp, v, knc, kpc, vc)
    xs = finish(o, xs, 1, wo, True)
    new_l1 = (l1_ckv.reshape(nb, seq, nkv), l1_kpe.transpose(0, 2, 1))

    g = norm_g[2]
    qw = SWA_HEADS * SWA_HEAD_DIM
    kw = SWA_KV_HEADS * SWA_HEAD_DIM
    wq, wk, wv = swa_w_qkv[:, :qw], swa_w_qkv[:, qw:qw + kw], swa_w_qkv[:, qw + kw:]
    wkd = _dup_heads(wk, SWA_KV_HEADS, SWA_HEAD_DIM)
    wvd = _dup_heads(wv, SWA_KV_HEADS, SWA_HEAD_DIM)
    w_ctx = jnp.concatenate([wq, wk, wv, wkd, wvd], axis=1).astype(BF16)
    w_lat = jnp.concatenate([wq, wkd, wvd], axis=1).astype(BF16)
    wo = swa_w_o.astype(BF16)
    q, kd, vd, l2_k, l2_v = _run_proj(
        _proj_c_kernel, False, xp, mods[2], g, [w_ctx], [],
        [(qw, BF16), (2 * kw, BF16), (2 * kw, BF16), (kw, F32, "T"), (kw, F32, "T")], "proj_swa_context", seq)
    o = _attention_c(swa_sink, as_ctx(q), as_ctx(kd), as_ctx(vd))
    xp = finish(as_slab(o), xp, 2, wo, False)
    tables = list(_rope_tables(dseq, SWA_HEAD_DIM))
    q, kd, vd = _run_proj(_proj_c_kernel, True, xs, mods[2], g, [w_lat], tables,
                          [(qw, BF16), (2 * kw, BF16), (2 * kw, BF16)], "proj_swa_latent")
    o = _attention_c(swa_sink, q, kd, vd, head_major(cache_l2_k), head_major(cache_l2_v))
    xs = finish(o, xs, 2, wo, True)
    new_l2 = (token_major(l2_k, SWA_KV_HEADS), token_major(l2_v, SWA_KV_HEADS))

    g = norm_g[3]
    hw = NAT_HEADS * NAT_HEAD_DIM
    w = nat_w_qkv.astype(BF16)
    wo = nat_w_o.astype(BF16)
    q, k, v, l3_k, l3_v = _run_proj(
        _proj_d_kernel, False, xp, mods[3], g, [w], [],
        [(hw, BF16), (hw, BF16), (hw, BF16), (hw, F32, "T"), (hw, F32, "T")], "proj_nat_context", seq)
    o = _attention_d_ctx(as_ctx(q), as_ctx(k), as_ctx(v))
    xp = finish(as_slab(o), xp, 3, wo, False)
    q, k, v = _run_proj(_proj_d_kernel, True, xs, mods[3], g, [w], [],
                        [(hw, BF16), (hw, BF16), (hw, BF16)], "proj_nat_latent")
    o = _attention_d_lat(q, k, v, head_major(cache_l3_k), head_major(cache_l3_v),
                         _nat_dense_bias(nat_rpb, dseq // GRID_W))
    xs = finish(o, xs, 3, wo, True)
    new_l3 = (token_major(l3_k, NAT_HEADS), token_major(l3_v, NAT_HEADS))

    return (xp.reshape(nb, seq, d), xs) + new_l0 + new_l1 + new_l2 + new_l3
```

```python
import functools

import numpy as np

import jax
import jax.numpy as jnp
from jax import lax
from jax.experimental import pallas as pl
from jax.experimental.pallas import tpu as pltpu

F32 = jnp.float32
BF16 = jnp.bfloat16

D_MODEL = 1024
DEPTH = 4
N_MOD = 6
D_FF = 4 * D_MODEL
GRID_W = 64
ROPE_THETA = 10000.0
NORM_EPS = 1e-6
NEG_INF = -1e30
LOG2E = 1.4426950408889634

ATTN_HEADS, ATTN_KV_HEADS, ATTN_HEAD_DIM = 8, 2, 128
MLA_HEADS, MLA_Q_LORA, MLA_KV_LORA = 16, 384, 256
MLA_NOPE, MLA_ROPE, MLA_V_DIM = 64, 32, 64
MLA_SCALE = (MLA_NOPE + MLA_ROPE) ** -0.5
SWA_HEADS, SWA_KV_HEADS, SWA_HEAD_DIM, SWA_WINDOW = 16, 4, 64, 128
NAT_HEADS, NAT_HEAD_DIM, NAT_WIN_R, NAT_WIN_C = 16, 64, 8, 16

LANES = 128
MXU_COLS = 256
COND_ROWS = 16
CTX_MOD_ROW = 8
VMEM_LIMIT = 48 * 1024 * 1024

TOK_TILE = 512
TOK_CHAIN = 256
MLP_TOK_TILE = 1024
MLP_FF_TILE = 1024
MLP_FF_CHAIN = 1024
MLP_VMEM_LIMIT = 56 * 1024 * 1024
Q_TILE = 256
HEADS_PER_STEP = 16
NAT_TILE_ROWS = Q_TILE // GRID_W
NAT_KEY_ROWS = NAT_WIN_R + NAT_TILE_ROWS


def _params(*sem):
    return pltpu.CompilerParams(dimension_semantics=sem, vmem_limit_bytes=VMEM_LIMIT)


def _full(shape):
    nd = len(shape)
    return pl.BlockSpec(shape, lambda *_: (0,) * nd)


def _row_chains(body, row_refs, tile, sub):
    def kernel(*refs):
        for c in range(tile // sub):
            rows = slice(c * sub, (c + 1) * sub)
            body(*[r.at[rows] if is_row else r for r, is_row in zip(refs, row_refs)])
    return kernel


def _rms(x, g):
    return x * lax.rsqrt(jnp.mean(x * x, axis=-1, keepdims=True) + NORM_EPS) * g


def _mod(m, i):
    return m[:, i * D_MODEL:(i + 1) * D_MODEL]


def _premod(x, g, m, sub):
    return _rms(x, g * (1.0 + _mod(m, 3 * sub + 1))) + _mod(m, 3 * sub)


def _rope(x, cos, sin_signed, quarter):
    n = x.shape[-1]
    lane = lax.broadcasted_iota(jnp.int32, x.shape, 1)
    first = ((lane // quarter) % 2) == 0
    partner = jnp.where(first, pltpu.roll(x, n - quarter, 1), pltpu.roll(x, quarter, 1))
    return x * cos + partner * sin_signed


def _dot(a, b):
    return jnp.dot(a, b, preferred_element_type=F32)


def _dot_nt(a, b):
    return lax.dot_general(a, b, (((1,), (1,)), ((), ())), preferred_element_type=F32)


def _with_ones(v):
    return jnp.concatenate([v, jnp.ones_like(v)], axis=1)


def _attend(q, segs, sink=None):
    logits = []
    for seg in segs:
        k, bias, transposed = seg[0], seg[2], len(seg) > 3 and seg[3]
        s = _dot(q, k) if transposed else _dot_nt(q, k)
        logits.append(s if bias is None else s + bias)
    m = logits[0].max(axis=-1, keepdims=True)
    for s in logits[1:]:
        m = jnp.maximum(m, s.max(axis=-1, keepdims=True))
    if sink is not None:
        m = jnp.maximum(m, sink)
    acc = None
    for s, seg in zip(logits, segs):
        p = jnp.exp2(s - m).astype(BF16)
        pv = _dot_nt(p, seg[1]) if len(seg) > 3 and seg[3] else _dot(p, seg[1])
        acc = pv if acc is None else acc + pv
    den = acc[:, LANES:LANES + 1]
    if sink is not None:
        den = den + jnp.exp2(sink - m)
    return acc[:, :LANES] / den


def _attend_group(qs, ks, v1s, sink_logits=None):
    tq = qs[0].shape[0]
    s = jnp.concatenate([_dot_nt(q, k) for q, k in zip(qs, ks)], axis=0)
    if sink_logits is not None:
        s = jnp.concatenate([s, sink_logits], axis=1)
        zeros = jnp.zeros((LANES, LANES), BF16)
        tail = jnp.concatenate([zeros, jnp.ones_like(zeros)], axis=1)
        v1s = [jnp.concatenate([v1, tail], axis=0) for v1 in v1s]
    p = jnp.exp2(s - s.max(axis=-1, keepdims=True)).astype(BF16)
    acc = jnp.concatenate([_dot(p[i * tq:(i + 1) * tq], v1) for i, v1 in enumerate(v1s)], axis=0)
    return acc[:, :LANES] / acc[:, LANES:LANES + 1]


def _split_pair(q):
    low = lax.broadcasted_iota(jnp.int32, q.shape, 1) < (LANES // 2)
    zero = jnp.zeros_like(q)
    return [jnp.where(low, q, zero), jnp.where(low, zero, q)]


def _merge_pair(o0, o1):
    low = lax.broadcasted_iota(jnp.int32, o0.shape, 1) < (LANES // 2)
    return jnp.where(low, o0, o1).astype(BF16)


def _attend_pair(q, segs, sinks=None):
    outs = []
    for j, qj in enumerate(_split_pair(q)):
        segs_j = [(s[0], s[1], None if s[2] is None else s[2][j]) + tuple(s[3:]) for s in segs]
        outs.append(_attend(qj, segs_j, None if sinks is None else sinks[j]))
    return _merge_pair(*outs)


def _mods_kernel(cond_ref, w_ref, b_ref, o_ref):
    cnd = cond_ref[...]
    act = cnd * jax.nn.sigmoid(cnd)
    o_ref[...] = _dot(act.astype(BF16), w_ref[...].astype(BF16)) + b_ref[...]


def _modulation(cond, ada_w, ada_b):
    tn = 1536
    n = N_MOD * D_MODEL
    return pl.pallas_call(
        _mods_kernel,
        grid=(DEPTH, n // tn),
        in_specs=[
            _full((COND_ROWS, D_MODEL)),
            pl.BlockSpec((None, D_MODEL, tn), lambda l, j: (l, 0, j)),
            pl.BlockSpec((None, 1, tn), lambda l, j: (l, 0, j)),
        ],
        out_specs=pl.BlockSpec((None, COND_ROWS, tn), lambda l, j: (l, 0, j)),
        out_shape=jax.ShapeDtypeStruct((DEPTH, COND_ROWS, n), F32),
        compiler_params=_params("arbitrary", "arbitrary"),
        name="adaln_mods",
    )(cond, ada_w, ada_b.reshape(DEPTH, 1, n))


def _proj_a_kernel(latent, x_ref, m_ref, g_ref, w_ref, qn_ref, kn_ref, *rest):
    if latent:
        cos_ref, sin_ref, q_ref, k_ref, v_ref = rest
    else:
        q_ref, k_ref, v_ref, ks_ref, vs_ref = rest
    hd = ATTN_HEAD_DIM
    h = _premod(x_ref[...], g_ref[0:1, :], m_ref[...], 0)
    z = _dot(h.astype(BF16), w_ref[...])
    n_qk = ATTN_HEADS + ATTN_KV_HEADS
    ys = [_rms(z[:, i * hd:(i + 1) * hd], qn_ref[...] if i < ATTN_HEADS else kn_ref[...]) for i in range(n_qk)]
    if not latent:
        for j in range(ATTN_KV_HEADS):
            ks_ref[:, j * hd:(j + 1) * hd] = ys[ATTN_HEADS + j]
    else:
        ys = [y * cos_ref[...] + pltpu.roll(y, hd // 2, 1) * sin_ref[...] for y in ys]
    for i in range(ATTN_HEADS):
        q_ref[:, i * hd:(i + 1) * hd] = (ys[i] * (hd ** -0.5 * LOG2E)).astype(BF16)
    for j in range(ATTN_KV_HEADS):
        k_ref[:, j * hd:(j + 1) * hd] = ys[ATTN_HEADS + j].astype(BF16)
    v = z[:, n_qk * hd:]
    v_ref[...] = v.astype(BF16)
    if not latent:
        vs_ref[...] = v


def _proj_b_kernel(latent, x_ref, m_ref, g_ref, win_ref, qn_ref, kvn_ref, wuq_ref, wukv_ref, *rest):
    if latent:
        cos_ref, sin_ref, qn_o, qr_o, kn_o, v_o, kp_o = rest
    else:
        qn_o, qr_o, kn_o, v_o, kp_o, ckv_s, kpe_s = rest
    nq, nkv = MLA_Q_LORA, MLA_KV_LORA
    nope_w = MLA_HEADS * MLA_NOPE
    scale = MLA_SCALE * LOG2E
    h = _premod(x_ref[...], g_ref[0:1, :], m_ref[...], 0)
    z = _dot(h.astype(BF16), win_ref[...])
    cq = _rms(z[:, :nq], qn_ref[...])
    ckv = _rms(z[:, nq:nq + nkv], kvn_ref[...])
    kp = z[:, nq + nkv:]
    q = _dot(cq.astype(BF16), wuq_ref[...])
    kv = _dot(ckv.astype(BF16), wukv_ref[...])
    if not latent:
        ckv_s[...] = ckv
        _store_transposed(kpe_s, kp, MLA_ROPE)
    qn_o[...] = (q[:, :nope_w] * scale).astype(BF16)
    for i in range(MLA_HEADS * MLA_ROPE // LANES):
        qr = q[:, nope_w + i * LANES:nope_w + (i + 1) * LANES]
        if latent:
            qr = _rope(qr, cos_ref[...], sin_ref[...], MLA_ROPE // 4)
        qr_o[:, i * LANES:(i + 1) * LANES] = (qr * scale).astype(BF16)
    if latent:
        kp = _rope(kp, cos_ref[...], sin_ref[...], MLA_ROPE // 4)
    kp_o[...] = kp.astype(BF16)
    kn_o[...] = kv[:, :nope_w].astype(BF16)
    v_o[...] = kv[:, nope_w:].astype(BF16)


def _proj_c_kernel(latent, x_ref, m_ref, g_ref, w_ref, *rest):
    if latent:
        cos_ref, sin_ref, q_o, kd_o, vd_o = rest
    else:
        q_o, kd_o, vd_o, ks_o, vs_o = rest
    qw = SWA_HEADS * SWA_HEAD_DIM
    kw = SWA_KV_HEADS * SWA_HEAD_DIM
    h = _premod(x_ref[...], g_ref[0:1, :], m_ref[...], 0)
    z = _dot(h.astype(BF16), w_ref[...])
    off = qw
    if not latent:
        _store_transposed(ks_o, z[:, qw:qw + kw], kw)
        _store_transposed(vs_o, z[:, qw + kw:qw + 2 * kw], kw)
        off = qw + 2 * kw
    scale = SWA_HEAD_DIM ** -0.5 * LOG2E
    for i in range(qw // LANES):
        y = z[:, i * LANES:(i + 1) * LANES]
        if latent:
            y = _rope(y, cos_ref[...], sin_ref[...], SWA_HEAD_DIM // 4)
        q_o[:, i * LANES:(i + 1) * LANES] = (y * scale).astype(BF16)
    for i in range(2 * kw // LANES):
        y = z[:, off + i * LANES:off + (i + 1) * LANES]
        if latent:
            y = _rope(y, cos_ref[...], sin_ref[...], SWA_HEAD_DIM // 4)
        kd_o[:, i * LANES:(i + 1) * LANES] = y.astype(BF16)
    vd_o[...] = z[:, off + 2 * kw:].astype(BF16)


def _proj_d_kernel(latent, x_ref, m_ref, g_ref, w_ref, *rest):
    if latent:
        q_o, k_o, v_o = rest
    else:
        q_o, k_o, v_o, ks_o, vs_o = rest
    hw = NAT_HEADS * NAT_HEAD_DIM
    h = _premod(x_ref[...], g_ref[0:1, :], m_ref[...], 0)
    z = _dot(h.astype(BF16), w_ref[...])
    q_o[...] = (z[:, :hw] * (NAT_HEAD_DIM ** -0.5 * LOG2E)).astype(BF16)
    k_o[...] = z[:, hw:2 * hw].astype(BF16)
    v_o[...] = z[:, 2 * hw:].astype(BF16)
    if not latent:
        _store_transposed(ks_o, z[:, hw:2 * hw], hw)
        _store_transposed(vs_o, z[:, 2 * hw:], hw)


def _store_transposed(ref, val, width):
    n, _, seq = ref.shape
    for i in range(n):
        ref[i] = val[i * seq:(i + 1) * seq, :].T[:width, :]


def _run_proj(kernel, latent, x, mods_l, g, consts, tables, outs, name, state_seq=None):
    bx, s, _ = x.shape
    tm = TOK_TILE
    row = (lambda b, j: (b, 0, 0)) if latent else (lambda b, j: (CTX_MOD_ROW, 0, 0))
    in_specs = [
        pl.BlockSpec((None, tm, D_MODEL), lambda b, j: (b, j, 0)),
        pl.BlockSpec((None, 1, N_MOD * D_MODEL), row),
        _full(g.shape),
    ] + [_full(c.shape) for c in consts]
    in_specs += [pl.BlockSpec((tm, LANES), lambda b, j: (j, 0)) for _ in tables]
    out_specs, out_shape = [], []
    for o in outs:
        if len(o) == 2:
            out_specs.append(pl.BlockSpec((None, tm, o[0]), lambda b, j: (b, j, 0)))
            out_shape.append(jax.ShapeDtypeStruct((bx, s, o[0]), o[1]))
        else:
            per_tile = tm // state_seq
            out_specs.append(pl.BlockSpec((per_tile, o[0], state_seq),
                                          lambda b, j: (b * (s // tm) + j, 0, 0)))
            out_shape.append(jax.ShapeDtypeStruct((bx * s // state_seq, o[0], state_seq), o[1]))
    return pl.pallas_call(
        functools.partial(kernel, latent),
        grid=(bx, s // tm),
        in_specs=in_specs,
        out_specs=out_specs,
        out_shape=out_shape,
        compiler_params=_params("arbitrary", "arbitrary"),
        name=name,
    )(x, mods_l, g, *consts, *tables)


def _attn_a_kernel(has_ctx, q_ref, k_ref, v_ref, *rest):
    if has_ctx:
        kc_ref, vc_ref, o_ref = rest
    else:
        (o_ref,) = rest
    group = ATTN_HEADS // ATTN_KV_HEADS
    head_cols = [slice(h * LANES, (h + 1) * LANES) for h in range(ATTN_HEADS)]
    kv_cols = [slice(j * LANES, (j + 1) * LANES) for j in range(ATTN_KV_HEADS)]
    if not has_ctx:
        tq = q_ref.shape[0]
        qs = [jnp.concatenate([q_ref[:, c] for c in head_cols[j * group:(j + 1) * group]], axis=0)
              for j in range(ATTN_KV_HEADS)]
        out = _attend_group(qs, [k_ref[:, c] for c in kv_cols], [_with_ones(v_ref[:, c]) for c in kv_cols])
        for h, c in enumerate(head_cols):
            o_ref[:, c] = out[h * tq:(h + 1) * tq].astype(BF16)
        return
    for j, kc in enumerate(kv_cols):
        segs = [(k_ref[:, kc], _with_ones(v_ref[:, kc]), None),
                (kc_ref[:, kc], _with_ones(vc_ref[:, kc]), None)]
        for c in head_cols[j * group:(j + 1) * group]:
            o_ref[:, c] = _attend(q_ref[:, c], segs).astype(BF16)


def _attention_a(q, k, v, kc=None, vc=None):
    b, s, qw = q.shape
    kl, kw = k.shape[1:]
    tq = min(Q_TILE, s)
    has_ctx = kc is not None
    in_specs = [
        pl.BlockSpec((None, tq, qw), lambda bi, t: (bi, t, 0)),
        pl.BlockSpec((None, kl, kw), lambda bi, t: (bi, 0, 0)),
        pl.BlockSpec((None, kl, kw), lambda bi, t: (bi, 0, 0)),
    ]
    args = [q, k, v]
    if has_ctx:
        cl = kc.shape[1]
        in_specs += [pl.BlockSpec((None, cl, kw), lambda bi, t: (bi, 0, 0))] * 2
        args += [kc, vc]
    return pl.pallas_call(
        functools.partial(_attn_a_kernel, has_ctx),
        grid=(b, s // tq),
        in_specs=in_specs,
        out_specs=pl.BlockSpec((None, tq, qw), lambda bi, t: (bi, t, 0)),
        out_shape=jax.ShapeDtypeStruct(q.shape, BF16),
        compiler_params=_params("arbitrary", "arbitrary"),
        name="attn_gqa_latent" if has_ctx else "attn_gqa_context",
    )(*args)


def _attn_b_kernel(has_ctx, qn_ref, qr_ref, kn_ref, kp_ref, v_ref, *rest):
    if has_ctx:
        knc_ref, kpc_ref, vc_ref, o_ref = rest
    else:
        (o_ref,) = rest
    tq = qn_ref.shape[0]
    lane = lax.broadcasted_iota(jnp.int32, (tq, LANES), 1)
    low = lane < (LANES // 2)
    quarter = lane // MLA_ROPE
    zero = jnp.zeros((tq, LANES), BF16)
    kp = kp_ref[...]
    kpc = kpc_ref[...] if has_ctx else None
    pairs_per_rope = LANES // MLA_ROPE // 2
    n_pairs = HEADS_PER_STEP // 2
    pair_cols = [slice(p * LANES, (p + 1) * LANES) for p in range(n_pairs)]

    def pair_lhs(p):
        qn = qn_ref[:, pair_cols[p]]
        rg = p // pairs_per_rope
        qr = qr_ref[:, rg * LANES:(rg + 1) * LANES]
        return [jnp.concatenate([jnp.where(low if j == 0 else ~low, qn, zero),
                                 jnp.where(quarter == 2 * (p % pairs_per_rope) + j, qr, zero)], axis=1)
                for j in range(2)]

    if not has_ctx:
        qs = [jnp.concatenate(pair_lhs(p), axis=0) for p in range(n_pairs)]
        out = _attend_group(qs, [jnp.concatenate([kn_ref[:, c], kp], axis=1) for c in pair_cols],
                            [_with_ones(v_ref[:, c]) for c in pair_cols])
        for p, c in enumerate(pair_cols):
            o_ref[:, c] = _merge_pair(out[2 * p * tq:(2 * p + 1) * tq], out[(2 * p + 1) * tq:(2 * p + 2) * tq])
        return
    for p, cols in enumerate(pair_cols):
        segs = [(jnp.concatenate([kn_ref[:, cols], kp], axis=1), _with_ones(v_ref[:, cols]), None),
                (jnp.concatenate([knc_ref[:, cols], kpc], axis=1), _with_ones(vc_ref[:, cols]), None)]
        o_ref[:, cols] = _merge_pair(*[_attend(lhs, segs) for lhs in pair_lhs(p)])


def _attention_b(qn, qr, kn, kp, v, knc=None, kpc=None, vc=None):
    b, s, _ = qn.shape
    kl = kn.shape[1]
    tq = min(Q_TILE, s)
    has_ctx = knc is not None
    gw = HEADS_PER_STEP * MLA_NOPE
    blk = lambda rows, w, fn: pl.BlockSpec((None, rows, w), fn)
    in_specs = [
        blk(tq, gw, lambda bi, g, t: (bi, t, g)),
        blk(tq, HEADS_PER_STEP * MLA_ROPE, lambda bi, g, t: (bi, t, g)),
        blk(kl, gw, lambda bi, g, t: (bi, 0, g)),
        blk(kl, LANES, lambda bi, g, t: (bi, 0, 0)),
        blk(kl, gw, lambda bi, g, t: (bi, 0, g)),
    ]
    args = [qn, qr, kn, kp, v]
    if has_ctx:
        cl = knc.shape[1]
        in_specs += [
            blk(cl, gw, lambda bi, g, t: (bi, 0, g)),
            blk(cl, LANES, lambda bi, g, t: (bi, 0, 0)),
            blk(cl, gw, lambda bi, g, t: (bi, 0, g)),
        ]
        args += [knc, kpc, vc]
    return pl.pallas_call(
        functools.partial(_attn_b_kernel, has_ctx),
        grid=(b, MLA_HEADS // HEADS_PER_STEP, s // tq),
        in_specs=in_specs,
        out_specs=blk(tq, gw, lambda bi, g, t: (bi, t, g)),
        out_shape=jax.ShapeDtypeStruct(qn.shape, BF16),
        compiler_params=_params("arbitrary", "arbitrary", "arbitrary"),
        name="attn_mla_latent" if has_ctx else "attn_mla_context",
    )(*args)


def _attn_c_kernel(latent, sink_ref, q_ref, kd_ref, vd_ref, *rest):
    if latent:
        kc_ref, vc_ref, o_ref = rest
    else:
        (o_ref,) = rest
    tq = q_ref.shape[0]
    group = SWA_HEADS // SWA_KV_HEADS
    kv_per_step = HEADS_PER_STEP // group
    head0 = pl.program_id(1) * HEADS_PER_STEP
    if latent:
        s_len = kd_ref.shape[0]
        win = tq + 2 * SWA_WINDOW
        t0 = pl.program_id(2) * tq
        ws = pl.multiple_of(jnp.clip(t0 - SWA_WINDOW, 0, s_len - win), LANES)
        qpos = t0 + lax.broadcasted_iota(jnp.int32, (tq, win), 0)
        kpos = ws + lax.broadcasted_iota(jnp.int32, (tq, win), 1)
        bias = jnp.where(jnp.abs(qpos - kpos) <= SWA_WINDOW, 0.0, NEG_INF).astype(F32)
    pairs_per_kv = group // 2
    pair_cols = [slice(p * LANES, (p + 1) * LANES) for p in range(HEADS_PER_STEP // 2)]
    kv_cols = [slice(j * LANES, (j + 1) * LANES) for j in range(kv_per_step)]
    if not latent:
        qs = [jnp.concatenate([h for c in pair_cols[j * pairs_per_kv:(j + 1) * pairs_per_kv]
                               for h in _split_pair(q_ref[:, c])], axis=0) for j in range(kv_per_step)]
        out = _attend_group(qs, [kd_ref[:, c] for c in kv_cols], [_with_ones(vd_ref[:, c]) for c in kv_cols],
                            sink_ref[...])
        for p, c in enumerate(pair_cols):
            o_ref[:, c] = _merge_pair(out[2 * p * tq:(2 * p + 1) * tq], out[(2 * p + 1) * tq:(2 * p + 2) * tq])
        return
    sinks = [sink_ref[head0 + h] * LOG2E for h in range(HEADS_PER_STEP)]
    hd = SWA_HEAD_DIM
    for j, kc in enumerate(kv_cols):
        kct = kc_ref[j * hd:(j + 1) * hd, :].astype(BF16)
        vct = vc_ref[j * hd:(j + 1) * hd, :].astype(BF16)
        segs = [(kd_ref[pl.ds(ws, win), kc], _with_ones(vd_ref[pl.ds(ws, win), kc]), (bias, bias)),
                (jnp.concatenate([kct, kct], axis=0),
                 jnp.concatenate([vct, vct, jnp.ones((LANES, vct.shape[1]), BF16)], axis=0), None, True)]
        for p in range(j * pairs_per_kv, (j + 1) * pairs_per_kv):
            o_ref[:, pair_cols[p]] = _attend_pair(q_ref[:, pair_cols[p]], segs, sinks[2 * p:2 * p + 2])


def _attention_c(sink, q, kd, vd, kc=None, vc=None):
    b, s, _ = q.shape
    kl = kd.shape[1]
    tq = min(Q_TILE, s)
    latent = kc is not None
    gw = HEADS_PER_STEP * SWA_HEAD_DIM
    kvw = 2 * SWA_HEAD_DIM * HEADS_PER_STEP // (SWA_HEADS // SWA_KV_HEADS)
    if latent:
        sink_spec = pl.BlockSpec(memory_space=pltpu.SMEM)
    else:
        rows = jnp.repeat(sink * LOG2E, tq).reshape(SWA_HEADS // HEADS_PER_STEP, HEADS_PER_STEP * tq, 1)
        lane0 = jnp.arange(LANES)[None, None, :] == 0
        sink = jnp.where(lane0, rows, NEG_INF)
        sink_spec = pl.BlockSpec((None, HEADS_PER_STEP * tq, LANES), lambda bi, g, t: (g, 0, 0))
    in_specs = [
        sink_spec,
        pl.BlockSpec((None, tq, gw), lambda bi, g, t: (bi, t, g)),
        pl.BlockSpec((None, kl, kvw), lambda bi, g, t: (bi, 0, g)),
        pl.BlockSpec((None, kl, kvw), lambda bi, g, t: (bi, 0, g)),
    ]
    args = [sink, q, kd, vd]
    if latent:
        cw, cl = kc.shape[1] * HEADS_PER_STEP // SWA_HEADS, kc.shape[2]
        in_specs += [pl.BlockSpec((None, cw, cl), lambda bi, g, t: (bi, g, 0))] * 2
        args += [kc, vc]
    return pl.pallas_call(
        functools.partial(_attn_c_kernel, latent),
        grid=(b, SWA_HEADS // HEADS_PER_STEP, s // tq),
        in_specs=in_specs,
        out_specs=pl.BlockSpec((None, tq, gw), lambda bi, g, t: (bi, t, g)),
        out_shape=jax.ShapeDtypeStruct(q.shape, BF16),
        compiler_params=_params("arbitrary", "arbitrary", "arbitrary"),
        name="attn_swa_latent" if latent else "attn_swa_context",
    )(*args)


def _attn_d_ctx_kernel(q_ref, k_ref, v_ref, o_ref):
    tq = q_ref.shape[0]
    pair_cols = [slice(p * LANES, (p + 1) * LANES) for p in range(HEADS_PER_STEP // 2)]
    qs = [jnp.concatenate(_split_pair(q_ref[:, c]), axis=0) for c in pair_cols]
    out = _attend_group(qs, [k_ref[:, c] for c in pair_cols], [_with_ones(v_ref[:, c]) for c in pair_cols])
    for p, c in enumerate(pair_cols):
        o_ref[:, c] = _merge_pair(out[2 * p * tq:(2 * p + 1) * tq], out[(2 * p + 1) * tq:(2 * p + 2) * tq])


def _attention_d_ctx(q, k, v):
    b, s, _ = q.shape
    gw = HEADS_PER_STEP * NAT_HEAD_DIM
    blk = pl.BlockSpec((None, s, gw), lambda bi, g: (bi, 0, g))
    return pl.pallas_call(
        _attn_d_ctx_kernel,
        grid=(b, NAT_HEADS // HEADS_PER_STEP),
        in_specs=[blk, blk, blk],
        out_specs=blk,
        out_shape=jax.ShapeDtypeStruct(q.shape, BF16),
        compiler_params=_params("arbitrary", "arbitrary"),
        name="attn_nat_context",
    )(q, k, v)


def _nat_window_start(first_row, rows):
    r0 = jnp.clip(first_row - NAT_WIN_R // 2, 0, rows - NAT_WIN_R)
    return jnp.minimum(r0, rows - NAT_KEY_ROWS)


def _attn_d_lat_kernel(q_ref, k_ref, v_ref, kc_ref, vc_ref, bias_ref, o_ref):
    rows = k_ref.shape[0] // GRID_W
    slab = NAT_KEY_ROWS * GRID_W
    start = _nat_window_start(pl.program_id(0) * NAT_TILE_ROWS, rows) * GRID_W
    start = pl.multiple_of(start, GRID_W)
    ones = jnp.ones((LANES, kc_ref.shape[1]), BF16)
    for p in range(HEADS_PER_STEP // 2):
        cols = slice(p * LANES, (p + 1) * LANES)
        segs = [(k_ref[pl.ds(start, slab), cols], _with_ones(v_ref[pl.ds(start, slab), cols]),
                 (bias_ref[2 * p], bias_ref[2 * p + 1])),
                (kc_ref[cols, :].astype(BF16),
                 jnp.concatenate([vc_ref[cols, :].astype(BF16), ones], axis=0), None, True)]
        o_ref[:, cols] = _attend_pair(q_ref[:, cols], segs)


def _attention_d_lat(q, k, v, kc, vc, bias):
    b, s, _ = q.shape
    cl = kc.shape[2]
    tq = NAT_TILE_ROWS * GRID_W
    slab = NAT_KEY_ROWS * GRID_W
    gw = HEADS_PER_STEP * NAT_HEAD_DIM
    return pl.pallas_call(
        _attn_d_lat_kernel,
        grid=(s // tq, NAT_HEADS // HEADS_PER_STEP, b),
        in_specs=[
            pl.BlockSpec((None, tq, gw), lambda t, g, bi: (bi, t, g)),
            pl.BlockSpec((None, s, gw), lambda t, g, bi: (bi, 0, g)),
            pl.BlockSpec((None, s, gw), lambda t, g, bi: (bi, 0, g)),
            pl.BlockSpec((None, gw, cl), lambda t, g, bi: (bi, g, 0)),
            pl.BlockSpec((None, gw, cl), lambda t, g, bi: (bi, g, 0)),
            pl.BlockSpec((HEADS_PER_STEP, None, tq, slab), lambda t, g, bi: (g, t, 0, 0)),
        ],
        out_specs=pl.BlockSpec((None, tq, gw), lambda t, g, bi: (bi, t, g)),
        out_shape=jax.ShapeDtypeStruct(q.shape, BF16),
        compiler_params=_params("arbitrary", "arbitrary", "arbitrary"),
        name="attn_nat_latent",
    )(q, k, v, kc, vc, bias)


def _mla_expand_kernel(c_ref, w_ref, kn_ref, v_ref):
    kv = _dot(c_ref[...].astype(BF16), w_ref[...])
    half = kv.shape[1] // 2
    kn_ref[...] = kv[:, :half].astype(BF16)
    v_ref[...] = kv[:, half:].astype(BF16)


def _mla_expand(ckv, w_ukv):
    b, l, c = ckv.shape
    n = w_ukv.shape[1] // 2
    out = pl.BlockSpec((None, l, n), lambda bi: (bi, 0, 0))
    return pl.pallas_call(
        _mla_expand_kernel,
        grid=(b,),
        in_specs=[pl.BlockSpec((None, l, c), lambda bi: (bi, 0, 0)), _full(w_ukv.shape)],
        out_specs=[out, out],
        out_shape=[jax.ShapeDtypeStruct((b, l, n), BF16)] * 2,
        compiler_params=_params("arbitrary"),
        name="mla_expand_cache",
    )(ckv, w_ukv)


def _post_attn_kernel(o_ref, x_ref, m_ref, g_ref, wo_ref, x1_ref, h2_ref):
    m = m_ref[...]
    a = _dot(o_ref[...], wo_ref[...])
    x1 = x_ref[...] + _rms(a, _mod(m, 2) * g_ref[1:2, :])
    x1_ref[...] = x1
    h2_ref[...] = _premod(x1, g_ref[2:3, :], m, 1).astype(BF16)


def _post_attn(o, x, mods_l, g, wo, latent):
    bx, s, _ = x.shape
    tm = TOK_TILE
    row = (lambda b, j: (b, 0, 0)) if latent else (lambda b, j: (CTX_MOD_ROW, 0, 0))
    tok = lambda w: pl.BlockSpec((None, tm, w), lambda b, j: (b, j, 0))
    return pl.pallas_call(
        _row_chains(_post_attn_kernel, [True, True, False, False, False, True, True], tm, TOK_CHAIN),
        grid=(bx, s // tm),
        in_specs=[tok(o.shape[-1]), tok(D_MODEL),
                  pl.BlockSpec((None, 1, N_MOD * D_MODEL), row), _full(g.shape), _full(wo.shape)],
        out_specs=[tok(D_MODEL), tok(D_MODEL)],
        out_shape=[jax.ShapeDtypeStruct(x.shape, F32), jax.ShapeDtypeStruct(x.shape, BF16)],
        compiler_params=_params("arbitrary", "arbitrary"),
        name="out_proj_residual",
    )(o, x, mods_l, g, wo)


def _mlp_kernel(h_ref, x1_ref, m_ref, g_ref, w1_ref, w2_ref, o_ref):
    f = pl.program_id(2)

    @pl.when(f == 0)
    def _():
        o_ref[...] = jnp.zeros_like(o_ref)

    h = h_ref[...]
    for c in range(w1_ref.shape[1] // MLP_FF_CHAIN):
        cols = slice(c * MLP_FF_CHAIN, (c + 1) * MLP_FF_CHAIN)
        u = _dot(h, w1_ref[:, cols].astype(BF16))
        u = jnp.square(jnp.maximum(u, 0.0)).astype(BF16)
        o_ref[...] += _dot(u, w2_ref[cols, :].astype(BF16))

    @pl.when(f == pl.num_programs(2) - 1)
    def _():
        o_ref[...] = x1_ref[...] + _rms(o_ref[...], _mod(m_ref[...], 5) * g_ref[3:4, :])


def _mlp(h2, x1, mods_l, g, w1, w2, layer, latent):
    bx, s, _ = x1.shape
    tm, tf = MLP_TOK_TILE, MLP_FF_TILE
    row = (lambda b, j, f: (b, 0, 0)) if latent else (lambda b, j, f: (CTX_MOD_ROW, 0, 0))
    tok = pl.BlockSpec((None, tm, D_MODEL), lambda b, j, f: (b, j, 0))
    return pl.pallas_call(
        _mlp_kernel,
        grid=(bx, s // tm, D_FF // tf),
        in_specs=[tok, tok, pl.BlockSpec((None, 1, N_MOD * D_MODEL), row), _full(g.shape),
                  pl.BlockSpec((None, D_MODEL, tf), lambda b, j, f: (layer, 0, f)),
                  pl.BlockSpec((None, tf, D_MODEL), lambda b, j, f: (layer, f, 0))],
        out_specs=tok,
        out_shape=jax.ShapeDtypeStruct(x1.shape, F32),
        compiler_params=pltpu.CompilerParams(dimension_semantics=("arbitrary",) * 3,
                                             vmem_limit_bytes=MLP_VMEM_LIMIT),
        name="mlp_relu2",
    )(h2, x1, mods_l, g, w1, w2)


def _rope_tables(s, dim):
    quarter = dim // 4
    t = jnp.arange(s)
    pos = jnp.stack([t // GRID_W, t % GRID_W], axis=-1).astype(F32)
    inv = ROPE_THETA ** (-jnp.arange(quarter, dtype=F32) / quarter)
    ang = pos[:, :, None] * inv
    cos = jnp.broadcast_to(jnp.cos(ang)[:, :, None, :], (s, 2, 2, quarter)).reshape(s, dim)
    sign = jnp.array([-1.0, 1.0], F32)[None, None, :, None]
    sin = (jnp.sin(ang)[:, :, None, :] * sign).reshape(s, dim)
    reps = LANES // dim
    return jnp.tile(cos, (1, reps)), jnp.tile(sin, (1, reps))


def _dup_heads(w, heads, dim):
    lead = w.shape[:-1]
    w = w.reshape(lead + (heads, 1, dim))
    return jnp.broadcast_to(w, lead + (heads, 2, dim)).reshape(lead + (heads * 2 * dim,))


def _nat_dense_bias(rpb, rows):
    heads = rpb.shape[0]
    c = np.arange(GRID_W)
    c0 = np.clip(c - NAT_WIN_C // 2, 0, GRID_W - NAT_WIN_C)
    in_c = (c[None, :] >= c0[:, None]) & (c[None, :] < c0[:, None] + NAT_WIN_C)
    dc = c[None, :] - c[:, None] + NAT_WIN_C - 1
    onehot = (dc[None] == np.arange(2 * NAT_WIN_C - 1)[:, None, None]) & in_c[None]
    toe = jnp.einsum("had,dck->hack", rpb * LOG2E, jnp.asarray(onehot, F32),
                     precision=lax.Precision.HIGHEST)
    toe = jnp.where(jnp.asarray(in_c)[None, None], toe, NEG_INF)
    pad = jnp.full((heads, 1, GRID_W, GRID_W), NEG_INF, F32)
    ext = jnp.concatenate([pad, toe, pad], axis=1)
    pairs = jnp.concatenate([ext[:, :-1], ext[:, 1:]], axis=-1)
    tiles = rows // NAT_TILE_ROWS
    tq, slab = NAT_TILE_ROWS * GRID_W, NAT_KEY_ROWS * GRID_W
    n_off = 2 * NAT_WIN_R
    return pl.pallas_call(
        functools.partial(_nat_bias_kernel, rows),
        grid=(heads,),
        in_specs=[pl.BlockSpec((None, n_off, GRID_W, LANES), lambda h: (h, 0, 0, 0))],
        out_specs=pl.BlockSpec((None, tiles, tq, slab), lambda h: (h, 0, 0, 0)),
        out_shape=jax.ShapeDtypeStruct((heads, tiles, tq, slab), F32),
        compiler_params=_params("arbitrary"),
        name="nat_bias_expand",
    )(pairs)


def _nat_bias_kernel(rows, pairs_ref, o_ref):
    low = lax.broadcasted_iota(jnp.int32, (GRID_W, LANES), 1) < GRID_W
    masked = jnp.full((GRID_W, LANES), NEG_INF, F32)
    for r in range(rows):
        tile, i = divmod(r, NAT_TILE_ROWS)
        r0 = min(max(r - NAT_WIN_R // 2, 0), rows - NAT_WIN_R)
        first0 = min(max(tile * NAT_TILE_ROWS - NAT_WIN_R // 2, 0), rows - NAT_WIN_R)
        ws = min(first0, rows - NAT_KEY_ROWS)
        for jb in range(NAT_KEY_ROWS // 2):
            kr = ws + 2 * jb
            ok_lo, ok_hi = r0 <= kr < r0 + NAT_WIN_R, r0 <= kr + 1 < r0 + NAT_WIN_R
            blk = masked
            if ok_lo or ok_hi:
                blk = pairs_ref[kr - r + NAT_WIN_R]
                if not ok_lo:
                    blk = jnp.where(low, NEG_INF, blk)
                elif not ok_hi:
                    blk = jnp.where(low, blk, NEG_INF)
            o_ref[tile, i * GRID_W:(i + 1) * GRID_W, jb * LANES:(jb + 1) * LANES] = blk


def kernel(x_prompt, x_sample, cache_l0_k, cache_l0_v, cache_l1_ckv, cache_l1_kpe, cache_l2_k, cache_l2_v, cache_l3_k, cache_l3_v, c, c_ctx, ada_w, ada_b, norm_g, mlp_w1, mlp_w2, attn_w_qkv, attn_q_norm, attn_k_norm, attn_w_o, mla_w_in, mla_q_norm, mla_kv_norm, mla_w_uq, mla_w_ukv, mla_w_o, swa_w_qkv, swa_sink, swa_w_o, nat_w_qkv, nat_rpb, nat_w_o):
    nb, seq, d = x_prompt.shape
    db, dseq, _ = x_sample.shape
    past = cache_l0_k.shape[1]
    ctx_b = nb * seq // dseq
    xp = x_prompt.reshape(ctx_b, dseq, d)
    xs = x_sample

    cond = jnp.zeros((COND_ROWS, d), F32).at[:db].set(c).at[CTX_MOD_ROW].set(c_ctx)
    mods = _modulation(cond, ada_w, ada_b).reshape(DEPTH, COND_ROWS, 1, N_MOD * d)

    row = lambda v: v.reshape(1, -1)

    def as_ctx(a):
        return a.reshape(nb, seq, a.shape[-1])

    def as_slab(a):
        return a.reshape(ctx_b, dseq, a.shape[-1])

    def head_major(cache):
        b_, l_, h_, dh = cache.shape
        return cache.transpose(0, 2, 3, 1).reshape(b_, h_ * dh, l_)

    def token_major(state, heads):
        b_, w_, l_ = state.shape
        return state.reshape(b_, heads, w_ // heads, l_).transpose(0, 3, 1, 2)

    def finish(o, x, layer, wo, latent):
        x1, h2 = _post_attn(o, x, mods[layer], norm_g[layer], wo, latent)
        return _mlp(h2, x1, mods[layer], norm_g[layer], mlp_w1, mlp_w2, layer, latent)

    g = norm_g[0]
    w = attn_w_qkv.astype(BF16)
    wo = attn_w_o.astype(BF16)
    consts = [w, row(attn_q_norm), row(attn_k_norm)]
    kvw = ATTN_KV_HEADS * ATTN_HEAD_DIM
    q, k, v, l0_k, l0_v = _run_proj(
        _proj_a_kernel, False, xp, mods[0], g, consts, [],
        [(d, BF16), (kvw, BF16), (kvw, BF16), (kvw, F32), (kvw, F32)], "proj_gqa_context")
    o = _attention_a(as_ctx(q), as_ctx(k), as_ctx(v))
    xp = finish(as_slab(o), xp, 0, wo, False)
    perm = np.arange(ATTN_HEAD_DIM).reshape(2, 2, ATTN_HEAD_DIM // 4).transpose(1, 0, 2).reshape(-1)
    qk_heads = ATTN_HEADS + ATTN_KV_HEADS
    cols = (np.arange(qk_heads)[:, None] * ATTN_HEAD_DIM + perm[None, :]).reshape(-1)
    w_lat = jnp.concatenate([attn_w_qkv[:, cols], attn_w_qkv[:, qk_heads * ATTN_HEAD_DIM:]], axis=1).astype(BF16)
    consts_lat = [w_lat, row(attn_q_norm[perm]), row(attn_k_norm[perm])]
    tables = [t[:, perm] for t in _rope_tables(dseq, ATTN_HEAD_DIM)]
    q, k, v = _run_proj(_proj_a_kernel, True, xs, mods[0], g, consts_lat, tables,
                        [(d, BF16), (kvw, BF16), (kvw, BF16)], "proj_gqa_latent")
    o = _attention_a(q, k, v, cache_l0_k[..., perm].reshape(db, past, kvw).astype(BF16),
                     cache_l0_v.reshape(db, past, kvw).astype(BF16))
    xs = finish(o, xs, 0, wo, True)
    new_l0 = (l0_k.reshape(nb, seq, ATTN_KV_HEADS, ATTN_HEAD_DIM),
              l0_v.reshape(nb, seq, ATTN_KV_HEADS, ATTN_HEAD_DIM))

    g = norm_g[1]
    nq, nkv = MLA_Q_LORA, MLA_KV_LORA
    w_in = jnp.concatenate([mla_w_in[:, :nq + nkv]] + [mla_w_in[:, nq + nkv:]] * (LANES // MLA_ROPE),
                           axis=1).astype(BF16)
    wuq = mla_w_uq.reshape(nq, MLA_HEADS, MLA_NOPE + MLA_ROPE)
    wuq = jnp.concatenate([wuq[:, :, :MLA_NOPE].reshape(nq, -1), wuq[:, :, MLA_NOPE:].reshape(nq, -1)],
                          axis=1).astype(BF16)
    wukv = mla_w_ukv.reshape(nkv, MLA_HEADS, MLA_NOPE + MLA_V_DIM)
    wukv = jnp.concatenate([wukv[:, :, :MLA_NOPE].reshape(nkv, -1), wukv[:, :, MLA_NOPE:].reshape(nkv, -1)],
                           axis=1).astype(BF16)
    wo = mla_w_o.astype(BF16)
    consts = [w_in, row(mla_q_norm), row(mla_kv_norm), wuq, wukv]
    hw = MLA_HEADS * MLA_NOPE
    rw = MLA_HEADS * MLA_ROPE
    outs = [(hw, BF16), (rw, BF16), (hw, BF16), (hw, BF16), (LANES, BF16)]
    qn, qr, kn, v, kp, l1_ckv, l1_kpe = _run_proj(
        _proj_b_kernel, False, xp, mods[1], g, consts, [],
        outs + [(nkv, F32), (MLA_ROPE, F32, "T")], "proj_mla_context", seq)
    o = _attention_b(as_ctx(qn), as_ctx(qr), as_ctx(kn), as_ctx(kp), as_ctx(v))
    xp = finish(as_slab(o), xp, 1, wo, False)
    tables = list(_rope_tables(dseq, MLA_ROPE))
    qn, qr, kn, v, kp = _run_proj(_proj_b_kernel, True, xs, mods[1], g, consts, tables, outs,
                                  "proj_mla_latent")
    knc, vc = _mla_expand(cache_l1_ckv, wukv)
    kpc = jnp.tile(cache_l1_kpe, (1, 1, LANES // MLA_ROPE)).astype(BF16)
    o = _attention_b(qn, qr, kn, kp, v, knc, kpc, vc)
    xs = finish(o, xs, 1, wo, True)
    new_l1 = (l1_ckv.reshape(nb, seq, nkv), l1_kpe.transpose(0, 2, 1))

    g = norm_g[2]
    qw = SWA_HEADS * SWA_HEAD_DIM
    kw = SWA_KV_HEADS * SWA_HEAD_DIM
    wq, wk, wv = swa_w_qkv[:, :qw], swa_w_qkv[:, qw:qw + kw], swa_w_qkv[:, qw + kw:]
    wkd = _dup_heads(wk, SWA_KV_HEADS, SWA_HEAD_DIM)
    wvd = _dup_heads(wv, SWA_KV_HEADS, SWA_HEAD_DIM)
    w_ctx = jnp.concatenate([wq, wk, wv, wkd, wvd], axis=1).astype(BF16)
    w_lat = jnp.concatenate([wq, wkd, wvd], axis=1).astype(BF16)
    wo = swa_w_o.astype(BF16)
    q, kd, vd, l2_k, l2_v = _run_proj(
        _proj_c_kernel, False, xp, mods[2], g, [w_ctx], [],
        [(qw, BF16), (2 * kw, BF16), (2 * kw, BF16), (kw, F32, "T"), (kw, F32, "T")], "proj_swa_context", seq)
    o = _attention_c(swa_sink, as_ctx(q), as_ctx(kd), as_ctx(vd))
    xp = finish(as_slab(o), xp, 2, wo, False)
    tables = list(_rope_tables(dseq, SWA_HEAD_DIM))
    q, kd, vd = _run_proj(_proj_c_kernel, True, xs, mods[2], g, [w_lat], tables,
                          [(qw, BF16), (2 * kw, BF16), (2 * kw, BF16)], "proj_swa_latent")
    o = _attention_c(swa_sink, q, kd, vd, head_major(cache_l2_k), head_major(cache_l2_v))
    xs = finish(o, xs, 2, wo, True)
    new_l2 = (token_major(l2_k, SWA_KV_HEADS), token_major(l2_v, SWA_KV_HEADS))

    g = norm_g[3]
    hw = NAT_HEADS * NAT_HEAD_DIM
    w = nat_w_qkv.astype(BF16)
    wo = nat_w_o.astype(BF16)
    q, k, v, l3_k, l3_v = _run_proj(
        _proj_d_kernel, False, xp, mods[3], g, [w], [],
        [(hw, BF16), (hw, BF16), (hw, BF16), (hw, F32, "T"), (hw, F32, "T")], "proj_nat_context", seq)
    o = _attention_d_ctx(as_ctx(q), as_ctx(k), as_ctx(v))
    xp = finish(as_slab(o), xp, 3, wo, False)
    q, k, v = _run_proj(_proj_d_kernel, True, xs, mods[3], g, [w], [],
                        [(hw, BF16), (hw, BF16), (hw, BF16)], "proj_nat_latent")
    o = _attention_d_lat(q, k, v, head_major(cache_l3_k), head_major(cache_l3_v),
                         _nat_dense_bias(nat_rpb, dseq // GRID_W))
    xs = finish(o, xs, 3, wo, True)
    new_l3 = (token_major(l3_k, NAT_HEADS), token_major(l3_v, NAT_HEADS))

    return (xp.reshape(nb, seq, d), xs) + new_l0 + new_l1 + new_l2 + new_l3
```

```python
import functools

import numpy as np

import jax
import jax.numpy as jnp
from jax import lax
from jax.experimental import pallas as pl
from jax.experimental.pallas import tpu as pltpu

F32 = jnp.float32
BF16 = jnp.bfloat16

D_MODEL = 1024
DEPTH = 4
N_MOD = 6
D_FF = 4 * D_MODEL
GRID_W = 64
ROPE_THETA = 10000.0
NORM_EPS = 1e-6
NEG_INF = -1e30
LOG2E = 1.4426950408889634

ATTN_HEADS, ATTN_KV_HEADS, ATTN_HEAD_DIM = 8, 2, 128
MLA_HEADS, MLA_Q_LORA, MLA_KV_LORA = 16, 384, 256
MLA_NOPE, MLA_ROPE, MLA_V_DIM = 64, 32, 64
MLA_SCALE = (MLA_NOPE + MLA_ROPE) ** -0.5
SWA_HEADS, SWA_KV_HEADS, SWA_HEAD_DIM, SWA_WINDOW = 16, 4, 64, 128
NAT_HEADS, NAT_HEAD_DIM, NAT_WIN_R, NAT_WIN_C = 16, 64, 8, 16

LANES = 128
MXU_COLS = 256
COND_ROWS = 16
CTX_MOD_ROW = 8
VMEM_LIMIT = 48 * 1024 * 1024

TOK_TILE = 512
TOK_CHAIN = 256
MLP_TOK_TILE = 1024
MLP_FF_TILE = 1024
MLP_FF_CHAIN = 1024
MLP_VMEM_LIMIT = 56 * 1024 * 1024
Q_TILE = 256
HEADS_PER_STEP = 16
NAT_TILE_ROWS = Q_TILE // GRID_W
NAT_KEY_ROWS = NAT_WIN_R + NAT_TILE_ROWS


def _params(*sem):
    return pltpu.CompilerParams(dimension_semantics=sem, vmem_limit_bytes=VMEM_LIMIT)


def _full(shape):
    nd = len(shape)
    return pl.BlockSpec(shape, lambda *_: (0,) * nd)


def _rms(x, g):
    return x * lax.rsqrt(jnp.mean(x * x, axis=-1, keepdims=True) + NORM_EPS) * g


def _mod(m, i):
    return m[:, i * D_MODEL:(i + 1) * D_MODEL]


def _premod(x, g, m, sub):
    return _rms(x, g * (1.0 + _mod(m, 3 * sub + 1))) + _mod(m, 3 * sub)


def _rope(x, cos, sin_signed, quarter):
    n = x.shape[-1]
    lane = lax.broadcasted_iota(jnp.int32, x.shape, 1)
    first = ((lane // quarter) % 2) == 0
    partner = jnp.where(first, pltpu.roll(x, n - quarter, 1), pltpu.roll(x, quarter, 1))
    return x * cos + partner * sin_signed


def _dot(a, b):
    return jnp.dot(a, b, preferred_element_type=F32)


def _dot_nt(a, b):
    return lax.dot_general(a, b, (((1,), (1,)), ((), ())), preferred_element_type=F32)


def _with_ones(v):
    return jnp.concatenate([v, jnp.ones_like(v)], axis=1)


def _attend(q, segs, sink=None):
    logits = []
    for seg in segs:
        k, bias, transposed = seg[0], seg[2], len(seg) > 3 and seg[3]
        s = _dot(q, k) if transposed else _dot_nt(q, k)
        logits.append(s if bias is None else s + bias)
    m = logits[0].max(axis=-1, keepdims=True)
    for s in logits[1:]:
        m = jnp.maximum(m, s.max(axis=-1, keepdims=True))
    if sink is not None:
        m = jnp.maximum(m, sink)
    acc = None
    for s, seg in zip(logits, segs):
        p = jnp.exp2(s - m).astype(BF16)
        pv = _dot_nt(p, seg[1]) if len(seg) > 3 and seg[3] else _dot(p, seg[1])
        acc = pv if acc is None else acc + pv
    den = acc[:, LANES:LANES + 1]
    if sink is not None:
        den = den + jnp.exp2(sink - m)
    return acc[:, :LANES] / den


def _attend_group(qs, ks, v1s, sink_logits=None):
    tq = qs[0].shape[0]
    s = jnp.concatenate([_dot_nt(q, k) for q, k in zip(qs, ks)], axis=0)
    if sink_logits is not None:
        s = jnp.concatenate([s, sink_logits], axis=1)
        zeros = jnp.zeros((LANES, LANES), BF16)
        tail = jnp.concatenate([zeros, jnp.ones_like(zeros)], axis=1)
        v1s = [jnp.concatenate([v1, tail], axis=0) for v1 in v1s]
    p = jnp.exp2(s - s.max(axis=-1, keepdims=True)).astype(BF16)
    acc = jnp.concatenate([_dot(p[i * tq:(i + 1) * tq], v1) for i, v1 in enumerate(v1s)], axis=0)
    return acc[:, :LANES] / acc[:, LANES:LANES + 1]


def _split_pair(q):
    low = lax.broadcasted_iota(jnp.int32, q.shape, 1) < (LANES // 2)
    zero = jnp.zeros_like(q)
    return [jnp.where(low, q, zero), jnp.where(low, zero, q)]


def _merge_pair(o0, o1):
    low = lax.broadcasted_iota(jnp.int32, o0.shape, 1) < (LANES // 2)
    return jnp.where(low, o0, o1).astype(BF16)


def _attend_pair(q, segs, sinks=None):
    outs = []
    for j, qj in enumerate(_split_pair(q)):
        segs_j = [(s[0], s[1], None if s[2] is None else s[2][j]) + tuple(s[3:]) for s in segs]
        outs.append(_attend(qj, segs_j, None if sinks is None else sinks[j]))
    return _merge_pair(*outs)


def _mods_kernel(cond_ref, w_ref, b_ref, o_ref):
    cnd = cond_ref[...]
    act = cnd * jax.nn.sigmoid(cnd)
    o_ref[...] = _dot(act.astype(BF16), w_ref[...].astype(BF16)) + b_ref[...]


def _modulation(cond, ada_w, ada_b):
    tn = 1536
    n = N_MOD * D_MODEL
    return pl.pallas_call(
        _mods_kernel,
        grid=(DEPTH, n // tn),
        in_specs=[
            _full((COND_ROWS, D_MODEL)),
            pl.BlockSpec((None, D_MODEL, tn), lambda l, j: (l, 0, j)),
            pl.BlockSpec((None, 1, tn), lambda l, j: (l, 0, j)),
        ],
        out_specs=pl.BlockSpec((None, COND_ROWS, tn), lambda l, j: (l, 0, j)),
        out_shape=jax.ShapeDtypeStruct((DEPTH, COND_ROWS, n), F32),
        compiler_params=_params("arbitrary", "arbitrary"),
        name="adaln_mods",
    )(cond, ada_w, ada_b.reshape(DEPTH, 1, n))


def _proj_a_kernel(latent, x_ref, m_ref, g_ref, w_ref, qn_ref, kn_ref, *rest):
    if latent:
        cos_ref, sin_ref, q_ref, k_ref, v_ref = rest
    else:
        q_ref, k_ref, v_ref, ks_ref, vs_ref = rest
    hd = ATTN_HEAD_DIM
    h = _premod(x_ref[...], g_ref[0:1, :], m_ref[...], 0)
    z = _dot(h.astype(BF16), w_ref[...])
    n_qk = ATTN_HEADS + ATTN_KV_HEADS
    ys = [_rms(z[:, i * hd:(i + 1) * hd], qn_ref[...] if i < ATTN_HEADS else kn_ref[...]) for i in range(n_qk)]
    if not latent:
        for j in range(ATTN_KV_HEADS):
            ks_ref[:, j * hd:(j + 1) * hd] = ys[ATTN_HEADS + j]
    else:
        ys = [y * cos_ref[...] + pltpu.roll(y, hd // 2, 1) * sin_ref[...] for y in ys]
    for i in range(ATTN_HEADS):
        q_ref[:, i * hd:(i + 1) * hd] = (ys[i] * (hd ** -0.5 * LOG2E)).astype(BF16)
    for j in range(ATTN_KV_HEADS):
        k_ref[:, j * hd:(j + 1) * hd] = ys[ATTN_HEADS + j].astype(BF16)
    v = z[:, n_qk * hd:]
    v_ref[...] = v.astype(BF16)
    if not latent:
        vs_ref[...] = v


def _proj_b_kernel(latent, x_ref, m_ref, g_ref, win_ref, qn_ref, kvn_ref, wuq_ref, wukv_ref, *rest):
    if latent:
        cos_ref, sin_ref, qn_o, qr_o, kn_o, v_o, kp_o = rest
    else:
        qn_o, qr_o, kn_o, v_o, kp_o, ckv_s, kpe_s = rest
    nq, nkv = MLA_Q_LORA, MLA_KV_LORA
    nope_w = MLA_HEADS * MLA_NOPE
    scale = MLA_SCALE * LOG2E
    h = _premod(x_ref[...], g_ref[0:1, :], m_ref[...], 0)
    z = _dot(h.astype(BF16), win_ref[...])
    cq = _rms(z[:, :nq], qn_ref[...])
    ckv = _rms(z[:, nq:nq + nkv], kvn_ref[...])
    kp = z[:, nq + nkv:]
    q = _dot(cq.astype(BF16), wuq_ref[...])
    kv = _dot(ckv.astype(BF16), wukv_ref[...])
    if not latent:
        ckv_s[...] = ckv
        _store_transposed(kpe_s, kp, MLA_ROPE)
    qn_o[...] = (q[:, :nope_w] * scale).astype(BF16)
    for i in range(MLA_HEADS * MLA_ROPE // LANES):
        qr = q[:, nope_w + i * LANES:nope_w + (i + 1) * LANES]
        if latent:
            qr = _rope(qr, cos_ref[...], sin_ref[...], MLA_ROPE // 4)
        qr_o[:, i * LANES:(i + 1) * LANES] = (qr * scale).astype(BF16)
    if latent:
        kp = _rope(kp, cos_ref[...], sin_ref[...], MLA_ROPE // 4)
    kp_o[...] = kp.astype(BF16)
    kn_o[...] = kv[:, :nope_w].astype(BF16)
    v_o[...] = kv[:, nope_w:].astype(BF16)


def _proj_c_kernel(latent, x_ref, m_ref, g_ref, w_ref, *rest):
    if latent:
        cos_ref, sin_ref, q_o, kd_o, vd_o = rest
    else:
        q_o, kd_o, vd_o, ks_o, vs_o = rest
    qw = SWA_HEADS * SWA_HEAD_DIM
    kw = SWA_KV_HEADS * SWA_HEAD_DIM
    h = _premod(x_ref[...], g_ref[0:1, :], m_ref[...], 0)
    z = _dot(h.astype(BF16), w_ref[...])
    off = qw
    if not latent:
        _store_transposed(ks_o, z[:, qw:qw + kw], kw)
        _store_transposed(vs_o, z[:, qw + kw:qw + 2 * kw], kw)
        off = qw + 2 * kw
    scale = SWA_HEAD_DIM ** -0.5 * LOG2E
    for i in range(qw // LANES):
        y = z[:, i * LANES:(i + 1) * LANES]
        if latent:
            y = _rope(y, cos_ref[...], sin_ref[...], SWA_HEAD_DIM // 4)
        q_o[:, i * LANES:(i + 1) * LANES] = (y * scale).astype(BF16)
    for i in range(2 * kw // LANES):
        y = z[:, off + i * LANES:off + (i + 1) * LANES]
        if latent:
            y = _rope(y, cos_ref[...], sin_ref[...], SWA_HEAD_DIM // 4)
        kd_o[:, i * LANES:(i + 1) * LANES] = y.astype(BF16)
    vd_o[...] = z[:, off + 2 * kw:].astype(BF16)


def _proj_d_kernel(latent, x_ref, m_ref, g_ref, w_ref, *rest):
    if latent:
        q_o, k_o, v_o = rest
    else:
        q_o, k_o, v_o, ks_o, vs_o = rest
    hw = NAT_HEADS * NAT_HEAD_DIM
    h = _premod(x_ref[...], g_ref[0:1, :], m_ref[...], 0)
    z = _dot(h.astype(BF16), w_ref[...])
    q_o[...] = (z[:, :hw] * (NAT_HEAD_DIM ** -0.5 * LOG2E)).astype(BF16)
    k_o[...] = z[:, hw:2 * hw].astype(BF16)
    v_o[...] = z[:, 2 * hw:].astype(BF16)
    if not latent:
        _store_transposed(ks_o, z[:, hw:2 * hw], hw)
        _store_transposed(vs_o, z[:, 2 * hw:], hw)


def _store_transposed(ref, val, width):
    n, _, seq = ref.shape
    for i in range(n):
        ref[i] = val[i * seq:(i + 1) * seq, :].T[:width, :]


def _run_proj(kernel, latent, x, mods_l, g, consts, tables, outs, name, state_seq=None):
    bx, s, _ = x.shape
    tm = TOK_TILE
    row = (lambda b, j: (b, 0, 0)) if latent else (lambda b, j: (CTX_MOD_ROW, 0, 0))
    in_specs = [
        pl.BlockSpec((None, tm, D_MODEL), lambda b, j: (b, j, 0)),
        pl.BlockSpec((None, 1, N_MOD * D_MODEL), row),
        _full(g.shape),
    ] + [_full(c.shape) for c in consts]
    in_specs += [pl.BlockSpec((tm, LANES), lambda b, j: (j, 0)) for _ in tables]
    out_specs, out_shape = [], []
    for o in outs:
        if len(o) == 2:
            out_specs.append(pl.BlockSpec((None, tm, o[0]), lambda b, j: (b, j, 0)))
            out_shape.append(jax.ShapeDtypeStruct((bx, s, o[0]), o[1]))
        else:
            per_tile = tm // state_seq
            out_specs.append(pl.BlockSpec((per_tile, o[0], state_seq),
                                          lambda b, j: (b * (s // tm) + j, 0, 0)))
            out_shape.append(jax.ShapeDtypeStruct((bx * s // state_seq, o[0], state_seq), o[1]))
    return pl.pallas_call(
        functools.partial(kernel, latent),
        grid=(bx, s // tm),
        in_specs=in_specs,
        out_specs=out_specs,
        out_shape=out_shape,
        compiler_params=_params("arbitrary", "arbitrary"),
        name=name,
    )(x, mods_l, g, *consts, *tables)


def _attn_a_kernel(has_ctx, q_ref, k_ref, v_ref, *rest):
    if has_ctx:
        kc_ref, vc_ref, o_ref = rest
    else:
        (o_ref,) = rest
    group = ATTN_HEADS // ATTN_KV_HEADS
    head_cols = [slice(h * LANES, (h + 1) * LANES) for h in range(ATTN_HEADS)]
    kv_cols = [slice(j * LANES, (j + 1) * LANES) for j in range(ATTN_KV_HEADS)]
    if not has_ctx:
        tq = q_ref.shape[0]
        qs = [jnp.concatenate([q_ref[:, c] for c in head_cols[j * group:(j + 1) * group]], axis=0)
              for j in range(ATTN_KV_HEADS)]
        out = _attend_group(qs, [k_ref[:, c] for c in kv_cols], [_with_ones(v_ref[:, c]) for c in kv_cols])
        for h, c in enumerate(head_cols):
            o_ref[:, c] = out[h * tq:(h + 1) * tq].astype(BF16)
        return
    for j, kc in enumerate(kv_cols):
        segs = [(k_ref[:, kc], _with_ones(v_ref[:, kc]), None),
                (kc_ref[:, kc], _with_ones(vc_ref[:, kc]), None)]
        for c in head_cols[j * group:(j + 1) * group]:
            o_ref[:, c] = _attend(q_ref[:, c], segs).astype(BF16)


def _attention_a(q, k, v, kc=None, vc=None):
    b, s, qw = q.shape
    kl, kw = k.shape[1:]
    tq = min(Q_TILE, s)
    has_ctx = kc is not None
    in_specs = [
        pl.BlockSpec((None, tq, qw), lambda bi, t: (bi, t, 0)),
        pl.BlockSpec((None, kl, kw), lambda bi, t: (bi, 0, 0)),
        pl.BlockSpec((None, kl, kw), lambda bi, t: (bi, 0, 0)),
    ]
    args = [q, k, v]
    if has_ctx:
        cl = kc.shape[1]
        in_specs += [pl.BlockSpec((None, cl, kw), lambda bi, t: (bi, 0, 0))] * 2
        args += [kc, vc]
    return pl.pallas_call(
        functools.partial(_attn_a_kernel, has_ctx),
        grid=(b, s // tq),
        in_specs=in_specs,
        out_specs=pl.BlockSpec((None, tq, qw), lambda bi, t: (bi, t, 0)),
        out_shape=jax.ShapeDtypeStruct(q.shape, BF16),
        compiler_params=_params("arbitrary", "arbitrary"),
        name="attn_gqa_latent" if has_ctx else "attn_gqa_context",
    )(*args)


def _attn_b_kernel(has_ctx, qn_ref, qr_ref, kn_ref, kp_ref, v_ref, *rest):
    if has_ctx:
        knc_ref, kpc_ref, vc_ref, o_ref = rest
    else:
        (o_ref,) = rest
    tq = qn_ref.shape[0]
    lane = lax.broadcasted_iota(jnp.int32, (tq, LANES), 1)
    low = lane < (LANES // 2)
    quarter = lane // MLA_ROPE
    zero = jnp.zeros((tq, LANES), BF16)
    kp = kp_ref[...]
    kpc = kpc_ref[...] if has_ctx else None
    pairs_per_rope = LANES // MLA_ROPE // 2
    n_pairs = HEADS_PER_STEP // 2
    pair_cols = [slice(p * LANES, (p + 1) * LANES) for p in range(n_pairs)]

    def pair_lhs(p):
        qn = qn_ref[:, pair_cols[p]]
        rg = p // pairs_per_rope
        qr = qr_ref[:, rg * LANES:(rg + 1) * LANES]
        return [jnp.concatenate([jnp.where(low if j == 0 else ~low, qn, zero),
                                 jnp.where(quarter == 2 * (p % pairs_per_rope) + j, qr, zero)], axis=1)
                for j in range(2)]

    if not has_ctx:
        qs = [jnp.concatenate(pair_lhs(p), axis=0) for p in range(n_pairs)]
        out = _attend_group(qs, [jnp.concatenate([kn_ref[:, c], kp], axis=1) for c in pair_cols],
                            [_with_ones(v_ref[:, c]) for c in pair_cols])
        for p, c in enumerate(pair_cols):
            o_ref[:, c] = _merge_pair(out[2 * p * tq:(2 * p + 1) * tq], out[(2 * p + 1) * tq:(2 * p + 2) * tq])
        return
    for p, cols in enumerate(pair_cols):
        segs = [(jnp.concatenate([kn_ref[:, cols], kp], axis=1), _with_ones(v_ref[:, cols]), None),
                (jnp.concatenate([knc_ref[:, cols], kpc], axis=1), _with_ones(vc_ref[:, cols]), None)]
        o_ref[:, cols] = _merge_pair(*[_attend(lhs, segs) for lhs in pair_lhs(p)])


def _attention_b(qn, qr, kn, kp, v, knc=None, kpc=None, vc=None):
    b, s, _ = qn.shape
    kl = kn.shape[1]
    tq = min(Q_TILE, s)
    has_ctx = knc is not None
    gw = HEADS_PER_STEP * MLA_NOPE
    blk = lambda rows, w, fn: pl.BlockSpec((None, rows, w), fn)
    in_specs = [
        blk(tq, gw, lambda bi, g, t: (bi, t, g)),
        blk(tq, HEADS_PER_STEP * MLA_ROPE, lambda bi, g, t: (bi, t, g)),
        blk(kl, gw, lambda bi, g, t: (bi, 0, g)),
        blk(kl, LANES, lambda bi, g, t: (bi, 0, 0)),
        blk(kl, gw, lambda bi, g, t: (bi, 0, g)),
    ]
    args = [qn, qr, kn, kp, v]
    if has_ctx:
        cl = knc.shape[1]
        in_specs += [
            blk(cl, gw, lambda bi, g, t: (bi, 0, g)),
            blk(cl, LANES, lambda bi, g, t: (bi, 0, 0)),
            blk(cl, gw, lambda bi, g, t: (bi, 0, g)),
        ]
        args += [knc, kpc, vc]
    return pl.pallas_call(
        functools.partial(_attn_b_kernel, has_ctx),
        grid=(b, MLA_HEADS // HEADS_PER_STEP, s // tq),
        in_specs=in_specs,
        out_specs=blk(tq, gw, lambda bi, g, t: (bi, t, g)),
        out_shape=jax.ShapeDtypeStruct(qn.shape, BF16),
        compiler_params=_params("arbitrary", "arbitrary", "arbitrary"),
        name="attn_mla_latent" if has_ctx else "attn_mla_context",
    )(*args)


def _attn_c_kernel(latent, sink_ref, q_ref, kd_ref, vd_ref, *rest):
    if latent:
        kc_ref, vc_ref, o_ref = rest
    else:
        (o_ref,) = rest
    tq = q_ref.shape[0]
    group = SWA_HEADS // SWA_KV_HEADS
    kv_per_step = HEADS_PER_STEP // group
    head0 = pl.program_id(1) * HEADS_PER_STEP
    if latent:
        s_len = kd_ref.shape[0]
        win = tq + 2 * SWA_WINDOW
        t0 = pl.program_id(2) * tq
        ws = pl.multiple_of(jnp.clip(t0 - SWA_WINDOW, 0, s_len - win), LANES)
        qpos = t0 + lax.broadcasted_iota(jnp.int32, (tq, win), 0)
        kpos = ws + lax.broadcasted_iota(jnp.int32, (tq, win), 1)
        bias = jnp.where(jnp.abs(qpos - kpos) <= SWA_WINDOW, 0.0, NEG_INF).astype(F32)
    pairs_per_kv = group // 2
    pair_cols = [slice(p * LANES, (p + 1) * LANES) for p in range(HEADS_PER_STEP // 2)]
    kv_cols = [slice(j * LANES, (j + 1) * LANES) for j in range(kv_per_step)]
    if not latent:
        qs = [jnp.concatenate([h for c in pair_cols[j * pairs_per_kv:(j + 1) * pairs_per_kv]
                               for h in _split_pair(q_ref[:, c])], axis=0) for j in range(kv_per_step)]
        out = _attend_group(qs, [kd_ref[:, c] for c in kv_cols], [_with_ones(vd_ref[:, c]) for c in kv_cols],
                            sink_ref[...])
        for p, c in enumerate(pair_cols):
            o_ref[:, c] = _merge_pair(out[2 * p * tq:(2 * p + 1) * tq], out[(2 * p + 1) * tq:(2 * p + 2) * tq])
        return
    sinks = [sink_ref[head0 + h] * LOG2E for h in range(HEADS_PER_STEP)]
    hd = SWA_HEAD_DIM
    for j, kc in enumerate(kv_cols):
        kct = kc_ref[j * hd:(j + 1) * hd, :].astype(BF16)
        vct = vc_ref[j * hd:(j + 1) * hd, :].astype(BF16)
        segs = [(kd_ref[pl.ds(ws, win), kc], _with_ones(vd_ref[pl.ds(ws, win), kc]), (bias, bias)),
                (jnp.concatenate([kct, kct], axis=0),
                 jnp.concatenate([vct, vct, jnp.ones((LANES, vct.shape[1]), BF16)], axis=0), None, True)]
        for p in range(j * pairs_per_kv, (j + 1) * pairs_per_kv):
            o_ref[:, pair_cols[p]] = _attend_pair(q_ref[:, pair_cols[p]], segs, sinks[2 * p:2 * p + 2])


def _attention_c(sink, q, kd, vd, kc=None, vc=None):
    b, s, _ = q.shape
    kl = kd.shape[1]
    tq = min(Q_TILE, s)
    latent = kc is not None
    gw = HEADS_PER_STEP * SWA_HEAD_DIM
    kvw = 2 * SWA_HEAD_DIM * HEADS_PER_STEP // (SWA_HEADS // SWA_KV_HEADS)
    if latent:
        sink_spec = pl.BlockSpec(memory_space=pltpu.SMEM)
    else:
        rows = jnp.repeat(sink * LOG2E, tq).reshape(SWA_HEADS // HEADS_PER_STEP, HEADS_PER_STEP * tq, 1)
        lane0 = jnp.arange(LANES)[None, None, :] == 0
        sink = jnp.where(lane0, rows, NEG_INF)
        sink_spec = pl.BlockSpec((None, HEADS_PER_STEP * tq, LANES), lambda bi, g, t: (g, 0, 0))
    in_specs = [
        sink_spec,
        pl.BlockSpec((None, tq, gw), lambda bi, g, t: (bi, t, g)),
        pl.BlockSpec((None, kl, kvw), lambda bi, g, t: (bi, 0, g)),
        pl.BlockSpec((None, kl, kvw), lambda bi, g, t: (bi, 0, g)),
    ]
    args = [sink, q, kd, vd]
    if latent:
        cw, cl = kc.shape[1] * HEADS_PER_STEP // SWA_HEADS, kc.shape[2]
        in_specs += [pl.BlockSpec((None, cw, cl), lambda bi, g, t: (bi, g, 0))] * 2
        args += [kc, vc]
    return pl.pallas_call(
        functools.partial(_attn_c_kernel, latent),
        grid=(b, SWA_HEADS // HEADS_PER_STEP, s // tq),
        in_specs=in_specs,
        out_specs=pl.BlockSpec((None, tq, gw), lambda bi, g, t: (bi, t, g)),
        out_shape=jax.ShapeDtypeStruct(q.shape, BF16),
        compiler_params=_params("arbitrary", "arbitrary", "arbitrary"),
        name="attn_swa_latent" if latent else "attn_swa_context",
    )(*args)


def _attn_d_ctx_kernel(q_ref, k_ref, v_ref, o_ref):
    tq = q_ref.shape[0]
    pair_cols = [slice(p * LANES, (p + 1) * LANES) for p in range(HEADS_PER_STEP // 2)]
    qs = [jnp.concatenate(_split_pair(q_ref[:, c]), axis=0) for c in pair_cols]
    out = _attend_group(qs, [k_ref[:, c] for c in pair_cols], [_with_ones(v_ref[:, c]) for c in pair_cols])
    for p, c in enumerate(pair_cols):
        o_ref[:, c] = _merge_pair(out[2 * p * tq:(2 * p + 1) * tq], out[(2 * p + 1) * tq:(2 * p + 2) * tq])


def _attention_d_ctx(q, k, v):
    b, s, _ = q.shape
    gw = HEADS_PER_STEP * NAT_HEAD_DIM
    blk = pl.BlockSpec((None, s, gw), lambda bi, g: (bi, 0, g))
    return pl.pallas_call(
        _attn_d_ctx_kernel,
        grid=(b, NAT_HEADS // HEADS_PER_STEP),
        in_specs=[blk, blk, blk],
        out_specs=blk,
        out_shape=jax.ShapeDtypeStruct(q.shape, BF16),
        compiler_params=_params("arbitrary", "arbitrary"),
        name="attn_nat_context",
    )(q, k, v)


def _nat_window_start(first_row, rows):
    r0 = jnp.clip(first_row - NAT_WIN_R // 2, 0, rows - NAT_WIN_R)
    return jnp.minimum(r0, rows - NAT_KEY_ROWS)


def _attn_d_lat_kernel(q_ref, k_ref, v_ref, kc_ref, vc_ref, bias_ref, o_ref):
    rows = k_ref.shape[0] // GRID_W
    slab = NAT_KEY_ROWS * GRID_W
    start = _nat_window_start(pl.program_id(0) * NAT_TILE_ROWS, rows) * GRID_W
    start = pl.multiple_of(start, GRID_W)
    ones = jnp.ones((LANES, kc_ref.shape[1]), BF16)
    for p in range(HEADS_PER_STEP // 2):
        cols = slice(p * LANES, (p + 1) * LANES)
        segs = [(k_ref[pl.ds(start, slab), cols], _with_ones(v_ref[pl.ds(start, slab), cols]),
                 (bias_ref[2 * p], bias_ref[2 * p + 1])),
                (kc_ref[cols, :].astype(BF16),
                 jnp.concatenate([vc_ref[cols, :].astype(BF16), ones], axis=0), None, True)]
        o_ref[:, cols] = _attend_pair(q_ref[:, cols], segs)


def _attention_d_lat(q, k, v, kc, vc, bias):
    b, s, _ = q.shape
    cl = kc.shape[2]
    tq = NAT_TILE_ROWS * GRID_W
    slab = NAT_KEY_ROWS * GRID_W
    gw = HEADS_PER_STEP * NAT_HEAD_DIM
    return pl.pallas_call(
        _attn_d_lat_kernel,
        grid=(s // tq, NAT_HEADS // HEADS_PER_STEP, b),
        in_specs=[
            pl.BlockSpec((None, tq, gw), lambda t, g, bi: (bi, t, g)),
            pl.BlockSpec((None, s, gw), lambda t, g, bi: (bi, 0, g)),
            pl.BlockSpec((None, s, gw), lambda t, g, bi: (bi, 0, g)),
            pl.BlockSpec((None, gw, cl), lambda t, g, bi: (bi, g, 0)),
            pl.BlockSpec((None, gw, cl), lambda t, g, bi: (bi, g, 0)),
            pl.BlockSpec((HEADS_PER_STEP, None, tq, slab), lambda t, g, bi: (g, t, 0, 0)),
        ],
        out_specs=pl.BlockSpec((None, tq, gw), lambda t, g, bi: (bi, t, g)),
        out_shape=jax.ShapeDtypeStruct(q.shape, BF16),
        compiler_params=_params("arbitrary", "arbitrary", "arbitrary"),
        name="attn_nat_latent",
    )(q, k, v, kc, vc, bias)


def _mla_expand_kernel(c_ref, w_ref, kn_ref, v_ref):
    kv = _dot(c_ref[...].astype(BF16), w_ref[...])
    half = kv.shape[1] // 2
    kn_ref[...] = kv[:, :half].astype(BF16)
    v_ref[...] = kv[:, half:].astype(BF16)


def _mla_expand(ckv, w_ukv):
    b, l, c = ckv.shape
    n = w_ukv.shape[1] // 2
    out = pl.BlockSpec((None, l, n), lambda bi: (bi, 0, 0))
    return pl.pallas_call(
        _mla_expand_kernel,
        grid=(b,),
        in_specs=[pl.BlockSpec((None, l, c), lambda bi: (bi, 0, 0)), _full(w_ukv.shape)],
        out_specs=[out, out],
        out_shape=[jax.ShapeDtypeStruct((b, l, n), BF16)] * 2,
        compiler_params=_params("arbitrary"),
        name="mla_expand_cache",
    )(ckv, w_ukv)


def _finish_kernel(o_ref, x_ref, m_ref, g_ref, wo_ref, w1_ref, w2_ref, y_ref, x1_ref, h2_ref):
    f = pl.program_id(2)

    @pl.when(f == 0)
    def _():
        m = m_ref[...]
        for c in range(y_ref.shape[0] // TOK_CHAIN):
            rows = slice(c * TOK_CHAIN, (c + 1) * TOK_CHAIN)
            a = _dot(o_ref[rows, :], wo_ref[...])
            x1 = x_ref[rows, :] + _rms(a, _mod(m, 2) * g_ref[1:2, :])
            x1_ref[rows, :] = x1
            h2_ref[rows, :] = _premod(x1, g_ref[2:3, :], m, 1).astype(BF16)
        y_ref[...] = jnp.zeros_like(y_ref)

    h = h2_ref[...]
    for c in range(w1_ref.shape[1] // MLP_FF_CHAIN):
        cols = slice(c * MLP_FF_CHAIN, (c + 1) * MLP_FF_CHAIN)
        u = _dot(h, w1_ref[:, cols].astype(BF16))
        u = jnp.square(jnp.maximum(u, 0.0)).astype(BF16)
        y_ref[...] += _dot(u, w2_ref[cols, :].astype(BF16))

    @pl.when(f == pl.num_programs(2) - 1)
    def _():
        y_ref[...] = x1_ref[...] + _rms(y_ref[...], _mod(m_ref[...], 5) * g_ref[3:4, :])


def _finish_layer(o, x, mods_l, g, wo, w1, w2, layer, latent):
    bx, s, _ = x.shape
    tm, tf = MLP_TOK_TILE, MLP_FF_TILE
    row = (lambda b, j, f: (b, 0, 0)) if latent else (lambda b, j, f: (CTX_MOD_ROW, 0, 0))
    tok = lambda w: pl.BlockSpec((None, tm, w), lambda b, j, f: (b, j, 0))
    return pl.pallas_call(
        _finish_kernel,
        grid=(bx, s // tm, D_FF // tf),
        in_specs=[tok(o.shape[-1]), tok(D_MODEL),
                  pl.BlockSpec((None, 1, N_MOD * D_MODEL), row), _full(g.shape), _full(wo.shape),
                  pl.BlockSpec((None, D_MODEL, tf), lambda b, j, f: (layer, 0, f)),
                  pl.BlockSpec((None, tf, D_MODEL), lambda b, j, f: (layer, f, 0))],
        out_specs=tok(D_MODEL),
        out_shape=jax.ShapeDtypeStruct(x.shape, F32),
        scratch_shapes=[pltpu.VMEM((tm, D_MODEL), F32), pltpu.VMEM((tm, D_MODEL), BF16)],
        compiler_params=pltpu.CompilerParams(dimension_semantics=("arbitrary",) * 3,
                                             vmem_limit_bytes=MLP_VMEM_LIMIT),
        name="out_proj_mlp",
    )(o, x, mods_l, g, wo, w1, w2)


def _rope_tables(s, dim):
    quarter = dim // 4
    t = jnp.arange(s)
    pos = jnp.stack([t // GRID_W, t % GRID_W], axis=-1).astype(F32)
    inv = ROPE_THETA ** (-jnp.arange(quarter, dtype=F32) / quarter)
    ang = pos[:, :, None] * inv
    cos = jnp.broadcast_to(jnp.cos(ang)[:, :, None, :], (s, 2, 2, quarter)).reshape(s, dim)
    sign = jnp.array([-1.0, 1.0], F32)[None, None, :, None]
    sin = (jnp.sin(ang)[:, :, None, :] * sign).reshape(s, dim)
    reps = LANES // dim
    return jnp.tile(cos, (1, reps)), jnp.tile(sin, (1, reps))


def _dup_heads(w, heads, dim):
    lead = w.shape[:-1]
    w = w.reshape(lead + (heads, 1, dim))
    return jnp.broadcast_to(w, lead + (heads, 2, dim)).reshape(lead + (heads * 2 * dim,))


def _nat_dense_bias(rpb, rows):
    heads = rpb.shape[0]
    c = np.arange(GRID_W)
    c0 = np.clip(c - NAT_WIN_C // 2, 0, GRID_W - NAT_WIN_C)
    in_c = (c[None, :] >= c0[:, None]) & (c[None, :] < c0[:, None] + NAT_WIN_C)
    dc = c[None, :] - c[:, None] + NAT_WIN_C - 1
    onehot = (dc[None] == np.arange(2 * NAT_WIN_C - 1)[:, None, None]) & in_c[None]
    toe = jnp.einsum("had,dck->hack", rpb * LOG2E, jnp.asarray(onehot, F32),
                     precision=lax.Precision.HIGHEST)
    toe = jnp.where(jnp.asarray(in_c)[None, None], toe, NEG_INF)
    pad = jnp.full((heads, 1, GRID_W, GRID_W), NEG_INF, F32)
    ext = jnp.concatenate([pad, toe, pad], axis=1)
    pairs = jnp.concatenate([ext[:, :-1], ext[:, 1:]], axis=-1)
    tiles = rows // NAT_TILE_ROWS
    tq, slab = NAT_TILE_ROWS * GRID_W, NAT_KEY_ROWS * GRID_W
    n_off = 2 * NAT_WIN_R
    return pl.pallas_call(
        functools.partial(_nat_bias_kernel, rows),
        grid=(heads,),
        in_specs=[pl.BlockSpec((None, n_off, GRID_W, LANES), lambda h: (h, 0, 0, 0))],
        out_specs=pl.BlockSpec((None, tiles, tq, slab), lambda h: (h, 0, 0, 0)),
        out_shape=jax.ShapeDtypeStruct((heads, tiles, tq, slab), F32),
        compiler_params=_params("arbitrary"),
        name="nat_bias_expand",
    )(pairs)


def _nat_bias_kernel(rows, pairs_ref, o_ref):
    low = lax.broadcasted_iota(jnp.int32, (GRID_W, LANES), 1) < GRID_W
    masked = jnp.full((GRID_W, LANES), NEG_INF, F32)
    for r in range(rows):
        tile, i = divmod(r, NAT_TILE_ROWS)
        r0 = min(max(r - NAT_WIN_R // 2, 0), rows - NAT_WIN_R)
        first0 = min(max(tile * NAT_TILE_ROWS - NAT_WIN_R // 2, 0), rows - NAT_WIN_R)
        ws = min(first0, rows - NAT_KEY_ROWS)
        for jb in range(NAT_KEY_ROWS // 2):
            kr = ws + 2 * jb
            ok_lo, ok_hi = r0 <= kr < r0 + NAT_WIN_R, r0 <= kr + 1 < r0 + NAT_WIN_R
            blk = masked
            if ok_lo or ok_hi:
                blk = pairs_ref[kr - r + NAT_WIN_R]
                if not ok_lo:
                    blk = jnp.where(low, NEG_INF, blk)
                elif not ok_hi:
                    blk = jnp.where(low, blk, NEG_INF)
            o_ref[tile, i * GRID_W:(i + 1) * GRID_W, jb * LANES:(jb + 1) * LANES] = blk


def kernel(x_prompt, x_sample, cache_l0_k, cache_l0_v, cache_l1_ckv, cache_l1_kpe, cache_l2_k, cache_l2_v, cache_l3_k, cache_l3_v, c, c_ctx, ada_w, ada_b, norm_g, mlp_w1, mlp_w2, attn_w_qkv, attn_q_norm, attn_k_norm, attn_w_o, mla_w_in, mla_q_norm, mla_kv_norm, mla_w_uq, mla_w_ukv, mla_w_o, swa_w_qkv, swa_sink, swa_w_o, nat_w_qkv, nat_rpb, nat_w_o):
    nb, seq, d = x_prompt.shape
    db, dseq, _ = x_sample.shape
    past = cache_l0_k.shape[1]
    ctx_b = nb * seq // dseq
    xp = x_prompt.reshape(ctx_b, dseq, d)
    xs = x_sample

    cond = jnp.zeros((COND_ROWS, d), F32).at[:db].set(c).at[CTX_MOD_ROW].set(c_ctx)
    mods = _modulation(cond, ada_w, ada_b).reshape(DEPTH, COND_ROWS, 1, N_MOD * d)

    row = lambda v: v.reshape(1, -1)

    def as_ctx(a):
        return a.reshape(nb, seq, a.shape[-1])

    def as_slab(a):
        return a.reshape(ctx_b, dseq, a.shape[-1])

    def head_major(cache):
        b_, l_, h_, dh = cache.shape
        return cache.transpose(0, 2, 3, 1).reshape(b_, h_ * dh, l_)

    def token_major(state, heads):
        b_, w_, l_ = state.shape
        return state.reshape(b_, heads, w_ // heads, l_).transpose(0, 3, 1, 2)

    def finish(o, x, layer, wo, latent):
        return _finish_layer(o, x, mods[layer], norm_g[layer], wo, mlp_w1, mlp_w2, layer, latent)

    g = norm_g[0]
    w = attn_w_qkv.astype(BF16)
    wo = attn_w_o.astype(BF16)
    consts = [w, row(attn_q_norm), row(attn_k_norm)]
    kvw = ATTN_KV_HEADS * ATTN_HEAD_DIM
    q, k, v, l0_k, l0_v = _run_proj(
        _proj_a_kernel, False, xp, mods[0], g, consts, [],
        [(d, BF16), (kvw, BF16), (kvw, BF16), (kvw, F32), (kvw, F32)], "proj_gqa_context")
    o = _attention_a(as_ctx(q), as_ctx(k), as_ctx(v))
    xp = finish(as_slab(o), xp, 0, wo, False)
    perm = np.arange(ATTN_HEAD_DIM).reshape(2, 2, ATTN_HEAD_DIM // 4).transpose(1, 0, 2).reshape(-1)
    qk_heads = ATTN_HEADS + ATTN_KV_HEADS
    cols = (np.arange(qk_heads)[:, None] * ATTN_HEAD_DIM + perm[None, :]).reshape(-1)
    w_lat = jnp.concatenate([attn_w_qkv[:, cols], attn_w_qkv[:, qk_heads * ATTN_HEAD_DIM:]], axis=1).astype(BF16)
    consts_lat = [w_lat, row(attn_q_norm[perm]), row(attn_k_norm[perm])]
    tables = [t[:, perm] for t in _rope_tables(dseq, ATTN_HEAD_DIM)]
    q, k, v = _run_proj(_proj_a_kernel, True, xs, mods[0], g, consts_lat, tables,
                        [(d, BF16), (kvw, BF16), (kvw, BF16)], "proj_gqa_latent")
    o = _attention_a(q, k, v, cache_l0_k[..., perm].reshape(db, past, kvw).astype(BF16),
                     cache_l0_v.reshape(db, past, kvw).astype(BF16))
    xs = finish(o, xs, 0, wo, True)
    new_l0 = (l0_k.reshape(nb, seq, ATTN_KV_HEADS, ATTN_HEAD_DIM),
              l0_v.reshape(nb, seq, ATTN_KV_HEADS, ATTN_HEAD_DIM))

    g = norm_g[1]
    nq, nkv = MLA_Q_LORA, MLA_KV_LORA
    w_in = jnp.concatenate([mla_w_in[:, :nq + nkv]] + [mla_w_in[:, nq + nkv:]] * (LANES // MLA_ROPE),
                           axis=1).astype(BF16)
    wuq = mla_w_uq.reshape(nq, MLA_HEADS, MLA_NOPE + MLA_ROPE)
    wuq = jnp.concatenate([wuq[:, :, :MLA_NOPE].reshape(nq, -1), wuq[:, :, MLA_NOPE:].reshape(nq, -1)],
                          axis=1).astype(BF16)
    wukv = mla_w_ukv.reshape(nkv, MLA_HEADS, MLA_NOPE + MLA_V_DIM)
    wukv = jnp.concatenate([wukv[:, :, :MLA_NOPE].reshape(nkv, -1), wukv[:, :, MLA_NOPE:].reshape(nkv, -1)],
                           axis=1).astype(BF16)
    wo = mla_w_o.astype(BF16)
    consts = [w_in, row(mla_q_norm), row(mla_kv_norm), wuq, wukv]
    hw = MLA_HEADS * MLA_NOPE
    rw = MLA_HEADS * MLA_ROPE
    outs = [(hw, BF16), (rw, BF16), (hw, BF16), (hw, BF16), (LANES, BF16)]
    qn, qr, kn, v, kp, l1_ckv, l1_kpe = _run_proj(
        _proj_b_kernel, False, xp, mods[1], g, consts, [],
        outs + [(nkv, F32), (MLA_ROPE, F32, "T")], "proj_mla_context", seq)
    o = _attention_b(as_ctx(qn), as_ctx(qr), as_ctx(kn), as_ctx(kp), as_ctx(v))
    xp = finish(as_slab(o), xp, 1, wo, False)
    tables = list(_rope_tables(dseq, MLA_ROPE))
    qn, qr, kn, v, kp = _run_proj(_proj_b_kernel, True, xs, mods[1], g, consts, tables, outs,
                                  "proj_mla_latent")
    knc, vc = _mla_expand(cache_l1_ckv, wukv)
    kpc = jnp.tile(cache_l1_kpe, (1, 1, LANES // MLA_ROPE)).astype(BF16)
    o = _attention_b(qn, qr, kn, kp, v, knc, kpc, vc)
    xs = finish(o, xs, 1, wo, True)
    new_l1 = (l1_ckv.reshape(nb, seq, nkv), l1_kpe.transpose(0, 2, 1))

    g = norm_g[2]
    qw = SWA_HEADS * SWA_HEAD_DIM
    kw = SWA_KV_HEADS * SWA_HEAD_DIM
    wq, wk, wv = swa_w_qkv[:, :qw], swa_w_qkv[:, qw:qw + kw], swa_w_qkv[:, qw + kw:]
    wkd = _dup_heads(wk, SWA_KV_HEADS, SWA_HEAD_DIM)
    wvd = _dup_heads(wv, SWA_KV_HEADS, SWA_HEAD_DIM)
    w_ctx = jnp.concatenate([wq, wk, wv, wkd, wvd], axis=1).astype(BF16)
    w_lat = jnp.concatenate([wq, wkd, wvd], axis=1).astype(BF16)
    wo = swa_w_o.astype(BF16)
    q, kd, vd, l2_k, l2_v = _run_proj(
        _proj_c_kernel, False, xp, mods[2], g, [w_ctx], [],
        [(qw, BF16), (2 * kw, BF16), (2 * kw, BF16), (kw, F32, "T"), (kw, F32, "T")], "proj_swa_context", seq)
    o = _attention_c(swa_sink, as_ctx(q), as_ctx(kd), as_ctx(vd))
    xp = finish(as_slab(o), xp, 2, wo, False)
    tables = list(_rope_tables(dseq, SWA_HEAD_DIM))
    q, kd, vd = _run_proj(_proj_c_kernel, True, xs, mods[2], g, [w_lat], tables,
                          [(qw, BF16), (2 * kw, BF16), (2 * kw, BF16)], "proj_swa_latent")
    o = _attention_c(swa_sink, q, kd, vd, head_major(cache_l2_k), head_major(cache_l2_v))
    xs = finish(o, xs, 2, wo, True)
    new_l2 = (token_major(l2_k, SWA_KV_HEADS), token_major(l2_v, SWA_KV_HEADS))

    g = norm_g[3]
    hw = NAT_HEADS * NAT_HEAD_DIM
    w = nat_w_qkv.astype(BF16)
    wo = nat_w_o.astype(BF16)
    q, k, v, l3_k, l3_v = _run_proj(
        _proj_d_kernel, False, xp, mods[3], g, [w], [],
        [(hw, BF16), (hw, BF16), (hw, BF16), (hw, F32, "T"), (hw, F32, "T")], "proj_nat_context", seq)
    o = _attention_d_ctx(as_ctx(q), as_ctx(k), as_ctx(v))
    xp = finish(as_slab(o), xp, 3, wo, False)
    q, k, v = _run_proj(_proj_d_kernel, True, xs, mods[3], g, [w], [],
                        [(hw, BF16), (hw, BF16), (hw, BF16)], "proj_nat_latent")
    o = _attention_d_lat(q, k, v, head_major(cache_l3_k), head_major(cache_l3_v),
                         _nat_dense_bias(nat_rpb, dseq // GRID_W))
    xs = finish(o, xs, 3, wo, True)
    new_l3 = (token_major(l3_k, NAT_HEADS), token_major(l3_v, NAT_HEADS))

    return (xp.reshape(nb, seq, d), xs) + new_l0 + new_l1 + new_l2 + new_l3
```

```python
import functools

import numpy as np

import jax
import jax.numpy as jnp
from jax import lax
from jax.experimental import pallas as pl
from jax.experimental.pallas import tpu as pltpu

F32 = jnp.float32
BF16 = jnp.bfloat16

D_MODEL = 1024
DEPTH = 4
N_MOD = 6
D_FF = 4 * D_MODEL
GRID_W = 64
ROPE_THETA = 10000.0
NORM_EPS = 1e-6
NEG_INF = -1e30
LOG2E = 1.4426950408889634

ATTN_HEADS, ATTN_KV_HEADS, ATTN_HEAD_DIM = 8, 2, 128
MLA_HEADS, MLA_Q_LORA, MLA_KV_LORA = 16, 384, 256
MLA_NOPE, MLA_ROPE, MLA_V_DIM = 64, 32, 64
MLA_SCALE = (MLA_NOPE + MLA_ROPE) ** -0.5
SWA_HEADS, SWA_KV_HEADS, SWA_HEAD_DIM, SWA_WINDOW = 16, 4, 64, 128
NAT_HEADS, NAT_HEAD_DIM, NAT_WIN_R, NAT_WIN_C = 16, 64, 8, 16

LANES = 128
COND_ROWS = 16
CTX_MOD_ROW = 8
VMEM_LIMIT = 48 * 1024 * 1024

TOK_TILE = 512
TOK_CHAIN = 256
MLP_TOK_TILE = 1024
MLP_FF_TILE = 1024
MLP_FF_CHAIN = 1024
MLP_VMEM_LIMIT = 56 * 1024 * 1024
Q_TILE = 256
LATENT_Q_ROWS = 512
HEADS_PER_STEP = 16
NAT_TILE_ROWS = Q_TILE // GRID_W
NAT_KEY_ROWS = NAT_WIN_R + NAT_TILE_ROWS


def _params(*sem):
    return pltpu.CompilerParams(dimension_semantics=sem, vmem_limit_bytes=VMEM_LIMIT)


def _full(shape):
    nd = len(shape)
    return pl.BlockSpec(shape, lambda *_: (0,) * nd)


def _rms(x, g):
    return x * lax.rsqrt(jnp.mean(x * x, axis=-1, keepdims=True) + NORM_EPS) * g


def _mod(m, i):
    return m[:, i * D_MODEL:(i + 1) * D_MODEL]


def _premod(x, g, m, sub):
    return _rms(x, g * (1.0 + _mod(m, 3 * sub + 1))) + _mod(m, 3 * sub)


def _rope(x, cos, sin_signed, quarter):
    n = x.shape[-1]
    lane = lax.broadcasted_iota(jnp.int32, x.shape, 1)
    first = ((lane // quarter) % 2) == 0
    partner = jnp.where(first, pltpu.roll(x, n - quarter, 1), pltpu.roll(x, quarter, 1))
    return x * cos + partner * sin_signed


def _dot(a, b):
    return jnp.dot(a, b, preferred_element_type=F32)


def _dot_nt(a, b):
    return lax.dot_general(a, b, (((1,), (1,)), ((), ())), preferred_element_type=F32)


def _with_ones(v):
    return jnp.concatenate([v, jnp.ones_like(v)], axis=1)


def _attend(q, segs, sink=None):
    logits = []
    for seg in segs:
        k, bias, transposed = seg[0], seg[2], len(seg) > 3 and seg[3]
        s = _dot(q, k) if transposed else _dot_nt(q, k)
        logits.append(s if bias is None else s + bias)
    m = logits[0].max(axis=-1, keepdims=True)
    for s in logits[1:]:
        m = jnp.maximum(m, s.max(axis=-1, keepdims=True))
    if sink is not None:
        m = jnp.maximum(m, sink)
    acc = None
    for s, seg in zip(logits, segs):
        p = jnp.exp2(s - m).astype(BF16)
        pv = _dot_nt(p, seg[1]) if len(seg) > 3 and seg[3] else _dot(p, seg[1])
        acc = pv if acc is None else acc + pv
    den = acc[:, LANES:LANES + 1]
    if sink is not None:
        den = den + jnp.exp2(sink - m)
    return acc[:, :LANES] / den


def _attend_group(qs, ks, v1s, sink_logits=None):
    tq = qs[0].shape[0]
    s = jnp.concatenate([_dot_nt(q, k) for q, k in zip(qs, ks)], axis=0)
    if sink_logits is not None:
        s = jnp.concatenate([s, sink_logits], axis=1)
        zeros = jnp.zeros((LANES, LANES), BF16)
        tail = jnp.concatenate([zeros, jnp.ones_like(zeros)], axis=1)
        v1s = [jnp.concatenate([v1, tail], axis=0) for v1 in v1s]
    p = jnp.exp2(s - s.max(axis=-1, keepdims=True)).astype(BF16)
    acc = jnp.concatenate([_dot(p[i * tq:(i + 1) * tq], v1) for i, v1 in enumerate(v1s)], axis=0)
    return acc[:, :LANES] / acc[:, LANES:LANES + 1]


def _split_pair(q):
    low = lax.broadcasted_iota(jnp.int32, q.shape, 1) < (LANES // 2)
    zero = jnp.zeros_like(q)
    return [jnp.where(low, q, zero), jnp.where(low, zero, q)]


def _merge_pair(o0, o1):
    low = lax.broadcasted_iota(jnp.int32, o0.shape, 1) < (LANES // 2)
    return jnp.where(low, o0, o1).astype(BF16)


def _attend_pair(q, segs, sinks=None):
    outs = []
    for j, qj in enumerate(_split_pair(q)):
        segs_j = [(s[0], s[1], None if s[2] is None else s[2][j]) + tuple(s[3:]) for s in segs]
        outs.append(_attend(qj, segs_j, None if sinks is None else sinks[j]))
    return _merge_pair(*outs)


def _mods_kernel(cond_ref, w_ref, b_ref, o_ref):
    cnd = cond_ref[...]
    act = cnd * jax.nn.sigmoid(cnd)
    o_ref[...] = _dot(act.astype(BF16), w_ref[...].astype(BF16)) + b_ref[...]


def _modulation(cond, ada_w, ada_b):
    tn = 1536
    n = N_MOD * D_MODEL
    return pl.pallas_call(
        _mods_kernel,
        grid=(DEPTH, n // tn),
        in_specs=[
            _full((COND_ROWS, D_MODEL)),
            pl.BlockSpec((None, D_MODEL, tn), lambda l, j: (l, 0, j)),
            pl.BlockSpec((None, 1, tn), lambda l, j: (l, 0, j)),
        ],
        out_specs=pl.BlockSpec((None, COND_ROWS, tn), lambda l, j: (l, 0, j)),
        out_shape=jax.ShapeDtypeStruct((DEPTH, COND_ROWS, n), F32),
        compiler_params=_params("arbitrary", "arbitrary"),
        name="adaln_mods",
    )(cond, ada_w, ada_b.reshape(DEPTH, 1, n))


def _proj_a_kernel(latent, x_ref, m_ref, g_ref, w_ref, qn_ref, kn_ref, *rest):
    if latent:
        cos_ref, sin_ref, q_ref, k_ref, v_ref = rest
    else:
        q_ref, k_ref, v_ref, ks_ref, vs_ref = rest
    hd = ATTN_HEAD_DIM
    h = _premod(x_ref[...], g_ref[0:1, :], m_ref[...], 0)
    z = _dot(h.astype(BF16), w_ref[...])
    n_qk = ATTN_HEADS + ATTN_KV_HEADS
    ys = [_rms(z[:, i * hd:(i + 1) * hd], qn_ref[...] if i < ATTN_HEADS else kn_ref[...]) for i in range(n_qk)]
    if not latent:
        for j in range(ATTN_KV_HEADS):
            ks_ref[:, j * hd:(j + 1) * hd] = ys[ATTN_HEADS + j]
    else:
        ys = [y * cos_ref[...] + pltpu.roll(y, hd // 2, 1) * sin_ref[...] for y in ys]
    for i in range(ATTN_HEADS):
        q_ref[:, i * hd:(i + 1) * hd] = (ys[i] * (hd ** -0.5 * LOG2E)).astype(BF16)
    for j in range(ATTN_KV_HEADS):
        k_ref[:, j * hd:(j + 1) * hd] = ys[ATTN_HEADS + j].astype(BF16)
    v = z[:, n_qk * hd:]
    v_ref[...] = v.astype(BF16)
    if not latent:
        vs_ref[...] = v


def _proj_b_kernel(latent, x_ref, m_ref, g_ref, win_ref, qn_ref, kvn_ref, wuq_ref, wukv_ref, *rest):
    if latent:
        cos_ref, sin_ref, qn_o, qr_o, kn_o, v_o, kp_o = rest
    else:
        qn_o, qr_o, kn_o, v_o, kp_o, ckv_s, kpe_s = rest
    nq, nkv = MLA_Q_LORA, MLA_KV_LORA
    nope_w = MLA_HEADS * MLA_NOPE
    scale = MLA_SCALE * LOG2E
    h = _premod(x_ref[...], g_ref[0:1, :], m_ref[...], 0)
    z = _dot(h.astype(BF16), win_ref[...])
    cq = _rms(z[:, :nq], qn_ref[...])
    ckv = _rms(z[:, nq:nq + nkv], kvn_ref[...])
    kp = z[:, nq + nkv:]
    q = _dot(cq.astype(BF16), wuq_ref[...])
    kv = _dot(ckv.astype(BF16), wukv_ref[...])
    if not latent:
        ckv_s[...] = ckv
        _store_transposed(kpe_s, kp, MLA_ROPE)
    qn_o[...] = (q[:, :nope_w] * scale).astype(BF16)
    for i in range(MLA_HEADS * MLA_ROPE // LANES):
        qr = q[:, nope_w + i * LANES:nope_w + (i + 1) * LANES]
        if latent:
            qr = _rope(qr, cos_ref[...], sin_ref[...], MLA_ROPE // 4)
        qr_o[:, i * LANES:(i + 1) * LANES] = (qr * scale).astype(BF16)
    if latent:
        kp = _rope(kp, cos_ref[...], sin_ref[...], MLA_ROPE // 4)
    kp_o[...] = kp.astype(BF16)
    kn_o[...] = kv[:, :nope_w].astype(BF16)
    v_o[...] = kv[:, nope_w:].astype(BF16)


def _proj_c_kernel(latent, x_ref, m_ref, g_ref, w_ref, *rest):
    if latent:
        cos_ref, sin_ref, q_o, kd_o, vd_o = rest
    else:
        q_o, kd_o, vd_o, ks_o, vs_o = rest
    qw = SWA_HEADS * SWA_HEAD_DIM
    kw = SWA_KV_HEADS * SWA_HEAD_DIM
    h = _premod(x_ref[...], g_ref[0:1, :], m_ref[...], 0)
    z = _dot(h.astype(BF16), w_ref[...])
    off = qw
    if not latent:
        _store_transposed(ks_o, z[:, qw:qw + kw], kw)
        _store_transposed(vs_o, z[:, qw + kw:qw + 2 * kw], kw)
        off = qw + 2 * kw
    scale = SWA_HEAD_DIM ** -0.5 * LOG2E
    for i in range(qw // LANES):
        y = z[:, i * LANES:(i + 1) * LANES]
        if latent:
            y = _rope(y, cos_ref[...], sin_ref[...], SWA_HEAD_DIM // 4)
        q_o[:, i * LANES:(i + 1) * LANES] = (y * scale).astype(BF16)
    for i in range(2 * kw // LANES):
        y = z[:, off + i * LANES:off + (i + 1) * LANES]
        if latent:
            y = _rope(y, cos_ref[...], sin_ref[...], SWA_HEAD_DIM // 4)
        kd_o[:, i * LANES:(i + 1) * LANES] = y.astype(BF16)
    vd_o[...] = z[:, off + 2 * kw:].astype(BF16)


def _proj_d_kernel(latent, x_ref, m_ref, g_ref, w_ref, *rest):
    if latent:
        q_o, k_o, v_o = rest
    else:
        q_o, k_o, v_o, ks_o, vs_o = rest
    hw = NAT_HEADS * NAT_HEAD_DIM
    h = _premod(x_ref[...], g_ref[0:1, :], m_ref[...], 0)
    z = _dot(h.astype(BF16), w_ref[...])
    q_o[...] = (z[:, :hw] * (NAT_HEAD_DIM ** -0.5 * LOG2E)).astype(BF16)
    k_o[...] = z[:, hw:2 * hw].astype(BF16)
    v_o[...] = z[:, 2 * hw:].astype(BF16)
    if not latent:
        _store_transposed(ks_o, z[:, hw:2 * hw], hw)
        _store_transposed(vs_o, z[:, 2 * hw:], hw)


def _store_transposed(ref, val, width):
    n, _, seq = ref.shape
    for i in range(n):
        ref[i] = val[i * seq:(i + 1) * seq, :].T[:width, :]


def _run_proj(kernel, latent, x, mods_l, g, consts, tables, outs, name, state_seq=None):
    bx, s, _ = x.shape
    tm = TOK_TILE
    row = (lambda b, j: (b, 0, 0)) if latent else (lambda b, j: (CTX_MOD_ROW, 0, 0))
    in_specs = [
        pl.BlockSpec((None, tm, D_MODEL), lambda b, j: (b, j, 0)),
        pl.BlockSpec((None, 1, N_MOD * D_MODEL), row),
        _full(g.shape),
    ] + [_full(c.shape) for c in consts]
    in_specs += [pl.BlockSpec((tm, LANES), lambda b, j: (j, 0)) for _ in tables]
    out_specs, out_shape = [], []
    for o in outs:
        if len(o) == 2:
            out_specs.append(pl.BlockSpec((None, tm, o[0]), lambda b, j: (b, j, 0)))
            out_shape.append(jax.ShapeDtypeStruct((bx, s, o[0]), o[1]))
        else:
            per_tile = tm // state_seq
            out_specs.append(pl.BlockSpec((per_tile, o[0], state_seq),
                                          lambda b, j: (b * (s // tm) + j, 0, 0)))
            out_shape.append(jax.ShapeDtypeStruct((bx * s // state_seq, o[0], state_seq), o[1]))
    return pl.pallas_call(
        functools.partial(kernel, latent),
        grid=(bx, s // tm),
        in_specs=in_specs,
        out_specs=out_specs,
        out_shape=out_shape,
        compiler_params=_params("arbitrary", "arbitrary"),
        name=name,
    )(x, mods_l, g, *consts, *tables)


def _attn_a_kernel(has_ctx, q_ref, k_ref, v_ref, *rest):
    if has_ctx:
        kc_ref, vc_ref, o_ref = rest
    else:
        (o_ref,) = rest
    group = ATTN_HEADS // ATTN_KV_HEADS
    head_cols = [slice(h * LANES, (h + 1) * LANES) for h in range(ATTN_HEADS)]
    kv_cols = [slice(j * LANES, (j + 1) * LANES) for j in range(ATTN_KV_HEADS)]
    if not has_ctx:
        tq = q_ref.shape[0]
        qs = [jnp.concatenate([q_ref[:, c] for c in head_cols[j * group:(j + 1) * group]], axis=0)
              for j in range(ATTN_KV_HEADS)]
        out = _attend_group(qs, [k_ref[:, c] for c in kv_cols], [_with_ones(v_ref[:, c]) for c in kv_cols])
        for h, c in enumerate(head_cols):
            o_ref[:, c] = out[h * tq:(h + 1) * tq].astype(BF16)
        return
    for j, kc in enumerate(kv_cols):
        segs = [(k_ref[:, kc], _with_ones(v_ref[:, kc]), None),
                (kc_ref[:, kc], _with_ones(vc_ref[:, kc]), None)]
        for r in range(q_ref.shape[0] // Q_TILE):
            rows = slice(r * Q_TILE, (r + 1) * Q_TILE)
            for c in head_cols[j * group:(j + 1) * group]:
                o_ref[rows, c] = _attend(q_ref[rows, c], segs).astype(BF16)


def _attention_a(q, k, v, kc=None, vc=None):
    b, s, qw = q.shape
    kl, kw = k.shape[1:]
    has_ctx = kc is not None
    tq = min(LATENT_Q_ROWS if has_ctx else Q_TILE, s)
    in_specs = [
        pl.BlockSpec((None, tq, qw), lambda bi, t: (bi, t, 0)),
        pl.BlockSpec((None, kl, kw), lambda bi, t: (bi, 0, 0)),
        pl.BlockSpec((None, kl, kw), lambda bi, t: (bi, 0, 0)),
    ]
    args = [q, k, v]
    if has_ctx:
        cl = kc.shape[1]
        in_specs += [pl.BlockSpec((None, cl, kw), lambda bi, t: (bi, 0, 0))] * 2
        args += [kc, vc]
    return pl.pallas_call(
        functools.partial(_attn_a_kernel, has_ctx),
        grid=(b, s // tq),
        in_specs=in_specs,
        out_specs=pl.BlockSpec((None, tq, qw), lambda bi, t: (bi, t, 0)),
        out_shape=jax.ShapeDtypeStruct(q.shape, BF16),
        compiler_params=_params("arbitrary", "arbitrary"),
        name="attn_gqa_latent" if has_ctx else "attn_gqa_context",
    )(*args)


def _attn_b_kernel(has_ctx, qn_ref, qr_ref, kn_ref, kp_ref, v_ref, *rest):
    if has_ctx:
        knc_ref, kpc_ref, vc_ref, o_ref = rest
    else:
        (o_ref,) = rest
    tq = min(Q_TILE, qn_ref.shape[0])
    lane = lax.broadcasted_iota(jnp.int32, (tq, LANES), 1)
    low = lane < (LANES // 2)
    quarter = lane // MLA_ROPE
    zero = jnp.zeros((tq, LANES), BF16)
    kp = kp_ref[...]
    kpc = kpc_ref[...] if has_ctx else None
    pairs_per_rope = LANES // MLA_ROPE // 2
    n_pairs = HEADS_PER_STEP // 2
    pair_cols = [slice(p * LANES, (p + 1) * LANES) for p in range(n_pairs)]

    def pair_lhs(p, rows=slice(None)):
        qn = qn_ref[rows, pair_cols[p]]
        rg = p // pairs_per_rope
        qr = qr_ref[rows, rg * LANES:(rg + 1) * LANES]
        return [jnp.concatenate([jnp.where(low if j == 0 else ~low, qn, zero),
                                 jnp.where(quarter == 2 * (p % pairs_per_rope) + j, qr, zero)], axis=1)
                for j in range(2)]

    if not has_ctx:
        qs = [jnp.concatenate(pair_lhs(p), axis=0) for p in range(n_pairs)]
        out = _attend_group(qs, [jnp.concatenate([kn_ref[:, c], kp], axis=1) for c in pair_cols],
                            [_with_ones(v_ref[:, c]) for c in pair_cols])
        for p, c in enumerate(pair_cols):
            o_ref[:, c] = _merge_pair(out[2 * p * tq:(2 * p + 1) * tq], out[(2 * p + 1) * tq:(2 * p + 2) * tq])
        return
    for p, cols in enumerate(pair_cols):
        segs = [(jnp.concatenate([kn_ref[:, cols], kp], axis=1), _with_ones(v_ref[:, cols]), None),
                (jnp.concatenate([knc_ref[:, cols], kpc], axis=1), _with_ones(vc_ref[:, cols]), None)]
        for r in range(qn_ref.shape[0] // tq):
            rows = slice(r * tq, (r + 1) * tq)
            o_ref[rows, cols] = _merge_pair(*[_attend(lhs, segs) for lhs in pair_lhs(p, rows)])


def _attention_b(qn, qr, kn, kp, v, knc=None, kpc=None, vc=None):
    b, s, _ = qn.shape
    kl = kn.shape[1]
    has_ctx = knc is not None
    tq = min(LATENT_Q_ROWS if has_ctx else Q_TILE, s)
    gw = HEADS_PER_STEP * MLA_NOPE
    blk = lambda rows, w, fn: pl.BlockSpec((None, rows, w), fn)
    in_specs = [
        blk(tq, gw, lambda bi, g, t: (bi, t, g)),
        blk(tq, HEADS_PER_STEP * MLA_ROPE, lambda bi, g, t: (bi, t, g)),
        blk(kl, gw, lambda bi, g, t: (bi, 0, g)),
        blk(kl, LANES, lambda bi, g, t: (bi, 0, 0)),
        blk(kl, gw, lambda bi, g, t: (bi, 0, g)),
    ]
    args = [qn, qr, kn, kp, v]
    if has_ctx:
        cl = knc.shape[1]
        in_specs += [
            blk(cl, gw, lambda bi, g, t: (bi, 0, g)),
            blk(cl, LANES, lambda bi, g, t: (bi, 0, 0)),
            blk(cl, gw, lambda bi, g, t: (bi, 0, g)),
        ]
        args += [knc, kpc, vc]
    return pl.pallas_call(
        functools.partial(_attn_b_kernel, has_ctx),
        grid=(b, MLA_HEADS // HEADS_PER_STEP, s // tq),
        in_specs=in_specs,
        out_specs=blk(tq, gw, lambda bi, g, t: (bi, t, g)),
        out_shape=jax.ShapeDtypeStruct(qn.shape, BF16),
        compiler_params=_params("arbitrary", "arbitrary", "arbitrary"),
        name="attn_mla_latent" if has_ctx else "attn_mla_context",
    )(*args)


def _attn_c_kernel(latent, sink_ref, q_ref, kd_ref, vd_ref, *rest):
    if latent:
        kc_ref, vc_ref, o_ref = rest
    else:
        (o_ref,) = rest
    tq = min(Q_TILE, q_ref.shape[0])
    group = SWA_HEADS // SWA_KV_HEADS
    kv_per_step = HEADS_PER_STEP // group
    head0 = pl.program_id(1) * HEADS_PER_STEP
    windows = []
    if latent:
        s_len = kd_ref.shape[0]
        win = tq + 2 * SWA_WINDOW
        for r in range(q_ref.shape[0] // tq):
            t0 = pl.program_id(2) * q_ref.shape[0] + r * tq
            ws = pl.multiple_of(jnp.clip(t0 - SWA_WINDOW, 0, s_len - win), LANES)
            qpos = t0 + lax.broadcasted_iota(jnp.int32, (tq, win), 0)
            kpos = ws + lax.broadcasted_iota(jnp.int32, (tq, win), 1)
            bias = jnp.where(jnp.abs(qpos - kpos) <= SWA_WINDOW, 0.0, NEG_INF).astype(F32)
            windows.append((slice(r * tq, (r + 1) * tq), ws, bias))
    pairs_per_kv = group // 2
    pair_cols = [slice(p * LANES, (p + 1) * LANES) for p in range(HEADS_PER_STEP // 2)]
    kv_cols = [slice(j * LANES, (j + 1) * LANES) for j in range(kv_per_step)]
    if not latent:
        qs = [jnp.concatenate([h for c in pair_cols[j * pairs_per_kv:(j + 1) * pairs_per_kv]
                               for h in _split_pair(q_ref[:, c])], axis=0) for j in range(kv_per_step)]
        out = _attend_group(qs, [kd_ref[:, c] for c in kv_cols], [_with_ones(vd_ref[:, c]) for c in kv_cols],
                            sink_ref[...])
        for p, c in enumerate(pair_cols):
            o_ref[:, c] = _merge_pair(out[2 * p * tq:(2 * p + 1) * tq], out[(2 * p + 1) * tq:(2 * p + 2) * tq])
        return
    sinks = [sink_ref[head0 + h] * LOG2E for h in range(HEADS_PER_STEP)]
    hd = SWA_HEAD_DIM
    for j, kc in enumerate(kv_cols):
        kct = kc_ref[j * hd:(j + 1) * hd, :].astype(BF16)
        vct = vc_ref[j * hd:(j + 1) * hd, :].astype(BF16)
        ctx_seg = (jnp.concatenate([kct, kct], axis=0),
                   jnp.concatenate([vct, vct, jnp.ones((LANES, vct.shape[1]), BF16)], axis=0), None, True)
        for rows, ws, bias in windows:
            segs = [(kd_ref[pl.ds(ws, win), kc], _with_ones(vd_ref[pl.ds(ws, win), kc]), (bias, bias)), ctx_seg]
            for p in range(j * pairs_per_kv, (j + 1) * pairs_per_kv):
                o_ref[rows, pair_cols[p]] = _attend_pair(q_ref[rows, pair_cols[p]], segs, sinks[2 * p:2 * p + 2])


def _attention_c(sink, q, kd, vd, kc=None, vc=None):
    b, s, _ = q.shape
    kl = kd.shape[1]
    latent = kc is not None
    tq = min(LATENT_Q_ROWS if latent else Q_TILE, s)
    gw = HEADS_PER_STEP * SWA_HEAD_DIM
    kvw = 2 * SWA_HEAD_DIM * HEADS_PER_STEP // (SWA_HEADS // SWA_KV_HEADS)
    if latent:
        sink_spec = pl.BlockSpec(memory_space=pltpu.SMEM)
    else:
        rows = jnp.repeat(sink * LOG2E, tq).reshape(SWA_HEADS // HEADS_PER_STEP, HEADS_PER_STEP * tq, 1)
        lane0 = jnp.arange(LANES)[None, None, :] == 0
        sink = jnp.where(lane0, rows, NEG_INF)
        sink_spec = pl.BlockSpec((None, HEADS_PER_STEP * tq, LANES), lambda bi, g, t: (g, 0, 0))
    in_specs = [
        sink_spec,
        pl.BlockSpec((None, tq, gw), lambda bi, g, t: (bi, t, g)),
        pl.BlockSpec((None, kl, kvw), lambda bi, g, t: (bi, 0, g)),
        pl.BlockSpec((None, kl, kvw), lambda bi, g, t: (bi, 0, g)),
    ]
    args = [sink, q, kd, vd]
    if latent:
        cw, cl = kc.shape[1] * HEADS_PER_STEP // SWA_HEADS, kc.shape[2]
        in_specs += [pl.BlockSpec((None, cw, cl), lambda bi, g, t: (bi, g, 0))] * 2
        args += [kc, vc]
    return pl.pallas_call(
        functools.partial(_attn_c_kernel, latent),
        grid=(b, SWA_HEADS // HEADS_PER_STEP, s // tq),
        in_specs=in_specs,
        out_specs=pl.BlockSpec((None, tq, gw), lambda bi, g, t: (bi, t, g)),
        out_shape=jax.ShapeDtypeStruct(q.shape, BF16),
        compiler_params=_params("arbitrary", "arbitrary", "arbitrary"),
        name="attn_swa_latent" if latent else "attn_swa_context",
    )(*args)


def _attn_d_ctx_kernel(q_ref, k_ref, v_ref, o_ref):
    tq = q_ref.shape[0]
    pair_cols = [slice(p * LANES, (p + 1) * LANES) for p in range(HEADS_PER_STEP // 2)]
    qs = [jnp.concatenate(_split_pair(q_ref[:, c]), axis=0) for c in pair_cols]
    out = _attend_group(qs, [k_ref[:, c] for c in pair_cols], [_with_ones(v_ref[:, c]) for c in pair_cols])
    for p, c in enumerate(pair_cols):
        o_ref[:, c] = _merge_pair(out[2 * p * tq:(2 * p + 1) * tq], out[(2 * p + 1) * tq:(2 * p + 2) * tq])


def _attention_d_ctx(q, k, v):
    b, s, _ = q.shape
    gw = HEADS_PER_STEP * NAT_HEAD_DIM
    blk = pl.BlockSpec((None, s, gw), lambda bi, g: (bi, 0, g))
    return pl.pallas_call(
        _attn_d_ctx_kernel,
        grid=(b, NAT_HEADS // HEADS_PER_STEP),
        in_specs=[blk, blk, blk],
        out_specs=blk,
        out_shape=jax.ShapeDtypeStruct(q.shape, BF16),
        compiler_params=_params("arbitrary", "arbitrary"),
        name="attn_nat_context",
    )(q, k, v)


def _nat_window_start(first_row, rows):
    r0 = jnp.clip(first_row - NAT_WIN_R // 2, 0, rows - NAT_WIN_R)
    return jnp.minimum(r0, rows - NAT_KEY_ROWS)


def _attn_d_lat_kernel(q_ref, k_ref, v_ref, kc_ref, vc_ref, bias_ref, o_ref):
    rows = k_ref.shape[0] // GRID_W
    slab = NAT_KEY_ROWS * GRID_W
    start = _nat_window_start(pl.program_id(0) * NAT_TILE_ROWS, rows) * GRID_W
    start = pl.multiple_of(start, GRID_W)
    ones = jnp.ones((LANES, kc_ref.shape[1]), BF16)
    for p in range(HEADS_PER_STEP // 2):
        cols = slice(p * LANES, (p + 1) * LANES)
        segs = [(k_ref[pl.ds(start, slab), cols], _with_ones(v_ref[pl.ds(start, slab), cols]),
                 (bias_ref[2 * p], bias_ref[2 * p + 1])),
                (kc_ref[cols, :].astype(BF16),
                 jnp.concatenate([vc_ref[cols, :].astype(BF16), ones], axis=0), None, True)]
        o_ref[:, cols] = _attend_pair(q_ref[:, cols], segs)


def _attention_d_lat(q, k, v, kc, vc, bias):
    b, s, _ = q.shape
    cl = kc.shape[2]
    tq = NAT_TILE_ROWS * GRID_W
    slab = NAT_KEY_ROWS * GRID_W
    gw = HEADS_PER_STEP * NAT_HEAD_DIM
    return pl.pallas_call(
        _attn_d_lat_kernel,
        grid=(s // tq, NAT_HEADS // HEADS_PER_STEP, b),
        in_specs=[
            pl.BlockSpec((None, tq, gw), lambda t, g, bi: (bi, t, g)),
            pl.BlockSpec((None, s, gw), lambda t, g, bi: (bi, 0, g)),
            pl.BlockSpec((None, s, gw), lambda t, g, bi: (bi, 0, g)),
            pl.BlockSpec((None, gw, cl), lambda t, g, bi: (bi, g, 0)),
            pl.BlockSpec((None, gw, cl), lambda t, g, bi: (bi, g, 0)),
            pl.BlockSpec((HEADS_PER_STEP, None, tq, slab), lambda t, g, bi: (g, t, 0, 0)),
        ],
        out_specs=pl.BlockSpec((None, tq, gw), lambda t, g, bi: (bi, t, g)),
        out_shape=jax.ShapeDtypeStruct(q.shape, BF16),
        compiler_params=_params("arbitrary", "arbitrary", "arbitrary"),
        name="attn_nat_latent",
    )(q, k, v, kc, vc, bias)


def _mla_expand_kernel(c_ref, w_ref, kn_ref, v_ref):
    kv = _dot(c_ref[...].astype(BF16), w_ref[...])
    half = kv.shape[1] // 2
    kn_ref[...] = kv[:, :half].astype(BF16)
    v_ref[...] = kv[:, half:].astype(BF16)


def _mla_expand(ckv, w_ukv):
    b, l, c = ckv.shape
    n = w_ukv.shape[1] // 2
    out = pl.BlockSpec((None, l, n), lambda bi: (bi, 0, 0))
    return pl.pallas_call(
        _mla_expand_kernel,
        grid=(b,),
        in_specs=[pl.BlockSpec((None, l, c), lambda bi: (bi, 0, 0)), _full(w_ukv.shape)],
        out_specs=[out, out],
        out_shape=[jax.ShapeDtypeStruct((b, l, n), BF16)] * 2,
        compiler_params=_params("arbitrary"),
        name="mla_expand_cache",
    )(ckv, w_ukv)


def _finish_kernel(o_ref, x_ref, m_ref, g_ref, wo_ref, w1_ref, w2_ref, y_ref, x1_ref, h2_ref):
    f = pl.program_id(2)

    @pl.when(f == 0)
    def _():
        m = m_ref[...]
        for c in range(y_ref.shape[0] // TOK_CHAIN):
            rows = slice(c * TOK_CHAIN, (c + 1) * TOK_CHAIN)
            a = _dot(o_ref[rows, :], wo_ref[...])
            x1 = x_ref[rows, :] + _rms(a, _mod(m, 2) * g_ref[1:2, :])
            x1_ref[rows, :] = x1
            h2_ref[rows, :] = _premod(x1, g_ref[2:3, :], m, 1).astype(BF16)
        y_ref[...] = jnp.zeros_like(y_ref)

    h = h2_ref[...]
    for c in range(w1_ref.shape[1] // MLP_FF_CHAIN):
        cols = slice(c * MLP_FF_CHAIN, (c + 1) * MLP_FF_CHAIN)
        u = _dot(h, w1_ref[:, cols].astype(BF16))
        u = jnp.square(jnp.maximum(u, 0.0)).astype(BF16)
        y_ref[...] += _dot(u, w2_ref[cols, :].astype(BF16))

    @pl.when(f == pl.num_programs(2) - 1)
    def _():
        y_ref[...] = x1_ref[...] + _rms(y_ref[...], _mod(m_ref[...], 5) * g_ref[3:4, :])


def _finish_layer(o, x, mods_l, g, wo, w1, w2, layer, latent):
    bx, s, _ = x.shape
    tm, tf = MLP_TOK_TILE, MLP_FF_TILE
    row = (lambda b, j, f: (b, 0, 0)) if latent else (lambda b, j, f: (CTX_MOD_ROW, 0, 0))
    tok = lambda w: pl.BlockSpec((None, tm, w), lambda b, j, f: (b, j, 0))
    return pl.pallas_call(
        _finish_kernel,
        grid=(bx, s // tm, D_FF // tf),
        in_specs=[tok(o.shape[-1]), tok(D_MODEL),
                  pl.BlockSpec((None, 1, N_MOD * D_MODEL), row), _full(g.shape), _full(wo.shape),
                  pl.BlockSpec((None, D_MODEL, tf), lambda b, j, f: (layer, 0, f)),
                  pl.BlockSpec((None, tf, D_MODEL), lambda b, j, f: (layer, f, 0))],
        out_specs=tok(D_MODEL),
        out_shape=jax.ShapeDtypeStruct(x.shape, F32),
        scratch_shapes=[pltpu.VMEM((tm, D_MODEL), F32), pltpu.VMEM((tm, D_MODEL), BF16)],
        compiler_params=pltpu.CompilerParams(dimension_semantics=("arbitrary",) * 3,
                                             vmem_limit_bytes=MLP_VMEM_LIMIT),
        name="out_proj_mlp",
    )(o, x, mods_l, g, wo, w1, w2)


def _rope_tables(s, dim):
    quarter = dim // 4
    t = jnp.arange(s)
    pos = jnp.stack([t // GRID_W, t % GRID_W], axis=-1).astype(F32)
    inv = ROPE_THETA ** (-jnp.arange(quarter, dtype=F32) / quarter)
    ang = pos[:, :, None] * inv
    cos = jnp.broadcast_to(jnp.cos(ang)[:, :, None, :], (s, 2, 2, quarter)).reshape(s, dim)
    sign = jnp.array([-1.0, 1.0], F32)[None, None, :, None]
    sin = (jnp.sin(ang)[:, :, None, :] * sign).reshape(s, dim)
    reps = LANES // dim
    return jnp.tile(cos, (1, reps)), jnp.tile(sin, (1, reps))


def _dup_heads(w, heads, dim):
    lead = w.shape[:-1]
    w = w.reshape(lead + (heads, 1, dim))
    return jnp.broadcast_to(w, lead + (heads, 2, dim)).reshape(lead + (heads * 2 * dim,))


def _nat_dense_bias(rpb, rows):
    heads = rpb.shape[0]
    c = np.arange(GRID_W)
    c0 = np.clip(c - NAT_WIN_C // 2, 0, GRID_W - NAT_WIN_C)
    in_c = (c[None, :] >= c0[:, None]) & (c[None, :] < c0[:, None] + NAT_WIN_C)
    dc = c[None, :] - c[:, None] + NAT_WIN_C - 1
    onehot = (dc[None] == np.arange(2 * NAT_WIN_C - 1)[:, None, None]) & in_c[None]
    toe = jnp.einsum("had,dck->hack", rpb * LOG2E, jnp.asarray(onehot, F32),
                     precision=lax.Precision.HIGHEST)
    toe = jnp.where(jnp.asarray(in_c)[None, None], toe, NEG_INF)
    pad = jnp.full((heads, 1, GRID_W, GRID_W), NEG_INF, F32)
    ext = jnp.concatenate([pad, toe, pad], axis=1)
    pairs = jnp.concatenate([ext[:, :-1], ext[:, 1:]], axis=-1)
    tiles = rows // NAT_TILE_ROWS
    tq, slab = NAT_TILE_ROWS * GRID_W, NAT_KEY_ROWS * GRID_W
    n_off = 2 * NAT_WIN_R
    return pl.pallas_call(
        functools.partial(_nat_bias_kernel, rows),
        grid=(heads,),
        in_specs=[pl.BlockSpec((None, n_off, GRID_W, LANES), lambda h: (h, 0, 0, 0))],
        out_specs=pl.BlockSpec((None, tiles, tq, slab), lambda h: (h, 0, 0, 0)),
        out_shape=jax.ShapeDtypeStruct((heads, tiles, tq, slab), F32),
        compiler_params=_params("arbitrary"),
        name="nat_bias_expand",
    )(pairs)


def _nat_bias_kernel(rows, pairs_ref, o_ref):
    low = lax.broadcasted_iota(jnp.int32, (GRID_W, LANES), 1) < GRID_W
    masked = jnp.full((GRID_W, LANES), NEG_INF, F32)
    for r in range(rows):
        tile, i = divmod(r, NAT_TILE_ROWS)
        r0 = min(max(r - NAT_WIN_R // 2, 0), rows - NAT_WIN_R)
        first0 = min(max(tile * NAT_TILE_ROWS - NAT_WIN_R // 2, 0), rows - NAT_WIN_R)
        ws = min(first0, rows - NAT_KEY_ROWS)
        for jb in range(NAT_KEY_ROWS // 2):
            kr = ws + 2 * jb
            ok_lo, ok_hi = r0 <= kr < r0 + NAT_WIN_R, r0 <= kr + 1 < r0 + NAT_WIN_R
            blk = masked
            if ok_lo or ok_hi:
                blk = pairs_ref[kr - r + NAT_WIN_R]
                if not ok_lo:
                    blk = jnp.where(low, NEG_INF, blk)
                elif not ok_hi:
                    blk = jnp.where(low, blk, NEG_INF)
            o_ref[tile, i * GRID_W:(i + 1) * GRID_W, jb * LANES:(jb + 1) * LANES] = blk


def kernel(x_prompt, x_sample, cache_l0_k, cache_l0_v, cache_l1_ckv, cache_l1_kpe, cache_l2_k, cache_l2_v, cache_l3_k, cache_l3_v, c, c_ctx, ada_w, ada_b, norm_g, mlp_w1, mlp_w2, attn_w_qkv, attn_q_norm, attn_k_norm, attn_w_o, mla_w_in, mla_q_norm, mla_kv_norm, mla_w_uq, mla_w_ukv, mla_w_o, swa_w_qkv, swa_sink, swa_w_o, nat_w_qkv, nat_rpb, nat_w_o):
    nb, seq, d = x_prompt.shape
    db, dseq, _ = x_sample.shape
    past = cache_l0_k.shape[1]
    ctx_b = nb * seq // dseq
    xp = x_prompt.reshape(ctx_b, dseq, d)
    xs = x_sample

    cond = jnp.zeros((COND_ROWS, d), F32).at[:db].set(c).at[CTX_MOD_ROW].set(c_ctx)
    mods = _modulation(cond, ada_w, ada_b).reshape(DEPTH, COND_ROWS, 1, N_MOD * d)

    row = lambda v: v.reshape(1, -1)

    def as_ctx(a):
        return a.reshape(nb, seq, a.shape[-1])

    def as_slab(a):
        return a.reshape(ctx_b, dseq, a.shape[-1])

    def head_major(cache):
        b_, l_, h_, dh = cache.shape
        return cache.transpose(0, 2, 3, 1).reshape(b_, h_ * dh, l_)

    def token_major(state, heads):
        b_, w_, l_ = state.shape
        return state.reshape(b_, heads, w_ // heads, l_).transpose(0, 3, 1, 2)

    def finish(o, x, layer, wo, latent):
        return _finish_layer(o, x, mods[layer], norm_g[layer], wo, mlp_w1, mlp_w2, layer, latent)

    g = norm_g[0]
    w = attn_w_qkv.astype(BF16)
    wo = attn_w_o.astype(BF16)
    consts = [w, row(attn_q_norm), row(attn_k_norm)]
    kvw = ATTN_KV_HEADS * ATTN_HEAD_DIM
    q, k, v, l0_k, l0_v = _run_proj(
        _proj_a_kernel, False, xp, mods[0], g, consts, [],
        [(d, BF16), (kvw, BF16), (kvw, BF16), (kvw, F32), (kvw, F32)], "proj_gqa_context")
    o = _attention_a(as_ctx(q), as_ctx(k), as_ctx(v))
    xp = finish(as_slab(o), xp, 0, wo, False)
    perm = np.arange(ATTN_HEAD_DIM).reshape(2, 2, ATTN_HEAD_DIM // 4).transpose(1, 0, 2).reshape(-1)
    qk_heads = ATTN_HEADS + ATTN_KV_HEADS
    cols = (np.arange(qk_heads)[:, None] * ATTN_HEAD_DIM + perm[None, :]).reshape(-1)
    w_lat = jnp.concatenate([attn_w_qkv[:, cols], attn_w_qkv[:, qk_heads * ATTN_HEAD_DIM:]], axis=1).astype(BF16)
    consts_lat = [w_lat, row(attn_q_norm[perm]), row(attn_k_norm[perm])]
    tables = [t[:, perm] for t in _rope_tables(dseq, ATTN_HEAD_DIM)]
    q, k, v = _run_proj(_proj_a_kernel, True, xs, mods[0], g, consts_lat, tables,
                        [(d, BF16), (kvw, BF16), (kvw, BF16)], "proj_gqa_latent")
    o = _attention_a(q, k, v, cache_l0_k[..., perm].reshape(db, past, kvw).astype(BF16),
                     cache_l0_v.reshape(db, past, kvw).astype(BF16))
    xs = finish(o, xs, 0, wo, True)
    new_l0 = (l0_k.reshape(nb, seq, ATTN_KV_HEADS, ATTN_HEAD_DIM),
              l0_v.reshape(nb, seq, ATTN_KV_HEADS, ATTN_HEAD_DIM))

    g = norm_g[1]
    nq, nkv = MLA_Q_LORA, MLA_KV_LORA
    w_in = jnp.concatenate([mla_w_in[:, :nq + nkv]] + [mla_w_in[:, nq + nkv:]] * (LANES // MLA_ROPE),
                           axis=1).astype(BF16)
    wuq = mla_w_uq.reshape(nq, MLA_HEADS, MLA_NOPE + MLA_ROPE)
    wuq = jnp.concatenate([wuq[:, :, :MLA_NOPE].reshape(nq, -1), wuq[:, :, MLA_NOPE:].reshape(nq, -1)],
                          axis=1).astype(BF16)
    wukv = mla_w_ukv.reshape(nkv, MLA_HEADS, MLA_NOPE + MLA_V_DIM)
    wukv = jnp.concatenate([wukv[:, :, :MLA_NOPE].reshape(nkv, -1), wukv[:, :, MLA_NOPE:].reshape(nkv, -1)],
                           axis=1).astype(BF16)
    wo = mla_w_o.astype(BF16)
    consts = [w_in, row(mla_q_norm), row(mla_kv_norm), wuq, wukv]
    hw = MLA_HEADS * MLA_NOPE
    rw = MLA_HEADS * MLA_ROPE
    outs = [(hw, BF16), (rw, BF16), (hw, BF16), (hw, BF16), (LANES, BF16)]
    qn, qr, kn, v, kp, l1_ckv, l1_kpe = _run_proj(
        _proj_b_kernel, False, xp, mods[1], g, consts, [],
        outs + [(nkv, F32), (MLA_ROPE, F32, "T")], "proj_mla_context", seq)
    o = _attention_b(as_ctx(qn), as_ctx(qr), as_ctx(kn), as_ctx(kp), as_ctx(v))
    xp = finish(as_slab(o), xp, 1, wo, False)
    tables = list(_rope_tables(dseq, MLA_ROPE))
    qn, qr, kn, v, kp = _run_proj(_proj_b_kernel, True, xs, mods[1], g, consts, tables, outs,
                                  "proj_mla_latent")
    knc, vc = _mla_expand(cache_l1_ckv, wukv)
    kpc = jnp.tile(cache_l1_kpe, (1, 1, LANES // MLA_ROPE)).astype(BF16)
    o = _attention_b(qn, qr, kn, kp, v, knc, kpc, vc)
    xs = finish(o, xs, 1, wo, True)
    new_l1 = (l1_ckv.reshape(nb, seq, nkv), l1_kpe.transpose(0, 2, 1))

    g = norm_g[2]
    qw = SWA_HEADS * SWA_HEAD_DIM
    kw = SWA_KV_HEADS * SWA_HEAD_DIM
    wq, wk, wv = swa_w_qkv[:, :qw], swa_w_qkv[:, qw:qw + kw], swa_w_qkv[:, qw + kw:]
    wkd = _dup_heads(wk, SWA_KV_HEADS, SWA_HEAD_DIM)
    wvd = _dup_heads(wv, SWA_KV_HEADS, SWA_HEAD_DIM)
    w_ctx = jnp.concatenate([wq, wk, wv, wkd, wvd], axis=1).astype(BF16)
    w_lat = jnp.concatenate([wq, wkd, wvd], axis=1).astype(BF16)
    wo = swa_w_o.astype(BF16)
    q, kd, vd, l2_k, l2_v = _run_proj(
        _proj_c_kernel, False, xp, mods[2], g, [w_ctx], [],
        [(qw, BF16), (2 * kw, BF16), (2 * kw, BF16), (kw, F32, "T"), (kw, F32, "T")], "proj_swa_context", seq)
    o = _attention_c(swa_sink, as_ctx(q), as_ctx(kd), as_ctx(vd))
    xp = finish(as_slab(o), xp, 2, wo, False)
    tables = list(_rope_tables(dseq, SWA_HEAD_DIM))
    q, kd, vd = _run_proj(_proj_c_kernel, True, xs, mods[2], g, [w_lat], tables,
                          [(qw, BF16), (2 * kw, BF16), (2 * kw, BF16)], "proj_swa_latent")
    o = _attention_c(swa_sink, q, kd, vd, head_major(cache_l2_k), head_major(cache_l2_v))
    xs = finish(o, xs, 2, wo, True)
    new_l2 = (token_major(l2_k, SWA_KV_HEADS), token_major(l2_v, SWA_KV_HEADS))

    g = norm_g[3]
    hw = NAT_HEADS * NAT_HEAD_DIM
    w = nat_w_qkv.astype(BF16)
    wo = nat_w_o.astype(BF16)
    q, k, v, l3_k, l3_v = _run_proj(
        _proj_d_kernel, False, xp, mods[3], g, [w], [],
        [(hw, BF16), (hw, BF16), (hw, BF16), (hw, F32, "T"), (hw, F32, "T")], "proj_nat_context", seq)
    o = _attention_d_ctx(as_ctx(q), as_ctx(k), as_ctx(v))
    xp = finish(as_slab(o), xp, 3, wo, False)
    q, k, v = _run_proj(_proj_d_kernel, True, xs, mods[3], g, [w], [],
                        [(hw, BF16), (hw, BF16), (hw, BF16)], "proj_nat_latent")
    o = _attention_d_lat(q, k, v, head_major(cache_l3_k), head_major(cache_l3_v),
                         _nat_dense_bias(nat_rpb, dseq // GRID_W))
    xs = finish(o, xs, 3, wo, True)
    new_l3 = (token_major(l3_k, NAT_HEADS), token_major(l3_v, NAT_HEADS))

    return (xp.reshape(nb, seq, d), xs) + new_l0 + new_l1 + new_l2 + new_l3
```

```python
import functools

import numpy as np

import jax
import jax.numpy as jnp
from jax import lax
from jax.experimental import pallas as pl
from jax.experimental.pallas import tpu as pltpu

F32 = jnp.float32
BF16 = jnp.bfloat16

D_MODEL = 1024
DEPTH = 4
N_MOD = 6
D_FF = 4 * D_MODEL
GRID_W = 64
ROPE_THETA = 10000.0
NORM_EPS = 1e-6
NEG_INF = -1e30
LOG2E = 1.4426950408889634

ATTN_HEADS, ATTN_KV_HEADS, ATTN_HEAD_DIM = 8, 2, 128
MLA_HEADS, MLA_Q_LORA, MLA_KV_LORA = 16, 384, 256
MLA_NOPE, MLA_ROPE, MLA_V_DIM = 64, 32, 64
MLA_SCALE = (MLA_NOPE + MLA_ROPE) ** -0.5
SWA_HEADS, SWA_KV_HEADS, SWA_HEAD_DIM, SWA_WINDOW = 16, 4, 64, 128
NAT_HEADS, NAT_HEAD_DIM, NAT_WIN_R, NAT_WIN_C = 16, 64, 8, 16

LANES = 128
COND_ROWS = 16
CTX_MOD_ROW = 8
VMEM_LIMIT = 48 * 1024 * 1024

TOK_TILE = 512
TOK_CHAIN = 256
MLP_TOK_TILE = 1024
MLP_FF_TILE = 1024
MLP_FF_CHAIN = 1024
MLP_VMEM_LIMIT = 56 * 1024 * 1024
Q_TILE = 256
LATENT_Q_ROWS = 512
HEADS_PER_STEP = 16
NAT_TILE_ROWS = Q_TILE // GRID_W
NAT_KEY_ROWS = NAT_WIN_R + NAT_TILE_ROWS


def _params(*sem):
    return pltpu.CompilerParams(dimension_semantics=sem, vmem_limit_bytes=VMEM_LIMIT)


def _full(shape):
    nd = len(shape)
    return pl.BlockSpec(shape, lambda *_: (0,) * nd)


def _rms(x, g):
    return x * lax.rsqrt(jnp.mean(x * x, axis=-1, keepdims=True) + NORM_EPS) * g


def _mod(m, i):
    return m[:, i * D_MODEL:(i + 1) * D_MODEL]


def _premod(x, g, m, sub):
    return _rms(x, g * (1.0 + _mod(m, 3 * sub + 1))) + _mod(m, 3 * sub)


def _rope(x, cos, sin_signed, quarter):
    n = x.shape[-1]
    lane = lax.broadcasted_iota(jnp.int32, x.shape, 1)
    first = ((lane // quarter) % 2) == 0
    partner = jnp.where(first, pltpu.roll(x, n - quarter, 1), pltpu.roll(x, quarter, 1))
    return x * cos + partner * sin_signed


def _dot(a, b):
    return jnp.dot(a, b, preferred_element_type=F32)


def _dot_nt(a, b):
    return lax.dot_general(a, b, (((1,), (1,)), ((), ())), preferred_element_type=F32)


def _with_ones(v):
    return jnp.concatenate([v, jnp.ones_like(v)], axis=1)


def _attend(q, segs, sink=None):
    logits = []
    for seg in segs:
        k, bias, transposed = seg[0], seg[2], len(seg) > 3 and seg[3]
        s = _dot(q, k) if transposed else _dot_nt(q, k)
        logits.append(s if bias is None else s + bias)
    m = logits[0].max(axis=-1, keepdims=True)
    for s in logits[1:]:
        m = jnp.maximum(m, s.max(axis=-1, keepdims=True))
    if sink is not None:
        m = jnp.maximum(m, sink)
    acc = None
    for s, seg in zip(logits, segs):
        p = jnp.exp2(s - m).astype(BF16)
        pv = _dot_nt(p, seg[1]) if len(seg) > 3 and seg[3] else _dot(p, seg[1])
        acc = pv if acc is None else acc + pv
    den = acc[:, LANES:LANES + 1]
    if sink is not None:
        den = den + jnp.exp2(sink - m)
    return acc[:, :LANES] / den


def _attend_group(qs, ks, v1s, sink_logits=None):
    tq = qs[0].shape[0]
    s = jnp.concatenate([_dot_nt(q, k) for q, k in zip(qs, ks)], axis=0)
    if sink_logits is not None:
        s = jnp.concatenate([s, sink_logits], axis=1)
        zeros = jnp.zeros((LANES, LANES), BF16)
        tail = jnp.concatenate([zeros, jnp.ones_like(zeros)], axis=1)
        v1s = [jnp.concatenate([v1, tail], axis=0) for v1 in v1s]
    p = jnp.exp2(s - s.max(axis=-1, keepdims=True)).astype(BF16)
    acc = jnp.concatenate([_dot(p[i * tq:(i + 1) * tq], v1) for i, v1 in enumerate(v1s)], axis=0)
    return acc[:, :LANES] / acc[:, LANES:LANES + 1]


def _split_pair(q):
    low = lax.broadcasted_iota(jnp.int32, q.shape, 1) < (LANES // 2)
    zero = jnp.zeros_like(q)
    return [jnp.where(low, q, zero), jnp.where(low, zero, q)]


def _merge_pair(o0, o1):
    low = lax.broadcasted_iota(jnp.int32, o0.shape, 1) < (LANES // 2)
    return jnp.where(low, o0, o1).astype(BF16)


def _attend_pair(q, segs, sinks=None):
    outs = []
    for j, qj in enumerate(_split_pair(q)):
        segs_j = [(s[0], s[1], None if s[2] is None else s[2][j]) + tuple(s[3:]) for s in segs]
        outs.append(_attend(qj, segs_j, None if sinks is None else sinks[j]))
    return _merge_pair(*outs)


def _mods_kernel(cond_ref, w_ref, b_ref, o_ref):
    cnd = cond_ref[...]
    act = cnd * jax.nn.sigmoid(cnd)
    o_ref[...] = _dot(act.astype(BF16), w_ref[...].astype(BF16)) + b_ref[...]


def _modulation(cond, ada_w, ada_b):
    tn = 1536
    n = N_MOD * D_MODEL
    return pl.pallas_call(
        _mods_kernel,
        grid=(DEPTH, n // tn),
        in_specs=[
            _full((COND_ROWS, D_MODEL)),
            pl.BlockSpec((None, D_MODEL, tn), lambda l, j: (l, 0, j)),
            pl.BlockSpec((None, 1, tn), lambda l, j: (l, 0, j)),
        ],
        out_specs=pl.BlockSpec((None, COND_ROWS, tn), lambda l, j: (l, 0, j)),
        out_shape=jax.ShapeDtypeStruct((DEPTH, COND_ROWS, n), F32),
        compiler_params=_params("arbitrary", "arbitrary"),
        name="adaln_mods",
    )(cond, ada_w, ada_b.reshape(DEPTH, 1, n))


def _proj_a_kernel(latent, x_ref, m_ref, g_ref, w_ref, qn_ref, kn_ref, *rest):
    if latent:
        cos_ref, sin_ref, q_ref, k_ref, v_ref = rest
    else:
        q_ref, k_ref, v_ref, ks_ref, vs_ref = rest
    hd = ATTN_HEAD_DIM
    h = _premod(x_ref[...], g_ref[0:1, :], m_ref[...], 0)
    z = _dot(h.astype(BF16), w_ref[...])
    n_qk = ATTN_HEADS + ATTN_KV_HEADS
    ys = [_rms(z[:, i * hd:(i + 1) * hd], qn_ref[...] if i < ATTN_HEADS else kn_ref[...]) for i in range(n_qk)]
    if not latent:
        for j in range(ATTN_KV_HEADS):
            ks_ref[:, j * hd:(j + 1) * hd] = ys[ATTN_HEADS + j]
    else:
        ys = [y * cos_ref[...] + pltpu.roll(y, hd // 2, 1) * sin_ref[...] for y in ys]
    for i in range(ATTN_HEADS):
        q_ref[:, i * hd:(i + 1) * hd] = (ys[i] * (hd ** -0.5 * LOG2E)).astype(BF16)
    for j in range(ATTN_KV_HEADS):
        k_ref[:, j * hd:(j + 1) * hd] = ys[ATTN_HEADS + j].astype(BF16)
    v = z[:, n_qk * hd:]
    v_ref[...] = v.astype(BF16)
    if not latent:
        vs_ref[...] = v


def _proj_b_kernel(latent, x_ref, m_ref, g_ref, win_ref, qn_ref, kvn_ref, wuq_ref, wukv_ref, *rest):
    if latent:
        cos_ref, sin_ref, qn_o, qr_o, kn_o, v_o, kp_o = rest
    else:
        qn_o, qr_o, kn_o, v_o, kp_o, ckv_s, kpe_s = rest
    nq, nkv = MLA_Q_LORA, MLA_KV_LORA
    nope_w = MLA_HEADS * MLA_NOPE
    scale = MLA_SCALE * LOG2E
    h = _premod(x_ref[...], g_ref[0:1, :], m_ref[...], 0)
    z = _dot(h.astype(BF16), win_ref[...])
    cq = _rms(z[:, :nq], qn_ref[...])
    ckv = _rms(z[:, nq:nq + nkv], kvn_ref[...])
    kp = z[:, nq + nkv:]
    q = _dot(cq.astype(BF16), wuq_ref[...])
    kv = _dot(ckv.astype(BF16), wukv_ref[...])
    if not latent:
        ckv_s[...] = ckv
        _store_transposed(kpe_s, kp, MLA_ROPE)
    qn_o[...] = (q[:, :nope_w] * scale).astype(BF16)
    for i in range(MLA_HEADS * MLA_ROPE // LANES):
        qr = q[:, nope_w + i * LANES:nope_w + (i + 1) * LANES]
        if latent:
            qr = _rope(qr, cos_ref[...], sin_ref[...], MLA_ROPE // 4)
        qr_o[:, i * LANES:(i + 1) * LANES] = (qr * scale).astype(BF16)
    if latent:
        kp = _rope(kp, cos_ref[...], sin_ref[...], MLA_ROPE // 4)
    kp_o[...] = kp.astype(BF16)
    kn_o[...] = kv[:, :nope_w].astype(BF16)
    v_o[...] = kv[:, nope_w:].astype(BF16)


def _proj_c_kernel(latent, x_ref, m_ref, g_ref, w_ref, *rest):
    if latent:
        cos_ref, sin_ref, q_o, kd_o, vd_o = rest
    else:
        q_o, kd_o, vd_o, ks_o, vs_o = rest
    qw = SWA_HEADS * SWA_HEAD_DIM
    kw = SWA_KV_HEADS * SWA_HEAD_DIM
    h = _premod(x_ref[...], g_ref[0:1, :], m_ref[...], 0)
    z = _dot(h.astype(BF16), w_ref[...])
    off = qw
    if not latent:
        _store_transposed(ks_o, z[:, qw:qw + kw], kw)
        _store_transposed(vs_o, z[:, qw + kw:qw + 2 * kw], kw)
        off = qw + 2 * kw
    scale = SWA_HEAD_DIM ** -0.5 * LOG2E
    for i in range(qw // LANES):
        y = z[:, i * LANES:(i + 1) * LANES]
        if latent:
            y = _rope(y, cos_ref[...], sin_ref[...], SWA_HEAD_DIM // 4)
        q_o[:, i * LANES:(i + 1) * LANES] = (y * scale).astype(BF16)
    for i in range(2 * kw // LANES):
        y = z[:, off + i * LANES:off + (i + 1) * LANES]
        if latent:
            y = _rope(y, cos_ref[...], sin_ref[...], SWA_HEAD_DIM // 4)
        kd_o[:, i * LANES:(i + 1) * LANES] = y.astype(BF16)
    vd_o[...] = z[:, off + 2 * kw:].astype(BF16)


def _proj_d_kernel(latent, x_ref, m_ref, g_ref, w_ref, *rest):
    if latent:
        q_o, k_o, v_o = rest
    else:
        q_o, k_o, v_o, ks_o, vs_o = rest
    hw = NAT_HEADS * NAT_HEAD_DIM
    h = _premod(x_ref[...], g_ref[0:1, :], m_ref[...], 0)
    z = _dot(h.astype(BF16), w_ref[...])
    q_o[...] = (z[:, :hw] * (NAT_HEAD_DIM ** -0.5 * LOG2E)).astype(BF16)
    k_o[...] = z[:, hw:2 * hw].astype(BF16)
    v_o[...] = z[:, 2 * hw:].astype(BF16)
    if not latent:
        _store_transposed(ks_o, z[:, hw:2 * hw], hw)
        _store_transposed(vs_o, z[:, 2 * hw:], hw)


def _store_transposed(ref, val, width):
    n, _, seq = ref.shape
    for i in range(n):
        ref[i] = val[i * seq:(i + 1) * seq, :].T[:width, :]


def _run_proj(kernel, latent, x, mods_l, g, consts, tables, outs, name, state_seq=None,
              attend=None, attend_inputs=(), qkv=()):
    bx, s, _ = x.shape
    tm = TOK_TILE
    row = (lambda b, j: (b, 0, 0)) if latent else (lambda b, j: (CTX_MOD_ROW, 0, 0))
    in_specs = [
        pl.BlockSpec((None, tm, D_MODEL), lambda b, j: (b, j, 0)),
        pl.BlockSpec((None, 1, N_MOD * D_MODEL), row),
        _full(g.shape),
    ] + [_full(c.shape) for c in consts]
    in_specs += [pl.BlockSpec((tm, LANES), lambda b, j: (j, 0)) for _ in tables]
    in_specs += [_full(a.shape) for a in attend_inputs]
    body = functools.partial(kernel, latent)
    if attend is not None:
        n_proj_in = 3 + len(consts) + len(tables)
        n_att_in = len(attend_inputs)
        proj_body = body
        outs = [(D_MODEL, BF16)] + list(outs)

        def body(*refs):
            ins, att_ins = refs[:n_proj_in], refs[n_proj_in:n_proj_in + n_att_in]
            o_ref = refs[n_proj_in + n_att_in]
            states, scratch = refs[n_proj_in + n_att_in + 1:-len(qkv)], refs[-len(qkv):]
            proj_body(*ins, *scratch, *states)
            for i in range(tm // state_seq):
                rows = slice(i * state_seq, (i + 1) * state_seq)
                attend(*att_ins, *[r.at[rows] for r in scratch], o_ref.at[rows])

    out_specs, out_shape = [], []
    for o in outs:
        if len(o) == 2:
            out_specs.append(pl.BlockSpec((None, tm, o[0]), lambda b, j: (b, j, 0)))
            out_shape.append(jax.ShapeDtypeStruct((bx, s, o[0]), o[1]))
        else:
            per_tile = tm // state_seq
            out_specs.append(pl.BlockSpec((per_tile, o[0], state_seq),
                                          lambda b, j: (b * (s // tm) + j, 0, 0)))
            out_shape.append(jax.ShapeDtypeStruct((bx * s // state_seq, o[0], state_seq), o[1]))
    return pl.pallas_call(
        body,
        grid=(bx, s // tm),
        in_specs=in_specs,
        out_specs=out_specs,
        out_shape=out_shape,
        scratch_shapes=[pltpu.VMEM((tm, w), dt) for w, dt in qkv],
        compiler_params=_params("arbitrary", "arbitrary"),
        name=name,
    )(x, mods_l, g, *consts, *tables, *attend_inputs)


def _attn_a_kernel(has_ctx, q_ref, k_ref, v_ref, *rest):
    if has_ctx:
        kc_ref, vc_ref, o_ref = rest
    else:
        (o_ref,) = rest
    group = ATTN_HEADS // ATTN_KV_HEADS
    head_cols = [slice(h * LANES, (h + 1) * LANES) for h in range(ATTN_HEADS)]
    kv_cols = [slice(j * LANES, (j + 1) * LANES) for j in range(ATTN_KV_HEADS)]
    if not has_ctx:
        tq = q_ref.shape[0]
        qs = [jnp.concatenate([q_ref[:, c] for c in head_cols[j * group:(j + 1) * group]], axis=0)
              for j in range(ATTN_KV_HEADS)]
        out = _attend_group(qs, [k_ref[:, c] for c in kv_cols], [_with_ones(v_ref[:, c]) for c in kv_cols])
        for h, c in enumerate(head_cols):
            o_ref[:, c] = out[h * tq:(h + 1) * tq].astype(BF16)
        return
    for j, kc in enumerate(kv_cols):
        segs = [(k_ref[:, kc], _with_ones(v_ref[:, kc]), None),
                (kc_ref[:, kc], _with_ones(vc_ref[:, kc]), None)]
        for r in range(q_ref.shape[0] // Q_TILE):
            rows = slice(r * Q_TILE, (r + 1) * Q_TILE)
            for c in head_cols[j * group:(j + 1) * group]:
                o_ref[rows, c] = _attend(q_ref[rows, c], segs).astype(BF16)


def _attention_a(q, k, v, kc=None, vc=None):
    b, s, qw = q.shape
    kl, kw = k.shape[1:]
    has_ctx = kc is not None
    tq = min(LATENT_Q_ROWS * HEADS_PER_STEP // ATTN_HEADS if has_ctx else Q_TILE, s)
    in_specs = [
        pl.BlockSpec((None, tq, qw), lambda bi, t: (bi, t, 0)),
        pl.BlockSpec((None, kl, kw), lambda bi, t: (bi, 0, 0)),
        pl.BlockSpec((None, kl, kw), lambda bi, t: (bi, 0, 0)),
    ]
    args = [q, k, v]
    if has_ctx:
        cl = kc.shape[1]
        in_specs += [pl.BlockSpec((None, cl, kw), lambda bi, t: (bi, 0, 0))] * 2
        args += [kc, vc]
    return pl.pallas_call(
        functools.partial(_attn_a_kernel, has_ctx),
        grid=(b, s // tq),
        in_specs=in_specs,
        out_specs=pl.BlockSpec((None, tq, qw), lambda bi, t: (bi, t, 0)),
        out_shape=jax.ShapeDtypeStruct(q.shape, BF16),
        compiler_params=_params("arbitrary", "arbitrary"),
        name="attn_gqa_latent" if has_ctx else "attn_gqa_context",
    )(*args)


def _attn_b_kernel(has_ctx, qn_ref, qr_ref, kn_ref, kp_ref, v_ref, *rest):
    if has_ctx:
        knc_ref, kpc_ref, vc_ref, o_ref = rest
    else:
        (o_ref,) = rest
    tq = min(Q_TILE, qn_ref.shape[0])
    lane = lax.broadcasted_iota(jnp.int32, (tq, LANES), 1)
    low = lane < (LANES // 2)
    quarter = lane // MLA_ROPE
    zero = jnp.zeros((tq, LANES), BF16)
    kp = kp_ref[...]
    kpc = kpc_ref[...] if has_ctx else None
    pairs_per_rope = LANES // MLA_ROPE // 2
    n_pairs = HEADS_PER_STEP // 2
    pair_cols = [slice(p * LANES, (p + 1) * LANES) for p in range(n_pairs)]

    def pair_lhs(p, rows=slice(None)):
        qn = qn_ref[rows, pair_cols[p]]
        rg = p // pairs_per_rope
        qr = qr_ref[rows, rg * LANES:(rg + 1) * LANES]
        return [jnp.concatenate([jnp.where(low if j == 0 else ~low, qn, zero),
                                 jnp.where(quarter == 2 * (p % pairs_per_rope) + j, qr, zero)], axis=1)
                for j in range(2)]

    if not has_ctx:
        qs = [jnp.concatenate(pair_lhs(p), axis=0) for p in range(n_pairs)]
        out = _attend_group(qs, [jnp.concatenate([kn_ref[:, c], kp], axis=1) for c in pair_cols],
                            [_with_ones(v_ref[:, c]) for c in pair_cols])
        for p, c in enumerate(pair_cols):
            o_ref[:, c] = _merge_pair(out[2 * p * tq:(2 * p + 1) * tq], out[(2 * p + 1) * tq:(2 * p + 2) * tq])
        return
    for p, cols in enumerate(pair_cols):
        segs = [(jnp.concatenate([kn_ref[:, cols], kp], axis=1), _with_ones(v_ref[:, cols]), None),
                (jnp.concatenate([knc_ref[:, cols], kpc], axis=1), _with_ones(vc_ref[:, cols]), None)]
        for r in range(qn_ref.shape[0] // tq):
            rows = slice(r * tq, (r + 1) * tq)
            o_ref[rows, cols] = _merge_pair(*[_attend(lhs, segs) for lhs in pair_lhs(p, rows)])


def _attention_b(qn, qr, kn, kp, v, knc=None, kpc=None, vc=None):
    b, s, _ = qn.shape
    kl = kn.shape[1]
    has_ctx = knc is not None
    tq = min(LATENT_Q_ROWS if has_ctx else Q_TILE, s)
    gw = HEADS_PER_STEP * MLA_NOPE
    blk = lambda rows, w, fn: pl.BlockSpec((None, rows, w), fn)
    in_specs = [
        blk(tq, gw, lambda bi, g, t: (bi, t, g)),
        blk(tq, HEADS_PER_STEP * MLA_ROPE, lambda bi, g, t: (bi, t, g)),
        blk(kl, gw, lambda bi, g, t: (bi, 0, g)),
        blk(kl, LANES, lambda bi, g, t: (bi, 0, 0)),
        blk(kl, gw, lambda bi, g, t: (bi, 0, g)),
    ]
    args = [qn, qr, kn, kp, v]
    if has_ctx:
        cl = knc.shape[1]
        in_specs += [
            blk(cl, gw, lambda bi, g, t: (bi, 0, g)),
            blk(cl, LANES, lambda bi, g, t: (bi, 0, 0)),
            blk(cl, gw, lambda bi, g, t: (bi, 0, g)),
        ]
        args += [knc, kpc, vc]
    return pl.pallas_call(
        functools.partial(_attn_b_kernel, has_ctx),
        grid=(b, MLA_HEADS // HEADS_PER_STEP, s // tq),
        in_specs=in_specs,
        out_specs=blk(tq, gw, lambda bi, g, t: (bi, t, g)),
        out_shape=jax.ShapeDtypeStruct(qn.shape, BF16),
        compiler_params=_params("arbitrary", "arbitrary", "arbitrary"),
        name="attn_mla_latent" if has_ctx else "attn_mla_context",
    )(*args)


def _attn_c_kernel(latent, sink_ref, q_ref, kd_ref, vd_ref, *rest):
    if latent:
        kc_ref, vc_ref, o_ref = rest
    else:
        (o_ref,) = rest
    tq = min(Q_TILE, q_ref.shape[0])
    group = SWA_HEADS // SWA_KV_HEADS
    kv_per_step = HEADS_PER_STEP // group
    head0 = pl.program_id(1) * HEADS_PER_STEP
    windows = []
    if latent:
        s_len = kd_ref.shape[0]
        win = tq + 2 * SWA_WINDOW
        for r in range(q_ref.shape[0] // tq):
            t0 = pl.program_id(2) * q_ref.shape[0] + r * tq
            ws = pl.multiple_of(jnp.clip(t0 - SWA_WINDOW, 0, s_len - win), LANES)
            qpos = t0 + lax.broadcasted_iota(jnp.int32, (tq, win), 0)
            kpos = ws + lax.broadcasted_iota(jnp.int32, (tq, win), 1)
            bias = jnp.where(jnp.abs(qpos - kpos) <= SWA_WINDOW, 0.0, NEG_INF).astype(F32)
            windows.append((slice(r * tq, (r + 1) * tq), ws, bias))
    pairs_per_kv = group // 2
    pair_cols = [slice(p * LANES, (p + 1) * LANES) for p in range(HEADS_PER_STEP // 2)]
    kv_cols = [slice(j * LANES, (j + 1) * LANES) for j in range(kv_per_step)]
    if not latent:
        qs = [jnp.concatenate([h for c in pair_cols[j * pairs_per_kv:(j + 1) * pairs_per_kv]
                               for h in _split_pair(q_ref[:, c])], axis=0) for j in range(kv_per_step)]
        out = _attend_group(qs, [kd_ref[:, c] for c in kv_cols], [_with_ones(vd_ref[:, c]) for c in kv_cols],
                            sink_ref[...])
        for p, c in enumerate(pair_cols):
            o_ref[:, c] = _merge_pair(out[2 * p * tq:(2 * p + 1) * tq], out[(2 * p + 1) * tq:(2 * p + 2) * tq])
        return
    sinks = [sink_ref[head0 + h] * LOG2E for h in range(HEADS_PER_STEP)]
    hd = SWA_HEAD_DIM
    for j, kc in enumerate(kv_cols):
        kct = kc_ref[j * hd:(j + 1) * hd, :].astype(BF16)
        vct = vc_ref[j * hd:(j + 1) * hd, :].astype(BF16)
        ctx_seg = (jnp.concatenate([kct, kct], axis=0),
                   jnp.concatenate([vct, vct, jnp.ones((LANES, vct.shape[1]), BF16)], axis=0), None, True)
        for rows, ws, bias in windows:
            segs = [(kd_ref[pl.ds(ws, win), kc], _with_ones(vd_ref[pl.ds(ws, win), kc]), (bias, bias)), ctx_seg]
            for p in range(j * pairs_per_kv, (j + 1) * pairs_per_kv):
                o_ref[rows, pair_cols[p]] = _attend_pair(q_ref[rows, pair_cols[p]], segs, sinks[2 * p:2 * p + 2])


def _sink_logit_block(sink, tq):
    rows = jnp.repeat(sink * LOG2E, tq)[:, None]
    return jnp.where(jnp.arange(LANES)[None, :] == 0, rows, NEG_INF)


def _attention_c(sink, q, kd, vd, kc, vc):
    b, s, _ = q.shape
    kl = kd.shape[1]
    tq = min(LATENT_Q_ROWS, s)
    gw = HEADS_PER_STEP * SWA_HEAD_DIM
    kvw = 2 * SWA_HEAD_DIM * HEADS_PER_STEP // (SWA_HEADS // SWA_KV_HEADS)
    cw, cl = kc.shape[1] * HEADS_PER_STEP // SWA_HEADS, kc.shape[2]
    return pl.pallas_call(
        functools.partial(_attn_c_kernel, True),
        grid=(b, SWA_HEADS // HEADS_PER_STEP, s // tq),
        in_specs=[
            pl.BlockSpec(memory_space=pltpu.SMEM),
            pl.BlockSpec((None, tq, gw), lambda bi, g, t: (bi, t, g)),
            pl.BlockSpec((None, kl, kvw), lambda bi, g, t: (bi, 0, g)),
            pl.BlockSpec((None, kl, kvw), lambda bi, g, t: (bi, 0, g)),
            pl.BlockSpec((None, cw, cl), lambda bi, g, t: (bi, g, 0)),
            pl.BlockSpec((None, cw, cl), lambda bi, g, t: (bi, g, 0)),
        ],
        out_specs=pl.BlockSpec((None, tq, gw), lambda bi, g, t: (bi, t, g)),
        out_shape=jax.ShapeDtypeStruct(q.shape, BF16),
        compiler_params=_params("arbitrary", "arbitrary", "arbitrary"),
        name="attn_swa_latent",
    )(sink, q, kd, vd, kc, vc)


def _attn_d_ctx_kernel(q_ref, k_ref, v_ref, o_ref):
    tq = q_ref.shape[0]
    pair_cols = [slice(p * LANES, (p + 1) * LANES) for p in range(HEADS_PER_STEP // 2)]
    qs = [jnp.concatenate(_split_pair(q_ref[:, c]), axis=0) for c in pair_cols]
    out = _attend_group(qs, [k_ref[:, c] for c in pair_cols], [_with_ones(v_ref[:, c]) for c in pair_cols])
    for p, c in enumerate(pair_cols):
        o_ref[:, c] = _merge_pair(out[2 * p * tq:(2 * p + 1) * tq], out[(2 * p + 1) * tq:(2 * p + 2) * tq])


def _nat_window_start(first_row, rows):
    r0 = jnp.clip(first_row - NAT_WIN_R // 2, 0, rows - NAT_WIN_R)
    return jnp.minimum(r0, rows - NAT_KEY_ROWS)


def _attn_d_lat_kernel(q_ref, k_ref, v_ref, kc_ref, vc_ref, bias_ref, o_ref):
    rows = k_ref.shape[0] // GRID_W
    slab = NAT_KEY_ROWS * GRID_W
    start = _nat_window_start(pl.program_id(0) * NAT_TILE_ROWS, rows) * GRID_W
    start = pl.multiple_of(start, GRID_W)
    ones = jnp.ones((LANES, kc_ref.shape[1]), BF16)
    for p in range(HEADS_PER_STEP // 2):
        cols = slice(p * LANES, (p + 1) * LANES)
        segs = [(k_ref[pl.ds(start, slab), cols], _with_ones(v_ref[pl.ds(start, slab), cols]),
                 (bias_ref[2 * p], bias_ref[2 * p + 1])),
                (kc_ref[cols, :].astype(BF16),
                 jnp.concatenate([vc_ref[cols, :].astype(BF16), ones], axis=0), None, True)]
        o_ref[:, cols] = _attend_pair(q_ref[:, cols], segs)


def _attention_d_lat(q, k, v, kc, vc, bias):
    b, s, _ = q.shape
    cl = kc.shape[2]
    tq = NAT_TILE_ROWS * GRID_W
    slab = NAT_KEY_ROWS * GRID_W
    gw = HEADS_PER_STEP * NAT_HEAD_DIM
    return pl.pallas_call(
        _attn_d_lat_kernel,
        grid=(s // tq, NAT_HEADS // HEADS_PER_STEP, b),
        in_specs=[
            pl.BlockSpec((None, tq, gw), lambda t, g, bi: (bi, t, g)),
            pl.BlockSpec((None, s, gw), lambda t, g, bi: (bi, 0, g)),
            pl.BlockSpec((None, s, gw), lambda t, g, bi: (bi, 0, g)),
            pl.BlockSpec((None, gw, cl), lambda t, g, bi: (bi, g, 0)),
            pl.BlockSpec((None, gw, cl), lambda t, g, bi: (bi, g, 0)),
            pl.BlockSpec((HEADS_PER_STEP, None, tq, slab), lambda t, g, bi: (g, t, 0, 0)),
        ],
        out_specs=pl.BlockSpec((None, tq, gw), lambda t, g, bi: (bi, t, g)),
        out_shape=jax.ShapeDtypeStruct(q.shape, BF16),
        compiler_params=_params("arbitrary", "arbitrary", "arbitrary"),
        name="attn_nat_latent",
    )(q, k, v, kc, vc, bias)


def _mla_expand_kernel(c_ref, w_ref, kn_ref, v_ref):
    kv = _dot(c_ref[...].astype(BF16), w_ref[...])
    half = kv.shape[1] // 2
    kn_ref[...] = kv[:, :half].astype(BF16)
    v_ref[...] = kv[:, half:].astype(BF16)


def _mla_expand(ckv, w_ukv):
    b, l, c = ckv.shape
    n = w_ukv.shape[1] // 2
    out = pl.BlockSpec((None, l, n), lambda bi: (bi, 0, 0))
    return pl.pallas_call(
        _mla_expand_kernel,
        grid=(b,),
        in_specs=[pl.BlockSpec((None, l, c), lambda bi: (bi, 0, 0)), _full(w_ukv.shape)],
        out_specs=[out, out],
        out_shape=[jax.ShapeDtypeStruct((b, l, n), BF16)] * 2,
        compiler_params=_params("arbitrary"),
        name="mla_expand_cache",
    )(ckv, w_ukv)


def _finish_kernel(o_ref, x_ref, m_ref, g_ref, wo_ref, w1_ref, w2_ref, y_ref, x1_ref, h2_ref):
    f = pl.program_id(2)

    @pl.when(f == 0)
    def _():
        m = m_ref[...]
        for c in range(y_ref.shape[0] // TOK_CHAIN):
            rows = slice(c * TOK_CHAIN, (c + 1) * TOK_CHAIN)
            a = _dot(o_ref[rows, :], wo_ref[...])
            x1 = x_ref[rows, :] + _rms(a, _mod(m, 2) * g_ref[1:2, :])
            x1_ref[rows, :] = x1
            h2_ref[rows, :] = _premod(x1, g_ref[2:3, :], m, 1).astype(BF16)
        y_ref[...] = jnp.zeros_like(y_ref)

    h = h2_ref[...]
    for c in range(w1_ref.shape[1] // MLP_FF_CHAIN):
        cols = slice(c * MLP_FF_CHAIN, (c + 1) * MLP_FF_CHAIN)
        u = _dot(h, w1_ref[:, cols].astype(BF16))
        u = jnp.square(jnp.maximum(u, 0.0)).astype(BF16)
        y_ref[...] += _dot(u, w2_ref[cols, :].astype(BF16))

    @pl.when(f == pl.num_programs(2) - 1)
    def _():
        y_ref[...] = x1_ref[...] + _rms(y_ref[...], _mod(m_ref[...], 5) * g_ref[3:4, :])


def _finish_layer(o, x, mods_l, g, wo, w1, w2, layer, latent):
    bx, s, _ = x.shape
    tm, tf = MLP_TOK_TILE, MLP_FF_TILE
    row = (lambda b, j, f: (b, 0, 0)) if latent else (lambda b, j, f: (CTX_MOD_ROW, 0, 0))
    tok = lambda w: pl.BlockSpec((None, tm, w), lambda b, j, f: (b, j, 0))
    return pl.pallas_call(
        _finish_kernel,
        grid=(bx, s // tm, D_FF // tf),
        in_specs=[tok(o.shape[-1]), tok(D_MODEL),
                  pl.BlockSpec((None, 1, N_MOD * D_MODEL), row), _full(g.shape), _full(wo.shape),
                  pl.BlockSpec((None, D_MODEL, tf), lambda b, j, f: (layer, 0, f)),
                  pl.BlockSpec((None, tf, D_MODEL), lambda b, j, f: (layer, f, 0))],
        out_specs=tok(D_MODEL),
        out_shape=jax.ShapeDtypeStruct(x.shape, F32),
        scratch_shapes=[pltpu.VMEM((tm, D_MODEL), F32), pltpu.VMEM((tm, D_MODEL), BF16)],
        compiler_params=pltpu.CompilerParams(dimension_semantics=("arbitrary",) * 3,
                                             vmem_limit_bytes=MLP_VMEM_LIMIT),
        name="out_proj_mlp",
    )(o, x, mods_l, g, wo, w1, w2)


def _rope_tables(s, dim):
    quarter = dim // 4
    t = jnp.arange(s)
    pos = jnp.stack([t // GRID_W, t % GRID_W], axis=-1).astype(F32)
    inv = ROPE_THETA ** (-jnp.arange(quarter, dtype=F32) / quarter)
    ang = pos[:, :, None] * inv
    cos = jnp.broadcast_to(jnp.cos(ang)[:, :, None, :], (s, 2, 2, quarter)).reshape(s, dim)
    sign = jnp.array([-1.0, 1.0], F32)[None, None, :, None]
    sin = (jnp.sin(ang)[:, :, None, :] * sign).reshape(s, dim)
    reps = LANES // dim
    return jnp.tile(cos, (1, reps)), jnp.tile(sin, (1, reps))


def _dup_heads(w, heads, dim):
    lead = w.shape[:-1]
    w = w.reshape(lead + (heads, 1, dim))
    return jnp.broadcast_to(w, lead + (heads, 2, dim)).reshape(lead + (heads * 2 * dim,))


def _nat_dense_bias(rpb, rows):
    heads = rpb.shape[0]
    c = np.arange(GRID_W)
    c0 = np.clip(c - NAT_WIN_C // 2, 0, GRID_W - NAT_WIN_C)
    in_c = (c[None, :] >= c0[:, None]) & (c[None, :] < c0[:, None] + NAT_WIN_C)
    dc = c[None, :] - c[:, None] + NAT_WIN_C - 1
    onehot = (dc[None] == np.arange(2 * NAT_WIN_C - 1)[:, None, None]) & in_c[None]
    toe = jnp.einsum("had,dck->hack", rpb * LOG2E, jnp.asarray(onehot, F32),
                     precision=lax.Precision.HIGHEST)
    toe = jnp.where(jnp.asarray(in_c)[None, None], toe, NEG_INF)
    pad = jnp.full((heads, 1, GRID_W, GRID_W), NEG_INF, F32)
    ext = jnp.concatenate([pad, toe, pad], axis=1)
    pairs = jnp.concatenate([ext[:, :-1], ext[:, 1:]], axis=-1)
    tiles = rows // NAT_TILE_ROWS
    tq, slab = NAT_TILE_ROWS * GRID_W, NAT_KEY_ROWS * GRID_W
    n_off = 2 * NAT_WIN_R
    return pl.pallas_call(
        functools.partial(_nat_bias_kernel, rows),
        grid=(heads,),
        in_specs=[pl.BlockSpec((None, n_off, GRID_W, LANES), lambda h: (h, 0, 0, 0))],
        out_specs=pl.BlockSpec((None, tiles, tq, slab), lambda h: (h, 0, 0, 0)),
        out_shape=jax.ShapeDtypeStruct((heads, tiles, tq, slab), F32),
        compiler_params=_params("arbitrary"),
        name="nat_bias_expand",
    )(pairs)


def _nat_bias_kernel(rows, pairs_ref, o_ref):
    low = lax.broadcasted_iota(jnp.int32, (GRID_W, LANES), 1) < GRID_W
    masked = jnp.full((GRID_W, LANES), NEG_INF, F32)
    for r in range(rows):
        tile, i = divmod(r, NAT_TILE_ROWS)
        r0 = min(max(r - NAT_WIN_R // 2, 0), rows - NAT_WIN_R)
        first0 = min(max(tile * NAT_TILE_ROWS - NAT_WIN_R // 2, 0), rows - NAT_WIN_R)
        ws = min(first0, rows - NAT_KEY_ROWS)
        for jb in range(NAT_KEY_ROWS // 2):
            kr = ws + 2 * jb
            ok_lo, ok_hi = r0 <= kr < r0 + NAT_WIN_R, r0 <= kr + 1 < r0 + NAT_WIN_R
            blk = masked
            if ok_lo or ok_hi:
                blk = pairs_ref[kr - r + NAT_WIN_R]
                if not ok_lo:
                    blk = jnp.where(low, NEG_INF, blk)
                elif not ok_hi:
                    blk = jnp.where(low, blk, NEG_INF)
            o_ref[tile, i * GRID_W:(i + 1) * GRID_W, jb * LANES:(jb + 1) * LANES] = blk


def kernel(x_prompt, x_sample, cache_l0_k, cache_l0_v, cache_l1_ckv, cache_l1_kpe, cache_l2_k, cache_l2_v, cache_l3_k, cache_l3_v, c, c_ctx, ada_w, ada_b, norm_g, mlp_w1, mlp_w2, attn_w_qkv, attn_q_norm, attn_k_norm, attn_w_o, mla_w_in, mla_q_norm, mla_kv_norm, mla_w_uq, mla_w_ukv, mla_w_o, swa_w_qkv, swa_sink, swa_w_o, nat_w_qkv, nat_rpb, nat_w_o):
    nb, seq, d = x_prompt.shape
    db, dseq, _ = x_sample.shape
    past = cache_l0_k.shape[1]
    ctx_b = nb * seq // dseq
    xp = x_prompt.reshape(ctx_b, dseq, d)
    xs = x_sample

    cond = jnp.zeros((COND_ROWS, d), F32).at[:db].set(c).at[CTX_MOD_ROW].set(c_ctx)
    mods = _modulation(cond, ada_w, ada_b).reshape(DEPTH, COND_ROWS, 1, N_MOD * d)

    row = lambda v: v.reshape(1, -1)

    def head_major(cache):
        b_, l_, h_, dh = cache.shape
        return cache.transpose(0, 2, 3, 1).reshape(b_, h_ * dh, l_)

    def token_major(state, heads):
        b_, w_, l_ = state.shape
        return state.reshape(b_, heads, w_ // heads, l_).transpose(0, 3, 1, 2)

    def finish(o, x, layer, wo, latent):
        return _finish_layer(o, x, mods[layer], norm_g[layer], wo, mlp_w1, mlp_w2, layer, latent)

    g = norm_g[0]
    w = attn_w_qkv.astype(BF16)
    wo = attn_w_o.astype(BF16)
    consts = [w, row(attn_q_norm), row(attn_k_norm)]
    kvw = ATTN_KV_HEADS * ATTN_HEAD_DIM
    o, l0_k, l0_v = _run_proj(
        _proj_a_kernel, False, xp, mods[0], g, consts, [], [(kvw, F32), (kvw, F32)], "proj_attn_gqa_context",
        seq, attend=functools.partial(_attn_a_kernel, False), qkv=[(d, BF16), (kvw, BF16), (kvw, BF16)])
    xp = finish(o, xp, 0, wo, False)
    perm = np.arange(ATTN_HEAD_DIM).reshape(2, 2, ATTN_HEAD_DIM // 4).transpose(1, 0, 2).reshape(-1)
    qk_heads = ATTN_HEADS + ATTN_KV_HEADS
    cols = (np.arange(qk_heads)[:, None] * ATTN_HEAD_DIM + perm[None, :]).reshape(-1)
    w_lat = jnp.concatenate([attn_w_qkv[:, cols], attn_w_qkv[:, qk_heads * ATTN_HEAD_DIM:]], axis=1).astype(BF16)
    consts_lat = [w_lat, row(attn_q_norm[perm]), row(attn_k_norm[perm])]
    tables = [t[:, perm] for t in _rope_tables(dseq, ATTN_HEAD_DIM)]
    q, k, v = _run_proj(_proj_a_kernel, True, xs, mods[0], g, consts_lat, tables,
                        [(d, BF16), (kvw, BF16), (kvw, BF16)], "proj_gqa_latent")
    o = _attention_a(q, k, v, cache_l0_k[..., perm].reshape(db, past, kvw).astype(BF16),
                     cache_l0_v.reshape(db, past, kvw).astype(BF16))
    xs = finish(o, xs, 0, wo, True)
    new_l0 = (l0_k.reshape(nb, seq, ATTN_KV_HEADS, ATTN_HEAD_DIM),
              l0_v.reshape(nb, seq, ATTN_KV_HEADS, ATTN_HEAD_DIM))

    g = norm_g[1]
    nq, nkv = MLA_Q_LORA, MLA_KV_LORA
    w_in = jnp.concatenate([mla_w_in[:, :nq + nkv]] + [mla_w_in[:, nq + nkv:]] * (LANES // MLA_ROPE),
                           axis=1).astype(BF16)
    wuq = mla_w_uq.reshape(nq, MLA_HEADS, MLA_NOPE + MLA_ROPE)
    wuq = jnp.concatenate([wuq[:, :, :MLA_NOPE].reshape(nq, -1), wuq[:, :, MLA_NOPE:].reshape(nq, -1)],
                          axis=1).astype(BF16)
    wukv = mla_w_ukv.reshape(nkv, MLA_HEADS, MLA_NOPE + MLA_V_DIM)
    wukv = jnp.concatenate([wukv[:, :, :MLA_NOPE].reshape(nkv, -1), wukv[:, :, MLA_NOPE:].reshape(nkv, -1)],
                           axis=1).astype(BF16)
    wo = mla_w_o.astype(BF16)
    consts = [w_in, row(mla_q_norm), row(mla_kv_norm), wuq, wukv]
    hw = MLA_HEADS * MLA_NOPE
    rw = MLA_HEADS * MLA_ROPE
    outs = [(hw, BF16), (rw, BF16), (hw, BF16), (hw, BF16), (LANES, BF16)]
    o, l1_ckv, l1_kpe = _run_proj(
        _proj_b_kernel, False, xp, mods[1], g, consts, [], [(nkv, F32), (MLA_ROPE, F32, "T")],
        "proj_attn_mla_context", seq, qkv=outs,
        attend=lambda qn, qr, kn, v, kp, o_ref: _attn_b_kernel(False, qn, qr, kn, kp, v, o_ref))
    xp = finish(o, xp, 1, wo, False)
    tables = list(_rope_tables(dseq, MLA_ROPE))
    qn, qr, kn, v, kp = _run_proj(_proj_b_kernel, True, xs, mods[1], g, consts, tables, outs,
                                  "proj_mla_latent")
    knc, vc = _mla_expand(cache_l1_ckv, wukv)
    kpc = jnp.tile(cache_l1_kpe, (1, 1, LANES // MLA_ROPE)).astype(BF16)
    o = _attention_b(qn, qr, kn, kp, v, knc, kpc, vc)
    xs = finish(o, xs, 1, wo, True)
    new_l1 = (l1_ckv.reshape(nb, seq, nkv), l1_kpe.transpose(0, 2, 1))

    g = norm_g[2]
    qw = SWA_HEADS * SWA_HEAD_DIM
    kw = SWA_KV_HEADS * SWA_HEAD_DIM
    wq, wk, wv = swa_w_qkv[:, :qw], swa_w_qkv[:, qw:qw + kw], swa_w_qkv[:, qw + kw:]
    wkd = _dup_heads(wk, SWA_KV_HEADS, SWA_HEAD_DIM)
    wvd = _dup_heads(wv, SWA_KV_HEADS, SWA_HEAD_DIM)
    w_ctx = jnp.concatenate([wq, wk, wv, wkd, wvd], axis=1).astype(BF16)
    w_lat = jnp.concatenate([wq, wkd, wvd], axis=1).astype(BF16)
    wo = swa_w_o.astype(BF16)
    o, l2_k, l2_v = _run_proj(
        _proj_c_kernel, False, xp, mods[2], g, [w_ctx], [], [(kw, F32, "T"), (kw, F32, "T")],
        "proj_attn_swa_context", seq, attend=functools.partial(_attn_c_kernel, False),
        attend_inputs=[_sink_logit_block(swa_sink, seq)], qkv=[(qw, BF16), (2 * kw, BF16), (2 * kw, BF16)])
    xp = finish(o, xp, 2, wo, False)
    tables = list(_rope_tables(dseq, SWA_HEAD_DIM))
    q, kd, vd = _run_proj(_proj_c_kernel, True, xs, mods[2], g, [w_lat], tables,
                          [(qw, BF16), (2 * kw, BF16), (2 * kw, BF16)], "proj_swa_latent")
    o = _attention_c(swa_sink, q, kd, vd, head_major(cache_l2_k), head_major(cache_l2_v))
    xs = finish(o, xs, 2, wo, True)
    new_l2 = (token_major(l2_k, SWA_KV_HEADS), token_major(l2_v, SWA_KV_HEADS))

    g = norm_g[3]
    hw = NAT_HEADS * NAT_HEAD_DIM
    w = nat_w_qkv.astype(BF16)
    wo = nat_w_o.astype(BF16)
    o, l3_k, l3_v = _run_proj(
        _proj_d_kernel, False, xp, mods[3], g, [w], [], [(hw, F32, "T"), (hw, F32, "T")],
        "proj_attn_nat_context", seq, attend=_attn_d_ctx_kernel, qkv=[(hw, BF16)] * 3)
    xp = finish(o, xp, 3, wo, False)
    q, k, v = _run_proj(_proj_d_kernel, True, xs, mods[3], g, [w], [],
                        [(hw, BF16), (hw, BF16), (hw, BF16)], "proj_nat_latent")
    o = _attention_d_lat(q, k, v, head_major(cache_l3_k), head_major(cache_l3_v),
                         _nat_dense_bias(nat_rpb, dseq // GRID_W))
    xs = finish(o, xs, 3, wo, True)
    new_l3 = (token_major(l3_k, NAT_HEADS), token_major(l3_v, NAT_HEADS))

    return (xp.reshape(nb, seq, d), xs) + new_l0 + new_l1 + new_l2 + new_l3
```

```python
import functools

import numpy as np

import jax
import jax.numpy as jnp
from jax import lax
from jax.experimental import pallas as pl
from jax.experimental.pallas import tpu as pltpu

F32 = jnp.float32
BF16 = jnp.bfloat16

D_MODEL = 1024
DEPTH = 4
N_MOD = 6
D_FF = 4 * D_MODEL
GRID_W = 64
ROPE_THETA = 10000.0
NORM_EPS = 1e-6
NEG_INF = -1e30
LOG2E = 1.4426950408889634

ATTN_HEADS, ATTN_KV_HEADS, ATTN_HEAD_DIM = 8, 2, 128
MLA_HEADS, MLA_Q_LORA, MLA_KV_LORA = 16, 384, 256
MLA_NOPE, MLA_ROPE, MLA_V_DIM = 64, 32, 64
MLA_SCALE = (MLA_NOPE + MLA_ROPE) ** -0.5
SWA_HEADS, SWA_KV_HEADS, SWA_HEAD_DIM, SWA_WINDOW = 16, 4, 64, 128
NAT_HEADS, NAT_HEAD_DIM, NAT_WIN_R, NAT_WIN_C = 16, 64, 8, 16

LANES = 128
COND_ROWS = 16
CTX_MOD_ROW = 8
VMEM_LIMIT = 48 * 1024 * 1024

TOK_TILE = 512
TOK_CHAIN = 256
MLP_TOK_TILE = 1024
MLP_FF_TILE = 1024
MLP_VMEM_LIMIT = 56 * 1024 * 1024
Q_TILE = 256
LATENT_Q_ROWS = 512
HEADS_PER_STEP = 16
NAT_TILE_ROWS = Q_TILE // GRID_W
NAT_KEY_ROWS = NAT_WIN_R + NAT_TILE_ROWS


def _params(*sem):
    return pltpu.CompilerParams(dimension_semantics=sem, vmem_limit_bytes=VMEM_LIMIT)


def _full(shape):
    nd = len(shape)
    return pl.BlockSpec(shape, lambda *_: (0,) * nd)


def _rms(x, g):
    return x * lax.rsqrt(jnp.mean(x * x, axis=-1, keepdims=True) + NORM_EPS) * g


def _mod(m, i):
    return m[:, i * D_MODEL:(i + 1) * D_MODEL]


def _premod(x, g, m, sub):
    return _rms(x, g * (1.0 + _mod(m, 3 * sub + 1))) + _mod(m, 3 * sub)


def _rope(x, cos, sin_signed, quarter):
    n = x.shape[-1]
    lane = lax.broadcasted_iota(jnp.int32, x.shape, 1)
    first = ((lane // quarter) % 2) == 0
    partner = jnp.where(first, pltpu.roll(x, n - quarter, 1), pltpu.roll(x, quarter, 1))
    return x * cos + partner * sin_signed


def _dot(a, b):
    return jnp.dot(a, b, preferred_element_type=F32)


def _dot_nt(a, b):
    return lax.dot_general(a, b, (((1,), (1,)), ((), ())), preferred_element_type=F32)


def _with_ones(v):
    return jnp.concatenate([v, jnp.ones_like(v)], axis=1)


def _attend(q, segs, sink=None):
    logits = []
    for seg in segs:
        k, bias, transposed = seg[0], seg[2], len(seg) > 3 and seg[3]
        s = _dot(q, k) if transposed else _dot_nt(q, k)
        logits.append(s if bias is None else s + bias)
    m = logits[0].max(axis=-1, keepdims=True)
    for s in logits[1:]:
        m = jnp.maximum(m, s.max(axis=-1, keepdims=True))
    if sink is not None:
        m = jnp.maximum(m, sink)
    acc = None
    for s, seg in zip(logits, segs):
        p = jnp.exp2(s - m).astype(BF16)
        pv = _dot_nt(p, seg[1]) if len(seg) > 3 and seg[3] else _dot(p, seg[1])
        acc = pv if acc is None else acc + pv
    den = acc[:, LANES:LANES + 1]
    if sink is not None:
        den = den + jnp.exp2(sink - m)
    return acc[:, :LANES] / den


def _attend_group(qs, ks, v1s, sink_logits=None):
    tq = qs[0].shape[0]
    s = jnp.concatenate([_dot_nt(q, k) for q, k in zip(qs, ks)], axis=0)
    if sink_logits is not None:
        s = jnp.concatenate([s, sink_logits], axis=1)
        zeros = jnp.zeros((LANES, LANES), BF16)
        tail = jnp.concatenate([zeros, jnp.ones_like(zeros)], axis=1)
        v1s = [jnp.concatenate([v1, tail], axis=0) for v1 in v1s]
    p = jnp.exp2(s - s.max(axis=-1, keepdims=True)).astype(BF16)
    acc = jnp.concatenate([_dot(p[i * tq:(i + 1) * tq], v1) for i, v1 in enumerate(v1s)], axis=0)
    return acc[:, :LANES] / acc[:, LANES:LANES + 1]


def _split_pair(q):
    low = lax.broadcasted_iota(jnp.int32, q.shape, 1) < (LANES // 2)
    zero = jnp.zeros_like(q)
    return [jnp.where(low, q, zero), jnp.where(low, zero, q)]


def _merge_pair(o0, o1):
    low = lax.broadcasted_iota(jnp.int32, o0.shape, 1) < (LANES // 2)
    return jnp.where(low, o0, o1).astype(BF16)


def _attend_pair(q, segs, sinks=None):
    outs = []
    for j, qj in enumerate(_split_pair(q)):
        segs_j = [(s[0], s[1], None if s[2] is None else s[2][j]) + tuple(s[3:]) for s in segs]
        outs.append(_attend(qj, segs_j, None if sinks is None else sinks[j]))
    return _merge_pair(*outs)


def _mods_kernel(cond_ref, w_ref, b_ref, o_ref):
    cnd = cond_ref[...]
    act = cnd * jax.nn.sigmoid(cnd)
    o_ref[...] = _dot(act.astype(BF16), w_ref[...].astype(BF16)) + b_ref[...]


def _modulation(cond, ada_w, ada_b):
    tn = 1536
    n = N_MOD * D_MODEL
    return pl.pallas_call(
        _mods_kernel,
        grid=(DEPTH, n // tn),
        in_specs=[
            _full((COND_ROWS, D_MODEL)),
            pl.BlockSpec((None, D_MODEL, tn), lambda l, j: (l, 0, j)),
            pl.BlockSpec((None, 1, tn), lambda l, j: (l, 0, j)),
        ],
        out_specs=pl.BlockSpec((None, COND_ROWS, tn), lambda l, j: (l, 0, j)),
        out_shape=jax.ShapeDtypeStruct((DEPTH, COND_ROWS, n), F32),
        compiler_params=_params("arbitrary", "arbitrary"),
        name="adaln_mods",
    )(cond, ada_w, ada_b.reshape(DEPTH, 1, n))


def _proj_a_kernel(latent, x_ref, m_ref, g_ref, w_ref, qn_ref, kn_ref, *rest):
    if latent:
        cos_ref, sin_ref, q_ref, k_ref, v_ref = rest
    else:
        q_ref, k_ref, v_ref, ks_ref, vs_ref = rest
    hd = ATTN_HEAD_DIM
    h = _premod(x_ref[...], g_ref[0:1, :], m_ref[...], 0)
    z = _dot(h.astype(BF16), w_ref[...])
    n_qk = ATTN_HEADS + ATTN_KV_HEADS
    ys = [_rms(z[:, i * hd:(i + 1) * hd], qn_ref[...] if i < ATTN_HEADS else kn_ref[...]) for i in range(n_qk)]
    if not latent:
        for j in range(ATTN_KV_HEADS):
            ks_ref[:, j * hd:(j + 1) * hd] = ys[ATTN_HEADS + j]
    else:
        ys = [y * cos_ref[...] + pltpu.roll(y, hd // 2, 1) * sin_ref[...] for y in ys]
    for i in range(ATTN_HEADS):
        q_ref[:, i * hd:(i + 1) * hd] = (ys[i] * (hd ** -0.5 * LOG2E)).astype(BF16)
    for j in range(ATTN_KV_HEADS):
        k_ref[:, j * hd:(j + 1) * hd] = ys[ATTN_HEADS + j].astype(BF16)
    v = z[:, n_qk * hd:]
    v_ref[...] = v.astype(BF16)
    if not latent:
        vs_ref[...] = v


def _proj_b_kernel(latent, x_ref, m_ref, g_ref, win_ref, qn_ref, kvn_ref, wuq_ref, wukv_ref, *rest):
    if latent:
        cos_ref, sin_ref, qn_o, qr_o, kn_o, v_o, kp_o = rest
    else:
        qn_o, qr_o, kn_o, v_o, kp_o, ckv_s, kpe_s = rest
    nq, nkv = MLA_Q_LORA, MLA_KV_LORA
    nope_w = MLA_HEADS * MLA_NOPE
    scale = MLA_SCALE * LOG2E
    h = _premod(x_ref[...], g_ref[0:1, :], m_ref[...], 0)
    z = _dot(h.astype(BF16), win_ref[...])
    cq = _rms(z[:, :nq], qn_ref[...])
    ckv = _rms(z[:, nq:nq + nkv], kvn_ref[...])
    kp = z[:, nq + nkv:]
    q = _dot(cq.astype(BF16), wuq_ref[...])
    kv = _dot(ckv.astype(BF16), wukv_ref[...])
    if not latent:
        ckv_s[...] = ckv
        _store_transposed(kpe_s, kp, MLA_ROPE)
    qn_o[...] = (q[:, :nope_w] * scale).astype(BF16)
    for i in range(MLA_HEADS * MLA_ROPE // LANES):
        qr = q[:, nope_w + i * LANES:nope_w + (i + 1) * LANES]
        if latent:
            qr = _rope(qr, cos_ref[...], sin_ref[...], MLA_ROPE // 4)
        qr_o[:, i * LANES:(i + 1) * LANES] = (qr * scale).astype(BF16)
    if latent:
        kp = _rope(kp, cos_ref[...], sin_ref[...], MLA_ROPE // 4)
    kp_o[...] = kp.astype(BF16)
    kn_o[...] = kv[:, :nope_w].astype(BF16)
    v_o[...] = kv[:, nope_w:].astype(BF16)


def _proj_c_kernel(latent, x_ref, m_ref, g_ref, w_ref, *rest):
    if latent:
        cos_ref, sin_ref, q_o, kd_o, vd_o = rest
    else:
        q_o, kd_o, vd_o, ks_o, vs_o = rest
    qw = SWA_HEADS * SWA_HEAD_DIM
    kw = SWA_KV_HEADS * SWA_HEAD_DIM
    h = _premod(x_ref[...], g_ref[0:1, :], m_ref[...], 0)
    z = _dot(h.astype(BF16), w_ref[...])
    off = qw
    if not latent:
        _store_transposed(ks_o, z[:, qw:qw + kw], kw)
        _store_transposed(vs_o, z[:, qw + kw:qw + 2 * kw], kw)
        off = qw + 2 * kw
    scale = SWA_HEAD_DIM ** -0.5 * LOG2E
    for i in range(qw // LANES):
        y = z[:, i * LANES:(i + 1) * LANES]
        if latent:
            y = _rope(y, cos_ref[...], sin_ref[...], SWA_HEAD_DIM // 4)
        q_o[:, i * LANES:(i + 1) * LANES] = (y * scale).astype(BF16)
    for i in range(2 * kw // LANES):
        y = z[:, off + i * LANES:off + (i + 1) * LANES]
        if latent:
            y = _rope(y, cos_ref[...], sin_ref[...], SWA_HEAD_DIM // 4)
        kd_o[:, i * LANES:(i + 1) * LANES] = y.astype(BF16)
    vd_o[...] = z[:, off + 2 * kw:].astype(BF16)


def _proj_d_kernel(latent, x_ref, m_ref, g_ref, w_ref, *rest):
    if latent:
        q_o, k_o, v_o = rest
    else:
        q_o, k_o, v_o, ks_o, vs_o = rest
    hw = NAT_HEADS * NAT_HEAD_DIM
    h = _premod(x_ref[...], g_ref[0:1, :], m_ref[...], 0)
    z = _dot(h.astype(BF16), w_ref[...])
    q_o[...] = (z[:, :hw] * (NAT_HEAD_DIM ** -0.5 * LOG2E)).astype(BF16)
    k_o[...] = z[:, hw:2 * hw].astype(BF16)
    v_o[...] = z[:, 2 * hw:].astype(BF16)
    if not latent:
        _store_transposed(ks_o, z[:, hw:2 * hw], hw)
        _store_transposed(vs_o, z[:, 2 * hw:], hw)


def _store_transposed(ref, val, width):
    n, _, seq = ref.shape
    for i in range(n):
        ref[i] = val[i * seq:(i + 1) * seq, :].T[:width, :]


def _run_proj(kernel, latent, x, mods_l, g, consts, tables, outs, name, state_seq=None,
              attend=None, attend_inputs=(), qkv=()):
    bx, s, _ = x.shape
    tm = TOK_TILE
    row = (lambda b, j: (b, 0, 0)) if latent else (lambda b, j: (CTX_MOD_ROW, 0, 0))
    in_specs = [
        pl.BlockSpec((None, tm, D_MODEL), lambda b, j: (b, j, 0)),
        pl.BlockSpec((None, 1, N_MOD * D_MODEL), row),
        _full(g.shape),
    ] + [_full(c.shape) for c in consts]
    in_specs += [pl.BlockSpec((tm, LANES), lambda b, j: (j, 0)) for _ in tables]
    in_specs += [_full(a.shape) for a in attend_inputs]
    body = functools.partial(kernel, latent)
    if attend is not None:
        n_proj_in = 3 + len(consts) + len(tables)
        n_att_in = len(attend_inputs)
        proj_body = body
        outs = [(D_MODEL, BF16)] + list(outs)

        def body(*refs):
            ins, att_ins = refs[:n_proj_in], refs[n_proj_in:n_proj_in + n_att_in]
            o_ref = refs[n_proj_in + n_att_in]
            states, scratch = refs[n_proj_in + n_att_in + 1:-len(qkv)], refs[-len(qkv):]
            proj_body(*ins, *scratch, *states)
            for i in range(tm // state_seq):
                rows = slice(i * state_seq, (i + 1) * state_seq)
                attend(*att_ins, *[r.at[rows] for r in scratch], o_ref.at[rows])

    out_specs, out_shape = [], []
    for o in outs:
        if len(o) == 2:
            out_specs.append(pl.BlockSpec((None, tm, o[0]), lambda b, j: (b, j, 0)))
            out_shape.append(jax.ShapeDtypeStruct((bx, s, o[0]), o[1]))
        else:
            per_tile = tm // state_seq
            out_specs.append(pl.BlockSpec((per_tile, o[0], state_seq),
                                          lambda b, j: (b * (s // tm) + j, 0, 0)))
            out_shape.append(jax.ShapeDtypeStruct((bx * s // state_seq, o[0], state_seq), o[1]))
    return pl.pallas_call(
        body,
        grid=(bx, s // tm),
        in_specs=in_specs,
        out_specs=out_specs,
        out_shape=out_shape,
        scratch_shapes=[pltpu.VMEM((tm, w), dt) for w, dt in qkv],
        compiler_params=_params("arbitrary", "arbitrary"),
        name=name,
    )(x, mods_l, g, *consts, *tables, *attend_inputs)


def _attn_a_kernel(has_ctx, q_ref, k_ref, v_ref, *rest):
    if has_ctx:
        kc_ref, vc_ref, o_ref = rest
    else:
        (o_ref,) = rest
    group = ATTN_HEADS // ATTN_KV_HEADS
    head_cols = [slice(h * LANES, (h + 1) * LANES) for h in range(ATTN_HEADS)]
    kv_cols = [slice(j * LANES, (j + 1) * LANES) for j in range(ATTN_KV_HEADS)]
    if not has_ctx:
        tq = q_ref.shape[0]
        qs = [jnp.concatenate([q_ref[:, c] for c in head_cols[j * group:(j + 1) * group]], axis=0)
              for j in range(ATTN_KV_HEADS)]
        out = _attend_group(qs, [k_ref[:, c] for c in kv_cols], [_with_ones(v_ref[:, c]) for c in kv_cols])
        for h, c in enumerate(head_cols):
            o_ref[:, c] = out[h * tq:(h + 1) * tq].astype(BF16)
        return
    for j, kc in enumerate(kv_cols):
        segs = [(k_ref[:, kc], _with_ones(v_ref[:, kc]), None),
                (kc_ref[:, kc], _with_ones(vc_ref[:, kc]), None)]
        for r in range(q_ref.shape[0] // Q_TILE):
            rows = slice(r * Q_TILE, (r + 1) * Q_TILE)
            for c in head_cols[j * group:(j + 1) * group]:
                o_ref[rows, c] = _attend(q_ref[rows, c], segs).astype(BF16)


def _attention_a(q, k, v, kc=None, vc=None):
    b, s, qw = q.shape
    kl, kw = k.shape[1:]
    has_ctx = kc is not None
    tq = min(LATENT_Q_ROWS if has_ctx else Q_TILE, s)
    in_specs = [
        pl.BlockSpec((None, tq, qw), lambda bi, t: (bi, t, 0)),
        pl.BlockSpec((None, kl, kw), lambda bi, t: (bi, 0, 0)),
        pl.BlockSpec((None, kl, kw), lambda bi, t: (bi, 0, 0)),
    ]
    args = [q, k, v]
    if has_ctx:
        cl = kc.shape[1]
        in_specs += [pl.BlockSpec((None, cl, kw), lambda bi, t: (bi, 0, 0))] * 2
        args += [kc, vc]
    return pl.pallas_call(
        functools.partial(_attn_a_kernel, has_ctx),
        grid=(b, s // tq),
        in_specs=in_specs,
        out_specs=pl.BlockSpec((None, tq, qw), lambda bi, t: (bi, t, 0)),
        out_shape=jax.ShapeDtypeStruct(q.shape, BF16),
        compiler_params=_params("arbitrary", "arbitrary"),
        name="attn_gqa_latent" if has_ctx else "attn_gqa_context",
    )(*args)


def _attn_b_kernel(has_ctx, qn_ref, qr_ref, kn_ref, kp_ref, v_ref, *rest):
    if has_ctx:
        knc_ref, kpc_ref, vc_ref, o_ref = rest
    else:
        (o_ref,) = rest
    tq = min(Q_TILE, qn_ref.shape[0])
    lane = lax.broadcasted_iota(jnp.int32, (tq, LANES), 1)
    low = lane < (LANES // 2)
    quarter = lane // MLA_ROPE
    zero = jnp.zeros((tq, LANES), BF16)
    kp = kp_ref[...]
    kpc = kpc_ref[...] if has_ctx else None
    pairs_per_rope = LANES // MLA_ROPE // 2
    n_pairs = HEADS_PER_STEP // 2
    pair_cols = [slice(p * LANES, (p + 1) * LANES) for p in range(n_pairs)]

    def pair_lhs(p, rows=slice(None)):
        qn = qn_ref[rows, pair_cols[p]]
        rg = p // pairs_per_rope
        qr = qr_ref[rows, rg * LANES:(rg + 1) * LANES]
        return [jnp.concatenate([jnp.where(low if j == 0 else ~low, qn, zero),
                                 jnp.where(quarter == 2 * (p % pairs_per_rope) + j, qr, zero)], axis=1)
                for j in range(2)]

    if not has_ctx:
        qs = [jnp.concatenate(pair_lhs(p), axis=0) for p in range(n_pairs)]
        out = _attend_group(qs, [jnp.concatenate([kn_ref[:, c], kp], axis=1) for c in pair_cols],
                            [_with_ones(v_ref[:, c]) for c in pair_cols])
        for p, c in enumerate(pair_cols):
            o_ref[:, c] = _merge_pair(out[2 * p * tq:(2 * p + 1) * tq], out[(2 * p + 1) * tq:(2 * p + 2) * tq])
        return
    for p, cols in enumerate(pair_cols):
        segs = [(jnp.concatenate([kn_ref[:, cols], kp], axis=1), _with_ones(v_ref[:, cols]), None),
                (jnp.concatenate([knc_ref[:, cols], kpc], axis=1), _with_ones(vc_ref[:, cols]), None)]
        for r in range(qn_ref.shape[0] // tq):
            rows = slice(r * tq, (r + 1) * tq)
            o_ref[rows, cols] = _merge_pair(*[_attend(lhs, segs) for lhs in pair_lhs(p, rows)])


def _attention_b(qn, qr, kn, kp, v, knc=None, kpc=None, vc=None):
    b, s, _ = qn.shape
    kl = kn.shape[1]
    has_ctx = knc is not None
    tq = min(LATENT_Q_ROWS if has_ctx else Q_TILE, s)
    gw = HEADS_PER_STEP * MLA_NOPE
    blk = lambda rows, w, fn: pl.BlockSpec((None, rows, w), fn)
    in_specs = [
        blk(tq, gw, lambda bi, g, t: (bi, t, g)),
        blk(tq, HEADS_PER_STEP * MLA_ROPE, lambda bi, g, t: (bi, t, g)),
        blk(kl, gw, lambda bi, g, t: (bi, 0, g)),
        blk(kl, LANES, lambda bi, g, t: (bi, 0, 0)),
        blk(kl, gw, lambda bi, g, t: (bi, 0, g)),
    ]
    args = [qn, qr, kn, kp, v]
    if has_ctx:
        cl = knc.shape[1]
        in_specs += [
            blk(cl, gw, lambda bi, g, t: (bi, 0, g)),
            blk(cl, LANES, lambda bi, g, t: (bi, 0, 0)),
            blk(cl, gw, lambda bi, g, t: (bi, 0, g)),
        ]
        args += [knc, kpc, vc]
    return pl.pallas_call(
        functools.partial(_attn_b_kernel, has_ctx),
        grid=(b, MLA_HEADS // HEADS_PER_STEP, s // tq),
        in_specs=in_specs,
        out_specs=blk(tq, gw, lambda bi, g, t: (bi, t, g)),
        out_shape=jax.ShapeDtypeStruct(qn.shape, BF16),
        compiler_params=_params("arbitrary", "arbitrary", "arbitrary"),
        name="attn_mla_latent" if has_ctx else "attn_mla_context",
    )(*args)


def _attn_c_kernel(latent, sink_ref, q_ref, kd_ref, vd_ref, *rest):
    if latent:
        kc_ref, vc_ref, o_ref = rest
    else:
        (o_ref,) = rest
    tq = min(Q_TILE, q_ref.shape[0])
    group = SWA_HEADS // SWA_KV_HEADS
    kv_per_step = HEADS_PER_STEP // group
    head0 = pl.program_id(1) * HEADS_PER_STEP
    windows = []
    if latent:
        s_len = kd_ref.shape[0]
        win = tq + 2 * SWA_WINDOW
        for r in range(q_ref.shape[0] // tq):
            t0 = pl.program_id(2) * q_ref.shape[0] + r * tq
            ws = pl.multiple_of(jnp.clip(t0 - SWA_WINDOW, 0, s_len - win), LANES)
            qpos = t0 + lax.broadcasted_iota(jnp.int32, (tq, win), 0)
            kpos = ws + lax.broadcasted_iota(jnp.int32, (tq, win), 1)
            bias = jnp.where(jnp.abs(qpos - kpos) <= SWA_WINDOW, 0.0, NEG_INF).astype(F32)
            windows.append((slice(r * tq, (r + 1) * tq), ws, bias))
    pairs_per_kv = group // 2
    pair_cols = [slice(p * LANES, (p + 1) * LANES) for p in range(HEADS_PER_STEP // 2)]
    kv_cols = [slice(j * LANES, (j + 1) * LANES) for j in range(kv_per_step)]
    if not latent:
        qs = [jnp.concatenate([h for c in pair_cols[j * pairs_per_kv:(j + 1) * pairs_per_kv]
                               for h in _split_pair(q_ref[:, c])], axis=0) for j in range(kv_per_step)]
        out = _attend_group(qs, [kd_ref[:, c] for c in kv_cols], [_with_ones(vd_ref[:, c]) for c in kv_cols],
                            sink_ref[...])
        for p, c in enumerate(pair_cols):
            o_ref[:, c] = _merge_pair(out[2 * p * tq:(2 * p + 1) * tq], out[(2 * p + 1) * tq:(2 * p + 2) * tq])
        return
    sinks = [sink_ref[head0 + h] * LOG2E for h in range(HEADS_PER_STEP)]
    hd = SWA_HEAD_DIM
    for j, kc in enumerate(kv_cols):
        kct = kc_ref[j * hd:(j + 1) * hd, :].astype(BF16)
        vct = vc_ref[j * hd:(j + 1) * hd, :].astype(BF16)
        ctx_seg = (jnp.concatenate([kct, kct], axis=0),
                   jnp.concatenate([vct, vct, jnp.ones((LANES, vct.shape[1]), BF16)], axis=0), None, True)
        for rows, ws, bias in windows:
            segs = [(kd_ref[pl.ds(ws, win), kc], _with_ones(vd_ref[pl.ds(ws, win), kc]), (bias, bias)), ctx_seg]
            for p in range(j * pairs_per_kv, (j + 1) * pairs_per_kv):
                o_ref[rows, pair_cols[p]] = _attend_pair(q_ref[rows, pair_cols[p]], segs, sinks[2 * p:2 * p + 2])


def _sink_logit_block(sink, tq):
    rows = jnp.repeat(sink * LOG2E, tq)[:, None]
    return jnp.where(jnp.arange(LANES)[None, :] == 0, rows, NEG_INF)


def _attention_c(sink, q, kd, vd, kc, vc):
    b, s, _ = q.shape
    kl = kd.shape[1]
    tq = min(LATENT_Q_ROWS, s)
    gw = HEADS_PER_STEP * SWA_HEAD_DIM
    kvw = 2 * SWA_HEAD_DIM * HEADS_PER_STEP // (SWA_HEADS // SWA_KV_HEADS)
    cw, cl = kc.shape[1] * HEADS_PER_STEP // SWA_HEADS, kc.shape[2]
    return pl.pallas_call(
        functools.partial(_attn_c_kernel, True),
        grid=(b, SWA_HEADS // HEADS_PER_STEP, s // tq),
        in_specs=[
            pl.BlockSpec(memory_space=pltpu.SMEM),
            pl.BlockSpec((None, tq, gw), lambda bi, g, t: (bi, t, g)),
            pl.BlockSpec((None, kl, kvw), lambda bi, g, t: (bi, 0, g)),
            pl.BlockSpec((None, kl, kvw), lambda bi, g, t: (bi, 0, g)),
            pl.BlockSpec((None, cw, cl), lambda bi, g, t: (bi, g, 0)),
            pl.BlockSpec((None, cw, cl), lambda bi, g, t: (bi, g, 0)),
        ],
        out_specs=pl.BlockSpec((None, tq, gw), lambda bi, g, t: (bi, t, g)),
        out_shape=jax.ShapeDtypeStruct(q.shape, BF16),
        compiler_params=_params("arbitrary", "arbitrary", "arbitrary"),
        name="attn_swa_latent",
    )(sink, q, kd, vd, kc, vc)


def _attn_d_ctx_kernel(q_ref, k_ref, v_ref, o_ref):
    tq = q_ref.shape[0]
    pair_cols = [slice(p * LANES, (p + 1) * LANES) for p in range(HEADS_PER_STEP // 2)]
    qs = [jnp.concatenate(_split_pair(q_ref[:, c]), axis=0) for c in pair_cols]
    out = _attend_group(qs, [k_ref[:, c] for c in pair_cols], [_with_ones(v_ref[:, c]) for c in pair_cols])
    for p, c in enumerate(pair_cols):
        o_ref[:, c] = _merge_pair(out[2 * p * tq:(2 * p + 1) * tq], out[(2 * p + 1) * tq:(2 * p + 2) * tq])


def _nat_window_start(first_row, rows):
    r0 = jnp.clip(first_row - NAT_WIN_R // 2, 0, rows - NAT_WIN_R)
    return jnp.minimum(r0, rows - NAT_KEY_ROWS)


def _attn_d_lat_kernel(q_ref, k_ref, v_ref, kc_ref, vc_ref, bias_ref, o_ref):
    rows = k_ref.shape[0] // GRID_W
    slab = NAT_KEY_ROWS * GRID_W
    start = _nat_window_start(pl.program_id(0) * NAT_TILE_ROWS, rows) * GRID_W
    start = pl.multiple_of(start, GRID_W)
    ones = jnp.ones((LANES, kc_ref.shape[1]), BF16)
    for p in range(HEADS_PER_STEP // 2):
        cols = slice(p * LANES, (p + 1) * LANES)
        segs = [(k_ref[pl.ds(start, slab), cols], _with_ones(v_ref[pl.ds(start, slab), cols]),
                 (bias_ref[2 * p], bias_ref[2 * p + 1])),
                (kc_ref[cols, :].astype(BF16),
                 jnp.concatenate([vc_ref[cols, :].astype(BF16), ones], axis=0), None, True)]
        o_ref[:, cols] = _attend_pair(q_ref[:, cols], segs)


def _attention_d_lat(q, k, v, kc, vc, bias):
    b, s, _ = q.shape
    cl = kc.shape[2]
    tq = NAT_TILE_ROWS * GRID_W
    slab = NAT_KEY_ROWS * GRID_W
    gw = HEADS_PER_STEP * NAT_HEAD_DIM
    return pl.pallas_call(
        _attn_d_lat_kernel,
        grid=(s // tq, NAT_HEADS // HEADS_PER_STEP, b),
        in_specs=[
            pl.BlockSpec((None, tq, gw), lambda t, g, bi: (bi, t, g)),
            pl.BlockSpec((None, s, gw), lambda t, g, bi: (bi, 0, g)),
            pl.BlockSpec((None, s, gw), lambda t, g, bi: (bi, 0, g)),
            pl.BlockSpec((None, gw, cl), lambda t, g, bi: (bi, g, 0)),
            pl.BlockSpec((None, gw, cl), lambda t, g, bi: (bi, g, 0)),
            pl.BlockSpec((HEADS_PER_STEP, None, tq, slab), lambda t, g, bi: (g, t, 0, 0)),
        ],
        out_specs=pl.BlockSpec((None, tq, gw), lambda t, g, bi: (bi, t, g)),
        out_shape=jax.ShapeDtypeStruct(q.shape, BF16),
        compiler_params=_params("arbitrary", "arbitrary", "arbitrary"),
        name="attn_nat_latent",
    )(q, k, v, kc, vc, bias)


def _mla_expand_kernel(c_ref, w_ref, kn_ref, v_ref):
    kv = _dot(c_ref[...].astype(BF16), w_ref[...])
    half = kv.shape[1] // 2
    kn_ref[...] = kv[:, :half].astype(BF16)
    v_ref[...] = kv[:, half:].astype(BF16)


def _mla_expand(ckv, w_ukv):
    b, l, c = ckv.shape
    n = w_ukv.shape[1] // 2
    out = pl.BlockSpec((None, l, n), lambda bi: (bi, 0, 0))
    return pl.pallas_call(
        _mla_expand_kernel,
        grid=(b,),
        in_specs=[pl.BlockSpec((None, l, c), lambda bi: (bi, 0, 0)), _full(w_ukv.shape)],
        out_specs=[out, out],
        out_shape=[jax.ShapeDtypeStruct((b, l, n), BF16)] * 2,
        compiler_params=_params("arbitrary"),
        name="mla_expand_cache",
    )(ckv, w_ukv)


def _finish_kernel(o_ref, x_ref, m_ref, g_ref, wo_ref, w1_ref, w2_ref, y_ref, x1_ref, h2_ref):
    f = pl.program_id(2)
    last = pl.num_programs(2) - 1
    chains = [slice(c * TOK_CHAIN, (c + 1) * TOK_CHAIN) for c in range(y_ref.shape[0] // TOK_CHAIN)]

    def mlp_part(h, w1, w2):
        u = jnp.square(jnp.maximum(_dot(h, w1), 0.0)).astype(BF16)
        return _dot(u, w2)

    @pl.when(f == 0)
    def _():
        m = m_ref[...]
        for rows in chains:
            a = _dot(o_ref[rows, :], wo_ref[...])
            x1 = x_ref[rows, :] + _rms(a, _mod(m, 2) * g_ref[1:2, :])
            x1_ref[rows, :] = x1
            h2_ref[rows, :] = _premod(x1, g_ref[2:3, :], m, 1).astype(BF16)
        y_ref[...] = mlp_part(h2_ref[...], w1_ref[...].astype(BF16), w2_ref[...].astype(BF16))

    @pl.when((f > 0) & (f < last))
    def _():
        y_ref[...] += mlp_part(h2_ref[...], w1_ref[...].astype(BF16), w2_ref[...].astype(BF16))

    @pl.when(f == last)
    def _():
        gain = _mod(m_ref[...], 5) * g_ref[3:4, :]
        w1, w2 = w1_ref[...].astype(BF16), w2_ref[...].astype(BF16)
        for rows in chains:
            y = y_ref[rows, :] + mlp_part(h2_ref[rows, :], w1, w2)
            y_ref[rows, :] = x1_ref[rows, :] + _rms(y, gain)


def _finish_layer(o, x, mods_l, g, wo, w1, w2, layer, latent):
    bx, s, _ = x.shape
    tm, tf = MLP_TOK_TILE, MLP_FF_TILE
    row = (lambda b, j, f: (b, 0, 0)) if latent else (lambda b, j, f: (CTX_MOD_ROW, 0, 0))
    tok = lambda w: pl.BlockSpec((None, tm, w), lambda b, j, f: (b, j, 0))
    return pl.pallas_call(
        _finish_kernel,
        grid=(bx, s // tm, D_FF // tf),
        in_specs=[tok(o.shape[-1]), tok(D_MODEL),
                  pl.BlockSpec((None, 1, N_MOD * D_MODEL), row), _full(g.shape), _full(wo.shape),
                  pl.BlockSpec((None, D_MODEL, tf), lambda b, j, f: (layer, 0, f)),
                  pl.BlockSpec((None, tf, D_MODEL), lambda b, j, f: (layer, f, 0))],
        out_specs=tok(D_MODEL),
        out_shape=jax.ShapeDtypeStruct(x.shape, F32),
        scratch_shapes=[pltpu.VMEM((tm, D_MODEL), F32), pltpu.VMEM((tm, D_MODEL), BF16)],
        compiler_params=pltpu.CompilerParams(dimension_semantics=("arbitrary",) * 3,
                                             vmem_limit_bytes=MLP_VMEM_LIMIT),
        name="out_proj_mlp",
    )(o, x, mods_l, g, wo, w1, w2)


def _rope_tables(s, dim):
    quarter = dim // 4
    t = jnp.arange(s)
    pos = jnp.stack([t // GRID_W, t % GRID_W], axis=-1).astype(F32)
    inv = ROPE_THETA ** (-jnp.arange(quarter, dtype=F32) / quarter)
    ang = pos[:, :, None] * inv
    cos = jnp.broadcast_to(jnp.cos(ang)[:, :, None, :], (s, 2, 2, quarter)).reshape(s, dim)
    sign = jnp.array([-1.0, 1.0], F32)[None, None, :, None]
    sin = (jnp.sin(ang)[:, :, None, :] * sign).reshape(s, dim)
    reps = LANES // dim
    return jnp.tile(cos, (1, reps)), jnp.tile(sin, (1, reps))


def _dup_heads(w, heads, dim):
    lead = w.shape[:-1]
    w = w.reshape(lead + (heads, 1, dim))
    return jnp.broadcast_to(w, lead + (heads, 2, dim)).reshape(lead + (heads * 2 * dim,))


def _nat_dense_bias(rpb, rows):
    heads = rpb.shape[0]
    c = np.arange(GRID_W)
    c0 = np.clip(c - NAT_WIN_C // 2, 0, GRID_W - NAT_WIN_C)
    in_c = (c[None, :] >= c0[:, None]) & (c[None, :] < c0[:, None] + NAT_WIN_C)
    dc = c[None, :] - c[:, None] + NAT_WIN_C - 1
    onehot = (dc[None] == np.arange(2 * NAT_WIN_C - 1)[:, None, None]) & in_c[None]
    toe = jnp.einsum("had,dck->hack", rpb * LOG2E, jnp.asarray(onehot, F32),
                     precision=lax.Precision.HIGHEST)
    toe = jnp.where(jnp.asarray(in_c)[None, None], toe, NEG_INF)
    pad = jnp.full((heads, 1, GRID_W, GRID_W), NEG_INF, F32)
    ext = jnp.concatenate([pad, toe, pad], axis=1)
    pairs = jnp.concatenate([ext[:, :-1], ext[:, 1:]], axis=-1)
    tiles = rows // NAT_TILE_ROWS
    tq, slab = NAT_TILE_ROWS * GRID_W, NAT_KEY_ROWS * GRID_W
    n_off = 2 * NAT_WIN_R
    return pl.pallas_call(
        functools.partial(_nat_bias_kernel, rows),
        grid=(heads,),
        in_specs=[pl.BlockSpec((None, n_off, GRID_W, LANES), lambda h: (h, 0, 0, 0))],
        out_specs=pl.BlockSpec((None, tiles, tq, slab), lambda h: (h, 0, 0, 0)),
        out_shape=jax.ShapeDtypeStruct((heads, tiles, tq, slab), F32),
        compiler_params=_params("arbitrary"),
        name="nat_bias_expand",
    )(pairs)


def _nat_bias_kernel(rows, pairs_ref, o_ref):
    low = lax.broadcasted_iota(jnp.int32, (GRID_W, LANES), 1) < GRID_W
    masked = jnp.full((GRID_W, LANES), NEG_INF, F32)
    for r in range(rows):
        tile, i = divmod(r, NAT_TILE_ROWS)
        r0 = min(max(r - NAT_WIN_R // 2, 0), rows - NAT_WIN_R)
        first0 = min(max(tile * NAT_TILE_ROWS - NAT_WIN_R // 2, 0), rows - NAT_WIN_R)
        ws = min(first0, rows - NAT_KEY_ROWS)
        for jb in range(NAT_KEY_ROWS // 2):
            kr = ws + 2 * jb
            ok_lo, ok_hi = r0 <= kr < r0 + NAT_WIN_R, r0 <= kr + 1 < r0 + NAT_WIN_R
            blk = masked
            if ok_lo or ok_hi:
                blk = pairs_ref[kr - r + NAT_WIN_R]
                if not ok_lo:
                    blk = jnp.where(low, NEG_INF, blk)
                elif not ok_hi:
                    blk = jnp.where(low, blk, NEG_INF)
            o_ref[tile, i * GRID_W:(i + 1) * GRID_W, jb * LANES:(jb + 1) * LANES] = blk


def kernel(x_prompt, x_sample, cache_l0_k, cache_l0_v, cache_l1_ckv, cache_l1_kpe, cache_l2_k, cache_l2_v, cache_l3_k, cache_l3_v, c, c_ctx, ada_w, ada_b, norm_g, mlp_w1, mlp_w2, attn_w_qkv, attn_q_norm, attn_k_norm, attn_w_o, mla_w_in, mla_q_norm, mla_kv_norm, mla_w_uq, mla_w_ukv, mla_w_o, swa_w_qkv, swa_sink, swa_w_o, nat_w_qkv, nat_rpb, nat_w_o):
    nb, seq, d = x_prompt.shape
    db, dseq, _ = x_sample.shape
    past = cache_l0_k.shape[1]
    ctx_b = nb * seq // dseq
    xp = x_prompt.reshape(ctx_b, dseq, d)
    xs = x_sample

    cond = jnp.zeros((COND_ROWS, d), F32).at[:db].set(c).at[CTX_MOD_ROW].set(c_ctx)
    mods = _modulation(cond, ada_w, ada_b).reshape(DEPTH, COND_ROWS, 1, N_MOD * d)

    row = lambda v: v.reshape(1, -1)

    def head_major(cache):
        b_, l_, h_, dh = cache.shape
        return cache.transpose(0, 2, 3, 1).reshape(b_, h_ * dh, l_)

    def token_major(state, heads):
        b_, w_, l_ = state.shape
        return state.reshape(b_, heads, w_ // heads, l_).transpose(0, 3, 1, 2)

    def finish(o, x, layer, wo, latent):
        return _finish_layer(o, x, mods[layer], norm_g[layer], wo, mlp_w1, mlp_w2, layer, latent)

    g = norm_g[0]
    w = attn_w_qkv.astype(BF16)
    wo = attn_w_o.astype(BF16)
    consts = [w, row(attn_q_norm), row(attn_k_norm)]
    kvw = ATTN_KV_HEADS * ATTN_HEAD_DIM
    o, l0_k, l0_v = _run_proj(
        _proj_a_kernel, False, xp, mods[0], g, consts, [], [(kvw, F32), (kvw, F32)], "proj_attn_gqa_context",
        seq, attend=functools.partial(_attn_a_kernel, False), qkv=[(d, BF16), (kvw, BF16), (kvw, BF16)])
    xp = finish(o, xp, 0, wo, False)
    perm = np.arange(ATTN_HEAD_DIM).reshape(2, 2, ATTN_HEAD_DIM // 4).transpose(1, 0, 2).reshape(-1)
    qk_heads = ATTN_HEADS + ATTN_KV_HEADS
    cols = (np.arange(qk_heads)[:, None] * ATTN_HEAD_DIM + perm[None, :]).reshape(-1)
    w_lat = jnp.concatenate([attn_w_qkv[:, cols], attn_w_qkv[:, qk_heads * ATTN_HEAD_DIM:]], axis=1).astype(BF16)
    consts_lat = [w_lat, row(attn_q_norm[perm]), row(attn_k_norm[perm])]
    tables = [t[:, perm] for t in _rope_tables(dseq, ATTN_HEAD_DIM)]
    q, k, v = _run_proj(_proj_a_kernel, True, xs, mods[0], g, consts_lat, tables,
                        [(d, BF16), (kvw, BF16), (kvw, BF16)], "proj_gqa_latent")
    o = _attention_a(q, k, v, cache_l0_k[..., perm].reshape(db, past, kvw).astype(BF16),
                     cache_l0_v.reshape(db, past, kvw).astype(BF16))
    xs = finish(o, xs, 0, wo, True)
    new_l0 = (l0_k.reshape(nb, seq, ATTN_KV_HEADS, ATTN_HEAD_DIM),
              l0_v.reshape(nb, seq, ATTN_KV_HEADS, ATTN_HEAD_DIM))

    g = norm_g[1]
    nq, nkv = MLA_Q_LORA, MLA_KV_LORA
    w_in = jnp.concatenate([mla_w_in[:, :nq + nkv]] + [mla_w_in[:, nq + nkv:]] * (LANES // MLA_ROPE),
                           axis=1).astype(BF16)
    wuq = mla_w_uq.reshape(nq, MLA_HEADS, MLA_NOPE + MLA_ROPE)
    wuq = jnp.concatenate([wuq[:, :, :MLA_NOPE].reshape(nq, -1), wuq[:, :, MLA_NOPE:].reshape(nq, -1)],
                          axis=1).astype(BF16)
    wukv = mla_w_ukv.reshape(nkv, MLA_HEADS, MLA_NOPE + MLA_V_DIM)
    wukv = jnp.concatenate([wukv[:, :, :MLA_NOPE].reshape(nkv, -1), wukv[:, :, MLA_NOPE:].reshape(nkv, -1)],
                           axis=1).astype(BF16)
    wo = mla_w_o.astype(BF16)
    consts = [w_in, row(mla_q_norm), row(mla_kv_norm), wuq, wukv]
    hw = MLA_HEADS * MLA_NOPE
    rw = MLA_HEADS * MLA_ROPE
    outs = [(hw, BF16), (rw, BF16), (hw, BF16), (hw, BF16), (LANES, BF16)]
    o, l1_ckv, l1_kpe = _run_proj(
        _proj_b_kernel, False, xp, mods[1], g, consts, [], [(nkv, F32), (MLA_ROPE, F32, "T")],
        "proj_attn_mla_context", seq, qkv=outs,
        attend=lambda qn, qr, kn, v, kp, o_ref: _attn_b_kernel(False, qn, qr, kn, kp, v, o_ref))
    xp = finish(o, xp, 1, wo, False)
    tables = list(_rope_tables(dseq, MLA_ROPE))
    qn, qr, kn, v, kp = _run_proj(_proj_b_kernel, True, xs, mods[1], g, consts, tables, outs,
                                  "proj_mla_latent")
    knc, vc = _mla_expand(cache_l1_ckv, wukv)
    kpc = jnp.tile(cache_l1_kpe, (1, 1, LANES // MLA_ROPE)).astype(BF16)
    o = _attention_b(qn, qr, kn, kp, v, knc, kpc, vc)
    xs = finish(o, xs, 1, wo, True)
    new_l1 = (l1_ckv.reshape(nb, seq, nkv), l1_kpe.transpose(0, 2, 1))

    g = norm_g[2]
    qw = SWA_HEADS * SWA_HEAD_DIM
    kw = SWA_KV_HEADS * SWA_HEAD_DIM
    wq, wk, wv = swa_w_qkv[:, :qw], swa_w_qkv[:, qw:qw + kw], swa_w_qkv[:, qw + kw:]
    wkd = _dup_heads(wk, SWA_KV_HEADS, SWA_HEAD_DIM)
    wvd = _dup_heads(wv, SWA_KV_HEADS, SWA_HEAD_DIM)
    w_ctx = jnp.concatenate([wq, wk, wv, wkd, wvd], axis=1).astype(BF16)
    w_lat = jnp.concatenate([wq, wkd, wvd], axis=1).astype(BF16)
    wo = swa_w_o.astype(BF16)
    o, l2_k, l2_v = _run_proj(
        _proj_c_kernel, False, xp, mods[2], g, [w_ctx], [], [(kw, F32, "T"), (kw, F32, "T")],
        "proj_attn_swa_context", seq, attend=functools.partial(_attn_c_kernel, False),
        attend_inputs=[_sink_logit_block(swa_sink, seq)], qkv=[(qw, BF16), (2 * kw, BF16), (2 * kw, BF16)])
    xp = finish(o, xp, 2, wo, False)
    tables = list(_rope_tables(dseq, SWA_HEAD_DIM))
    q, kd, vd = _run_proj(_proj_c_kernel, True, xs, mods[2], g, [w_lat], tables,
                          [(qw, BF16), (2 * kw, BF16), (2 * kw, BF16)], "proj_swa_latent")
    o = _attention_c(swa_sink, q, kd, vd, head_major(cache_l2_k), head_major(cache_l2_v))
    xs = finish(o, xs, 2, wo, True)
    new_l2 = (token_major(l2_k, SWA_KV_HEADS), token_major(l2_v, SWA_KV_HEADS))

    g = norm_g[3]
    hw = NAT_HEADS * NAT_HEAD_DIM
    w = nat_w_qkv.astype(BF16)
    wo = nat_w_o.astype(BF16)
    o, l3_k, l3_v = _run_proj(
        _proj_d_kernel, False, xp, mods[3], g, [w], [], [(hw, F32, "T"), (hw, F32, "T")],
        "proj_attn_nat_context", seq, attend=_attn_d_ctx_kernel, qkv=[(hw, BF16)] * 3)
    xp = finish(o, xp, 3, wo, False)
    q, k, v = _run_proj(_proj_d_kernel, True, xs, mods[3], g, [w], [],
                        [(hw, BF16), (hw, BF16), (hw, BF16)], "proj_nat_latent")
    o = _attention_d_lat(q, k, v, head_major(cache_l3_k), head_major(cache_l3_v),
                         _nat_dense_bias(nat_rpb, dseq // GRID_W))
    xs = finish(o, xs, 3, wo, True)
    new_l3 = (token_major(l3_k, NAT_HEADS), token_major(l3_v, NAT_HEADS))

    return (xp.reshape(nb, seq, d), xs) + new_l0 + new_l1 + new_l2 + new_l3
```

```python
import functools

import numpy as np

import jax
import jax.numpy as jnp
from jax import lax
from jax.experimental import pallas as pl
from jax.experimental.pallas import tpu as pltpu

F32 = jnp.float32
BF16 = jnp.bfloat16

D_MODEL = 1024
DEPTH = 4
N_MOD = 6
D_FF = 4 * D_MODEL
GRID_W = 64
ROPE_THETA = 10000.0
NORM_EPS = 1e-6
NEG_INF = -1e30
LOG2E = 1.4426950408889634

ATTN_HEADS, ATTN_KV_HEADS, ATTN_HEAD_DIM = 8, 2, 128
MLA_HEADS, MLA_Q_LORA, MLA_KV_LORA = 16, 384, 256
MLA_NOPE, MLA_ROPE, MLA_V_DIM = 64, 32, 64
MLA_SCALE = (MLA_NOPE + MLA_ROPE) ** -0.5
SWA_HEADS, SWA_KV_HEADS, SWA_HEAD_DIM, SWA_WINDOW = 16, 4, 64, 128
NAT_HEADS, NAT_HEAD_DIM, NAT_WIN_R, NAT_WIN_C = 16, 64, 8, 16

LANES = 128
COND_ROWS = 16
CTX_MOD_ROW = 8
VMEM_LIMIT = 48 * 1024 * 1024

TOK_TILE = 512
TOK_CHAIN = 256
MLP_TOK_TILE = 1024
MLP_FF_TILE = 1024
MLP_VMEM_LIMIT = 56 * 1024 * 1024
Q_TILE = 256
LATENT_Q_ROWS = 512
HEADS_PER_STEP = 16
NAT_TILE_ROWS = Q_TILE // GRID_W
NAT_KEY_ROWS = NAT_WIN_R + NAT_TILE_ROWS


def _params(*sem):
    return pltpu.CompilerParams(dimension_semantics=sem, vmem_limit_bytes=VMEM_LIMIT)


def _full(shape):
    nd = len(shape)
    return pl.BlockSpec(shape, lambda *_: (0,) * nd)


def _rms(x, g):
    return x * lax.rsqrt(jnp.mean(x * x, axis=-1, keepdims=True) + NORM_EPS) * g


def _mod(m, i):
    return m[:, i * D_MODEL:(i + 1) * D_MODEL]


def _premod(x, g, m, sub):
    return _rms(x, g * (1.0 + _mod(m, 3 * sub + 1))) + _mod(m, 3 * sub)


def _rope(x, cos, sin_signed, quarter):
    n = x.shape[-1]
    lane = lax.broadcasted_iota(jnp.int32, x.shape, 1)
    first = ((lane // quarter) % 2) == 0
    partner = jnp.where(first, pltpu.roll(x, n - quarter, 1), pltpu.roll(x, quarter, 1))
    return x * cos + partner * sin_signed


def _dot(a, b):
    return jnp.dot(a, b, preferred_element_type=F32)


def _dot_nt(a, b):
    return lax.dot_general(a, b, (((1,), (1,)), ((), ())), preferred_element_type=F32)


def _with_ones(v):
    return jnp.concatenate([v, jnp.ones_like(v)], axis=1)


def _attend(q, segs, sink=None):
    logits = []
    for seg in segs:
        k, bias, transposed = seg[0], seg[2], len(seg) > 3 and seg[3]
        s = _dot(q, k) if transposed else _dot_nt(q, k)
        logits.append(s if bias is None else s + bias)
    m = logits[0].max(axis=-1, keepdims=True)
    for s in logits[1:]:
        m = jnp.maximum(m, s.max(axis=-1, keepdims=True))
    if sink is not None:
        m = jnp.maximum(m, sink)
    acc = None
    for s, seg in zip(logits, segs):
        p = jnp.exp2(s - m).astype(BF16)
        pv = _dot_nt(p, seg[1]) if len(seg) > 3 and seg[3] else _dot(p, seg[1])
        acc = pv if acc is None else acc + pv
    den = acc[:, LANES:LANES + 1]
    if sink is not None:
        den = den + jnp.exp2(sink - m)
    return acc[:, :LANES] / den


def _attend_group(qs, ks, v1s, sink_logits=None):
    tq = qs[0].shape[0]
    s = jnp.concatenate([_dot_nt(q, k) for q, k in zip(qs, ks)], axis=0)
    if sink_logits is not None:
        s = jnp.concatenate([s, sink_logits], axis=1)
        zeros = jnp.zeros((LANES, LANES), BF16)
        tail = jnp.concatenate([zeros, jnp.ones_like(zeros)], axis=1)
        v1s = [jnp.concatenate([v1, tail], axis=0) for v1 in v1s]
    p = jnp.exp2(s - s.max(axis=-1, keepdims=True)).astype(BF16)
    acc = jnp.concatenate([_dot(p[i * tq:(i + 1) * tq], v1) for i, v1 in enumerate(v1s)], axis=0)
    return acc[:, :LANES] / acc[:, LANES:LANES + 1]


def _split_pair(q):
    low = lax.broadcasted_iota(jnp.int32, q.shape, 1) < (LANES // 2)
    zero = jnp.zeros_like(q)
    return [jnp.where(low, q, zero), jnp.where(low, zero, q)]


def _merge_pair(o0, o1):
    low = lax.broadcasted_iota(jnp.int32, o0.shape, 1) < (LANES // 2)
    return jnp.where(low, o0, o1).astype(BF16)


def _attend_pair(q, segs, sinks=None):
    outs = []
    for j, qj in enumerate(_split_pair(q)):
        segs_j = [(s[0], s[1], None if s[2] is None else s[2][j]) + tuple(s[3:]) for s in segs]
        outs.append(_attend(qj, segs_j, None if sinks is None else sinks[j]))
    return _merge_pair(*outs)


def _mods_kernel(cond_ref, w_ref, b_ref, o_ref):
    cnd = cond_ref[...]
    act = cnd * jax.nn.sigmoid(cnd)
    o_ref[...] = _dot(act.astype(BF16), w_ref[...].astype(BF16)) + b_ref[...]


def _modulation(cond, ada_w, ada_b):
    tn = 1536
    n = N_MOD * D_MODEL
    return pl.pallas_call(
        _mods_kernel,
        grid=(DEPTH, n // tn),
        in_specs=[
            _full((COND_ROWS, D_MODEL)),
            pl.BlockSpec((None, D_MODEL, tn), lambda l, j: (l, 0, j)),
            pl.BlockSpec((None, 1, tn), lambda l, j: (l, 0, j)),
        ],
        out_specs=pl.BlockSpec((None, COND_ROWS, tn), lambda l, j: (l, 0, j)),
        out_shape=jax.ShapeDtypeStruct((DEPTH, COND_ROWS, n), F32),
        compiler_params=_params("arbitrary", "arbitrary"),
        name="adaln_mods",
    )(cond, ada_w, ada_b.reshape(DEPTH, 1, n))


def _proj_a_kernel(latent, x_ref, m_ref, g_ref, w_ref, qn_ref, kn_ref, *rest):
    if latent:
        cos_ref, sin_ref, q_ref, k_ref, v_ref = rest
    else:
        q_ref, k_ref, v_ref, ks_ref, vs_ref = rest
    hd = ATTN_HEAD_DIM
    n_qk = ATTN_HEADS + ATTN_KV_HEADS
    h = _premod(x_ref[...], g_ref[0:1, :], m_ref[...], 0).astype(BF16)
    for c in range(x_ref.shape[0] // TOK_CHAIN):
        rows = slice(c * TOK_CHAIN, (c + 1) * TOK_CHAIN)
        z = _dot(h[rows], w_ref[...])
        ys = [_rms(z[:, i * hd:(i + 1) * hd], qn_ref[...] if i < ATTN_HEADS else kn_ref[...]) for i in range(n_qk)]
        if not latent:
            for j in range(ATTN_KV_HEADS):
                ks_ref[rows, j * hd:(j + 1) * hd] = ys[ATTN_HEADS + j]
        else:
            ys = [y * cos_ref[rows, :] + pltpu.roll(y, hd // 2, 1) * sin_ref[rows, :] for y in ys]
        for i in range(ATTN_HEADS):
            q_ref[rows, i * hd:(i + 1) * hd] = (ys[i] * (hd ** -0.5 * LOG2E)).astype(BF16)
        for j in range(ATTN_KV_HEADS):
            k_ref[rows, j * hd:(j + 1) * hd] = ys[ATTN_HEADS + j].astype(BF16)
        v = z[:, n_qk * hd:]
        v_ref[rows, :] = v.astype(BF16)
        if not latent:
            vs_ref[rows, :] = v


def _proj_b_kernel(latent, x_ref, m_ref, g_ref, win_ref, qn_ref, kvn_ref, wuq_ref, wukv_ref, *rest):
    if latent:
        cos_ref, sin_ref, qn_o, qr_o, kn_o, v_o, kp_o = rest
    else:
        qn_o, qr_o, kn_o, v_o, kp_o, ckv_s, kpe_s = rest
    nq, nkv = MLA_Q_LORA, MLA_KV_LORA
    nope_w = MLA_HEADS * MLA_NOPE
    scale = MLA_SCALE * LOG2E
    h = _premod(x_ref[...], g_ref[0:1, :], m_ref[...], 0)
    z = _dot(h.astype(BF16), win_ref[...])
    cq = _rms(z[:, :nq], qn_ref[...])
    ckv = _rms(z[:, nq:nq + nkv], kvn_ref[...])
    kp = z[:, nq + nkv:]
    q = _dot(cq.astype(BF16), wuq_ref[...])
    kv = _dot(ckv.astype(BF16), wukv_ref[...])
    if not latent:
        ckv_s[...] = ckv
        _store_transposed(kpe_s, kp, MLA_ROPE)
    qn_o[...] = (q[:, :nope_w] * scale).astype(BF16)
    for i in range(MLA_HEADS * MLA_ROPE // LANES):
        qr = q[:, nope_w + i * LANES:nope_w + (i + 1) * LANES]
        if latent:
            qr = _rope(qr, cos_ref[...], sin_ref[...], MLA_ROPE // 4)
        qr_o[:, i * LANES:(i + 1) * LANES] = (qr * scale).astype(BF16)
    if latent:
        kp = _rope(kp, cos_ref[...], sin_ref[...], MLA_ROPE // 4)
    kp_o[...] = kp.astype(BF16)
    kn_o[...] = kv[:, :nope_w].astype(BF16)
    v_o[...] = kv[:, nope_w:].astype(BF16)


def _proj_c_kernel(latent, x_ref, m_ref, g_ref, w_ref, *rest):
    if latent:
        cos_ref, sin_ref, q_o, kd_o, vd_o = rest
    else:
        q_o, kd_o, vd_o, ks_o, vs_o = rest
    qw = SWA_HEADS * SWA_HEAD_DIM
    kw = SWA_KV_HEADS * SWA_HEAD_DIM
    h = _premod(x_ref[...], g_ref[0:1, :], m_ref[...], 0)
    z = _dot(h.astype(BF16), w_ref[...])
    off = qw
    if not latent:
        _store_transposed(ks_o, z[:, qw:qw + kw], kw)
        _store_transposed(vs_o, z[:, qw + kw:qw + 2 * kw], kw)
        off = qw + 2 * kw
    scale = SWA_HEAD_DIM ** -0.5 * LOG2E
    for i in range(qw // LANES):
        y = z[:, i * LANES:(i + 1) * LANES]
        if latent:
            y = _rope(y, cos_ref[...], sin_ref[...], SWA_HEAD_DIM // 4)
        q_o[:, i * LANES:(i + 1) * LANES] = (y * scale).astype(BF16)
    for i in range(2 * kw // LANES):
        y = z[:, off + i * LANES:off + (i + 1) * LANES]
        if latent:
            y = _rope(y, cos_ref[...], sin_ref[...], SWA_HEAD_DIM // 4)
        kd_o[:, i * LANES:(i + 1) * LANES] = y.astype(BF16)
    vd_o[...] = z[:, off + 2 * kw:].astype(BF16)


def _proj_d_kernel(latent, x_ref, m_ref, g_ref, w_ref, *rest):
    if latent:
        q_o, k_o, v_o = rest
    else:
        q_o, k_o, v_o, ks_o, vs_o = rest
    hw = NAT_HEADS * NAT_HEAD_DIM
    h = _premod(x_ref[...], g_ref[0:1, :], m_ref[...], 0)
    z = _dot(h.astype(BF16), w_ref[...])
    q_o[...] = (z[:, :hw] * (NAT_HEAD_DIM ** -0.5 * LOG2E)).astype(BF16)
    k_o[...] = z[:, hw:2 * hw].astype(BF16)
    v_o[...] = z[:, 2 * hw:].astype(BF16)
    if not latent:
        _store_transposed(ks_o, z[:, hw:2 * hw], hw)
        _store_transposed(vs_o, z[:, 2 * hw:], hw)


def _store_transposed(ref, val, width):
    n, _, seq = ref.shape
    for i in range(n):
        ref[i] = val[i * seq:(i + 1) * seq, :].T[:width, :]


def _run_proj(kernel, latent, x, mods_l, g, consts, tables, outs, name, state_seq=None,
              attend=None, attend_inputs=(), qkv=()):
    bx, s, _ = x.shape
    tm = TOK_TILE
    row = (lambda b, j: (b, 0, 0)) if latent else (lambda b, j: (CTX_MOD_ROW, 0, 0))
    in_specs = [
        pl.BlockSpec((None, tm, D_MODEL), lambda b, j: (b, j, 0)),
        pl.BlockSpec((None, 1, N_MOD * D_MODEL), row),
        _full(g.shape),
    ] + [_full(c.shape) for c in consts]
    in_specs += [pl.BlockSpec((tm, LANES), lambda b, j: (j, 0)) for _ in tables]
    in_specs += [_full(a.shape) for a in attend_inputs]
    body = functools.partial(kernel, latent)
    if attend is not None:
        n_proj_in = 3 + len(consts) + len(tables)
        n_att_in = len(attend_inputs)
        proj_body = body
        outs = [(D_MODEL, BF16)] + list(outs)

        def body(*refs):
            ins, att_ins = refs[:n_proj_in], refs[n_proj_in:n_proj_in + n_att_in]
            o_ref = refs[n_proj_in + n_att_in]
            states, scratch = refs[n_proj_in + n_att_in + 1:-len(qkv)], refs[-len(qkv):]
            proj_body(*ins, *scratch, *states)
            for i in range(tm // state_seq):
                rows = slice(i * state_seq, (i + 1) * state_seq)
                attend(*att_ins, *[r.at[rows] for r in scratch], o_ref.at[rows])

    out_specs, out_shape = [], []
    for o in outs:
        if len(o) == 2:
            out_specs.append(pl.BlockSpec((None, tm, o[0]), lambda b, j: (b, j, 0)))
            out_shape.append(jax.ShapeDtypeStruct((bx, s, o[0]), o[1]))
        else:
            per_tile = tm // state_seq
            out_specs.append(pl.BlockSpec((per_tile, o[0], state_seq),
                                          lambda b, j: (b * (s // tm) + j, 0, 0)))
            out_shape.append(jax.ShapeDtypeStruct((bx * s // state_seq, o[0], state_seq), o[1]))
    return pl.pallas_call(
        body,
        grid=(bx, s // tm),
        in_specs=in_specs,
        out_specs=out_specs,
        out_shape=out_shape,
        scratch_shapes=[pltpu.VMEM((tm, w), dt) for w, dt in qkv],
        compiler_params=_params("arbitrary", "arbitrary"),
        name=name,
    )(x, mods_l, g, *consts, *tables, *attend_inputs)


def _attn_a_kernel(has_ctx, q_ref, k_ref, v_ref, *rest):
    if has_ctx:
        kc_ref, vc_ref, o_ref = rest
    else:
        (o_ref,) = rest
    group = ATTN_HEADS // ATTN_KV_HEADS
    head_cols = [slice(h * LANES, (h + 1) * LANES) for h in range(ATTN_HEADS)]
    kv_cols = [slice(j * LANES, (j + 1) * LANES) for j in range(ATTN_KV_HEADS)]
    if not has_ctx:
        tq = q_ref.shape[0]
        qs = [jnp.concatenate([q_ref[:, c] for c in head_cols[j * group:(j + 1) * group]], axis=0)
              for j in range(ATTN_KV_HEADS)]
        out = _attend_group(qs, [k_ref[:, c] for c in kv_cols], [_with_ones(v_ref[:, c]) for c in kv_cols])
        for h, c in enumerate(head_cols):
            o_ref[:, c] = out[h * tq:(h + 1) * tq].astype(BF16)
        return
    for j, kc in enumerate(kv_cols):
        segs = [(k_ref[:, kc], _with_ones(v_ref[:, kc]), None),
                (kc_ref[:, kc], _with_ones(vc_ref[:, kc]), None)]
        for r in range(q_ref.shape[0] // Q_TILE):
            rows = slice(r * Q_TILE, (r + 1) * Q_TILE)
            for c in head_cols[j * group:(j + 1) * group]:
                o_ref[rows, c] = _attend(q_ref[rows, c], segs).astype(BF16)


def _attention_a(q, k, v, kc=None, vc=None):
    b, s, qw = q.shape
    kl, kw = k.shape[1:]
    has_ctx = kc is not None
    tq = min(LATENT_Q_ROWS if has_ctx else Q_TILE, s)
    in_specs = [
        pl.BlockSpec((None, tq, qw), lambda bi, t: (bi, t, 0)),
        pl.BlockSpec((None, kl, kw), lambda bi, t: (bi, 0, 0)),
        pl.BlockSpec((None, kl, kw), lambda bi, t: (bi, 0, 0)),
    ]
    args = [q, k, v]
    if has_ctx:
        cl = kc.shape[1]
        in_specs += [pl.BlockSpec((None, cl, kw), lambda bi, t: (bi, 0, 0))] * 2
        args += [kc, vc]
    return pl.pallas_call(
        functools.partial(_attn_a_kernel, has_ctx),
        grid=(b, s // tq),
        in_specs=in_specs,
        out_specs=pl.BlockSpec((None, tq, qw), lambda bi, t: (bi, t, 0)),
        out_shape=jax.ShapeDtypeStruct(q.shape, BF16),
        compiler_params=_params("arbitrary", "arbitrary"),
        name="attn_gqa_latent" if has_ctx else "attn_gqa_context",
    )(*args)


def _attn_b_kernel(has_ctx, qn_ref, qr_ref, kn_ref, kp_ref, v_ref, *rest):
    if has_ctx:
        knc_ref, kpc_ref, vc_ref, o_ref = rest
    else:
        (o_ref,) = rest
    tq = min(Q_TILE, qn_ref.shape[0])
    lane = lax.broadcasted_iota(jnp.int32, (tq, LANES), 1)
    low = lane < (LANES // 2)
    quarter = lane // MLA_ROPE
    zero = jnp.zeros((tq, LANES), BF16)
    kp = kp_ref[...]
    kpc = kpc_ref[...] if has_ctx else None
    pairs_per_rope = LANES // MLA_ROPE // 2
    n_pairs = HEADS_PER_STEP // 2
    pair_cols = [slice(p * LANES, (p + 1) * LANES) for p in range(n_pairs)]

    def pair_lhs(p, rows=slice(None)):
        qn = qn_ref[rows, pair_cols[p]]
        rg = p // pairs_per_rope
        qr = qr_ref[rows, rg * LANES:(rg + 1) * LANES]
        return [jnp.concatenate([jnp.where(low if j == 0 else ~low, qn, zero),
                                 jnp.where(quarter == 2 * (p % pairs_per_rope) + j, qr, zero)], axis=1)
                for j in range(2)]

    if not has_ctx:
        qs = [jnp.concatenate(pair_lhs(p), axis=0) for p in range(n_pairs)]
        out = _attend_group(qs, [jnp.concatenate([kn_ref[:, c], kp], axis=1) for c in pair_cols],
                            [_with_ones(v_ref[:, c]) for c in pair_cols])
        for p, c in enumerate(pair_cols):
            o_ref[:, c] = _merge_pair(out[2 * p * tq:(2 * p + 1) * tq], out[(2 * p + 1) * tq:(2 * p + 2) * tq])
        return
    for p, cols in enumerate(pair_cols):
        segs = [(jnp.concatenate([kn_ref[:, cols], kp], axis=1), _with_ones(v_ref[:, cols]), None),
                (jnp.concatenate([knc_ref[:, cols], kpc], axis=1), _with_ones(vc_ref[:, cols]), None)]
        for r in range(qn_ref.shape[0] // tq):
            rows = slice(r * tq, (r + 1) * tq)
            o_ref[rows, cols] = _merge_pair(*[_attend(lhs, segs) for lhs in pair_lhs(p, rows)])


def _attention_b(qn, qr, kn, kp, v, knc=None, kpc=None, vc=None):
    b, s, _ = qn.shape
    kl = kn.shape[1]
    has_ctx = knc is not None
    tq = min(LATENT_Q_ROWS if has_ctx else Q_TILE, s)
    gw = HEADS_PER_STEP * MLA_NOPE
    blk = lambda rows, w, fn: pl.BlockSpec((None, rows, w), fn)
    in_specs = [
        blk(tq, gw, lambda bi, g, t: (bi, t, g)),
        blk(tq, HEADS_PER_STEP * MLA_ROPE, lambda bi, g, t: (bi, t, g)),
        blk(kl, gw, lambda bi, g, t: (bi, 0, g)),
        blk(kl, LANES, lambda bi, g, t: (bi, 0, 0)),
        blk(kl, gw, lambda bi, g, t: (bi, 0, g)),
    ]
    args = [qn, qr, kn, kp, v]
    if has_ctx:
        cl = knc.shape[1]
        in_specs += [
            blk(cl, gw, lambda bi, g, t: (bi, 0, g)),
            blk(cl, LANES, lambda bi, g, t: (bi, 0, 0)),
            blk(cl, gw, lambda bi, g, t: (bi, 0, g)),
        ]
        args += [knc, kpc, vc]
    return pl.pallas_call(
        functools.partial(_attn_b_kernel, has_ctx),
        grid=(b, MLA_HEADS // HEADS_PER_STEP, s // tq),
        in_specs=in_specs,
        out_specs=blk(tq, gw, lambda bi, g, t: (bi, t, g)),
        out_shape=jax.ShapeDtypeStruct(qn.shape, BF16),
        compiler_params=_params("arbitrary", "arbitrary", "arbitrary"),
        name="attn_mla_latent" if has_ctx else "attn_mla_context",
    )(*args)


def _attn_c_kernel(latent, sink_ref, q_ref, kd_ref, vd_ref, *rest):
    if latent:
        kc_ref, vc_ref, o_ref = rest
    else:
        (o_ref,) = rest
    tq = min(Q_TILE, q_ref.shape[0])
    group = SWA_HEADS // SWA_KV_HEADS
    kv_per_step = HEADS_PER_STEP // group
    head0 = pl.program_id(1) * HEADS_PER_STEP
    windows = []
    if latent:
        s_len = kd_ref.shape[0]
        win = tq + 2 * SWA_WINDOW
        for r in range(q_ref.shape[0] // tq):
            t0 = pl.program_id(2) * q_ref.shape[0] + r * tq
            ws = pl.multiple_of(jnp.clip(t0 - SWA_WINDOW, 0, s_len - win), LANES)
            qpos = t0 + lax.broadcasted_iota(jnp.int32, (tq, win), 0)
            kpos = ws + lax.broadcasted_iota(jnp.int32, (tq, win), 1)
            bias = jnp.where(jnp.abs(qpos - kpos) <= SWA_WINDOW, 0.0, NEG_INF).astype(F32)
            windows.append((slice(r * tq, (r + 1) * tq), ws, bias))
    pairs_per_kv = group // 2
    pair_cols = [slice(p * LANES, (p + 1) * LANES) for p in range(HEADS_PER_STEP // 2)]
    kv_cols = [slice(j * LANES, (j + 1) * LANES) for j in range(kv_per_step)]
    if not latent:
        qs = [jnp.concatenate([h for c in pair_cols[j * pairs_per_kv:(j + 1) * pairs_per_kv]
                               for h in _split_pair(q_ref[:, c])], axis=0) for j in range(kv_per_step)]
        out = _attend_group(qs, [kd_ref[:, c] for c in kv_cols], [_with_ones(vd_ref[:, c]) for c in kv_cols],
                            sink_ref[...])
        for p, c in enumerate(pair_cols):
            o_ref[:, c] = _merge_pair(out[2 * p * tq:(2 * p + 1) * tq], out[(2 * p + 1) * tq:(2 * p + 2) * tq])
        return
    sinks = [sink_ref[head0 + h] * LOG2E for h in range(HEADS_PER_STEP)]
    hd = SWA_HEAD_DIM
    for j, kc in enumerate(kv_cols):
        kct = kc_ref[j * hd:(j + 1) * hd, :].astype(BF16)
        vct = vc_ref[j * hd:(j + 1) * hd, :].astype(BF16)
        ctx_seg = (jnp.concatenate([kct, kct], axis=0),
                   jnp.concatenate([vct, vct, jnp.ones((LANES, vct.shape[1]), BF16)], axis=0), None, True)
        for rows, ws, bias in windows:
            segs = [(kd_ref[pl.ds(ws, win), kc], _with_ones(vd_ref[pl.ds(ws, win), kc]), (bias, bias)), ctx_seg]
            for p in range(j * pairs_per_kv, (j + 1) * pairs_per_kv):
                o_ref[rows, pair_cols[p]] = _attend_pair(q_ref[rows, pair_cols[p]], segs, sinks[2 * p:2 * p + 2])


def _sink_logit_block(sink, tq):
    rows = jnp.repeat(sink * LOG2E, tq)[:, None]
    return jnp.where(jnp.arange(LANES)[None, :] == 0, rows, NEG_INF)


def _attention_c(sink, q, kd, vd, kc, vc):
    b, s, _ = q.shape
    kl = kd.shape[1]
    tq = min(LATENT_Q_ROWS, s)
    gw = HEADS_PER_STEP * SWA_HEAD_DIM
    kvw = 2 * SWA_HEAD_DIM * HEADS_PER_STEP // (SWA_HEADS // SWA_KV_HEADS)
    cw, cl = kc.shape[1] * HEADS_PER_STEP // SWA_HEADS, kc.shape[2]
    return pl.pallas_call(
        functools.partial(_attn_c_kernel, True),
        grid=(b, SWA_HEADS // HEADS_PER_STEP, s // tq),
        in_specs=[
            pl.BlockSpec(memory_space=pltpu.SMEM),
            pl.BlockSpec((None, tq, gw), lambda bi, g, t: (bi, t, g)),
            pl.BlockSpec((None, kl, kvw), lambda bi, g, t: (bi, 0, g)),
            pl.BlockSpec((None, kl, kvw), lambda bi, g, t: (bi, 0, g)),
            pl.BlockSpec((None, cw, cl), lambda bi, g, t: (bi, g, 0)),
            pl.BlockSpec((None, cw, cl), lambda bi, g, t: (bi, g, 0)),
        ],
        out_specs=pl.BlockSpec((None, tq, gw), lambda bi, g, t: (bi, t, g)),
        out_shape=jax.ShapeDtypeStruct(q.shape, BF16),
        compiler_params=_params("arbitrary", "arbitrary", "arbitrary"),
        name="attn_swa_latent",
    )(sink, q, kd, vd, kc, vc)


def _attn_d_ctx_kernel(q_ref, k_ref, v_ref, o_ref):
    tq = q_ref.shape[0]
    pair_cols = [slice(p * LANES, (p + 1) * LANES) for p in range(HEADS_PER_STEP // 2)]
    qs = [jnp.concatenate(_split_pair(q_ref[:, c]), axis=0) for c in pair_cols]
    out = _attend_group(qs, [k_ref[:, c] for c in pair_cols], [_with_ones(v_ref[:, c]) for c in pair_cols])
    for p, c in enumerate(pair_cols):
        o_ref[:, c] = _merge_pair(out[2 * p * tq:(2 * p + 1) * tq], out[(2 * p + 1) * tq:(2 * p + 2) * tq])


def _nat_row0(r, rows):
    return min(max(r - NAT_WIN_R // 2, 0), rows - NAT_WIN_R)


def _nat_tile_key_rows(tile, rows):
    first, last = tile * NAT_TILE_ROWS, (tile + 1) * NAT_TILE_ROWS - 1
    need = _nat_row0(last, rows) + NAT_WIN_R - _nat_row0(first, rows)
    return need + need % 2


def _nat_tile_start(tile, rows):
    return min(_nat_row0(tile * NAT_TILE_ROWS, rows), rows - _nat_tile_key_rows(tile, rows))


def _attn_d_lat_kernel(q_ref, k_ref, v_ref, kc_ref, vc_ref, bias_ref, o_ref):
    rows = k_ref.shape[0] // GRID_W
    tile = pl.program_id(0)
    ones = jnp.ones((LANES, kc_ref.shape[1]), BF16)

    def body(key_rows, start_row):
        slab = key_rows * GRID_W
        start = pl.multiple_of(start_row * GRID_W, GRID_W)
        for p in range(HEADS_PER_STEP // 2):
            cols = slice(p * LANES, (p + 1) * LANES)
            segs = [(k_ref[pl.ds(start, slab), cols], _with_ones(v_ref[pl.ds(start, slab), cols]),
                     (bias_ref[2 * p, :, :slab], bias_ref[2 * p + 1, :, :slab])),
                    (kc_ref[cols, :].astype(BF16),
                     jnp.concatenate([vc_ref[cols, :].astype(BF16), ones], axis=0), None, True)]
            o_ref[:, cols] = _attend_pair(q_ref[:, cols], segs)

    tiles = range(rows // NAT_TILE_ROWS)
    for key_rows in sorted({_nat_tile_key_rows(t, rows) for t in tiles}):
        members = [t for t in tiles if _nat_tile_key_rows(t, rows) == key_rows]
        cond = functools.reduce(jnp.logical_or, [tile == t for t in members])
        start_row = functools.reduce(lambda acc, t: jnp.where(tile == t, _nat_tile_start(t, rows), acc),
                                     members, jnp.int32(0))
        pl.when(cond)(functools.partial(body, key_rows, start_row))


def _attention_d_lat(q, k, v, kc, vc, bias):
    b, s, _ = q.shape
    cl = kc.shape[2]
    tq = NAT_TILE_ROWS * GRID_W
    slab = NAT_KEY_ROWS * GRID_W
    gw = HEADS_PER_STEP * NAT_HEAD_DIM
    return pl.pallas_call(
        _attn_d_lat_kernel,
        grid=(s // tq, NAT_HEADS // HEADS_PER_STEP, b),
        in_specs=[
            pl.BlockSpec((None, tq, gw), lambda t, g, bi: (bi, t, g)),
            pl.BlockSpec((None, s, gw), lambda t, g, bi: (bi, 0, g)),
            pl.BlockSpec((None, s, gw), lambda t, g, bi: (bi, 0, g)),
            pl.BlockSpec((None, gw, cl), lambda t, g, bi: (bi, g, 0)),
            pl.BlockSpec((None, gw, cl), lambda t, g, bi: (bi, g, 0)),
            pl.BlockSpec((HEADS_PER_STEP, None, tq, slab), lambda t, g, bi: (g, t, 0, 0)),
        ],
        out_specs=pl.BlockSpec((None, tq, gw), lambda t, g, bi: (bi, t, g)),
        out_shape=jax.ShapeDtypeStruct(q.shape, BF16),
        compiler_params=_params("arbitrary", "arbitrary", "arbitrary"),
        name="attn_nat_latent",
    )(q, k, v, kc, vc, bias)


def _mla_expand_kernel(c_ref, w_ref, kn_ref, v_ref):
    kv = _dot(c_ref[...].astype(BF16), w_ref[...])
    half = kv.shape[1] // 2
    kn_ref[...] = kv[:, :half].astype(BF16)
    v_ref[...] = kv[:, half:].astype(BF16)


def _mla_expand(ckv, w_ukv):
    b, l, c = ckv.shape
    n = w_ukv.shape[1] // 2
    out = pl.BlockSpec((None, l, n), lambda bi: (bi, 0, 0))
    return pl.pallas_call(
        _mla_expand_kernel,
        grid=(b,),
        in_specs=[pl.BlockSpec((None, l, c), lambda bi: (bi, 0, 0)), _full(w_ukv.shape)],
        out_specs=[out, out],
        out_shape=[jax.ShapeDtypeStruct((b, l, n), BF16)] * 2,
        compiler_params=_params("arbitrary"),
        name="mla_expand_cache",
    )(ckv, w_ukv)


def _finish_kernel(o_ref, x_ref, m_ref, g_ref, wo_ref, w1_ref, w2_ref, y_ref, x1_ref, h2_ref):
    f = pl.program_id(2)
    last = pl.num_programs(2) - 1
    chains = [slice(c * TOK_CHAIN, (c + 1) * TOK_CHAIN) for c in range(y_ref.shape[0] // TOK_CHAIN)]

    def mlp_part(h, w1, w2):
        u = jnp.square(jnp.maximum(_dot(h, w1), 0.0)).astype(BF16)
        return _dot(u, w2)

    @pl.when(f == 0)
    def _():
        m = m_ref[...]
        for rows in chains:
            a = _dot(o_ref[rows, :], wo_ref[...])
            x1 = x_ref[rows, :] + _rms(a, _mod(m, 2) * g_ref[1:2, :])
            x1_ref[rows, :] = x1
            h2_ref[rows, :] = _premod(x1, g_ref[2:3, :], m, 1).astype(BF16)
        y_ref[...] = mlp_part(h2_ref[...], w1_ref[...].astype(BF16), w2_ref[...].astype(BF16))

    @pl.when((f > 0) & (f < last))
    def _():
        y_ref[...] += mlp_part(h2_ref[...], w1_ref[...].astype(BF16), w2_ref[...].astype(BF16))

    @pl.when(f == last)
    def _():
        gain = _mod(m_ref[...], 5) * g_ref[3:4, :]
        w1, w2 = w1_ref[...].astype(BF16), w2_ref[...].astype(BF16)
        for rows in chains:
            y = y_ref[rows, :] + mlp_part(h2_ref[rows, :], w1, w2)
            y_ref[rows, :] = x1_ref[rows, :] + _rms(y, gain)


def _finish_layer(o, x, mods_l, g, wo, w1, w2, layer, latent):
    bx, s, _ = x.shape
    tm, tf = MLP_TOK_TILE, MLP_FF_TILE
    row = (lambda b, j, f: (b, 0, 0)) if latent else (lambda b, j, f: (CTX_MOD_ROW, 0, 0))
    tok = lambda w: pl.BlockSpec((None, tm, w), lambda b, j, f: (b, j, 0))
    return pl.pallas_call(
        _finish_kernel,
        grid=(bx, s // tm, D_FF // tf),
        in_specs=[tok(o.shape[-1]), tok(D_MODEL),
                  pl.BlockSpec((None, 1, N_MOD * D_MODEL), row), _full(g.shape), _full(wo.shape),
                  pl.BlockSpec((None, D_MODEL, tf), lambda b, j, f: (layer, 0, f)),
                  pl.BlockSpec((None, tf, D_MODEL), lambda b, j, f: (layer, f, 0))],
        out_specs=tok(D_MODEL),
        out_shape=jax.ShapeDtypeStruct(x.shape, F32),
        scratch_shapes=[pltpu.VMEM((tm, D_MODEL), F32), pltpu.VMEM((tm, D_MODEL), BF16)],
        compiler_params=pltpu.CompilerParams(dimension_semantics=("arbitrary",) * 3,
                                             vmem_limit_bytes=MLP_VMEM_LIMIT),
        name="out_proj_mlp",
    )(o, x, mods_l, g, wo, w1, w2)


def _rope_tables(s, dim):
    quarter = dim // 4
    t = jnp.arange(s)
    pos = jnp.stack([t // GRID_W, t % GRID_W], axis=-1).astype(F32)
    inv = ROPE_THETA ** (-jnp.arange(quarter, dtype=F32) / quarter)
    ang = pos[:, :, None] * inv
    cos = jnp.broadcast_to(jnp.cos(ang)[:, :, None, :], (s, 2, 2, quarter)).reshape(s, dim)
    sign = jnp.array([-1.0, 1.0], F32)[None, None, :, None]
    sin = (jnp.sin(ang)[:, :, None, :] * sign).reshape(s, dim)
    reps = LANES // dim
    return jnp.tile(cos, (1, reps)), jnp.tile(sin, (1, reps))


def _dup_heads(w, heads, dim):
    lead = w.shape[:-1]
    w = w.reshape(lead + (heads, 1, dim))
    return jnp.broadcast_to(w, lead + (heads, 2, dim)).reshape(lead + (heads * 2 * dim,))


def _nat_dense_bias(rpb, rows):
    heads = rpb.shape[0]
    c = np.arange(GRID_W)
    c0 = np.clip(c - NAT_WIN_C // 2, 0, GRID_W - NAT_WIN_C)
    in_c = (c[None, :] >= c0[:, None]) & (c[None, :] < c0[:, None] + NAT_WIN_C)
    dc = c[None, :] - c[:, None] + NAT_WIN_C - 1
    onehot = (dc[None] == np.arange(2 * NAT_WIN_C - 1)[:, None, None]) & in_c[None]
    toe = jnp.einsum("had,dck->hack", rpb * LOG2E, jnp.asarray(onehot, F32),
                     precision=lax.Precision.HIGHEST)
    toe = jnp.where(jnp.asarray(in_c)[None, None], toe, NEG_INF)
    pad = jnp.full((heads, 1, GRID_W, GRID_W), NEG_INF, F32)
    ext = jnp.concatenate([pad, toe, pad], axis=1)
    pairs = jnp.concatenate([ext[:, :-1], ext[:, 1:]], axis=-1)
    tiles = rows // NAT_TILE_ROWS
    tq, slab = NAT_TILE_ROWS * GRID_W, NAT_KEY_ROWS * GRID_W
    n_off = 2 * NAT_WIN_R
    return pl.pallas_call(
        functools.partial(_nat_bias_kernel, rows),
        grid=(heads,),
        in_specs=[pl.BlockSpec((None, n_off, GRID_W, LANES), lambda h: (h, 0, 0, 0))],
        out_specs=pl.BlockSpec((None, tiles, tq, slab), lambda h: (h, 0, 0, 0)),
        out_shape=jax.ShapeDtypeStruct((heads, tiles, tq, slab), F32),
        compiler_params=_params("arbitrary"),
        name="nat_bias_expand",
    )(pairs)


def _nat_bias_kernel(rows, pairs_ref, o_ref):
    low = lax.broadcasted_iota(jnp.int32, (GRID_W, LANES), 1) < GRID_W
    masked = jnp.full((GRID_W, LANES), NEG_INF, F32)
    for r in range(rows):
        tile, i = divmod(r, NAT_TILE_ROWS)
        r0 = _nat_row0(r, rows)
        ws = _nat_tile_start(tile, rows)
        for jb in range(NAT_KEY_ROWS // 2):
            kr = ws + 2 * jb
            ok_lo, ok_hi = r0 <= kr < r0 + NAT_WIN_R, r0 <= kr + 1 < r0 + NAT_WIN_R
            blk = masked
            if ok_lo or ok_hi:
                blk = pairs_ref[kr - r + NAT_WIN_R]
                if not ok_lo:
                    blk = jnp.where(low, NEG_INF, blk)
                elif not ok_hi:
                    blk = jnp.where(low, blk, NEG_INF)
            o_ref[tile, i * GRID_W:(i + 1) * GRID_W, jb * LANES:(jb + 1) * LANES] = blk


def kernel(x_prompt, x_sample, cache_l0_k, cache_l0_v, cache_l1_ckv, cache_l1_kpe, cache_l2_k, cache_l2_v, cache_l3_k, cache_l3_v, c, c_ctx, ada_w, ada_b, norm_g, mlp_w1, mlp_w2, attn_w_qkv, attn_q_norm, attn_k_norm, attn_w_o, mla_w_in, mla_q_norm, mla_kv_norm, mla_w_uq, mla_w_ukv, mla_w_o, swa_w_qkv, swa_sink, swa_w_o, nat_w_qkv, nat_rpb, nat_w_o):
    nb, seq, d = x_prompt.shape
    db, dseq, _ = x_sample.shape
    past = cache_l0_k.shape[1]
    ctx_b = nb * seq // dseq
    xp = x_prompt.reshape(ctx_b, dseq, d)
    xs = x_sample

    cond = jnp.zeros((COND_ROWS, d), F32).at[:db].set(c).at[CTX_MOD_ROW].set(c_ctx)
    mods = _modulation(cond, ada_w, ada_b).reshape(DEPTH, COND_ROWS, 1, N_MOD * d)

    row = lambda v: v.reshape(1, -1)

    def head_major(cache):
        b_, l_, h_, dh = cache.shape
        return cache.transpose(0, 2, 3, 1).reshape(b_, h_ * dh, l_)

    def token_major(state, heads):
        b_, w_, l_ = state.shape
        return state.reshape(b_, heads, w_ // heads, l_).transpose(0, 3, 1, 2)

    def finish(o, x, layer, wo, latent):
        return _finish_layer(o, x, mods[layer], norm_g[layer], wo, mlp_w1, mlp_w2, layer, latent)

    g = norm_g[0]
    w = attn_w_qkv.astype(BF16)
    wo = attn_w_o.astype(BF16)
    consts = [w, row(attn_q_norm), row(attn_k_norm)]
    kvw = ATTN_KV_HEADS * ATTN_HEAD_DIM
    o, l0_k, l0_v = _run_proj(
        _proj_a_kernel, False, xp, mods[0], g, consts, [], [(kvw, F32), (kvw, F32)], "proj_attn_gqa_context",
        seq, attend=functools.partial(_attn_a_kernel, False), qkv=[(d, BF16), (kvw, BF16), (kvw, BF16)])
    xp = finish(o, xp, 0, wo, False)
    perm = np.arange(ATTN_HEAD_DIM).reshape(2, 2, ATTN_HEAD_DIM // 4).transpose(1, 0, 2).reshape(-1)
    qk_heads = ATTN_HEADS + ATTN_KV_HEADS
    cols = (np.arange(qk_heads)[:, None] * ATTN_HEAD_DIM + perm[None, :]).reshape(-1)
    w_lat = jnp.concatenate([attn_w_qkv[:, cols], attn_w_qkv[:, qk_heads * ATTN_HEAD_DIM:]], axis=1).astype(BF16)
    consts_lat = [w_lat, row(attn_q_norm[perm]), row(attn_k_norm[perm])]
    tables = [t[:, perm] for t in _rope_tables(dseq, ATTN_HEAD_DIM)]
    q, k, v = _run_proj(_proj_a_kernel, True, xs, mods[0], g, consts_lat, tables,
                        [(d, BF16), (kvw, BF16), (kvw, BF16)], "proj_gqa_latent")
    o = _attention_a(q, k, v, cache_l0_k[..., perm].reshape(db, past, kvw).astype(BF16),
                     cache_l0_v.reshape(db, past, kvw).astype(BF16))
    xs = finish(o, xs, 0, wo, True)
    new_l0 = (l0_k.reshape(nb, seq, ATTN_KV_HEADS, ATTN_HEAD_DIM),
              l0_v.reshape(nb, seq, ATTN_KV_HEADS, ATTN_HEAD_DIM))

    g = norm_g[1]
    nq, nkv = MLA_Q_LORA, MLA_KV_LORA
    w_in = jnp.concatenate([mla_w_in[:, :nq + nkv]] + [mla_w_in[:, nq + nkv:]] * (LANES // MLA_ROPE),
                           axis=1).astype(BF16)
    wuq = mla_w_uq.reshape(nq, MLA_HEADS, MLA_NOPE + MLA_ROPE)
    wuq = jnp.concatenate([wuq[:, :, :MLA_NOPE].reshape(nq, -1), wuq[:, :, MLA_NOPE:].reshape(nq, -1)],
                          axis=1).astype(BF16)
    wukv = mla_w_ukv.reshape(nkv, MLA_HEADS, MLA_NOPE + MLA_V_DIM)
    wukv = jnp.concatenate([wukv[:, :, :MLA_NOPE].reshape(nkv, -1), wukv[:, :, MLA_NOPE:].reshape(nkv, -1)],
                           axis=1).astype(BF16)
    wo = mla_w_o.astype(BF16)
    consts = [w_in, row(mla_q_norm), row(mla_kv_norm), wuq, wukv]
    hw = MLA_HEADS * MLA_NOPE
    rw = MLA_HEADS * MLA_ROPE
    outs = [(hw, BF16), (rw, BF16), (hw, BF16), (hw, BF16), (LANES, BF16)]
    o, l1_ckv, l1_kpe = _run_proj(
        _proj_b_kernel, False, xp, mods[1], g, consts, [], [(nkv, F32), (MLA_ROPE, F32, "T")],
        "proj_attn_mla_context", seq, qkv=outs,
        attend=lambda qn, qr, kn, v, kp, o_ref: _attn_b_kernel(False, qn, qr, kn, kp, v, o_ref))
    xp = finish(o, xp, 1, wo, False)
    tables = list(_rope_tables(dseq, MLA_ROPE))
    qn, qr, kn, v, kp = _run_proj(_proj_b_kernel, True, xs, mods[1], g, consts, tables, outs,
                                  "proj_mla_latent")
    knc, vc = _mla_expand(cache_l1_ckv, wukv)
    kpc = jnp.tile(cache_l1_kpe, (1, 1, LANES // MLA_ROPE)).astype(BF16)
    o = _attention_b(qn, qr, kn, kp, v, knc, kpc, vc)
    xs = finish(o, xs, 1, wo, True)
    new_l1 = (l1_ckv.reshape(nb, seq, nkv), l1_kpe.transpose(0, 2, 1))

    g = norm_g[2]
    qw = SWA_HEADS * SWA_HEAD_DIM
    kw = SWA_KV_HEADS * SWA_HEAD_DIM
    wq, wk, wv = swa_w_qkv[:, :qw], swa_w_qkv[:, qw:qw + kw], swa_w_qkv[:, qw + kw:]
    wkd = _dup_heads(wk, SWA_KV_HEADS, SWA_HEAD_DIM)
    wvd = _dup_heads(wv, SWA_KV_HEADS, SWA_HEAD_DIM)
    w_ctx = jnp.concatenate([wq, wk, wv, wkd, wvd], axis=1).astype(BF16)
    w_lat = jnp.concatenate([wq, wkd, wvd], axis=1).astype(BF16)
    wo = swa_w_o.astype(BF16)
    o, l2_k, l2_v = _run_proj(
        _proj_c_kernel, False, xp, mods[2], g, [w_ctx], [], [(kw, F32, "T"), (kw, F32, "T")],
        "proj_attn_swa_context", seq, attend=functools.partial(_attn_c_kernel, False),
        attend_inputs=[_sink_logit_block(swa_sink, seq)], qkv=[(qw, BF16), (2 * kw, BF16), (2 * kw, BF16)])
    xp = finish(o, xp, 2, wo, False)
    tables = list(_rope_tables(dseq, SWA_HEAD_DIM))
    q, kd, vd = _run_proj(_proj_c_kernel, True, xs, mods[2], g, [w_lat], tables,
                          [(qw, BF16), (2 * kw, BF16), (2 * kw, BF16)], "proj_swa_latent")
    o = _attention_c(swa_sink, q, kd, vd, head_major(cache_l2_k), head_major(cache_l2_v))
    xs = finish(o, xs, 2, wo, True)
    new_l2 = (token_major(l2_k, SWA_KV_HEADS), token_major(l2_v, SWA_KV_HEADS))

    g = norm_g[3]
    hw = NAT_HEADS * NAT_HEAD_DIM
    w = nat_w_qkv.astype(BF16)
    wo = nat_w_o.astype(BF16)
    o, l3_k, l3_v = _run_proj(
        _proj_d_kernel, False, xp, mods[3], g, [w], [], [(hw, F32, "T"), (hw, F32, "T")],
        "proj_attn_nat_context", seq, attend=_attn_d_ctx_kernel, qkv=[(hw, BF16)] * 3)
    xp = finish(o, xp, 3, wo, False)
    q, k, v = _run_proj(_proj_d_kernel, True, xs, mods[3], g, [w], [],
                        [(hw, BF16), (hw, BF16), (hw, BF16)], "proj_nat_latent")
    o = _attention_d_lat(q, k, v, head_major(cache_l3_k), head_major(cache_l3_v),
                         _nat_dense_bias(nat_rpb, dseq // GRID_W))
    xs = finish(o, xs, 3, wo, True)
    new_l3 = (token_major(l3_k, NAT_HEADS), token_major(l3_v, NAT_HEADS))

    return (xp.reshape(nb, seq, d), xs) + new_l0 + new_l1 + new_l2 + new_l3
```

```python
import functools

import numpy as np

import jax
import jax.numpy as jnp
from jax import lax
from jax.experimental import pallas as pl
from jax.experimental.pallas import tpu as pltpu

F32 = jnp.float32
BF16 = jnp.bfloat16

D_MODEL = 1024
DEPTH = 4
N_MOD = 6
D_FF = 4 * D_MODEL
GRID_W = 64
ROPE_THETA = 10000.0
NORM_EPS = 1e-6
NEG_INF = -1e30
LOG2E = 1.4426950408889634

ATTN_HEADS, ATTN_KV_HEADS, ATTN_HEAD_DIM = 8, 2, 128
MLA_HEADS, MLA_Q_LORA, MLA_KV_LORA = 16, 384, 256
MLA_NOPE, MLA_ROPE, MLA_V_DIM = 64, 32, 64
MLA_SCALE = (MLA_NOPE + MLA_ROPE) ** -0.5
SWA_HEADS, SWA_KV_HEADS, SWA_HEAD_DIM, SWA_WINDOW = 16, 4, 64, 128
NAT_HEADS, NAT_HEAD_DIM, NAT_WIN_R, NAT_WIN_C = 16, 64, 8, 16

LANES = 128
COND_ROWS = 16
CTX_MOD_ROW = 8
VMEM_LIMIT = 48 * 1024 * 1024

MOD_COL_TILE = 1536
TOK_TILE = 512
TOK_CHAIN = 256
MLP_TOK_TILE = 1024
MLP_FF_TILE = 1024
MLP_VMEM_LIMIT = 56 * 1024 * 1024
Q_TILE = 256
LATENT_Q_ROWS = 512
HEADS_PER_STEP = 16
NAT_TILE_ROWS = Q_TILE // GRID_W
NAT_KEY_ROWS = NAT_WIN_R + NAT_TILE_ROWS


def _params(*sem):
    return pltpu.CompilerParams(dimension_semantics=sem, vmem_limit_bytes=VMEM_LIMIT)


def _full(shape):
    nd = len(shape)
    return pl.BlockSpec(shape, lambda *_: (0,) * nd)


def _rms(x, g):
    return x * lax.rsqrt(jnp.mean(x * x, axis=-1, keepdims=True) + NORM_EPS) * g


def _mod(m, i):
    return m[:, i * D_MODEL:(i + 1) * D_MODEL]


def _premod(x, g, m, sub):
    return _rms(x, g * (1.0 + _mod(m, 3 * sub + 1))) + _mod(m, 3 * sub)


def _rope(x, cos, sin_signed, quarter):
    n = x.shape[-1]
    lane = lax.broadcasted_iota(jnp.int32, x.shape, 1)
    first = ((lane // quarter) % 2) == 0
    partner = jnp.where(first, pltpu.roll(x, n - quarter, 1), pltpu.roll(x, quarter, 1))
    return x * cos + partner * sin_signed


def _dot(a, b):
    return jnp.dot(a, b, preferred_element_type=F32)


def _dot_nt(a, b):
    return lax.dot_general(a, b, (((1,), (1,)), ((), ())), preferred_element_type=F32)


def _with_ones(v):
    return jnp.concatenate([v, jnp.ones_like(v)], axis=1)


def _attend(q, segs, sink=None):
    logits = []
    for seg in segs:
        k, bias, transposed = seg[0], seg[2], len(seg) > 3 and seg[3]
        s = _dot(q, k) if transposed else _dot_nt(q, k)
        logits.append(s if bias is None else s + bias)
    m = logits[0].max(axis=-1, keepdims=True)
    for s in logits[1:]:
        m = jnp.maximum(m, s.max(axis=-1, keepdims=True))
    if sink is not None:
        m = jnp.maximum(m, sink)
    acc = None
    for s, seg in zip(logits, segs):
        p = jnp.exp2(s - m).astype(BF16)
        pv = _dot_nt(p, seg[1]) if len(seg) > 3 and seg[3] else _dot(p, seg[1])
        acc = pv if acc is None else acc + pv
    den = acc[:, LANES:LANES + 1]
    if sink is not None:
        den = den + jnp.exp2(sink - m)
    return acc[:, :LANES] / den


def _attend_group(qs, ks, v1s, sink_logits=None):
    tq = qs[0].shape[0]
    s = jnp.concatenate([_dot_nt(q, k) for q, k in zip(qs, ks)], axis=0)
    if sink_logits is not None:
        s = jnp.concatenate([s, sink_logits], axis=1)
        zeros = jnp.zeros((LANES, LANES), BF16)
        tail = jnp.concatenate([zeros, jnp.ones_like(zeros)], axis=1)
        v1s = [jnp.concatenate([v1, tail], axis=0) for v1 in v1s]
    p = jnp.exp2(s - s.max(axis=-1, keepdims=True)).astype(BF16)
    acc = jnp.concatenate([_dot(p[i * tq:(i + 1) * tq], v1) for i, v1 in enumerate(v1s)], axis=0)
    return acc[:, :LANES] / acc[:, LANES:LANES + 1]


def _split_pair(q):
    low = lax.broadcasted_iota(jnp.int32, q.shape, 1) < (LANES // 2)
    zero = jnp.zeros_like(q)
    return [jnp.where(low, q, zero), jnp.where(low, zero, q)]


def _merge_pair(o0, o1):
    low = lax.broadcasted_iota(jnp.int32, o0.shape, 1) < (LANES // 2)
    return jnp.where(low, o0, o1).astype(BF16)


def _attend_pair(q, segs, sinks=None):
    outs = []
    for j, qj in enumerate(_split_pair(q)):
        segs_j = [(s[0], s[1], None if s[2] is None else s[2][j]) + tuple(s[3:]) for s in segs]
        outs.append(_attend(qj, segs_j, None if sinks is None else sinks[j]))
    return _merge_pair(*outs)


def _mods_kernel(cond_ref, w_ref, b_ref, o_ref):
    cnd = cond_ref[...]
    act = cnd * jax.nn.sigmoid(cnd)
    o_ref[...] = _dot(act.astype(BF16), w_ref[...].astype(BF16)) + b_ref[...]


def _modulation(cond, ada_w, ada_b):
    tn = MOD_COL_TILE
    n = N_MOD * D_MODEL
    return pl.pallas_call(
        _mods_kernel,
        grid=(DEPTH, n // tn),
        in_specs=[
            _full((COND_ROWS, D_MODEL)),
            pl.BlockSpec((None, D_MODEL, tn), lambda l, j: (l, 0, j)),
            pl.BlockSpec((None, 1, tn), lambda l, j: (l, 0, j)),
        ],
        out_specs=pl.BlockSpec((None, COND_ROWS, tn), lambda l, j: (l, 0, j)),
        out_shape=jax.ShapeDtypeStruct((DEPTH, COND_ROWS, n), F32),
        compiler_params=_params("arbitrary", "arbitrary"),
        name="adaln_mods",
    )(cond, ada_w, ada_b.reshape(DEPTH, 1, n))


def _proj_a_kernel(latent, x_ref, m_ref, g_ref, w_ref, qn_ref, kn_ref, *rest):
    if latent:
        cos_ref, sin_ref, q_ref, k_ref, v_ref = rest
    else:
        q_ref, k_ref, v_ref, ks_ref, vs_ref = rest
    hd = ATTN_HEAD_DIM
    n_qk = ATTN_HEADS + ATTN_KV_HEADS
    h = _premod(x_ref[...], g_ref[0:1, :], m_ref[...], 0).astype(BF16)
    for c in range(x_ref.shape[0] // TOK_CHAIN):
        rows = slice(c * TOK_CHAIN, (c + 1) * TOK_CHAIN)
        z = _dot(h[rows], w_ref[...])
        ys = [_rms(z[:, i * hd:(i + 1) * hd], qn_ref[...] if i < ATTN_HEADS else kn_ref[...]) for i in range(n_qk)]
        if not latent:
            for j in range(ATTN_KV_HEADS):
                ks_ref[rows, j * hd:(j + 1) * hd] = ys[ATTN_HEADS + j]
        else:
            ys = [y * cos_ref[rows, :] + pltpu.roll(y, hd // 2, 1) * sin_ref[rows, :] for y in ys]
        for i in range(ATTN_HEADS):
            q_ref[rows, i * hd:(i + 1) * hd] = (ys[i] * (hd ** -0.5 * LOG2E)).astype(BF16)
        for j in range(ATTN_KV_HEADS):
            k_ref[rows, j * hd:(j + 1) * hd] = ys[ATTN_HEADS + j].astype(BF16)
        v = z[:, n_qk * hd:]
        v_ref[rows, :] = v.astype(BF16)
        if not latent:
            vs_ref[rows, :] = v


def _proj_b_kernel(latent, x_ref, m_ref, g_ref, win_ref, qn_ref, kvn_ref, wuq_ref, wukv_ref, *rest):
    if latent:
        cos_ref, sin_ref, qn_o, qr_o, kn_o, v_o, kp_o = rest
    else:
        qn_o, qr_o, kn_o, v_o, kp_o, ckv_s, kpe_s = rest
    nq, nkv = MLA_Q_LORA, MLA_KV_LORA
    nope_w = MLA_HEADS * MLA_NOPE
    scale = MLA_SCALE * LOG2E
    h = _premod(x_ref[...], g_ref[0:1, :], m_ref[...], 0)
    z = _dot(h.astype(BF16), win_ref[...])
    cq = _rms(z[:, :nq], qn_ref[...])
    ckv = _rms(z[:, nq:nq + nkv], kvn_ref[...])
    kp = z[:, nq + nkv:]
    q = _dot(cq.astype(BF16), wuq_ref[...])
    kv = _dot(ckv.astype(BF16), wukv_ref[...])
    if not latent:
        ckv_s[...] = ckv
        _store_transposed(kpe_s, kp, MLA_ROPE)
    qn_o[...] = (q[:, :nope_w] * scale).astype(BF16)
    for i in range(MLA_HEADS * MLA_ROPE // LANES):
        qr = q[:, nope_w + i * LANES:nope_w + (i + 1) * LANES]
        if latent:
            qr = _rope(qr, cos_ref[...], sin_ref[...], MLA_ROPE // 4)
        qr_o[:, i * LANES:(i + 1) * LANES] = (qr * scale).astype(BF16)
    if latent:
        kp = _rope(kp, cos_ref[...], sin_ref[...], MLA_ROPE // 4)
    kp_o[...] = kp.astype(BF16)
    kn_o[...] = kv[:, :nope_w].astype(BF16)
    v_o[...] = kv[:, nope_w:].astype(BF16)


def _proj_c_kernel(latent, x_ref, m_ref, g_ref, w_ref, *rest):
    if latent:
        cos_ref, sin_ref, q_o, kd_o, vd_o = rest
    else:
        q_o, kd_o, vd_o, ks_o, vs_o = rest
    qw = SWA_HEADS * SWA_HEAD_DIM
    kw = SWA_KV_HEADS * SWA_HEAD_DIM
    h = _premod(x_ref[...], g_ref[0:1, :], m_ref[...], 0)
    z = _dot(h.astype(BF16), w_ref[...])
    off = qw
    if not latent:
        _store_transposed(ks_o, z[:, qw:qw + kw], kw)
        _store_transposed(vs_o, z[:, qw + kw:qw + 2 * kw], kw)
        off = qw + 2 * kw
    scale = SWA_HEAD_DIM ** -0.5 * LOG2E
    for i in range(qw // LANES):
        y = z[:, i * LANES:(i + 1) * LANES]
        if latent:
            y = _rope(y, cos_ref[...], sin_ref[...], SWA_HEAD_DIM // 4)
        q_o[:, i * LANES:(i + 1) * LANES] = (y * scale).astype(BF16)
    for i in range(2 * kw // LANES):
        y = z[:, off + i * LANES:off + (i + 1) * LANES]
        if latent:
            y = _rope(y, cos_ref[...], sin_ref[...], SWA_HEAD_DIM // 4)
        kd_o[:, i * LANES:(i + 1) * LANES] = y.astype(BF16)
    vd_o[...] = z[:, off + 2 * kw:].astype(BF16)


def _proj_d_kernel(latent, x_ref, m_ref, g_ref, w_ref, *rest):
    if latent:
        q_o, k_o, v_o = rest
    else:
        q_o, k_o, v_o, ks_o, vs_o = rest
    hw = NAT_HEADS * NAT_HEAD_DIM
    h = _premod(x_ref[...], g_ref[0:1, :], m_ref[...], 0)
    z = _dot(h.astype(BF16), w_ref[...])
    q_o[...] = (z[:, :hw] * (NAT_HEAD_DIM ** -0.5 * LOG2E)).astype(BF16)
    k_o[...] = z[:, hw:2 * hw].astype(BF16)
    v_o[...] = z[:, 2 * hw:].astype(BF16)
    if not latent:
        _store_transposed(ks_o, z[:, hw:2 * hw], hw)
        _store_transposed(vs_o, z[:, 2 * hw:], hw)


def _store_transposed(ref, val, width):
    n, _, seq = ref.shape
    for i in range(n):
        ref[i] = val[i * seq:(i + 1) * seq, :].T[:width, :]


def _run_proj(kernel, latent, x, mods_l, g, consts, tables, outs, name, state_seq=None,
              attend=None, attend_inputs=(), qkv=()):
    bx, s, _ = x.shape
    tm = TOK_TILE
    row = (lambda b, j: (b, 0, 0)) if latent else (lambda b, j: (CTX_MOD_ROW, 0, 0))
    in_specs = [
        pl.BlockSpec((None, tm, D_MODEL), lambda b, j: (b, j, 0)),
        pl.BlockSpec((None, 1, N_MOD * D_MODEL), row),
        _full(g.shape),
    ] + [_full(c.shape) for c in consts]
    in_specs += [pl.BlockSpec((tm, LANES), lambda b, j: (j, 0)) for _ in tables]
    in_specs += [_full(a.shape) for a in attend_inputs]
    body = functools.partial(kernel, latent)
    if attend is not None:
        n_proj_in = 3 + len(consts) + len(tables)
        n_att_in = len(attend_inputs)
        proj_body = body
        outs = [(D_MODEL, BF16)] + list(outs)

        def body(*refs):
            ins, att_ins = refs[:n_proj_in], refs[n_proj_in:n_proj_in + n_att_in]
            o_ref = refs[n_proj_in + n_att_in]
            states, scratch = refs[n_proj_in + n_att_in + 1:-len(qkv)], refs[-len(qkv):]
            proj_body(*ins, *scratch, *states)
            for i in range(tm // state_seq):
                rows = slice(i * state_seq, (i + 1) * state_seq)
                attend(*att_ins, *[r.at[rows] for r in scratch], o_ref.at[rows])

    out_specs, out_shape = [], []
    for o in outs:
        if len(o) == 2:
            out_specs.append(pl.BlockSpec((None, tm, o[0]), lambda b, j: (b, j, 0)))
            out_shape.append(jax.ShapeDtypeStruct((bx, s, o[0]), o[1]))
        else:
            per_tile = tm // state_seq
            out_specs.append(pl.BlockSpec((per_tile, o[0], state_seq),
                                          lambda b, j: (b * (s // tm) + j, 0, 0)))
            out_shape.append(jax.ShapeDtypeStruct((bx * s // state_seq, o[0], state_seq), o[1]))
    return pl.pallas_call(
        body,
        grid=(bx, s // tm),
        in_specs=in_specs,
        out_specs=out_specs,
        out_shape=out_shape,
        scratch_shapes=[pltpu.VMEM((tm, w), dt) for w, dt in qkv],
        compiler_params=_params("arbitrary", "arbitrary"),
        name=name,
    )(x, mods_l, g, *consts, *tables, *attend_inputs)


def _attn_a_kernel(has_ctx, q_ref, k_ref, v_ref, *rest):
    if has_ctx:
        kc_ref, vc_ref, o_ref = rest
    else:
        (o_ref,) = rest
    group = ATTN_HEADS // ATTN_KV_HEADS
    head_cols = [slice(h * LANES, (h + 1) * LANES) for h in range(ATTN_HEADS)]
    kv_cols = [slice(j * LANES, (j + 1) * LANES) for j in range(ATTN_KV_HEADS)]
    if not has_ctx:
        tq = q_ref.shape[0]
        qs = [jnp.concatenate([q_ref[:, c] for c in head_cols[j * group:(j + 1) * group]], axis=0)
              for j in range(ATTN_KV_HEADS)]
        out = _attend_group(qs, [k_ref[:, c] for c in kv_cols], [_with_ones(v_ref[:, c]) for c in kv_cols])
        for h, c in enumerate(head_cols):
            o_ref[:, c] = out[h * tq:(h + 1) * tq].astype(BF16)
        return
    for j, kc in enumerate(kv_cols):
        segs = [(k_ref[:, kc], _with_ones(v_ref[:, kc]), None),
                (kc_ref[:, kc], _with_ones(vc_ref[:, kc]), None)]
        for r in range(q_ref.shape[0] // Q_TILE):
            rows = slice(r * Q_TILE, (r + 1) * Q_TILE)
            for c in head_cols[j * group:(j + 1) * group]:
                o_ref[rows, c] = _attend(q_ref[rows, c], segs).astype(BF16)


def _attention_a(q, k, v, kc=None, vc=None):
    b, s, qw = q.shape
    kl, kw = k.shape[1:]
    has_ctx = kc is not None
    tq = min(LATENT_Q_ROWS if has_ctx else Q_TILE, s)
    in_specs = [
        pl.BlockSpec((None, tq, qw), lambda bi, t: (bi, t, 0)),
        pl.BlockSpec((None, kl, kw), lambda bi, t: (bi, 0, 0)),
        pl.BlockSpec((None, kl, kw), lambda bi, t: (bi, 0, 0)),
    ]
    args = [q, k, v]
    if has_ctx:
        cl = kc.shape[1]
        in_specs += [pl.BlockSpec((None, cl, kw), lambda bi, t: (bi, 0, 0))] * 2
        args += [kc, vc]
    return pl.pallas_call(
        functools.partial(_attn_a_kernel, has_ctx),
        grid=(b, s // tq),
        in_specs=in_specs,
        out_specs=pl.BlockSpec((None, tq, qw), lambda bi, t: (bi, t, 0)),
        out_shape=jax.ShapeDtypeStruct(q.shape, BF16),
        compiler_params=_params("arbitrary", "arbitrary"),
        name="attn_gqa_latent" if has_ctx else "attn_gqa_context",
    )(*args)


def _attn_b_kernel(has_ctx, qn_ref, qr_ref, kn_ref, kp_ref, v_ref, *rest):
    if has_ctx:
        knc_ref, kpc_ref, vc_ref, o_ref = rest
    else:
        (o_ref,) = rest
    tq = min(Q_TILE, qn_ref.shape[0])
    lane = lax.broadcasted_iota(jnp.int32, (tq, LANES), 1)
    low = lane < (LANES // 2)
    quarter = lane // MLA_ROPE
    zero = jnp.zeros((tq, LANES), BF16)
    kp = kp_ref[...]
    kpc = kpc_ref[...] if has_ctx else None
    pairs_per_rope = LANES // MLA_ROPE // 2
    n_pairs = HEADS_PER_STEP // 2
    pair_cols = [slice(p * LANES, (p + 1) * LANES) for p in range(n_pairs)]

    def pair_lhs(p, rows=slice(None)):
        qn = qn_ref[rows, pair_cols[p]]
        rg = p // pairs_per_rope
        qr = qr_ref[rows, rg * LANES:(rg + 1) * LANES]
        return [jnp.concatenate([jnp.where(low if j == 0 else ~low, qn, zero),
                                 jnp.where(quarter == 2 * (p % pairs_per_rope) + j, qr, zero)], axis=1)
                for j in range(2)]

    if not has_ctx:
        qs = [jnp.concatenate(pair_lhs(p), axis=0) for p in range(n_pairs)]
        out = _attend_group(qs, [jnp.concatenate([kn_ref[:, c], kp], axis=1) for c in pair_cols],
                            [_with_ones(v_ref[:, c]) for c in pair_cols])
        for p, c in enumerate(pair_cols):
            o_ref[:, c] = _merge_pair(out[2 * p * tq:(2 * p + 1) * tq], out[(2 * p + 1) * tq:(2 * p + 2) * tq])
        return
    for p, cols in enumerate(pair_cols):
        segs = [(jnp.concatenate([kn_ref[:, cols], kp], axis=1), _with_ones(v_ref[:, cols]), None),
                (jnp.concatenate([knc_ref[:, cols], kpc], axis=1), _with_ones(vc_ref[:, cols]), None)]
        for r in range(qn_ref.shape[0] // tq):
            rows = slice(r * tq, (r + 1) * tq)
            o_ref[rows, cols] = _merge_pair(*[_attend(lhs, segs) for lhs in pair_lhs(p, rows)])


def _attention_b(qn, qr, kn, kp, v, knc=None, kpc=None, vc=None):
    b, s, _ = qn.shape
    kl = kn.shape[1]
    has_ctx = knc is not None
    tq = min(2 * LATENT_Q_ROWS if has_ctx else Q_TILE, s)
    gw = HEADS_PER_STEP * MLA_NOPE
    blk = lambda rows, w, fn: pl.BlockSpec((None, rows, w), fn)
    in_specs = [
        blk(tq, gw, lambda bi, g, t: (bi, t, g)),
        blk(tq, HEADS_PER_STEP * MLA_ROPE, lambda bi, g, t: (bi, t, g)),
        blk(kl, gw, lambda bi, g, t: (bi, 0, g)),
        blk(kl, LANES, lambda bi, g, t: (bi, 0, 0)),
        blk(kl, gw, lambda bi, g, t: (bi, 0, g)),
    ]
    args = [qn, qr, kn, kp, v]
    if has_ctx:
        cl = knc.shape[1]
        in_specs += [
            blk(cl, gw, lambda bi, g, t: (bi, 0, g)),
            blk(cl, LANES, lambda bi, g, t: (bi, 0, 0)),
            blk(cl, gw, lambda bi, g, t: (bi, 0, g)),
        ]
        args += [knc, kpc, vc]
    return pl.pallas_call(
        functools.partial(_attn_b_kernel, has_ctx),
        grid=(b, MLA_HEADS // HEADS_PER_STEP, s // tq),
        in_specs=in_specs,
        out_specs=blk(tq, gw, lambda bi, g, t: (bi, t, g)),
        out_shape=jax.ShapeDtypeStruct(qn.shape, BF16),
        compiler_params=_params("arbitrary", "arbitrary", "arbitrary"),
        name="attn_mla_latent" if has_ctx else "attn_mla_context",
    )(*args)


def _attn_c_kernel(latent, sink_ref, q_ref, kd_ref, vd_ref, *rest):
    if latent:
        kc_ref, vc_ref, o_ref = rest
    else:
        (o_ref,) = rest
    tq = min(Q_TILE, q_ref.shape[0])
    group = SWA_HEADS // SWA_KV_HEADS
    kv_per_step = HEADS_PER_STEP // group
    head0 = pl.program_id(1) * HEADS_PER_STEP
    windows = []
    if latent:
        s_len = kd_ref.shape[0]
        win = tq + 2 * SWA_WINDOW
        for r in range(q_ref.shape[0] // tq):
            t0 = pl.program_id(2) * q_ref.shape[0] + r * tq
            ws = pl.multiple_of(jnp.clip(t0 - SWA_WINDOW, 0, s_len - win), LANES)
            qpos = t0 + lax.broadcasted_iota(jnp.int32, (tq, win), 0)
            kpos = ws + lax.broadcasted_iota(jnp.int32, (tq, win), 1)
            bias = jnp.where(jnp.abs(qpos - kpos) <= SWA_WINDOW, 0.0, NEG_INF).astype(F32)
            windows.append((slice(r * tq, (r + 1) * tq), ws, bias))
    pairs_per_kv = group // 2
    pair_cols = [slice(p * LANES, (p + 1) * LANES) for p in range(HEADS_PER_STEP // 2)]
    kv_cols = [slice(j * LANES, (j + 1) * LANES) for j in range(kv_per_step)]
    if not latent:
        qs = [jnp.concatenate([h for c in pair_cols[j * pairs_per_kv:(j + 1) * pairs_per_kv]
                               for h in _split_pair(q_ref[:, c])], axis=0) for j in range(kv_per_step)]
        out = _attend_group(qs, [kd_ref[:, c] for c in kv_cols], [_with_ones(vd_ref[:, c]) for c in kv_cols],
                            sink_ref[...])
        for p, c in enumerate(pair_cols):
            o_ref[:, c] = _merge_pair(out[2 * p * tq:(2 * p + 1) * tq], out[(2 * p + 1) * tq:(2 * p + 2) * tq])
        return
    sinks = [sink_ref[head0 + h] * LOG2E for h in range(HEADS_PER_STEP)]
    hd = SWA_HEAD_DIM
    for j, kc in enumerate(kv_cols):
        kct = kc_ref[j * hd:(j + 1) * hd, :].astype(BF16)
        vct = vc_ref[j * hd:(j + 1) * hd, :].astype(BF16)
        ctx_seg = (jnp.concatenate([kct, kct], axis=0),
                   jnp.concatenate([vct, vct, jnp.ones((LANES, vct.shape[1]), BF16)], axis=0), None, True)
        for rows, ws, bias in windows:
            segs = [(kd_ref[pl.ds(ws, win), kc], _with_ones(vd_ref[pl.ds(ws, win), kc]), (bias, bias)), ctx_seg]
            for p in range(j * pairs_per_kv, (j + 1) * pairs_per_kv):
                o_ref[rows, pair_cols[p]] = _attend_pair(q_ref[rows, pair_cols[p]], segs, sinks[2 * p:2 * p + 2])


def _sink_logit_block(sink, tq):
    rows = jnp.repeat(sink * LOG2E, tq)[:, None]
    return jnp.where(jnp.arange(LANES)[None, :] == 0, rows, NEG_INF)


def _attention_c(sink, q, kd, vd, kc, vc):
    b, s, _ = q.shape
    kl = kd.shape[1]
    tq = min(LATENT_Q_ROWS, s)
    gw = HEADS_PER_STEP * SWA_HEAD_DIM
    kvw = 2 * SWA_HEAD_DIM * HEADS_PER_STEP // (SWA_HEADS // SWA_KV_HEADS)
    cw, cl = kc.shape[1] * HEADS_PER_STEP // SWA_HEADS, kc.shape[2]
    return pl.pallas_call(
        functools.partial(_attn_c_kernel, True),
        grid=(b, SWA_HEADS // HEADS_PER_STEP, s // tq),
        in_specs=[
            pl.BlockSpec(memory_space=pltpu.SMEM),
            pl.BlockSpec((None, tq, gw), lambda bi, g, t: (bi, t, g)),
            pl.BlockSpec((None, kl, kvw), lambda bi, g, t: (bi, 0, g)),
            pl.BlockSpec((None, kl, kvw), lambda bi, g, t: (bi, 0, g)),
            pl.BlockSpec((None, cw, cl), lambda bi, g, t: (bi, g, 0)),
            pl.BlockSpec((None, cw, cl), lambda bi, g, t: (bi, g, 0)),
        ],
        out_specs=pl.BlockSpec((None, tq, gw), lambda bi, g, t: (bi, t, g)),
        out_shape=jax.ShapeDtypeStruct(q.shape, BF16),
        compiler_params=_params("arbitrary", "arbitrary", "arbitrary"),
        name="attn_swa_latent",
    )(sink, q, kd, vd, kc, vc)


def _attn_d_ctx_kernel(q_ref, k_ref, v_ref, o_ref):
    tq = q_ref.shape[0]
    pair_cols = [slice(p * LANES, (p + 1) * LANES) for p in range(HEADS_PER_STEP // 2)]
    qs = [jnp.concatenate(_split_pair(q_ref[:, c]), axis=0) for c in pair_cols]
    out = _attend_group(qs, [k_ref[:, c] for c in pair_cols], [_with_ones(v_ref[:, c]) for c in pair_cols])
    for p, c in enumerate(pair_cols):
        o_ref[:, c] = _merge_pair(out[2 * p * tq:(2 * p + 1) * tq], out[(2 * p + 1) * tq:(2 * p + 2) * tq])


def _nat_row0(r, rows):
    return min(max(r - NAT_WIN_R // 2, 0), rows - NAT_WIN_R)


def _nat_tile_key_rows(tile, rows):
    first, last = tile * NAT_TILE_ROWS, (tile + 1) * NAT_TILE_ROWS - 1
    need = _nat_row0(last, rows) + NAT_WIN_R - _nat_row0(first, rows)
    return need + need % 2


def _nat_tile_start(tile, rows):
    return min(_nat_row0(tile * NAT_TILE_ROWS, rows), rows - _nat_tile_key_rows(tile, rows))


def _attn_d_lat_kernel(q_ref, k_ref, v_ref, kc_ref, vc_ref, bias_ref, o_ref):
    rows = k_ref.shape[0] // GRID_W
    tile = pl.program_id(0)
    ones = jnp.ones((LANES, kc_ref.shape[1]), BF16)

    def body(key_rows, start_row):
        slab = key_rows * GRID_W
        start = pl.multiple_of(start_row * GRID_W, GRID_W)
        for p in range(HEADS_PER_STEP // 2):
            cols = slice(p * LANES, (p + 1) * LANES)
            segs = [(k_ref[pl.ds(start, slab), cols], _with_ones(v_ref[pl.ds(start, slab), cols]),
                     (bias_ref[2 * p, :, :slab], bias_ref[2 * p + 1, :, :slab])),
                    (kc_ref[cols, :].astype(BF16),
                     jnp.concatenate([vc_ref[cols, :].astype(BF16), ones], axis=0), None, True)]
            o_ref[:, cols] = _attend_pair(q_ref[:, cols], segs)

    tiles = range(rows // NAT_TILE_ROWS)
    for key_rows in sorted({_nat_tile_key_rows(t, rows) for t in tiles}):
        members = [t for t in tiles if _nat_tile_key_rows(t, rows) == key_rows]
        cond = functools.reduce(jnp.logical_or, [tile == t for t in members])
        start_row = functools.reduce(lambda acc, t: jnp.where(tile == t, _nat_tile_start(t, rows), acc),
                                     members, jnp.int32(0))
        pl.when(cond)(functools.partial(body, key_rows, start_row))


def _attention_d_lat(q, k, v, kc, vc, bias):
    b, s, _ = q.shape
    cl = kc.shape[2]
    tq = NAT_TILE_ROWS * GRID_W
    slab = NAT_KEY_ROWS * GRID_W
    gw = HEADS_PER_STEP * NAT_HEAD_DIM
    return pl.pallas_call(
        _attn_d_lat_kernel,
        grid=(s // tq, NAT_HEADS // HEADS_PER_STEP, b),
        in_specs=[
            pl.BlockSpec((None, tq, gw), lambda t, g, bi: (bi, t, g)),
            pl.BlockSpec((None, s, gw), lambda t, g, bi: (bi, 0, g)),
            pl.BlockSpec((None, s, gw), lambda t, g, bi: (bi, 0, g)),
            pl.BlockSpec((None, gw, cl), lambda t, g, bi: (bi, g, 0)),
            pl.BlockSpec((None, gw, cl), lambda t, g, bi: (bi, g, 0)),
            pl.BlockSpec((HEADS_PER_STEP, None, tq, slab), lambda t, g, bi: (g, t, 0, 0)),
        ],
        out_specs=pl.BlockSpec((None, tq, gw), lambda t, g, bi: (bi, t, g)),
        out_shape=jax.ShapeDtypeStruct(q.shape, BF16),
        compiler_params=_params("arbitrary", "arbitrary", "arbitrary"),
        name="attn_nat_latent",
    )(q, k, v, kc, vc, bias)


def _mla_expand_kernel(c_ref, w_ref, kn_ref, v_ref):
    kv = _dot(c_ref[...].astype(BF16), w_ref[...])
    half = kv.shape[1] // 2
    kn_ref[...] = kv[:, :half].astype(BF16)
    v_ref[...] = kv[:, half:].astype(BF16)


def _mla_expand(ckv, w_ukv):
    b, l, c = ckv.shape
    n = w_ukv.shape[1] // 2
    out = pl.BlockSpec((None, l, n), lambda bi: (bi, 0, 0))
    return pl.pallas_call(
        _mla_expand_kernel,
        grid=(b,),
        in_specs=[pl.BlockSpec((None, l, c), lambda bi: (bi, 0, 0)), _full(w_ukv.shape)],
        out_specs=[out, out],
        out_shape=[jax.ShapeDtypeStruct((b, l, n), BF16)] * 2,
        compiler_params=_params("arbitrary"),
        name="mla_expand_cache",
    )(ckv, w_ukv)


def _finish_kernel(o_ref, x_ref, m_ref, g_ref, wo_ref, w1_ref, w2_ref, y_ref, x1_ref, h2_ref):
    f = pl.program_id(2)
    last = pl.num_programs(2) - 1
    chains = [slice(c * TOK_CHAIN, (c + 1) * TOK_CHAIN) for c in range(y_ref.shape[0] // TOK_CHAIN)]

    def mlp_part(h, w1, w2):
        u = jnp.square(jnp.maximum(_dot(h, w1), 0.0)).astype(BF16)
        return _dot(u, w2)

    @pl.when(f == 0)
    def _():
        m = m_ref[...]
        for rows in chains:
            a = _dot(o_ref[rows, :], wo_ref[...])
            x1 = x_ref[rows, :] + _rms(a, _mod(m, 2) * g_ref[1:2, :])
            x1_ref[rows, :] = x1
            h2_ref[rows, :] = _premod(x1, g_ref[2:3, :], m, 1).astype(BF16)
        y_ref[...] = mlp_part(h2_ref[...], w1_ref[...].astype(BF16), w2_ref[...].astype(BF16))

    @pl.when((f > 0) & (f < last))
    def _():
        y_ref[...] += mlp_part(h2_ref[...], w1_ref[...].astype(BF16), w2_ref[...].astype(BF16))

    @pl.when(f == last)
    def _():
        gain = _mod(m_ref[...], 5) * g_ref[3:4, :]
        w1, w2 = w1_ref[...].astype(BF16), w2_ref[...].astype(BF16)
        for rows in chains:
            y = y_ref[rows, :] + mlp_part(h2_ref[rows, :], w1, w2)
            y_ref[rows, :] = x1_ref[rows, :] + _rms(y, gain)


def _finish_layer(o, x, mods_l, g, wo, w1, w2, layer, latent):
    bx, s, _ = x.shape
    tm, tf = MLP_TOK_TILE, MLP_FF_TILE
    row = (lambda b, j, f: (b, 0, 0)) if latent else (lambda b, j, f: (CTX_MOD_ROW, 0, 0))
    tok = lambda w: pl.BlockSpec((None, tm, w), lambda b, j, f: (b, j, 0))
    return pl.pallas_call(
        _finish_kernel,
        grid=(bx, s // tm, D_FF // tf),
        in_specs=[tok(o.shape[-1]), tok(D_MODEL),
                  pl.BlockSpec((None, 1, N_MOD * D_MODEL), row), _full(g.shape), _full(wo.shape),
                  pl.BlockSpec((None, D_MODEL, tf), lambda b, j, f: (layer, 0, f)),
                  pl.BlockSpec((None, tf, D_MODEL), lambda b, j, f: (layer, f, 0))],
        out_specs=tok(D_MODEL),
        out_shape=jax.ShapeDtypeStruct(x.shape, F32),
        scratch_shapes=[pltpu.VMEM((tm, D_MODEL), F32), pltpu.VMEM((tm, D_MODEL), BF16)],
        compiler_params=pltpu.CompilerParams(dimension_semantics=("arbitrary",) * 3,
                                             vmem_limit_bytes=MLP_VMEM_LIMIT),
        name="out_proj_mlp",
    )(o, x, mods_l, g, wo, w1, w2)


def _rope_tables(s, dim):
    quarter = dim // 4
    t = jnp.arange(s)
    pos = jnp.stack([t // GRID_W, t % GRID_W], axis=-1).astype(F32)
    inv = ROPE_THETA ** (-jnp.arange(quarter, dtype=F32) / quarter)
    ang = pos[:, :, None] * inv
    cos = jnp.broadcast_to(jnp.cos(ang)[:, :, None, :], (s, 2, 2, quarter)).reshape(s, dim)
    sign = jnp.array([-1.0, 1.0], F32)[None, None, :, None]
    sin = (jnp.sin(ang)[:, :, None, :] * sign).reshape(s, dim)
    reps = LANES // dim
    return jnp.tile(cos, (1, reps)), jnp.tile(sin, (1, reps))


def _dup_heads(w, heads, dim):
    lead = w.shape[:-1]
    w = w.reshape(lead + (heads, 1, dim))
    return jnp.broadcast_to(w, lead + (heads, 2, dim)).reshape(lead + (heads * 2 * dim,))


def _nat_dense_bias(rpb, rows):
    heads = rpb.shape[0]
    c = np.arange(GRID_W)
    c0 = np.clip(c - NAT_WIN_C // 2, 0, GRID_W - NAT_WIN_C)
    in_c = (c[None, :] >= c0[:, None]) & (c[None, :] < c0[:, None] + NAT_WIN_C)
    dc = c[None, :] - c[:, None] + NAT_WIN_C - 1
    onehot = (dc[None] == np.arange(2 * NAT_WIN_C - 1)[:, None, None]) & in_c[None]
    toe = jnp.einsum("had,dck->hack", rpb * LOG2E, jnp.asarray(onehot, F32),
                     precision=lax.Precision.HIGHEST)
    toe = jnp.where(jnp.asarray(in_c)[None, None], toe, NEG_INF)
    pad = jnp.full((heads, 1, GRID_W, GRID_W), NEG_INF, F32)
    ext = jnp.concatenate([pad, toe, pad], axis=1)
    pairs = jnp.concatenate([ext[:, :-1], ext[:, 1:]], axis=-1)
    tiles = rows // NAT_TILE_ROWS
    tq, slab = NAT_TILE_ROWS * GRID_W, NAT_KEY_ROWS * GRID_W
    n_off = 2 * NAT_WIN_R
    return pl.pallas_call(
        functools.partial(_nat_bias_kernel, rows),
        grid=(heads,),
        in_specs=[pl.BlockSpec((None, n_off, GRID_W, LANES), lambda h: (h, 0, 0, 0))],
        out_specs=pl.BlockSpec((None, tiles, tq, slab), lambda h: (h, 0, 0, 0)),
        out_shape=jax.ShapeDtypeStruct((heads, tiles, tq, slab), F32),
        compiler_params=_params("arbitrary"),
        name="nat_bias_expand",
    )(pairs)


def _nat_bias_kernel(rows, pairs_ref, o_ref):
    low = lax.broadcasted_iota(jnp.int32, (GRID_W, LANES), 1) < GRID_W
    masked = jnp.full((GRID_W, LANES), NEG_INF, F32)
    for r in range(rows):
        tile, i = divmod(r, NAT_TILE_ROWS)
        r0 = _nat_row0(r, rows)
        ws = _nat_tile_start(tile, rows)
        for jb in range(NAT_KEY_ROWS // 2):
            kr = ws + 2 * jb
            ok_lo, ok_hi = r0 <= kr < r0 + NAT_WIN_R, r0 <= kr + 1 < r0 + NAT_WIN_R
            blk = masked
            if ok_lo or ok_hi:
                blk = pairs_ref[kr - r + NAT_WIN_R]
                if not ok_lo:
                    blk = jnp.where(low, NEG_INF, blk)
                elif not ok_hi:
                    blk = jnp.where(low, blk, NEG_INF)
            o_ref[tile, i * GRID_W:(i + 1) * GRID_W, jb * LANES:(jb + 1) * LANES] = blk


def kernel(x_prompt, x_sample, cache_l0_k, cache_l0_v, cache_l1_ckv, cache_l1_kpe, cache_l2_k, cache_l2_v, cache_l3_k, cache_l3_v, c, c_ctx, ada_w, ada_b, norm_g, mlp_w1, mlp_w2, attn_w_qkv, attn_q_norm, attn_k_norm, attn_w_o, mla_w_in, mla_q_norm, mla_kv_norm, mla_w_uq, mla_w_ukv, mla_w_o, swa_w_qkv, swa_sink, swa_w_o, nat_w_qkv, nat_rpb, nat_w_o):
    nb, seq, d = x_prompt.shape
    db, dseq, _ = x_sample.shape
    past = cache_l0_k.shape[1]
    ctx_b = nb * seq // dseq
    xp = x_prompt.reshape(ctx_b, dseq, d)
    xs = x_sample

    cond = jnp.zeros((COND_ROWS, d), F32).at[:db].set(c).at[CTX_MOD_ROW].set(c_ctx)
    mods = _modulation(cond, ada_w, ada_b).reshape(DEPTH, COND_ROWS, 1, N_MOD * d)

    row = lambda v: v.reshape(1, -1)

    def head_major(cache):
        b_, l_, h_, dh = cache.shape
        return cache.transpose(0, 2, 3, 1).reshape(b_, h_ * dh, l_)

    def token_major(state, heads):
        b_, w_, l_ = state.shape
        return state.reshape(b_, heads, w_ // heads, l_).transpose(0, 3, 1, 2)

    def finish(o, x, layer, wo, latent):
        return _finish_layer(o, x, mods[layer], norm_g[layer], wo, mlp_w1, mlp_w2, layer, latent)

    g = norm_g[0]
    w = attn_w_qkv.astype(BF16)
    wo = attn_w_o.astype(BF16)
    consts = [w, row(attn_q_norm), row(attn_k_norm)]
    kvw = ATTN_KV_HEADS * ATTN_HEAD_DIM
    o, l0_k, l0_v = _run_proj(
        _proj_a_kernel, False, xp, mods[0], g, consts, [], [(kvw, F32), (kvw, F32)], "proj_attn_gqa_context",
        seq, attend=functools.partial(_attn_a_kernel, False), qkv=[(d, BF16), (kvw, BF16), (kvw, BF16)])
    xp = finish(o, xp, 0, wo, False)
    perm = np.arange(ATTN_HEAD_DIM).reshape(2, 2, ATTN_HEAD_DIM // 4).transpose(1, 0, 2).reshape(-1)
    qk_heads = ATTN_HEADS + ATTN_KV_HEADS
    cols = (np.arange(qk_heads)[:, None] * ATTN_HEAD_DIM + perm[None, :]).reshape(-1)
    w_lat = jnp.concatenate([attn_w_qkv[:, cols], attn_w_qkv[:, qk_heads * ATTN_HEAD_DIM:]], axis=1).astype(BF16)
    consts_lat = [w_lat, row(attn_q_norm[perm]), row(attn_k_norm[perm])]
    tables = [t[:, perm] for t in _rope_tables(dseq, ATTN_HEAD_DIM)]
    q, k, v = _run_proj(_proj_a_kernel, True, xs, mods[0], g, consts_lat, tables,
                        [(d, BF16), (kvw, BF16), (kvw, BF16)], "proj_gqa_latent")
    o = _attention_a(q, k, v, cache_l0_k[..., perm].reshape(db, past, kvw).astype(BF16),
                     cache_l0_v.reshape(db, past, kvw).astype(BF16))
    xs = finish(o, xs, 0, wo, True)
    new_l0 = (l0_k.reshape(nb, seq, ATTN_KV_HEADS, ATTN_HEAD_DIM),
              l0_v.reshape(nb, seq, ATTN_KV_HEADS, ATTN_HEAD_DIM))

    g = norm_g[1]
    nq, nkv = MLA_Q_LORA, MLA_KV_LORA
    w_in = jnp.concatenate([mla_w_in[:, :nq + nkv]] + [mla_w_in[:, nq + nkv:]] * (LANES // MLA_ROPE),
                           axis=1).astype(BF16)
    wuq = mla_w_uq.reshape(nq, MLA_HEADS, MLA_NOPE + MLA_ROPE)
    wuq = jnp.concatenate([wuq[:, :, :MLA_NOPE].reshape(nq, -1), wuq[:, :, MLA_NOPE:].reshape(nq, -1)],
                          axis=1).astype(BF16)
    wukv = mla_w_ukv.reshape(nkv, MLA_HEADS, MLA_NOPE + MLA_V_DIM)
    wukv = jnp.concatenate([wukv[:, :, :MLA_NOPE].reshape(nkv, -1), wukv[:, :, MLA_NOPE:].reshape(nkv, -1)],
                           axis=1).astype(BF16)
    wo = mla_w_o.astype(BF16)
    consts = [w_in, row(mla_q_norm), row(mla_kv_norm), wuq, wukv]
    hw = MLA_HEADS * MLA_NOPE
    rw = MLA_HEADS * MLA_ROPE
    outs = [(hw, BF16), (rw, BF16), (hw, BF16), (hw, BF16), (LANES, BF16)]
    o, l1_ckv, l1_kpe = _run_proj(
        _proj_b_kernel, False, xp, mods[1], g, consts, [], [(nkv, F32), (MLA_ROPE, F32, "T")],
        "proj_attn_mla_context", seq, qkv=outs,
        attend=lambda qn, qr, kn, v, kp, o_ref: _attn_b_kernel(False, qn, qr, kn, kp, v, o_ref))
    xp = finish(o, xp, 1, wo, False)
    tables = list(_rope_tables(dseq, MLA_ROPE))
    qn, qr, kn, v, kp = _run_proj(_proj_b_kernel, True, xs, mods[1], g, consts, tables, outs,
                                  "proj_mla_latent")
    knc, vc = _mla_expand(cache_l1_ckv, wukv)
    kpc = jnp.tile(cache_l1_kpe, (1, 1, LANES // MLA_ROPE)).astype(BF16)
    o = _attention_b(qn, qr, kn, kp, v, knc, kpc, vc)
    xs = finish(o, xs, 1, wo, True)
    new_l1 = (l1_ckv.reshape(nb, seq, nkv), l1_kpe.transpose(0, 2, 1))

    g = norm_g[2]
    qw = SWA_HEADS * SWA_HEAD_DIM
    kw = SWA_KV_HEADS * SWA_HEAD_DIM
    wq, wk, wv = swa_w_qkv[:, :qw], swa_w_qkv[:, qw:qw + kw], swa_w_qkv[:, qw + kw:]
    wkd = _dup_heads(wk, SWA_KV_HEADS, SWA_HEAD_DIM)
    wvd = _dup_heads(wv, SWA_KV_HEADS, SWA_HEAD_DIM)
    w_ctx = jnp.concatenate([wq, wk, wv, wkd, wvd], axis=1).astype(BF16)
    w_lat = jnp.concatenate([wq, wkd, wvd], axis=1).astype(BF16)
    wo = swa_w_o.astype(BF16)
    o, l2_k, l2_v = _run_proj(
        _proj_c_kernel, False, xp, mods[2], g, [w_ctx], [], [(kw, F32, "T"), (kw, F32, "T")],
        "proj_attn_swa_context", seq, attend=functools.partial(_attn_c_kernel, False),
        attend_inputs=[_sink_logit_block(swa_sink, seq)], qkv=[(qw, BF16), (2 * kw, BF16), (2 * kw, BF16)])
    xp = finish(o, xp, 2, wo, False)
    tables = list(_rope_tables(dseq, SWA_HEAD_DIM))
    q, kd, vd = _run_proj(_proj_c_kernel, True, xs, mods[2], g, [w_lat], tables,
                          [(qw, BF16), (2 * kw, BF16), (2 * kw, BF16)], "proj_swa_latent")
    o = _attention_c(swa_sink, q, kd, vd, head_major(cache_l2_k), head_major(cache_l2_v))
    xs = finish(o, xs, 2, wo, True)
    new_l2 = (token_major(l2_k, SWA_KV_HEADS), token_major(l2_v, SWA_KV_HEADS))

    g = norm_g[3]
    hw = NAT_HEADS * NAT_HEAD_DIM
    w = nat_w_qkv.astype(BF16)
    wo = nat_w_o.astype(BF16)
    o, l3_k, l3_v = _run_proj(
        _proj_d_kernel, False, xp, mods[3], g, [w], [], [(hw, F32, "T"), (hw, F32, "T")],
        "proj_attn_nat_context", seq, attend=_attn_d_ctx_kernel, qkv=[(hw, BF16)] * 3)
    xp = finish(o, xp, 3, wo, False)
    q, k, v = _run_proj(_proj_d_kernel, True, xs, mods[3], g, [w], [],
                        [(hw, BF16), (hw, BF16), (hw, BF16)], "proj_nat_latent")
    o = _attention_d_lat(q, k, v, head_major(cache_l3_k), head_major(cache_l3_v),
                         _nat_dense_bias(nat_rpb, dseq // GRID_W))
    xs = finish(o, xs, 3, wo, True)
    new_l3 = (token_major(l3_k, NAT_HEADS), token_major(l3_v, NAT_HEADS))

    return (xp.reshape(nb, seq, d), xs) + new_l0 + new_l1 + new_l2 + new_l3
```

```python
import functools

import numpy as np

import jax
import jax.numpy as jnp
from jax import lax
from jax.experimental import pallas as pl
from jax.experimental.pallas import tpu as pltpu

F32 = jnp.float32
BF16 = jnp.bfloat16

D_MODEL = 1024
DEPTH = 4
N_MOD = 6
D_FF = 4 * D_MODEL
GRID_W = 64
ROPE_THETA = 10000.0
NORM_EPS = 1e-6
NEG_INF = -1e30
LOG2E = 1.4426950408889634

ATTN_HEADS, ATTN_KV_HEADS, ATTN_HEAD_DIM = 8, 2, 128
MLA_HEADS, MLA_Q_LORA, MLA_KV_LORA = 16, 384, 256
MLA_NOPE, MLA_ROPE, MLA_V_DIM = 64, 32, 64
MLA_SCALE = (MLA_NOPE + MLA_ROPE) ** -0.5
SWA_HEADS, SWA_KV_HEADS, SWA_HEAD_DIM, SWA_WINDOW = 16, 4, 64, 128
NAT_HEADS, NAT_HEAD_DIM, NAT_WIN_R, NAT_WIN_C = 16, 64, 8, 16

LANES = 128
COND_ROWS = 16
CTX_MOD_ROW = 8
VMEM_LIMIT = 48 * 1024 * 1024

MOD_COL_TILE = 1536
TOK_TILE = 512
TOK_CHAIN = 256
MLP_TOK_TILE = 1024
MLP_FF_TILE = 1024
MLP_VMEM_LIMIT = 56 * 1024 * 1024
Q_TILE = 256
LATENT_Q_ROWS = 512
HEADS_PER_STEP = 16
NAT_TILE_ROWS = Q_TILE // GRID_W
NAT_KEY_ROWS = NAT_WIN_R + NAT_TILE_ROWS


def _params(*sem):
    return pltpu.CompilerParams(dimension_semantics=sem, vmem_limit_bytes=VMEM_LIMIT)


def _full(shape):
    nd = len(shape)
    return pl.BlockSpec(shape, lambda *_: (0,) * nd)


def _rms(x, g):
    return x * lax.rsqrt(jnp.mean(x * x, axis=-1, keepdims=True) + NORM_EPS) * g


def _mod(m, i):
    return m[:, i * D_MODEL:(i + 1) * D_MODEL]


def _premod(x, g, m, sub):
    return _rms(x, g * (1.0 + _mod(m, 3 * sub + 1))) + _mod(m, 3 * sub)


def _rope(x, cos, sin_signed, quarter):
    n = x.shape[-1]
    lane = lax.broadcasted_iota(jnp.int32, x.shape, 1)
    first = ((lane // quarter) % 2) == 0
    partner = jnp.where(first, pltpu.roll(x, n - quarter, 1), pltpu.roll(x, quarter, 1))
    return x * cos + partner * sin_signed


def _dot(a, b):
    return jnp.dot(a, b, preferred_element_type=F32)


def _dot_nt(a, b):
    return lax.dot_general(a, b, (((1,), (1,)), ((), ())), preferred_element_type=F32)


def _with_ones(v):
    return jnp.concatenate([v, jnp.ones_like(v)], axis=1)


def _attend(q, segs, sink=None):
    logits = []
    for seg in segs:
        k, bias, transposed = seg[0], seg[2], len(seg) > 3 and seg[3]
        s = _dot(q, k) if transposed else _dot_nt(q, k)
        logits.append(s if bias is None else s + bias)
    m = logits[0].max(axis=-1, keepdims=True)
    for s in logits[1:]:
        m = jnp.maximum(m, s.max(axis=-1, keepdims=True))
    if sink is not None:
        m = jnp.maximum(m, sink)
    acc = None
    for s, seg in zip(logits, segs):
        p = jnp.exp2(s - m).astype(BF16)
        pv = _dot_nt(p, seg[1]) if len(seg) > 3 and seg[3] else _dot(p, seg[1])
        acc = pv if acc is None else acc + pv
    den = acc[:, LANES:LANES + 1]
    if sink is not None:
        den = den + jnp.exp2(sink - m)
    return acc[:, :LANES] / den


def _attend_group(qs, ks, v1s, sink_logits=None):
    tq = qs[0].shape[0]
    s = jnp.concatenate([_dot_nt(q, k) for q, k in zip(qs, ks)], axis=0)
    if sink_logits is not None:
        s = jnp.concatenate([s, sink_logits], axis=1)
        zeros = jnp.zeros((LANES, LANES), BF16)
        tail = jnp.concatenate([zeros, jnp.ones_like(zeros)], axis=1)
        v1s = [jnp.concatenate([v1, tail], axis=0) for v1 in v1s]
    p = jnp.exp2(s - s.max(axis=-1, keepdims=True)).astype(BF16)
    acc = jnp.concatenate([_dot(p[i * tq:(i + 1) * tq], v1) for i, v1 in enumerate(v1s)], axis=0)
    return acc[:, :LANES] / acc[:, LANES:LANES + 1]


def _split_pair(q):
    low = lax.broadcasted_iota(jnp.int32, q.shape, 1) < (LANES // 2)
    zero = jnp.zeros_like(q)
    return [jnp.where(low, q, zero), jnp.where(low, zero, q)]


def _merge_pair(o0, o1):
    low = lax.broadcasted_iota(jnp.int32, o0.shape, 1) < (LANES // 2)
    return jnp.where(low, o0, o1).astype(BF16)


def _attend_pair(q, segs, sinks=None):
    outs = []
    for j, qj in enumerate(_split_pair(q)):
        segs_j = [(s[0], s[1], None if s[2] is None else s[2][j]) + tuple(s[3:]) for s in segs]
        outs.append(_attend(qj, segs_j, None if sinks is None else sinks[j]))
    return _merge_pair(*outs)


def _mods_kernel(cond_ref, w_ref, b_ref, o_ref):
    cnd = cond_ref[...]
    act = cnd * jax.nn.sigmoid(cnd)
    o_ref[...] = _dot(act.astype(BF16), w_ref[...].astype(BF16)) + b_ref[...]


def _modulation(cond, ada_w, ada_b):
    tn = MOD_COL_TILE
    n = N_MOD * D_MODEL
    return pl.pallas_call(
        _mods_kernel,
        grid=(DEPTH, n // tn),
        in_specs=[
            _full((COND_ROWS, D_MODEL)),
            pl.BlockSpec((None, D_MODEL, tn), lambda l, j: (l, 0, j)),
            pl.BlockSpec((None, 1, tn), lambda l, j: (l, 0, j)),
        ],
        out_specs=pl.BlockSpec((None, COND_ROWS, tn), lambda l, j: (l, 0, j)),
        out_shape=jax.ShapeDtypeStruct((DEPTH, COND_ROWS, n), F32),
        compiler_params=_params("arbitrary", "arbitrary"),
        name="adaln_mods",
    )(cond, ada_w, ada_b.reshape(DEPTH, 1, n))


def _proj_a_kernel(latent, x_ref, m_ref, g_ref, w_ref, qn_ref, kn_ref, *rest):
    if latent:
        cos_ref, sin_ref, q_ref, k_ref, v_ref = rest
    else:
        q_ref, k_ref, v_ref, ks_ref, vs_ref = rest
    hd = ATTN_HEAD_DIM
    n_qk = ATTN_HEADS + ATTN_KV_HEADS
    h = _premod(x_ref[...], g_ref[0:1, :], m_ref[...], 0).astype(BF16)
    w = w_ref[...].astype(BF16)
    for c in range(x_ref.shape[0] // TOK_CHAIN):
        rows = slice(c * TOK_CHAIN, (c + 1) * TOK_CHAIN)
        z = _dot(h[rows], w)
        ys = [_rms(z[:, i * hd:(i + 1) * hd], qn_ref[...] if i < ATTN_HEADS else kn_ref[...]) for i in range(n_qk)]
        if not latent:
            for j in range(ATTN_KV_HEADS):
                ks_ref[rows, j * hd:(j + 1) * hd] = ys[ATTN_HEADS + j]
        else:
            ys = [y * cos_ref[rows, :] + pltpu.roll(y, hd // 2, 1) * sin_ref[rows, :] for y in ys]
        for i in range(ATTN_HEADS):
            q_ref[rows, i * hd:(i + 1) * hd] = (ys[i] * (hd ** -0.5 * LOG2E)).astype(BF16)
        for j in range(ATTN_KV_HEADS):
            k_ref[rows, j * hd:(j + 1) * hd] = ys[ATTN_HEADS + j].astype(BF16)
        v = z[:, n_qk * hd:]
        v_ref[rows, :] = v.astype(BF16)
        if not latent:
            vs_ref[rows, :] = v


def _proj_b_kernel(latent, x_ref, m_ref, g_ref, win_ref, qn_ref, kvn_ref, wuq_ref, wukv_ref, *rest):
    if latent:
        cos_ref, sin_ref, qn_o, qr_o, kn_o, v_o, kp_o = rest
    else:
        qn_o, qr_o, kn_o, v_o, kp_o, ckv_s, kpe_s = rest
    nq, nkv = MLA_Q_LORA, MLA_KV_LORA
    nope_w = MLA_HEADS * MLA_NOPE
    scale = MLA_SCALE * LOG2E
    h = _premod(x_ref[...], g_ref[0:1, :], m_ref[...], 0)
    z = _dot(h.astype(BF16), win_ref[...])
    cq = _rms(z[:, :nq], qn_ref[...])
    ckv = _rms(z[:, nq:nq + nkv], kvn_ref[...])
    kp = z[:, nq + nkv:]
    q = _dot(cq.astype(BF16), wuq_ref[...])
    kv = _dot(ckv.astype(BF16), wukv_ref[...])
    if not latent:
        ckv_s[...] = ckv
        _store_transposed(kpe_s, kp, MLA_ROPE)
    qn_o[...] = (q[:, :nope_w] * scale).astype(BF16)
    for i in range(MLA_HEADS * MLA_ROPE // LANES):
        qr = q[:, nope_w + i * LANES:nope_w + (i + 1) * LANES]
        if latent:
            qr = _rope(qr, cos_ref[...], sin_ref[...], MLA_ROPE // 4)
        qr_o[:, i * LANES:(i + 1) * LANES] = (qr * scale).astype(BF16)
    if latent:
        kp = _rope(kp, cos_ref[...], sin_ref[...], MLA_ROPE // 4)
    kp_o[...] = kp.astype(BF16)
    kn_o[...] = kv[:, :nope_w].astype(BF16)
    v_o[...] = kv[:, nope_w:].astype(BF16)


def _proj_c_kernel(latent, x_ref, m_ref, g_ref, w_ref, *rest):
    if latent:
        cos_ref, sin_ref, q_o, kd_o, vd_o = rest
    else:
        q_o, kd_o, vd_o, ks_o, vs_o = rest
    qw = SWA_HEADS * SWA_HEAD_DIM
    kw = SWA_KV_HEADS * SWA_HEAD_DIM
    h = _premod(x_ref[...], g_ref[0:1, :], m_ref[...], 0)
    z = _dot(h.astype(BF16), w_ref[...])
    off = qw
    if not latent:
        _store_transposed(ks_o, z[:, qw:qw + kw], kw)
        _store_transposed(vs_o, z[:, qw + kw:qw + 2 * kw], kw)
        off = qw + 2 * kw
    scale = SWA_HEAD_DIM ** -0.5 * LOG2E
    for i in range(qw // LANES):
        y = z[:, i * LANES:(i + 1) * LANES]
        if latent:
            y = _rope(y, cos_ref[...], sin_ref[...], SWA_HEAD_DIM // 4)
        q_o[:, i * LANES:(i + 1) * LANES] = (y * scale).astype(BF16)
    for i in range(2 * kw // LANES):
        y = z[:, off + i * LANES:off + (i + 1) * LANES]
        if latent:
            y = _rope(y, cos_ref[...], sin_ref[...], SWA_HEAD_DIM // 4)
        kd_o[:, i * LANES:(i + 1) * LANES] = y.astype(BF16)
    vd_o[...] = z[:, off + 2 * kw:].astype(BF16)


def _proj_d_kernel(latent, x_ref, m_ref, g_ref, w_ref, *rest):
    if latent:
        q_o, k_o, v_o = rest
    else:
        q_o, k_o, v_o, ks_o, vs_o = rest
    hw = NAT_HEADS * NAT_HEAD_DIM
    h = _premod(x_ref[...], g_ref[0:1, :], m_ref[...], 0)
    z = _dot(h.astype(BF16), w_ref[...].astype(BF16))
    q_o[...] = (z[:, :hw] * (NAT_HEAD_DIM ** -0.5 * LOG2E)).astype(BF16)
    k_o[...] = z[:, hw:2 * hw].astype(BF16)
    v_o[...] = z[:, 2 * hw:].astype(BF16)
    if not latent:
        _store_transposed(ks_o, z[:, hw:2 * hw], hw)
        _store_transposed(vs_o, z[:, 2 * hw:], hw)


def _store_transposed(ref, val, width):
    n, _, seq = ref.shape
    for i in range(n):
        ref[i] = val[i * seq:(i + 1) * seq, :].T[:width, :]


def _run_proj(kernel, latent, x, mods_l, g, consts, tables, outs, name, state_seq=None,
              attend=None, attend_inputs=(), qkv=()):
    bx, s, _ = x.shape
    tm = TOK_TILE
    row = (lambda b, j: (b, 0, 0)) if latent else (lambda b, j: (CTX_MOD_ROW, 0, 0))
    in_specs = [
        pl.BlockSpec((None, tm, D_MODEL), lambda b, j: (b, j, 0)),
        pl.BlockSpec((None, 1, N_MOD * D_MODEL), row),
        _full(g.shape),
    ] + [_full(c.shape) for c in consts]
    in_specs += [pl.BlockSpec((tm, LANES), lambda b, j: (j, 0)) for _ in tables]
    in_specs += [_full(a.shape) for a in attend_inputs]
    body = functools.partial(kernel, latent)
    if attend is not None:
        n_proj_in = 3 + len(consts) + len(tables)
        n_att_in = len(attend_inputs)
        proj_body = body
        outs = [(D_MODEL, BF16)] + list(outs)

        def body(*refs):
            ins, att_ins = refs[:n_proj_in], refs[n_proj_in:n_proj_in + n_att_in]
            o_ref = refs[n_proj_in + n_att_in]
            states, scratch = refs[n_proj_in + n_att_in + 1:-len(qkv)], refs[-len(qkv):]
            proj_body(*ins, *scratch, *states)
            for i in range(tm // state_seq):
                rows = slice(i * state_seq, (i + 1) * state_seq)
                attend(*att_ins, *[r.at[rows] for r in scratch], o_ref.at[rows])

    out_specs, out_shape = [], []
    for o in outs:
        if len(o) == 2:
            out_specs.append(pl.BlockSpec((None, tm, o[0]), lambda b, j: (b, j, 0)))
            out_shape.append(jax.ShapeDtypeStruct((bx, s, o[0]), o[1]))
        else:
            per_tile = tm // state_seq
            out_specs.append(pl.BlockSpec((per_tile, o[0], state_seq),
                                          lambda b, j: (b * (s // tm) + j, 0, 0)))
            out_shape.append(jax.ShapeDtypeStruct((bx * s // state_seq, o[0], state_seq), o[1]))
    return pl.pallas_call(
        body,
        grid=(bx, s // tm),
        in_specs=in_specs,
        out_specs=out_specs,
        out_shape=out_shape,
        scratch_shapes=[pltpu.VMEM((tm, w), dt) for w, dt in qkv],
        compiler_params=_params("arbitrary", "arbitrary"),
        name=name,
    )(x, mods_l, g, *consts, *tables, *attend_inputs)


def _attn_a_kernel(has_ctx, q_ref, k_ref, v_ref, *rest):
    if has_ctx:
        kc_ref, vc_ref, o_ref = rest
    else:
        (o_ref,) = rest
    group = ATTN_HEADS // ATTN_KV_HEADS
    head_cols = [slice(h * LANES, (h + 1) * LANES) for h in range(ATTN_HEADS)]
    kv_cols = [slice(j * LANES, (j + 1) * LANES) for j in range(ATTN_KV_HEADS)]
    if not has_ctx:
        tq = q_ref.shape[0]
        qs = [jnp.concatenate([q_ref[:, c] for c in head_cols[j * group:(j + 1) * group]], axis=0)
              for j in range(ATTN_KV_HEADS)]
        out = _attend_group(qs, [k_ref[:, c] for c in kv_cols], [_with_ones(v_ref[:, c]) for c in kv_cols])
        for h, c in enumerate(head_cols):
            o_ref[:, c] = out[h * tq:(h + 1) * tq].astype(BF16)
        return
    for j, kc in enumerate(kv_cols):
        segs = [(k_ref[:, kc], _with_ones(v_ref[:, kc]), None),
                (kc_ref[:, kc], _with_ones(vc_ref[:, kc]), None)]
        for r in range(q_ref.shape[0] // Q_TILE):
            rows = slice(r * Q_TILE, (r + 1) * Q_TILE)
            for c in head_cols[j * group:(j + 1) * group]:
                o_ref[rows, c] = _attend(q_ref[rows, c], segs).astype(BF16)


def _attention_a(q, k, v, kc=None, vc=None):
    b, s, qw = q.shape
    kl, kw = k.shape[1:]
    has_ctx = kc is not None
    tq = min(LATENT_Q_ROWS if has_ctx else Q_TILE, s)
    in_specs = [
        pl.BlockSpec((None, tq, qw), lambda bi, t: (bi, t, 0)),
        pl.BlockSpec((None, kl, kw), lambda bi, t: (bi, 0, 0)),
        pl.BlockSpec((None, kl, kw), lambda bi, t: (bi, 0, 0)),
    ]
    args = [q, k, v]
    if has_ctx:
        cl = kc.shape[1]
        in_specs += [pl.BlockSpec((None, cl, kw), lambda bi, t: (bi, 0, 0))] * 2
        args += [kc, vc]
    return pl.pallas_call(
        functools.partial(_attn_a_kernel, has_ctx),
        grid=(b, s // tq),
        in_specs=in_specs,
        out_specs=pl.BlockSpec((None, tq, qw), lambda bi, t: (bi, t, 0)),
        out_shape=jax.ShapeDtypeStruct(q.shape, BF16),
        compiler_params=_params("arbitrary", "arbitrary"),
        name="attn_gqa_latent" if has_ctx else "attn_gqa_context",
    )(*args)


def _attn_b_kernel(has_ctx, qn_ref, qr_ref, kn_ref, kp_ref, v_ref, *rest):
    if has_ctx:
        knc_ref, kpc_ref, vc_ref, o_ref = rest
    else:
        (o_ref,) = rest
    tq = min(Q_TILE, qn_ref.shape[0])
    lane = lax.broadcasted_iota(jnp.int32, (tq, LANES), 1)
    low = lane < (LANES // 2)
    quarter = lane // MLA_ROPE
    zero = jnp.zeros((tq, LANES), BF16)
    kp = kp_ref[...]
    kpc = kpc_ref[...] if has_ctx else None
    pairs_per_rope = LANES // MLA_ROPE // 2
    n_pairs = HEADS_PER_STEP // 2
    pair_cols = [slice(p * LANES, (p + 1) * LANES) for p in range(n_pairs)]

    def pair_lhs(p, rows=slice(None)):
        qn = qn_ref[rows, pair_cols[p]]
        rg = p // pairs_per_rope
        qr = qr_ref[rows, rg * LANES:(rg + 1) * LANES]
        return [jnp.concatenate([jnp.where(low if j == 0 else ~low, qn, zero),
                                 jnp.where(quarter == 2 * (p % pairs_per_rope) + j, qr, zero)], axis=1)
                for j in range(2)]

    if not has_ctx:
        qs = [jnp.concatenate(pair_lhs(p), axis=0) for p in range(n_pairs)]
        out = _attend_group(qs, [jnp.concatenate([kn_ref[:, c], kp], axis=1) for c in pair_cols],
                            [_with_ones(v_ref[:, c]) for c in pair_cols])
        for p, c in enumerate(pair_cols):
            o_ref[:, c] = _merge_pair(out[2 * p * tq:(2 * p + 1) * tq], out[(2 * p + 1) * tq:(2 * p + 2) * tq])
        return
    for p, cols in enumerate(pair_cols):
        segs = [(jnp.concatenate([kn_ref[:, cols], kp], axis=1), _with_ones(v_ref[:, cols]), None),
                (jnp.concatenate([knc_ref[:, cols], kpc], axis=1), _with_ones(vc_ref[:, cols]), None)]
        for r in range(qn_ref.shape[0] // tq):
            rows = slice(r * tq, (r + 1) * tq)
            o_ref[rows, cols] = _merge_pair(*[_attend(lhs, segs) for lhs in pair_lhs(p, rows)])


def _attention_b(qn, qr, kn, kp, v, knc=None, kpc=None, vc=None):
    b, s, _ = qn.shape
    kl = kn.shape[1]
    has_ctx = knc is not None
    tq = min(LATENT_Q_ROWS if has_ctx else Q_TILE, s)
    gw = HEADS_PER_STEP * MLA_NOPE
    blk = lambda rows, w, fn: pl.BlockSpec((None, rows, w), fn)
    in_specs = [
        blk(tq, gw, lambda bi, g, t: (bi, t, g)),
        blk(tq, HEADS_PER_STEP * MLA_ROPE, lambda bi, g, t: (bi, t, g)),
        blk(kl, gw, lambda bi, g, t: (bi, 0, g)),
        blk(kl, LANES, lambda bi, g, t: (bi, 0, 0)),
        blk(kl, gw, lambda bi, g, t: (bi, 0, g)),
    ]
    args = [qn, qr, kn, kp, v]
    if has_ctx:
        cl = knc.shape[1]
        in_specs += [
            blk(cl, gw, lambda bi, g, t: (bi, 0, g)),
            blk(cl, LANES, lambda bi, g, t: (bi, 0, 0)),
            blk(cl, gw, lambda bi, g, t: (bi, 0, g)),
        ]
        args += [knc, kpc, vc]
    return pl.pallas_call(
        functools.partial(_attn_b_kernel, has_ctx),
        grid=(b, MLA_HEADS // HEADS_PER_STEP, s // tq),
        in_specs=in_specs,
        out_specs=blk(tq, gw, lambda bi, g, t: (bi, t, g)),
        out_shape=jax.ShapeDtypeStruct(qn.shape, BF16),
        compiler_params=_params("arbitrary", "arbitrary", "arbitrary"),
        name="attn_mla_latent" if has_ctx else "attn_mla_context",
    )(*args)


def _attn_c_kernel(latent, sink_ref, q_ref, kd_ref, vd_ref, *rest):
    if latent:
        kc_ref, vc_ref, o_ref = rest
    else:
        (o_ref,) = rest
    tq = min(Q_TILE, q_ref.shape[0])
    group = SWA_HEADS // SWA_KV_HEADS
    kv_per_step = HEADS_PER_STEP // group
    head0 = pl.program_id(1) * HEADS_PER_STEP
    windows = []
    if latent:
        s_len = kd_ref.shape[0]
        win = tq + 2 * SWA_WINDOW
        for r in range(q_ref.shape[0] // tq):
            t0 = pl.program_id(2) * q_ref.shape[0] + r * tq
            ws = pl.multiple_of(jnp.clip(t0 - SWA_WINDOW, 0, s_len - win), LANES)
            qpos = t0 + lax.broadcasted_iota(jnp.int32, (tq, win), 0)
            kpos = ws + lax.broadcasted_iota(jnp.int32, (tq, win), 1)
            bias = jnp.where(jnp.abs(qpos - kpos) <= SWA_WINDOW, 0.0, NEG_INF).astype(F32)
            windows.append((slice(r * tq, (r + 1) * tq), ws, bias))
    pairs_per_kv = group // 2
    pair_cols = [slice(p * LANES, (p + 1) * LANES) for p in range(HEADS_PER_STEP // 2)]
    kv_cols = [slice(j * LANES, (j + 1) * LANES) for j in range(kv_per_step)]
    if not latent:
        qs = [jnp.concatenate([h for c in pair_cols[j * pairs_per_kv:(j + 1) * pairs_per_kv]
                               for h in _split_pair(q_ref[:, c])], axis=0) for j in range(kv_per_step)]
        out = _attend_group(qs, [kd_ref[:, c] for c in kv_cols], [_with_ones(vd_ref[:, c]) for c in kv_cols],
                            sink_ref[...])
        for p, c in enumerate(pair_cols):
            o_ref[:, c] = _merge_pair(out[2 * p * tq:(2 * p + 1) * tq], out[(2 * p + 1) * tq:(2 * p + 2) * tq])
        return
    sinks = [sink_ref[head0 + h] * LOG2E for h in range(HEADS_PER_STEP)]
    hd = SWA_HEAD_DIM
    for j, kc in enumerate(kv_cols):
        kct = kc_ref[j * hd:(j + 1) * hd, :].astype(BF16)
        vct = vc_ref[j * hd:(j + 1) * hd, :].astype(BF16)
        ctx_seg = (jnp.concatenate([kct, kct], axis=0),
                   jnp.concatenate([vct, vct, jnp.ones((LANES, vct.shape[1]), BF16)], axis=0), None, True)
        for rows, ws, bias in windows:
            segs = [(kd_ref[pl.ds(ws, win), kc], _with_ones(vd_ref[pl.ds(ws, win), kc]), (bias, bias)), ctx_seg]
            for p in range(j * pairs_per_kv, (j + 1) * pairs_per_kv):
                o_ref[rows, pair_cols[p]] = _attend_pair(q_ref[rows, pair_cols[p]], segs, sinks[2 * p:2 * p + 2])


def _sink_logit_block(sink, tq):
    rows = jnp.repeat(sink * LOG2E, tq)[:, None]
    return jnp.where(jnp.arange(LANES)[None, :] == 0, rows, NEG_INF)


def _attention_c(sink, q, kd, vd, kc, vc):
    b, s, _ = q.shape
    kl = kd.shape[1]
    tq = min(LATENT_Q_ROWS, s)
    gw = HEADS_PER_STEP * SWA_HEAD_DIM
    kvw = 2 * SWA_HEAD_DIM * HEADS_PER_STEP // (SWA_HEADS // SWA_KV_HEADS)
    cw, cl = kc.shape[1] * HEADS_PER_STEP // SWA_HEADS, kc.shape[2]
    return pl.pallas_call(
        functools.partial(_attn_c_kernel, True),
        grid=(b, SWA_HEADS // HEADS_PER_STEP, s // tq),
        in_specs=[
            pl.BlockSpec(memory_space=pltpu.SMEM),
            pl.BlockSpec((None, tq, gw), lambda bi, g, t: (bi, t, g)),
            pl.BlockSpec((None, kl, kvw), lambda bi, g, t: (bi, 0, g)),
            pl.BlockSpec((None, kl, kvw), lambda bi, g, t: (bi, 0, g)),
            pl.BlockSpec((None, cw, cl), lambda bi, g, t: (bi, g, 0)),
            pl.BlockSpec((None, cw, cl), lambda bi, g, t: (bi, g, 0)),
        ],
        out_specs=pl.BlockSpec((None, tq, gw), lambda bi, g, t: (bi, t, g)),
        out_shape=jax.ShapeDtypeStruct(q.shape, BF16),
        compiler_params=_params("arbitrary", "arbitrary", "arbitrary"),
        name="attn_swa_latent",
    )(sink, q, kd, vd, kc, vc)


def _attn_d_ctx_kernel(q_ref, k_ref, v_ref, o_ref):
    tq = q_ref.shape[0]
    pair_cols = [slice(p * LANES, (p + 1) * LANES) for p in range(HEADS_PER_STEP // 2)]
    qs = [jnp.concatenate(_split_pair(q_ref[:, c]), axis=0) for c in pair_cols]
    out = _attend_group(qs, [k_ref[:, c] for c in pair_cols], [_with_ones(v_ref[:, c]) for c in pair_cols])
    for p, c in enumerate(pair_cols):
        o_ref[:, c] = _merge_pair(out[2 * p * tq:(2 * p + 1) * tq], out[(2 * p + 1) * tq:(2 * p + 2) * tq])


def _nat_row0(r, rows):
    return min(max(r - NAT_WIN_R // 2, 0), rows - NAT_WIN_R)


def _nat_tile_key_rows(tile, rows):
    first, last = tile * NAT_TILE_ROWS, (tile + 1) * NAT_TILE_ROWS - 1
    need = _nat_row0(last, rows) + NAT_WIN_R - _nat_row0(first, rows)
    return need + need % 2


def _nat_tile_start(tile, rows):
    return min(_nat_row0(tile * NAT_TILE_ROWS, rows), rows - _nat_tile_key_rows(tile, rows))


def _attn_d_lat_kernel(q_ref, k_ref, v_ref, kc_ref, vc_ref, bias_ref, o_ref):
    rows = k_ref.shape[0] // GRID_W
    tile = pl.program_id(0)
    ones = jnp.ones((LANES, kc_ref.shape[1]), BF16)

    def body(key_rows, start_row):
        slab = key_rows * GRID_W
        start = pl.multiple_of(start_row * GRID_W, GRID_W)
        for p in range(HEADS_PER_STEP // 2):
            cols = slice(p * LANES, (p + 1) * LANES)
            segs = [(k_ref[pl.ds(start, slab), cols], _with_ones(v_ref[pl.ds(start, slab), cols]),
                     (bias_ref[2 * p, :, :slab], bias_ref[2 * p + 1, :, :slab])),
                    (kc_ref[cols, :].astype(BF16),
                     jnp.concatenate([vc_ref[cols, :].astype(BF16), ones], axis=0), None, True)]
            o_ref[:, cols] = _attend_pair(q_ref[:, cols], segs)

    tiles = range(rows // NAT_TILE_ROWS)
    for key_rows in sorted({_nat_tile_key_rows(t, rows) for t in tiles}):
        members = [t for t in tiles if _nat_tile_key_rows(t, rows) == key_rows]
        cond = functools.reduce(jnp.logical_or, [tile == t for t in members])
        start_row = functools.reduce(lambda acc, t: jnp.where(tile == t, _nat_tile_start(t, rows), acc),
                                     members, jnp.int32(0))
        pl.when(cond)(functools.partial(body, key_rows, start_row))


def _attention_d_lat(q, k, v, kc, vc, bias):
    b, s, _ = q.shape
    cl = kc.shape[2]
    tq = NAT_TILE_ROWS * GRID_W
    slab = NAT_KEY_ROWS * GRID_W
    gw = HEADS_PER_STEP * NAT_HEAD_DIM
    return pl.pallas_call(
        _attn_d_lat_kernel,
        grid=(s // tq, NAT_HEADS // HEADS_PER_STEP, b),
        in_specs=[
            pl.BlockSpec((None, tq, gw), lambda t, g, bi: (bi, t, g)),
            pl.BlockSpec((None, s, gw), lambda t, g, bi: (bi, 0, g)),
            pl.BlockSpec((None, s, gw), lambda t, g, bi: (bi, 0, g)),
            pl.BlockSpec((None, gw, cl), lambda t, g, bi: (bi, g, 0)),
            pl.BlockSpec((None, gw, cl), lambda t, g, bi: (bi, g, 0)),
            pl.BlockSpec((HEADS_PER_STEP, None, tq, slab), lambda t, g, bi: (g, t, 0, 0)),
        ],
        out_specs=pl.BlockSpec((None, tq, gw), lambda t, g, bi: (bi, t, g)),
        out_shape=jax.ShapeDtypeStruct(q.shape, BF16),
        compiler_params=_params("arbitrary", "arbitrary", "arbitrary"),
        name="attn_nat_latent",
    )(q, k, v, kc, vc, bias)


def _mla_expand_kernel(c_ref, w_ref, kn_ref, v_ref):
    kv = _dot(c_ref[...].astype(BF16), w_ref[...])
    half = kv.shape[1] // 2
    kn_ref[...] = kv[:, :half].astype(BF16)
    v_ref[...] = kv[:, half:].astype(BF16)


def _mla_expand(ckv, w_ukv):
    b, l, c = ckv.shape
    n = w_ukv.shape[1] // 2
    out = pl.BlockSpec((None, l, n), lambda bi: (bi, 0, 0))
    return pl.pallas_call(
        _mla_expand_kernel,
        grid=(b,),
        in_specs=[pl.BlockSpec((None, l, c), lambda bi: (bi, 0, 0)), _full(w_ukv.shape)],
        out_specs=[out, out],
        out_shape=[jax.ShapeDtypeStruct((b, l, n), BF16)] * 2,
        compiler_params=_params("arbitrary"),
        name="mla_expand_cache",
    )(ckv, w_ukv)


def _finish_kernel(o_ref, x_ref, m_ref, g_ref, wo_ref, w1_ref, w2_ref, y_ref, x1_ref, h2_ref):
    f = pl.program_id(2)
    last = pl.num_programs(2) - 1
    chains = [slice(c * TOK_CHAIN, (c + 1) * TOK_CHAIN) for c in range(y_ref.shape[0] // TOK_CHAIN)]

    def mlp_part(h, w1, w2):
        u = jnp.square(jnp.maximum(_dot(h, w1), 0.0)).astype(BF16)
        return _dot(u, w2)

    @pl.when(f == 0)
    def _():
        m = m_ref[...]
        for rows in chains:
            a = _dot(o_ref[rows, :], wo_ref[...])
            x1 = x_ref[rows, :] + _rms(a, _mod(m, 2) * g_ref[1:2, :])
            x1_ref[rows, :] = x1
            h2_ref[rows, :] = _premod(x1, g_ref[2:3, :], m, 1).astype(BF16)
        y_ref[...] = mlp_part(h2_ref[...], w1_ref[...].astype(BF16), w2_ref[...].astype(BF16))

    @pl.when((f > 0) & (f < last))
    def _():
        y_ref[...] += mlp_part(h2_ref[...], w1_ref[...].astype(BF16), w2_ref[...].astype(BF16))

    @pl.when(f == last)
    def _():
        gain = _mod(m_ref[...], 5) * g_ref[3:4, :]
        w1, w2 = w1_ref[...].astype(BF16), w2_ref[...].astype(BF16)
        for rows in chains:
            y = y_ref[rows, :] + mlp_part(h2_ref[rows, :], w1, w2)
            y_ref[rows, :] = x1_ref[rows, :] + _rms(y, gain)


def _finish_layer(o, x, mods_l, g, wo, w1, w2, layer, latent):
    bx, s, _ = x.shape
    tm, tf = MLP_TOK_TILE, MLP_FF_TILE
    row = (lambda b, j, f: (b, 0, 0)) if latent else (lambda b, j, f: (CTX_MOD_ROW, 0, 0))
    tok = lambda w: pl.BlockSpec((None, tm, w), lambda b, j, f: (b, j, 0))
    return pl.pallas_call(
        _finish_kernel,
        grid=(bx, s // tm, D_FF // tf),
        in_specs=[tok(o.shape[-1]), tok(D_MODEL),
                  pl.BlockSpec((None, 1, N_MOD * D_MODEL), row), _full(g.shape), _full(wo.shape),
                  pl.BlockSpec((None, D_MODEL, tf), lambda b, j, f: (layer, 0, f)),
                  pl.BlockSpec((None, tf, D_MODEL), lambda b, j, f: (layer, f, 0))],
        out_specs=tok(D_MODEL),
        out_shape=jax.ShapeDtypeStruct(x.shape, F32),
        scratch_shapes=[pltpu.VMEM((tm, D_MODEL), F32), pltpu.VMEM((tm, D_MODEL), BF16)],
        compiler_params=pltpu.CompilerParams(dimension_semantics=("arbitrary",) * 3,
                                             vmem_limit_bytes=MLP_VMEM_LIMIT),
        name="out_proj_mlp",
    )(o, x, mods_l, g, wo, w1, w2)


def _rope_tables(s, dim):
    quarter = dim // 4
    t = jnp.arange(s)
    pos = jnp.stack([t // GRID_W, t % GRID_W], axis=-1).astype(F32)
    inv = ROPE_THETA ** (-jnp.arange(quarter, dtype=F32) / quarter)
    ang = pos[:, :, None] * inv
    cos = jnp.broadcast_to(jnp.cos(ang)[:, :, None, :], (s, 2, 2, quarter)).reshape(s, dim)
    sign = jnp.array([-1.0, 1.0], F32)[None, None, :, None]
    sin = (jnp.sin(ang)[:, :, None, :] * sign).reshape(s, dim)
    reps = LANES // dim
    return jnp.tile(cos, (1, reps)), jnp.tile(sin, (1, reps))


def _dup_heads(w, heads, dim):
    lead = w.shape[:-1]
    w = w.reshape(lead + (heads, 1, dim))
    return jnp.broadcast_to(w, lead + (heads, 2, dim)).reshape(lead + (heads * 2 * dim,))


def _nat_dense_bias(rpb, rows):
    heads = rpb.shape[0]
    c = np.arange(GRID_W)
    c0 = np.clip(c - NAT_WIN_C // 2, 0, GRID_W - NAT_WIN_C)
    in_c = (c[None, :] >= c0[:, None]) & (c[None, :] < c0[:, None] + NAT_WIN_C)
    dc = c[None, :] - c[:, None] + NAT_WIN_C - 1
    onehot = (dc[None] == np.arange(2 * NAT_WIN_C - 1)[:, None, None]) & in_c[None]
    toe = jnp.einsum("had,dck->hack", rpb * LOG2E, jnp.asarray(onehot, F32),
                     precision=lax.Precision.HIGHEST)
    toe = jnp.where(jnp.asarray(in_c)[None, None], toe, NEG_INF)
    pad = jnp.full((heads, 1, GRID_W, GRID_W), NEG_INF, F32)
    ext = jnp.concatenate([pad, toe, pad], axis=1)
    pairs = jnp.concatenate([ext[:, :-1], ext[:, 1:]], axis=-1)
    tiles = rows // NAT_TILE_ROWS
    tq, slab = NAT_TILE_ROWS * GRID_W, NAT_KEY_ROWS * GRID_W
    n_off = 2 * NAT_WIN_R
    return pl.pallas_call(
        functools.partial(_nat_bias_kernel, rows),
        grid=(heads,),
        in_specs=[pl.BlockSpec((None, n_off, GRID_W, LANES), lambda h: (h, 0, 0, 0))],
        out_specs=pl.BlockSpec((None, tiles, tq, slab), lambda h: (h, 0, 0, 0)),
        out_shape=jax.ShapeDtypeStruct((heads, tiles, tq, slab), F32),
        compiler_params=_params("arbitrary"),
        name="nat_bias_expand",
    )(pairs)


def _nat_bias_kernel(rows, pairs_ref, o_ref):
    low = lax.broadcasted_iota(jnp.int32, (GRID_W, LANES), 1) < GRID_W
    masked = jnp.full((GRID_W, LANES), NEG_INF, F32)
    for r in range(rows):
        tile, i = divmod(r, NAT_TILE_ROWS)
        r0 = _nat_row0(r, rows)
        ws = _nat_tile_start(tile, rows)
        for jb in range(NAT_KEY_ROWS // 2):
            kr = ws + 2 * jb
            ok_lo, ok_hi = r0 <= kr < r0 + NAT_WIN_R, r0 <= kr + 1 < r0 + NAT_WIN_R
            blk = masked
            if ok_lo or ok_hi:
                blk = pairs_ref[kr - r + NAT_WIN_R]
                if not ok_lo:
                    blk = jnp.where(low, NEG_INF, blk)
                elif not ok_hi:
                    blk = jnp.where(low, blk, NEG_INF)
            o_ref[tile, i * GRID_W:(i + 1) * GRID_W, jb * LANES:(jb + 1) * LANES] = blk


def kernel(x_prompt, x_sample, cache_l0_k, cache_l0_v, cache_l1_ckv, cache_l1_kpe, cache_l2_k, cache_l2_v, cache_l3_k, cache_l3_v, c, c_ctx, ada_w, ada_b, norm_g, mlp_w1, mlp_w2, attn_w_qkv, attn_q_norm, attn_k_norm, attn_w_o, mla_w_in, mla_q_norm, mla_kv_norm, mla_w_uq, mla_w_ukv, mla_w_o, swa_w_qkv, swa_sink, swa_w_o, nat_w_qkv, nat_rpb, nat_w_o):
    nb, seq, d = x_prompt.shape
    db, dseq, _ = x_sample.shape
    past = cache_l0_k.shape[1]
    ctx_b = nb * seq // dseq
    xp = x_prompt.reshape(ctx_b, dseq, d)
    xs = x_sample

    cond = jnp.zeros((COND_ROWS, d), F32).at[:db].set(c).at[CTX_MOD_ROW].set(c_ctx)
    mods = _modulation(cond, ada_w, ada_b).reshape(DEPTH, COND_ROWS, 1, N_MOD * d)

    row = lambda v: v.reshape(1, -1)

    def head_major(cache):
        b_, l_, h_, dh = cache.shape
        return cache.transpose(0, 2, 3, 1).reshape(b_, h_ * dh, l_)

    def token_major(state, heads):
        b_, w_, l_ = state.shape
        return state.reshape(b_, heads, w_ // heads, l_).transpose(0, 3, 1, 2)

    def finish(o, x, layer, wo, latent):
        return _finish_layer(o, x, mods[layer], norm_g[layer], wo, mlp_w1, mlp_w2, layer, latent)

    g = norm_g[0]
    w = attn_w_qkv
    wo = attn_w_o.astype(BF16)
    consts = [w, row(attn_q_norm), row(attn_k_norm)]
    kvw = ATTN_KV_HEADS * ATTN_HEAD_DIM
    o, l0_k, l0_v = _run_proj(
        _proj_a_kernel, False, xp, mods[0], g, consts, [], [(kvw, F32), (kvw, F32)], "proj_attn_gqa_context",
        seq, attend=functools.partial(_attn_a_kernel, False), qkv=[(d, BF16), (kvw, BF16), (kvw, BF16)])
    xp = finish(o, xp, 0, wo, False)
    perm = np.arange(ATTN_HEAD_DIM).reshape(2, 2, ATTN_HEAD_DIM // 4).transpose(1, 0, 2).reshape(-1)
    qk_heads = ATTN_HEADS + ATTN_KV_HEADS
    cols = (np.arange(qk_heads)[:, None] * ATTN_HEAD_DIM + perm[None, :]).reshape(-1)
    w_lat = jnp.concatenate([attn_w_qkv[:, cols], attn_w_qkv[:, qk_heads * ATTN_HEAD_DIM:]], axis=1).astype(BF16)
    consts_lat = [w_lat, row(attn_q_norm[perm]), row(attn_k_norm[perm])]
    tables = [t[:, perm] for t in _rope_tables(dseq, ATTN_HEAD_DIM)]
    q, k, v = _run_proj(_proj_a_kernel, True, xs, mods[0], g, consts_lat, tables,
                        [(d, BF16), (kvw, BF16), (kvw, BF16)], "proj_gqa_latent")
    o = _attention_a(q, k, v, cache_l0_k[..., perm].reshape(db, past, kvw).astype(BF16),
                     cache_l0_v.reshape(db, past, kvw).astype(BF16))
    xs = finish(o, xs, 0, wo, True)
    new_l0 = (l0_k.reshape(nb, seq, ATTN_KV_HEADS, ATTN_HEAD_DIM),
              l0_v.reshape(nb, seq, ATTN_KV_HEADS, ATTN_HEAD_DIM))

    g = norm_g[1]
    nq, nkv = MLA_Q_LORA, MLA_KV_LORA
    w_in = jnp.concatenate([mla_w_in[:, :nq + nkv]] + [mla_w_in[:, nq + nkv:]] * (LANES // MLA_ROPE),
                           axis=1).astype(BF16)
    wuq = mla_w_uq.reshape(nq, MLA_HEADS, MLA_NOPE + MLA_ROPE)
    wuq = jnp.concatenate([wuq[:, :, :MLA_NOPE].reshape(nq, -1), wuq[:, :, MLA_NOPE:].reshape(nq, -1)],
                          axis=1).astype(BF16)
    wukv = mla_w_ukv.reshape(nkv, MLA_HEADS, MLA_NOPE + MLA_V_DIM)
    wukv = jnp.concatenate([wukv[:, :, :MLA_NOPE].reshape(nkv, -1), wukv[:, :, MLA_NOPE:].reshape(nkv, -1)],
                           axis=1).astype(BF16)
    wo = mla_w_o.astype(BF16)
    consts = [w_in, row(mla_q_norm), row(mla_kv_norm), wuq, wukv]
    hw = MLA_HEADS * MLA_NOPE
    rw = MLA_HEADS * MLA_ROPE
    outs = [(hw, BF16), (rw, BF16), (hw, BF16), (hw, BF16), (LANES, BF16)]
    o, l1_ckv, l1_kpe = _run_proj(
        _proj_b_kernel, False, xp, mods[1], g, consts, [], [(nkv, F32), (MLA_ROPE, F32, "T")],
        "proj_attn_mla_context", seq, qkv=outs,
        attend=lambda qn, qr, kn, v, kp, o_ref: _attn_b_kernel(False, qn, qr, kn, kp, v, o_ref))
    xp = finish(o, xp, 1, wo, False)
    tables = list(_rope_tables(dseq, MLA_ROPE))
    qn, qr, kn, v, kp = _run_proj(_proj_b_kernel, True, xs, mods[1], g, consts, tables, outs,
                                  "proj_mla_latent")
    knc, vc = _mla_expand(cache_l1_ckv, wukv)
    kpc = jnp.tile(cache_l1_kpe, (1, 1, LANES // MLA_ROPE)).astype(BF16)
    o = _attention_b(qn, qr, kn, kp, v, knc, kpc, vc)
    xs = finish(o, xs, 1, wo, True)
    new_l1 = (l1_ckv.reshape(nb, seq, nkv), l1_kpe.transpose(0, 2, 1))

    g = norm_g[2]
    qw = SWA_HEADS * SWA_HEAD_DIM
    kw = SWA_KV_HEADS * SWA_HEAD_DIM
    wq, wk, wv = swa_w_qkv[:, :qw], swa_w_qkv[:, qw:qw + kw], swa_w_qkv[:, qw + kw:]
    wkd = _dup_heads(wk, SWA_KV_HEADS, SWA_HEAD_DIM)
    wvd = _dup_heads(wv, SWA_KV_HEADS, SWA_HEAD_DIM)
    w_ctx = jnp.concatenate([wq, wk, wv, wkd, wvd], axis=1).astype(BF16)
    w_lat = jnp.concatenate([wq, wkd, wvd], axis=1).astype(BF16)
    wo = swa_w_o.astype(BF16)
    o, l2_k, l2_v = _run_proj(
        _proj_c_kernel, False, xp, mods[2], g, [w_ctx], [], [(kw, F32, "T"), (kw, F32, "T")],
        "proj_attn_swa_context", seq, attend=functools.partial(_attn_c_kernel, False),
        attend_inputs=[_sink_logit_block(swa_sink, seq)], qkv=[(qw, BF16), (2 * kw, BF16), (2 * kw, BF16)])
    xp = finish(o, xp, 2, wo, False)
    tables = list(_rope_tables(dseq, SWA_HEAD_DIM))
    q, kd, vd = _run_proj(_proj_c_kernel, True, xs, mods[2], g, [w_lat], tables,
                          [(qw, BF16), (2 * kw, BF16), (2 * kw, BF16)], "proj_swa_latent")
    o = _attention_c(swa_sink, q, kd, vd, head_major(cache_l2_k), head_major(cache_l2_v))
    xs = finish(o, xs, 2, wo, True)
    new_l2 = (token_major(l2_k, SWA_KV_HEADS), token_major(l2_v, SWA_KV_HEADS))

    g = norm_g[3]
    hw = NAT_HEADS * NAT_HEAD_DIM
    w = nat_w_qkv
    wo = nat_w_o.astype(BF16)
    o, l3_k, l3_v = _run_proj(
        _proj_d_kernel, False, xp, mods[3], g, [w], [], [(hw, F32, "T"), (hw, F32, "T")],
        "proj_attn_nat_context", seq, attend=_attn_d_ctx_kernel, qkv=[(hw, BF16)] * 3)
    xp = finish(o, xp, 3, wo, False)
    q, k, v = _run_proj(_proj_d_kernel, True, xs, mods[3], g, [w], [],
                        [(hw, BF16), (hw, BF16), (hw, BF16)], "proj_nat_latent")
    o = _attention_d_lat(q, k, v, head_major(cache_l3_k), head_major(cache_l3_v),
                         _nat_dense_bias(nat_rpb, dseq // GRID_W))
    xs = finish(o, xs, 3, wo, True)
    new_l3 = (token_major(l3_k, NAT_HEADS), token_major(l3_v, NAT_HEADS))

    return (xp.reshape(nb, seq, d), xs) + new_l0 + new_l1 + new_l2 + new_l3
```

```python
import functools

import numpy as np

import jax
import jax.numpy as jnp
from jax import lax
from jax.experimental import pallas as pl
from jax.experimental.pallas import tpu as pltpu

F32 = jnp.float32
BF16 = jnp.bfloat16

D_MODEL = 1024
DEPTH = 4
N_MOD = 6
D_FF = 4 * D_MODEL
GRID_W = 64
ROPE_THETA = 10000.0
NORM_EPS = 1e-6
NEG_INF = -1e30
LOG2E = 1.4426950408889634

ATTN_HEADS, ATTN_KV_HEADS, ATTN_HEAD_DIM = 8, 2, 128
MLA_HEADS, MLA_Q_LORA, MLA_KV_LORA = 16, 384, 256
MLA_NOPE, MLA_ROPE, MLA_V_DIM = 64, 32, 64
MLA_SCALE = (MLA_NOPE + MLA_ROPE) ** -0.5
SWA_HEADS, SWA_KV_HEADS, SWA_HEAD_DIM, SWA_WINDOW = 16, 4, 64, 128
NAT_HEADS, NAT_HEAD_DIM, NAT_WIN_R, NAT_WIN_C = 16, 64, 8, 16

LANES = 128
COND_ROWS = 16
CTX_MOD_ROW = 8
VMEM_LIMIT = 48 * 1024 * 1024

MOD_COL_TILE = 1536
TOK_TILE = 512
TOK_CHAIN = 256
MLP_TOK_TILE = 1024
MLP_FF_TILE = 1024
MLP_VMEM_LIMIT = 56 * 1024 * 1024
Q_TILE = 256
LATENT_Q_ROWS = 512
HEADS_PER_STEP = 16
NAT_TILE_ROWS = Q_TILE // GRID_W
NAT_KEY_ROWS = NAT_WIN_R + NAT_TILE_ROWS


def _params(*sem):
    return pltpu.CompilerParams(dimension_semantics=sem, vmem_limit_bytes=VMEM_LIMIT)


def _full(shape):
    nd = len(shape)
    return pl.BlockSpec(shape, lambda *_: (0,) * nd)


def _rms(x, g):
    return x * lax.rsqrt(jnp.mean(x * x, axis=-1, keepdims=True) + NORM_EPS) * g


def _mod(m, i):
    return m[:, i * D_MODEL:(i + 1) * D_MODEL]


def _premod(x, g, m, sub):
    return _rms(x, g * (1.0 + _mod(m, 3 * sub + 1))) + _mod(m, 3 * sub)


def _rope(x, cos, sin_signed, quarter):
    n = x.shape[-1]
    lane = lax.broadcasted_iota(jnp.int32, x.shape, 1)
    first = ((lane // quarter) % 2) == 0
    partner = jnp.where(first, pltpu.roll(x, n - quarter, 1), pltpu.roll(x, quarter, 1))
    return x * cos + partner * sin_signed


def _dot(a, b):
    return jnp.dot(a, b, preferred_element_type=F32)


def _dot_nt(a, b):
    return lax.dot_general(a, b, (((1,), (1,)), ((), ())), preferred_element_type=F32)


def _with_ones(v):
    return jnp.concatenate([v, jnp.ones_like(v)], axis=1)


def _attend(q, segs, sink=None):
    logits = []
    for seg in segs:
        k, bias, transposed = seg[0], seg[2], len(seg) > 3 and seg[3]
        s = _dot(q, k) if transposed else _dot_nt(q, k)
        logits.append(s if bias is None else s + bias)
    m = logits[0].max(axis=-1, keepdims=True)
    for s in logits[1:]:
        m = jnp.maximum(m, s.max(axis=-1, keepdims=True))
    if sink is not None:
        m = jnp.maximum(m, sink)
    acc = None
    for s, seg in zip(logits, segs):
        p = jnp.exp2(s - m).astype(BF16)
        pv = _dot_nt(p, seg[1]) if len(seg) > 3 and seg[3] else _dot(p, seg[1])
        acc = pv if acc is None else acc + pv
    den = acc[:, LANES:LANES + 1]
    if sink is not None:
        den = den + jnp.exp2(sink - m)
    return acc[:, :LANES] / den


def _attend_group(qs, ks, v1s, sink_logits=None):
    tq = qs[0].shape[0]
    s = jnp.concatenate([_dot_nt(q, k) for q, k in zip(qs, ks)], axis=0)
    if sink_logits is not None:
        s = jnp.concatenate([s, sink_logits], axis=1)
        zeros = jnp.zeros((LANES, LANES), BF16)
        tail = jnp.concatenate([zeros, jnp.ones_like(zeros)], axis=1)
        v1s = [jnp.concatenate([v1, tail], axis=0) for v1 in v1s]
    p = jnp.exp2(s - s.max(axis=-1, keepdims=True)).astype(BF16)
    acc = jnp.concatenate([_dot(p[i * tq:(i + 1) * tq], v1) for i, v1 in enumerate(v1s)], axis=0)
    return acc[:, :LANES] / acc[:, LANES:LANES + 1]


def _split_pair(q):
    low = lax.broadcasted_iota(jnp.int32, q.shape, 1) < (LANES // 2)
    zero = jnp.zeros_like(q)
    return [jnp.where(low, q, zero), jnp.where(low, zero, q)]


def _merge_pair(o0, o1):
    low = lax.broadcasted_iota(jnp.int32, o0.shape, 1) < (LANES // 2)
    return jnp.where(low, o0, o1).astype(BF16)


def _attend_pair(q, segs, sinks=None):
    outs = []
    for j, qj in enumerate(_split_pair(q)):
        segs_j = [(s[0], s[1], None if s[2] is None else s[2][j]) + tuple(s[3:]) for s in segs]
        outs.append(_attend(qj, segs_j, None if sinks is None else sinks[j]))
    return _merge_pair(*outs)


def _mods_kernel(cond_ref, w_ref, b_ref, o_ref):
    cnd = cond_ref[...]
    act = cnd * jax.nn.sigmoid(cnd)
    o_ref[...] = _dot(act.astype(BF16), w_ref[...].astype(BF16)) + b_ref[...]


def _modulation(cond, ada_w, ada_b):
    tn = MOD_COL_TILE
    n = N_MOD * D_MODEL
    return pl.pallas_call(
        _mods_kernel,
        grid=(DEPTH, n // tn),
        in_specs=[
            _full((COND_ROWS, D_MODEL)),
            pl.BlockSpec((None, D_MODEL, tn), lambda l, j: (l, 0, j)),
            pl.BlockSpec((None, 1, tn), lambda l, j: (l, 0, j)),
        ],
        out_specs=pl.BlockSpec((None, COND_ROWS, tn), lambda l, j: (l, 0, j)),
        out_shape=jax.ShapeDtypeStruct((DEPTH, COND_ROWS, n), F32),
        compiler_params=_params("arbitrary", "arbitrary"),
        name="adaln_mods",
    )(cond, ada_w, ada_b.reshape(DEPTH, 1, n))


def _proj_a_kernel(latent, x_ref, m_ref, g_ref, w_ref, qn_ref, kn_ref, *rest):
    if latent:
        cos_ref, sin_ref, q_ref, k_ref, v_ref = rest
    else:
        q_ref, k_ref, v_ref, ks_ref, vs_ref = rest
    hd = ATTN_HEAD_DIM
    n_qk = ATTN_HEADS + ATTN_KV_HEADS
    h = _premod(x_ref[...], g_ref[0:1, :], m_ref[...], 0).astype(BF16)
    w = w_ref[...].astype(BF16)
    for c in range(x_ref.shape[0] // TOK_CHAIN):
        rows = slice(c * TOK_CHAIN, (c + 1) * TOK_CHAIN)
        z = _dot(h[rows], w)
        ys = [_rms(z[:, i * hd:(i + 1) * hd], qn_ref[...] if i < ATTN_HEADS else kn_ref[...]) for i in range(n_qk)]
        if not latent:
            for j in range(ATTN_KV_HEADS):
                ks_ref[rows, j * hd:(j + 1) * hd] = ys[ATTN_HEADS + j]
        else:
            ys = [y * cos_ref[rows, :] + pltpu.roll(y, hd // 2, 1) * sin_ref[rows, :] for y in ys]
        for i in range(ATTN_HEADS):
            q_ref[rows, i * hd:(i + 1) * hd] = (ys[i] * (hd ** -0.5 * LOG2E)).astype(BF16)
        for j in range(ATTN_KV_HEADS):
            k_ref[rows, j * hd:(j + 1) * hd] = ys[ATTN_HEADS + j].astype(BF16)
        v = z[:, n_qk * hd:]
        v_ref[rows, :] = v.astype(BF16)
        if not latent:
            vs_ref[rows, :] = v


def _proj_b_kernel(latent, x_ref, m_ref, g_ref, win_ref, qn_ref, kvn_ref, wuq_ref, wukv_ref, *rest):
    if latent:
        cos_ref, sin_ref, qn_o, qr_o, kn_o, v_o, kp_o = rest
    else:
        qn_o, qr_o, kn_o, v_o, kp_o, ckv_s, kpe_s = rest
    nq, nkv = MLA_Q_LORA, MLA_KV_LORA
    nope_w = MLA_HEADS * MLA_NOPE
    scale = MLA_SCALE * LOG2E
    h = _premod(x_ref[...], g_ref[0:1, :], m_ref[...], 0)
    z = _dot(h.astype(BF16), win_ref[...])
    cq = _rms(z[:, :nq], qn_ref[...])
    ckv = _rms(z[:, nq:nq + nkv], kvn_ref[...])
    kp = z[:, nq + nkv:]
    q = _dot(cq.astype(BF16), wuq_ref[...])
    kv = _dot(ckv.astype(BF16), wukv_ref[...])
    if not latent:
        ckv_s[...] = ckv
        _store_transposed(kpe_s, kp, MLA_ROPE)
    qn_o[...] = (q[:, :nope_w] * scale).astype(BF16)
    for i in range(MLA_HEADS * MLA_ROPE // LANES):
        qr = q[:, nope_w + i * LANES:nope_w + (i + 1) * LANES]
        if latent:
            qr = _rope(qr, cos_ref[...], sin_ref[...], MLA_ROPE // 4)
        qr_o[:, i * LANES:(i + 1) * LANES] = (qr * scale).astype(BF16)
    if latent:
        kp = _rope(kp, cos_ref[...], sin_ref[...], MLA_ROPE // 4)
    kp_o[...] = kp.astype(BF16)
    kn_o[...] = kv[:, :nope_w].astype(BF16)
    v_o[...] = kv[:, nope_w:].astype(BF16)


def _proj_c_kernel(latent, x_ref, m_ref, g_ref, w_ref, *rest):
    if latent:
        cos_ref, sin_ref, q_o, kd_o, vd_o = rest
    else:
        q_o, kd_o, vd_o, ks_o, vs_o = rest
    qw = SWA_HEADS * SWA_HEAD_DIM
    kw = SWA_KV_HEADS * SWA_HEAD_DIM
    h = _premod(x_ref[...], g_ref[0:1, :], m_ref[...], 0)
    z = _dot(h.astype(BF16), w_ref[...])
    off = qw
    if not latent:
        _store_transposed(ks_o, z[:, qw:qw + kw], kw)
        _store_transposed(vs_o, z[:, qw + kw:qw + 2 * kw], kw)
        off = qw + 2 * kw
    scale = SWA_HEAD_DIM ** -0.5 * LOG2E
    for i in range(qw // LANES):
        y = z[:, i * LANES:(i + 1) * LANES]
        if latent:
            y = _rope(y, cos_ref[...], sin_ref[...], SWA_HEAD_DIM // 4)
        q_o[:, i * LANES:(i + 1) * LANES] = (y * scale).astype(BF16)
    for i in range(2 * kw // LANES):
        y = z[:, off + i * LANES:off + (i + 1) * LANES]
        if latent:
            y = _rope(y, cos_ref[...], sin_ref[...], SWA_HEAD_DIM // 4)
        kd_o[:, i * LANES:(i + 1) * LANES] = y.astype(BF16)
    vd_o[...] = z[:, off + 2 * kw:].astype(BF16)


def _proj_d_kernel(latent, x_ref, m_ref, g_ref, w_ref, *rest):
    if latent:
        q_o, k_o, v_o = rest
    else:
        q_o, k_o, v_o, ks_o, vs_o = rest
    hw = NAT_HEADS * NAT_HEAD_DIM
    h = _premod(x_ref[...], g_ref[0:1, :], m_ref[...], 0)
    z = _dot(h.astype(BF16), w_ref[...].astype(BF16))
    q_o[...] = (z[:, :hw] * (NAT_HEAD_DIM ** -0.5 * LOG2E)).astype(BF16)
    k_o[...] = z[:, hw:2 * hw].astype(BF16)
    v_o[...] = z[:, 2 * hw:].astype(BF16)
    if not latent:
        _store_transposed(ks_o, z[:, hw:2 * hw], hw)
        _store_transposed(vs_o, z[:, 2 * hw:], hw)


def _store_transposed(ref, val, width):
    n, _, seq = ref.shape
    for i in range(n):
        ref[i] = val[i * seq:(i + 1) * seq, :].T[:width, :]


def _run_proj(kernel, latent, x, mods_l, g, consts, tables, outs, name, state_seq=None,
              attend=None, attend_inputs=(), qkv=()):
    bx, s, _ = x.shape
    tm = TOK_TILE
    row = (lambda b, j: (b, 0, 0)) if latent else (lambda b, j: (CTX_MOD_ROW, 0, 0))
    in_specs = [
        pl.BlockSpec((None, tm, D_MODEL), lambda b, j: (b, j, 0)),
        pl.BlockSpec((None, 1, N_MOD * D_MODEL), row),
        _full(g.shape),
    ] + [_full(c.shape) for c in consts]
    in_specs += [pl.BlockSpec((tm, LANES), lambda b, j: (j, 0)) for _ in tables]
    in_specs += [_full(a.shape) for a in attend_inputs]
    body = functools.partial(kernel, latent)
    if attend is not None:
        n_proj_in = 3 + len(consts) + len(tables)
        n_att_in = len(attend_inputs)
        proj_body = body
        outs = [(D_MODEL, BF16)] + list(outs)

        def body(*refs):
            ins, att_ins = refs[:n_proj_in], refs[n_proj_in:n_proj_in + n_att_in]
            o_ref = refs[n_proj_in + n_att_in]
            states, scratch = refs[n_proj_in + n_att_in + 1:-len(qkv)], refs[-len(qkv):]
            proj_body(*ins, *scratch, *states)
            for i in range(tm // state_seq):
                rows = slice(i * state_seq, (i + 1) * state_seq)
                attend(*att_ins, *[r.at[rows] for r in scratch], o_ref.at[rows])

    out_specs, out_shape = [], []
    for o in outs:
        if len(o) == 2:
            out_specs.append(pl.BlockSpec((None, tm, o[0]), lambda b, j: (b, j, 0)))
            out_shape.append(jax.ShapeDtypeStruct((bx, s, o[0]), o[1]))
        else:
            per_tile = tm // state_seq
            out_specs.append(pl.BlockSpec((per_tile, o[0], state_seq),
                                          lambda b, j: (b * (s // tm) + j, 0, 0)))
            out_shape.append(jax.ShapeDtypeStruct((bx * s // state_seq, o[0], state_seq), o[1]))
    return pl.pallas_call(
        body,
        grid=(bx, s // tm),
        in_specs=in_specs,
        out_specs=out_specs,
        out_shape=out_shape,
        scratch_shapes=[pltpu.VMEM((tm, w), dt) for w, dt in qkv],
        compiler_params=_params("arbitrary", "arbitrary"),
        name=name,
    )(x, mods_l, g, *consts, *tables, *attend_inputs)


def _attn_a_kernel(has_ctx, q_ref, k_ref, v_ref, *rest):
    if has_ctx:
        kc_ref, vc_ref, o_ref = rest
    else:
        (o_ref,) = rest
    group = ATTN_HEADS // ATTN_KV_HEADS
    head_cols = [slice(h * LANES, (h + 1) * LANES) for h in range(ATTN_HEADS)]
    kv_cols = [slice(j * LANES, (j + 1) * LANES) for j in range(ATTN_KV_HEADS)]
    if not has_ctx:
        tq = q_ref.shape[0]
        qs = [jnp.concatenate([q_ref[:, c] for c in head_cols[j * group:(j + 1) * group]], axis=0)
              for j in range(ATTN_KV_HEADS)]
        out = _attend_group(qs, [k_ref[:, c] for c in kv_cols], [_with_ones(v_ref[:, c]) for c in kv_cols])
        for h, c in enumerate(head_cols):
            o_ref[:, c] = out[h * tq:(h + 1) * tq].astype(BF16)
        return
    for j, kc in enumerate(kv_cols):
        segs = [(k_ref[:, kc], _with_ones(v_ref[:, kc]), None),
                (kc_ref[:, kc], _with_ones(vc_ref[:, kc]), None)]
        for r in range(q_ref.shape[0] // Q_TILE):
            rows = slice(r * Q_TILE, (r + 1) * Q_TILE)
            for c in head_cols[j * group:(j + 1) * group]:
                o_ref[rows, c] = _attend(q_ref[rows, c], segs).astype(BF16)


def _attention_a(q, k, v, kc=None, vc=None):
    b, s, qw = q.shape
    kl, kw = k.shape[1:]
    has_ctx = kc is not None
    tq = min(LATENT_Q_ROWS if has_ctx else Q_TILE, s)
    in_specs = [
        pl.BlockSpec((None, tq, qw), lambda bi, t: (bi, t, 0)),
        pl.BlockSpec((None, kl, kw), lambda bi, t: (bi, 0, 0)),
        pl.BlockSpec((None, kl, kw), lambda bi, t: (bi, 0, 0)),
    ]
    args = [q, k, v]
    if has_ctx:
        cl = kc.shape[1]
        in_specs += [pl.BlockSpec((None, cl, kw), lambda bi, t: (bi, 0, 0))] * 2
        args += [kc, vc]
    return pl.pallas_call(
        functools.partial(_attn_a_kernel, has_ctx),
        grid=(b, s // tq),
        in_specs=in_specs,
        out_specs=pl.BlockSpec((None, tq, qw), lambda bi, t: (bi, t, 0)),
        out_shape=jax.ShapeDtypeStruct(q.shape, BF16),
        compiler_params=_params("arbitrary", "arbitrary"),
        name="attn_gqa_latent" if has_ctx else "attn_gqa_context",
    )(*args)


def _attn_b_kernel(has_ctx, qn_ref, qr_ref, kn_ref, kp_ref, v_ref, *rest):
    if has_ctx:
        knc_ref, kpc_ref, vc_ref, o_ref = rest
    else:
        (o_ref,) = rest
    tq = min(Q_TILE, qn_ref.shape[0])
    lane = lax.broadcasted_iota(jnp.int32, (tq, LANES), 1)
    low = lane < (LANES // 2)
    quarter = lane // MLA_ROPE
    zero = jnp.zeros((tq, LANES), BF16)
    kp = kp_ref[...]
    kpc = kpc_ref[...] if has_ctx else None
    pairs_per_rope = LANES // MLA_ROPE // 2
    n_pairs = HEADS_PER_STEP // 2
    pair_cols = [slice(p * LANES, (p + 1) * LANES) for p in range(n_pairs)]

    def pair_lhs(p, rows=slice(None)):
        qn = qn_ref[rows, pair_cols[p]]
        rg = p // pairs_per_rope
        qr = qr_ref[rows, rg * LANES:(rg + 1) * LANES]
        return [jnp.concatenate([jnp.where(low if j == 0 else ~low, qn, zero),
                                 jnp.where(quarter == 2 * (p % pairs_per_rope) + j, qr, zero)], axis=1)
                for j in range(2)]

    if not has_ctx:
        qs = [jnp.concatenate(pair_lhs(p), axis=0) for p in range(n_pairs)]
        out = _attend_group(qs, [jnp.concatenate([kn_ref[:, c], kp], axis=1) for c in pair_cols],
                            [_with_ones(v_ref[:, c]) for c in pair_cols])
        for p, c in enumerate(pair_cols):
            o_ref[:, c] = _merge_pair(out[2 * p * tq:(2 * p + 1) * tq], out[(2 * p + 1) * tq:(2 * p + 2) * tq])
        return
    for p, cols in enumerate(pair_cols):
        segs = [(jnp.concatenate([kn_ref[:, cols], kp], axis=1), _with_ones(v_ref[:, cols]), None),
                (jnp.concatenate([knc_ref[:, cols], kpc], axis=1), _with_ones(vc_ref[:, cols]), None)]
        for r in range(qn_ref.shape[0] // tq):
            rows = slice(r * tq, (r + 1) * tq)
            o_ref[rows, cols] = _merge_pair(*[_attend(lhs, segs) for lhs in pair_lhs(p, rows)])


def _attention_b(qn, qr, kn, kp, v, knc=None, kpc=None, vc=None):
    b, s, _ = qn.shape
    kl = kn.shape[1]
    has_ctx = knc is not None
    tq = min(LATENT_Q_ROWS if has_ctx else Q_TILE, s)
    gw = HEADS_PER_STEP * MLA_NOPE
    blk = lambda rows, w, fn: pl.BlockSpec((None, rows, w), fn)
    in_specs = [
        blk(tq, gw, lambda bi, g, t: (bi, t, g)),
        blk(tq, HEADS_PER_STEP * MLA_ROPE, lambda bi, g, t: (bi, t, g)),
        blk(kl, gw, lambda bi, g, t: (bi, 0, g)),
        blk(kl, LANES, lambda bi, g, t: (bi, 0, 0)),
        blk(kl, gw, lambda bi, g, t: (bi, 0, g)),
    ]
    args = [qn, qr, kn, kp, v]
    if has_ctx:
        cl = knc.shape[1]
        in_specs += [
            blk(cl, gw, lambda bi, g, t: (bi, 0, g)),
            blk(cl, LANES, lambda bi, g, t: (bi, 0, 0)),
            blk(cl, gw, lambda bi, g, t: (bi, 0, g)),
        ]
        args += [knc, kpc, vc]
    return pl.pallas_call(
        functools.partial(_attn_b_kernel, has_ctx),
        grid=(b, MLA_HEADS // HEADS_PER_STEP, s // tq),
        in_specs=in_specs,
        out_specs=blk(tq, gw, lambda bi, g, t: (bi, t, g)),
        out_shape=jax.ShapeDtypeStruct(qn.shape, BF16),
        compiler_params=_params("arbitrary", "arbitrary", "arbitrary"),
        name="attn_mla_latent" if has_ctx else "attn_mla_context",
    )(*args)


def _attn_c_kernel(latent, sink_ref, q_ref, kd_ref, vd_ref, *rest):
    if latent:
        kc_ref, vc_ref, o_ref = rest
    else:
        (o_ref,) = rest
    tq = min(Q_TILE, q_ref.shape[0])
    group = SWA_HEADS // SWA_KV_HEADS
    kv_per_step = HEADS_PER_STEP // group
    head0 = pl.program_id(1) * HEADS_PER_STEP
    windows = []
    if latent:
        s_len = kd_ref.shape[0]
        win = tq + 2 * SWA_WINDOW
        for r in range(q_ref.shape[0] // tq):
            t0 = pl.program_id(2) * q_ref.shape[0] + r * tq
            ws = pl.multiple_of(jnp.clip(t0 - SWA_WINDOW, 0, s_len - win), LANES)
            qpos = t0 + lax.broadcasted_iota(jnp.int32, (tq, win), 0)
            kpos = ws + lax.broadcasted_iota(jnp.int32, (tq, win), 1)
            bias = jnp.where(jnp.abs(qpos - kpos) <= SWA_WINDOW, 0.0, NEG_INF).astype(F32)
            windows.append((slice(r * tq, (r + 1) * tq), ws, bias))
    pairs_per_kv = group // 2
    pair_cols = [slice(p * LANES, (p + 1) * LANES) for p in range(HEADS_PER_STEP // 2)]
    kv_cols = [slice(j * LANES, (j + 1) * LANES) for j in range(kv_per_step)]
    if not latent:
        qs = [jnp.concatenate([h for c in pair_cols[j * pairs_per_kv:(j + 1) * pairs_per_kv]
                               for h in _split_pair(q_ref[:, c])], axis=0) for j in range(kv_per_step)]
        out = _attend_group(qs, [kd_ref[:, c] for c in kv_cols], [_with_ones(vd_ref[:, c]) for c in kv_cols],
                            sink_ref[...])
        for p, c in enumerate(pair_cols):
            o_ref[:, c] = _merge_pair(out[2 * p * tq:(2 * p + 1) * tq], out[(2 * p + 1) * tq:(2 * p + 2) * tq])
        return
    sinks = [sink_ref[head0 + h] * LOG2E for h in range(HEADS_PER_STEP)]
    hd = SWA_HEAD_DIM
    for j, kc in enumerate(kv_cols):
        kct = kc_ref[j * hd:(j + 1) * hd, :].astype(BF16)
        vct = vc_ref[j * hd:(j + 1) * hd, :].astype(BF16)
        ctx_seg = (jnp.concatenate([kct, kct], axis=0),
                   jnp.concatenate([vct, vct, jnp.ones((LANES, vct.shape[1]), BF16)], axis=0), None, True)
        for rows, ws, bias in windows:
            segs = [(kd_ref[pl.ds(ws, win), kc], _with_ones(vd_ref[pl.ds(ws, win), kc]), (bias, bias)), ctx_seg]
            for p in range(j * pairs_per_kv, (j + 1) * pairs_per_kv):
                o_ref[rows, pair_cols[p]] = _attend_pair(q_ref[rows, pair_cols[p]], segs, sinks[2 * p:2 * p + 2])


def _sink_logit_block(sink, tq):
    rows = jnp.repeat(sink * LOG2E, tq)[:, None]
    return jnp.where(jnp.arange(LANES)[None, :] == 0, rows, NEG_INF)


def _attention_c(sink, q, kd, vd, kc, vc):
    b, s, _ = q.shape
    kl = kd.shape[1]
    tq = min(LATENT_Q_ROWS, s)
    gw = HEADS_PER_STEP * SWA_HEAD_DIM
    kvw = 2 * SWA_HEAD_DIM * HEADS_PER_STEP // (SWA_HEADS // SWA_KV_HEADS)
    cw, cl = kc.shape[1] * HEADS_PER_STEP // SWA_HEADS, kc.shape[2]
    return pl.pallas_call(
        functools.partial(_attn_c_kernel, True),
        grid=(b, SWA_HEADS // HEADS_PER_STEP, s // tq),
        in_specs=[
            pl.BlockSpec(memory_space=pltpu.SMEM),
            pl.BlockSpec((None, tq, gw), lambda bi, g, t: (bi, t, g)),
            pl.BlockSpec((None, kl, kvw), lambda bi, g, t: (bi, 0, g)),
            pl.BlockSpec((None, kl, kvw), lambda bi, g, t: (bi, 0, g)),
            pl.BlockSpec((None, cw, cl), lambda bi, g, t: (bi, g, 0)),
            pl.BlockSpec((None, cw, cl), lambda bi, g, t: (bi, g, 0)),
        ],
        out_specs=pl.BlockSpec((None, tq, gw), lambda bi, g, t: (bi, t, g)),
        out_shape=jax.ShapeDtypeStruct(q.shape, BF16),
        compiler_params=_params("arbitrary", "arbitrary", "arbitrary"),
        name="attn_swa_latent",
    )(sink, q, kd, vd, kc, vc)


def _attn_d_ctx_kernel(q_ref, k_ref, v_ref, o_ref):
    tq = q_ref.shape[0]
    pair_cols = [slice(p * LANES, (p + 1) * LANES) for p in range(HEADS_PER_STEP // 2)]
    qs = [jnp.concatenate(_split_pair(q_ref[:, c]), axis=0) for c in pair_cols]
    out = _attend_group(qs, [k_ref[:, c] for c in pair_cols], [_with_ones(v_ref[:, c]) for c in pair_cols])
    for p, c in enumerate(pair_cols):
        o_ref[:, c] = _merge_pair(out[2 * p * tq:(2 * p + 1) * tq], out[(2 * p + 1) * tq:(2 * p + 2) * tq])


def _nat_row0(r, rows):
    return min(max(r - NAT_WIN_R // 2, 0), rows - NAT_WIN_R)


def _nat_tile_key_rows(tile, rows):
    first, last = tile * NAT_TILE_ROWS, (tile + 1) * NAT_TILE_ROWS - 1
    need = _nat_row0(last, rows) + NAT_WIN_R - _nat_row0(first, rows)
    return need + need % 2


def _nat_tile_start(tile, rows):
    return min(_nat_row0(tile * NAT_TILE_ROWS, rows), rows - _nat_tile_key_rows(tile, rows))


def _attn_d_lat_kernel(q_ref, k_ref, v_ref, kc_ref, vc_ref, bias_ref, o_ref):
    rows = k_ref.shape[0] // GRID_W
    tile = pl.program_id(0)
    ones = jnp.ones((LANES, kc_ref.shape[1]), BF16)

    def body(key_rows, start_row):
        slab = key_rows * GRID_W
        start = pl.multiple_of(start_row * GRID_W, GRID_W)
        for p in range(HEADS_PER_STEP // 2):
            cols = slice(p * LANES, (p + 1) * LANES)
            segs = [(k_ref[pl.ds(start, slab), cols], _with_ones(v_ref[pl.ds(start, slab), cols]),
                     (bias_ref[2 * p, :, :slab], bias_ref[2 * p + 1, :, :slab])),
                    (kc_ref[cols, :].astype(BF16),
                     jnp.concatenate([vc_ref[cols, :].astype(BF16), ones], axis=0), None, True)]
            o_ref[:, cols] = _attend_pair(q_ref[:, cols], segs)

    tiles = range(rows // NAT_TILE_ROWS)
    for key_rows in sorted({_nat_tile_key_rows(t, rows) for t in tiles}):
        members = [t for t in tiles if _nat_tile_key_rows(t, rows) == key_rows]
        cond = functools.reduce(jnp.logical_or, [tile == t for t in members])
        start_row = functools.reduce(lambda acc, t: jnp.where(tile == t, _nat_tile_start(t, rows), acc),
                                     members, jnp.int32(0))
        pl.when(cond)(functools.partial(body, key_rows, start_row))


def _attention_d_lat(q, k, v, kc, vc, bias):
    b, s, _ = q.shape
    cl = kc.shape[2]
    tq = NAT_TILE_ROWS * GRID_W
    slab = NAT_KEY_ROWS * GRID_W
    gw = HEADS_PER_STEP * NAT_HEAD_DIM
    return pl.pallas_call(
        _attn_d_lat_kernel,
        grid=(s // tq, NAT_HEADS // HEADS_PER_STEP, b),
        in_specs=[
            pl.BlockSpec((None, tq, gw), lambda t, g, bi: (bi, t, g)),
            pl.BlockSpec((None, s, gw), lambda t, g, bi: (bi, 0, g)),
            pl.BlockSpec((None, s, gw), lambda t, g, bi: (bi, 0, g)),
            pl.BlockSpec((None, gw, cl), lambda t, g, bi: (bi, g, 0)),
            pl.BlockSpec((None, gw, cl), lambda t, g, bi: (bi, g, 0)),
            pl.BlockSpec((HEADS_PER_STEP, None, tq, slab), lambda t, g, bi: (g, t, 0, 0)),
        ],
        out_specs=pl.BlockSpec((None, tq, gw), lambda t, g, bi: (bi, t, g)),
        out_shape=jax.ShapeDtypeStruct(q.shape, BF16),
        compiler_params=_params("arbitrary", "arbitrary", "arbitrary"),
        name="attn_nat_latent",
    )(q, k, v, kc, vc, bias)


def _mla_expand_kernel(c_ref, w_ref, kn_ref, v_ref):
    kv = _dot(c_ref[...].astype(BF16), w_ref[...])
    half = kv.shape[1] // 2
    kn_ref[...] = kv[:, :half].astype(BF16)
    v_ref[...] = kv[:, half:].astype(BF16)


def _mla_expand(ckv, w_ukv):
    b, l, c = ckv.shape
    n = w_ukv.shape[1] // 2
    out = pl.BlockSpec((None, l, n), lambda bi: (bi, 0, 0))
    return pl.pallas_call(
        _mla_expand_kernel,
        grid=(b,),
        in_specs=[pl.BlockSpec((None, l, c), lambda bi: (bi, 0, 0)), _full(w_ukv.shape)],
        out_specs=[out, out],
        out_shape=[jax.ShapeDtypeStruct((b, l, n), BF16)] * 2,
        compiler_params=_params("arbitrary"),
        name="mla_expand_cache",
    )(ckv, w_ukv)


def _finish_kernel(o_ref, x_ref, m_ref, g_ref, wo_ref, w1_ref, w2_ref, y_ref, x1_ref, h2_ref):
    f = pl.program_id(2)
    last = pl.num_programs(2) - 1
    chains = [slice(c * TOK_CHAIN, (c + 1) * TOK_CHAIN) for c in range(y_ref.shape[0] // TOK_CHAIN)]

    def mlp_part(h, w1, w2):
        u = jnp.square(jnp.maximum(_dot(h, w1), 0.0)).astype(BF16)
        return _dot(u, w2)

    @pl.when(f == 0)
    def _():
        m = m_ref[...]
        wo = wo_ref[...].astype(BF16)
        for rows in chains:
            a = _dot(o_ref[rows, :], wo)
            x1 = x_ref[rows, :] + _rms(a, _mod(m, 2) * g_ref[1:2, :])
            x1_ref[rows, :] = x1
            h2_ref[rows, :] = _premod(x1, g_ref[2:3, :], m, 1).astype(BF16)
        y_ref[...] = mlp_part(h2_ref[...], w1_ref[...].astype(BF16), w2_ref[...].astype(BF16))

    @pl.when((f > 0) & (f < last))
    def _():
        y_ref[...] += mlp_part(h2_ref[...], w1_ref[...].astype(BF16), w2_ref[...].astype(BF16))

    @pl.when(f == last)
    def _():
        gain = _mod(m_ref[...], 5) * g_ref[3:4, :]
        w1, w2 = w1_ref[...].astype(BF16), w2_ref[...].astype(BF16)
        for rows in chains:
            y = y_ref[rows, :] + mlp_part(h2_ref[rows, :], w1, w2)
            y_ref[rows, :] = x1_ref[rows, :] + _rms(y, gain)


def _finish_layer(o, x, mods_l, g, wo, w1, w2, layer, latent):
    bx, s, _ = x.shape
    tm, tf = MLP_TOK_TILE, MLP_FF_TILE
    row = (lambda b, j, f: (b, 0, 0)) if latent else (lambda b, j, f: (CTX_MOD_ROW, 0, 0))
    tok = lambda w: pl.BlockSpec((None, tm, w), lambda b, j, f: (b, j, 0))
    return pl.pallas_call(
        _finish_kernel,
        grid=(bx, s // tm, D_FF // tf),
        in_specs=[tok(o.shape[-1]), tok(D_MODEL),
                  pl.BlockSpec((None, 1, N_MOD * D_MODEL), row), _full(g.shape), _full(wo.shape),
                  pl.BlockSpec((None, D_MODEL, tf), lambda b, j, f: (layer, 0, f)),
                  pl.BlockSpec((None, tf, D_MODEL), lambda b, j, f: (layer, f, 0))],
        out_specs=tok(D_MODEL),
        out_shape=jax.ShapeDtypeStruct(x.shape, F32),
        scratch_shapes=[pltpu.VMEM((tm, D_MODEL), F32), pltpu.VMEM((tm, D_MODEL), BF16)],
        compiler_params=pltpu.CompilerParams(dimension_semantics=("arbitrary",) * 3,
                                             vmem_limit_bytes=MLP_VMEM_LIMIT),
        name="out_proj_mlp",
    )(o, x, mods_l, g, wo, w1, w2)


def _rope_tables(s, dim):
    quarter = dim // 4
    t = jnp.arange(s)
    pos = jnp.stack([t // GRID_W, t % GRID_W], axis=-1).astype(F32)
    inv = ROPE_THETA ** (-jnp.arange(quarter, dtype=F32) / quarter)
    ang = pos[:, :, None] * inv
    cos = jnp.broadcast_to(jnp.cos(ang)[:, :, None, :], (s, 2, 2, quarter)).reshape(s, dim)
    sign = jnp.array([-1.0, 1.0], F32)[None, None, :, None]
    sin = (jnp.sin(ang)[:, :, None, :] * sign).reshape(s, dim)
    reps = LANES // dim
    return jnp.tile(cos, (1, reps)), jnp.tile(sin, (1, reps))


def _dup_heads(w, heads, dim):
    lead = w.shape[:-1]
    w = w.reshape(lead + (heads, 1, dim))
    return jnp.broadcast_to(w, lead + (heads, 2, dim)).reshape(lead + (heads * 2 * dim,))


def _nat_dense_bias(rpb, rows):
    heads = rpb.shape[0]
    c = np.arange(GRID_W)
    c0 = np.clip(c - NAT_WIN_C // 2, 0, GRID_W - NAT_WIN_C)
    in_c = (c[None, :] >= c0[:, None]) & (c[None, :] < c0[:, None] + NAT_WIN_C)
    dc = c[None, :] - c[:, None] + NAT_WIN_C - 1
    onehot = (dc[None] == np.arange(2 * NAT_WIN_C - 1)[:, None, None]) & in_c[None]
    toe = jnp.einsum("had,dck->hack", rpb * LOG2E, jnp.asarray(onehot, F32),
                     precision=lax.Precision.HIGHEST)
    toe = jnp.where(jnp.asarray(in_c)[None, None], toe, NEG_INF)
    pad = jnp.full((heads, 1, GRID_W, GRID_W), NEG_INF, F32)
    ext = jnp.concatenate([pad, toe, pad], axis=1)
    pairs = jnp.concatenate([ext[:, :-1], ext[:, 1:]], axis=-1)
    tiles = rows // NAT_TILE_ROWS
    tq, slab = NAT_TILE_ROWS * GRID_W, NAT_KEY_ROWS * GRID_W
    n_off = 2 * NAT_WIN_R
    return pl.pallas_call(
        functools.partial(_nat_bias_kernel, rows),
        grid=(heads,),
        in_specs=[pl.BlockSpec((None, n_off, GRID_W, LANES), lambda h: (h, 0, 0, 0))],
        out_specs=pl.BlockSpec((None, tiles, tq, slab), lambda h: (h, 0, 0, 0)),
        out_shape=jax.ShapeDtypeStruct((heads, tiles, tq, slab), F32),
        compiler_params=_params("arbitrary"),
        name="nat_bias_expand",
    )(pairs)


def _nat_bias_kernel(rows, pairs_ref, o_ref):
    low = lax.broadcasted_iota(jnp.int32, (GRID_W, LANES), 1) < GRID_W
    masked = jnp.full((GRID_W, LANES), NEG_INF, F32)
    for r in range(rows):
        tile, i = divmod(r, NAT_TILE_ROWS)
        r0 = _nat_row0(r, rows)
        ws = _nat_tile_start(tile, rows)
        for jb in range(NAT_KEY_ROWS // 2):
            kr = ws + 2 * jb
            ok_lo, ok_hi = r0 <= kr < r0 + NAT_WIN_R, r0 <= kr + 1 < r0 + NAT_WIN_R
            blk = masked
            if ok_lo or ok_hi:
                blk = pairs_ref[kr - r + NAT_WIN_R]
                if not ok_lo:
                    blk = jnp.where(low, NEG_INF, blk)
                elif not ok_hi:
                    blk = jnp.where(low, blk, NEG_INF)
            o_ref[tile, i * GRID_W:(i + 1) * GRID_W, jb * LANES:(jb + 1) * LANES] = blk


def kernel(x_prompt, x_sample, cache_l0_k, cache_l0_v, cache_l1_ckv, cache_l1_kpe, cache_l2_k, cache_l2_v, cache_l3_k, cache_l3_v, c, c_ctx, ada_w, ada_b, norm_g, mlp_w1, mlp_w2, attn_w_qkv, attn_q_norm, attn_k_norm, attn_w_o, mla_w_in, mla_q_norm, mla_kv_norm, mla_w_uq, mla_w_ukv, mla_w_o, swa_w_qkv, swa_sink, swa_w_o, nat_w_qkv, nat_rpb, nat_w_o):
    nb, seq, d = x_prompt.shape
    db, dseq, _ = x_sample.shape
    past = cache_l0_k.shape[1]
    ctx_b = nb * seq // dseq
    xp = x_prompt.reshape(ctx_b, dseq, d)
    xs = x_sample

    cond = jnp.zeros((COND_ROWS, d), F32).at[:db].set(c).at[CTX_MOD_ROW].set(c_ctx)
    mods = _modulation(cond, ada_w, ada_b).reshape(DEPTH, COND_ROWS, 1, N_MOD * d)

    row = lambda v: v.reshape(1, -1)

    def head_major(cache):
        b_, l_, h_, dh = cache.shape
        return cache.transpose(0, 2, 3, 1).reshape(b_, h_ * dh, l_)

    def token_major(state, heads):
        b_, w_, l_ = state.shape
        return state.reshape(b_, heads, w_ // heads, l_).transpose(0, 3, 1, 2)

    def finish(o, x, layer, wo, latent):
        return _finish_layer(o, x, mods[layer], norm_g[layer], wo, mlp_w1, mlp_w2, layer, latent)

    g = norm_g[0]
    w = attn_w_qkv
    wo = attn_w_o
    consts = [w, row(attn_q_norm), row(attn_k_norm)]
    kvw = ATTN_KV_HEADS * ATTN_HEAD_DIM
    o, l0_k, l0_v = _run_proj(
        _proj_a_kernel, False, xp, mods[0], g, consts, [], [(kvw, F32), (kvw, F32)], "proj_attn_gqa_context",
        seq, attend=functools.partial(_attn_a_kernel, False), qkv=[(d, BF16), (kvw, BF16), (kvw, BF16)])
    xp = finish(o, xp, 0, wo, False)
    perm = np.arange(ATTN_HEAD_DIM).reshape(2, 2, ATTN_HEAD_DIM // 4).transpose(1, 0, 2).reshape(-1)
    qk_heads = ATTN_HEADS + ATTN_KV_HEADS
    cols = (np.arange(qk_heads)[:, None] * ATTN_HEAD_DIM + perm[None, :]).reshape(-1)
    w_lat = jnp.concatenate([attn_w_qkv[:, cols], attn_w_qkv[:, qk_heads * ATTN_HEAD_DIM:]], axis=1).astype(BF16)
    consts_lat = [w_lat, row(attn_q_norm[perm]), row(attn_k_norm[perm])]
    tables = [t[:, perm] for t in _rope_tables(dseq, ATTN_HEAD_DIM)]
    q, k, v = _run_proj(_proj_a_kernel, True, xs, mods[0], g, consts_lat, tables,
                        [(d, BF16), (kvw, BF16), (kvw, BF16)], "proj_gqa_latent")
    o = _attention_a(q, k, v, cache_l0_k[..., perm].reshape(db, past, kvw).astype(BF16),
                     cache_l0_v.reshape(db, past, kvw).astype(BF16))
    xs = finish(o, xs, 0, wo, True)
    new_l0 = (l0_k.reshape(nb, seq, ATTN_KV_HEADS, ATTN_HEAD_DIM),
              l0_v.reshape(nb, seq, ATTN_KV_HEADS, ATTN_HEAD_DIM))

    g = norm_g[1]
    nq, nkv = MLA_Q_LORA, MLA_KV_LORA
    w_in = jnp.concatenate([mla_w_in[:, :nq + nkv]] + [mla_w_in[:, nq + nkv:]] * (LANES // MLA_ROPE),
                           axis=1).astype(BF16)
    wuq = mla_w_uq.reshape(nq, MLA_HEADS, MLA_NOPE + MLA_ROPE)
    wuq = jnp.concatenate([wuq[:, :, :MLA_NOPE].reshape(nq, -1), wuq[:, :, MLA_NOPE:].reshape(nq, -1)],
                          axis=1).astype(BF16)
    wukv = mla_w_ukv.reshape(nkv, MLA_HEADS, MLA_NOPE + MLA_V_DIM)
    wukv = jnp.concatenate([wukv[:, :, :MLA_NOPE].reshape(nkv, -1), wukv[:, :, MLA_NOPE:].reshape(nkv, -1)],
                           axis=1).astype(BF16)
    wo = mla_w_o
    consts = [w_in, row(mla_q_norm), row(mla_kv_norm), wuq, wukv]
    hw = MLA_HEADS * MLA_NOPE
    rw = MLA_HEADS * MLA_ROPE
    outs = [(hw, BF16), (rw, BF16), (hw, BF16), (hw, BF16), (LANES, BF16)]
    o, l1_ckv, l1_kpe = _run_proj(
        _proj_b_kernel, False, xp, mods[1], g, consts, [], [(nkv, F32), (MLA_ROPE, F32, "T")],
        "proj_attn_mla_context", seq, qkv=outs,
        attend=lambda qn, qr, kn, v, kp, o_ref: _attn_b_kernel(False, qn, qr, kn, kp, v, o_ref))
    xp = finish(o, xp, 1, wo, False)
    tables = list(_rope_tables(dseq, MLA_ROPE))
    qn, qr, kn, v, kp = _run_proj(_proj_b_kernel, True, xs, mods[1], g, consts, tables, outs,
                                  "proj_mla_latent")
    knc, vc = _mla_expand(cache_l1_ckv, wukv)
    kpc = jnp.tile(cache_l1_kpe, (1, 1, LANES // MLA_ROPE)).astype(BF16)
    o = _attention_b(qn, qr, kn, kp, v, knc, kpc, vc)
    xs = finish(o, xs, 1, wo, True)
    new_l1 = (l1_ckv.reshape(nb, seq, nkv), l1_kpe.transpose(0, 2, 1))

    g = norm_g[2]
    qw = SWA_HEADS * SWA_HEAD_DIM
    kw = SWA_KV_HEADS * SWA_HEAD_DIM
    wq, wk, wv = swa_w_qkv[:, :qw], swa_w_qkv[:, qw:qw + kw], swa_w_qkv[:, qw + kw:]
    wkd = _dup_heads(wk, SWA_KV_HEADS, SWA_HEAD_DIM)
    wvd = _dup_heads(wv, SWA_KV_HEADS, SWA_HEAD_DIM)
    w_ctx = jnp.concatenate([wq, wk, wv, wkd, wvd], axis=1).astype(BF16)
    w_lat = jnp.concatenate([wq, wkd, wvd], axis=1).astype(BF16)
    wo = swa_w_o
    o, l2_k, l2_v = _run_proj(
        _proj_c_kernel, False, xp, mods[2], g, [w_ctx], [], [(kw, F32, "T"), (kw, F32, "T")],
        "proj_attn_swa_context", seq, attend=functools.partial(_attn_c_kernel, False),
        attend_inputs=[_sink_logit_block(swa_sink, seq)], qkv=[(qw, BF16), (2 * kw, BF16), (2 * kw, BF16)])
    xp = finish(o, xp, 2, wo, False)
    tables = list(_rope_tables(dseq, SWA_HEAD_DIM))
    q, kd, vd = _run_proj(_proj_c_kernel, True, xs, mods[2], g, [w_lat], tables,
                          [(qw, BF16), (2 * kw, BF16), (2 * kw, BF16)], "proj_swa_latent")
    o = _attention_c(swa_sink, q, kd, vd, head_major(cache_l2_k), head_major(cache_l2_v))
    xs = finish(o, xs, 2, wo, True)
    new_l2 = (token_major(l2_k, SWA_KV_HEADS), token_major(l2_v, SWA_KV_HEADS))

    g = norm_g[3]
    hw = NAT_HEADS * NAT_HEAD_DIM
    w = nat_w_qkv
    wo = nat_w_o
    o, l3_k, l3_v = _run_proj(
        _proj_d_kernel, False, xp, mods[3], g, [w], [], [(hw, F32, "T"), (hw, F32, "T")],
        "proj_attn_nat_context", seq, attend=_attn_d_ctx_kernel, qkv=[(hw, BF16)] * 3)
    xp = finish(o, xp, 3, wo, False)
    q, k, v = _run_proj(_proj_d_kernel, True, xs, mods[3], g, [w], [],
                        [(hw, BF16), (hw, BF16), (hw, BF16)], "proj_nat_latent")
    o = _attention_d_lat(q, k, v, head_major(cache_l3_k), head_major(cache_l3_v),
                         _nat_dense_bias(nat_rpb, dseq // GRID_W))
    xs = finish(o, xs, 3, wo, True)
    new_l3 = (token_major(l3_k, NAT_HEADS), token_major(l3_v, NAT_HEADS))

    return (xp.reshape(nb, seq, d), xs) + new_l0 + new_l1 + new_l2 + new_l3
```

```python
import functools

import numpy as np

import jax
import jax.numpy as jnp
from jax import lax
from jax.experimental import pallas as pl
from jax.experimental.pallas import tpu as pltpu

F32 = jnp.float32
BF16 = jnp.bfloat16

D_MODEL = 1024
DEPTH = 4
N_MOD = 6
D_FF = 4 * D_MODEL
GRID_W = 64
ROPE_THETA = 10000.0
NORM_EPS = 1e-6
NEG_INF = -1e30
LOG2E = 1.4426950408889634

ATTN_HEADS, ATTN_KV_HEADS, ATTN_HEAD_DIM = 8, 2, 128
MLA_HEADS, MLA_Q_LORA, MLA_KV_LORA = 16, 384, 256
MLA_NOPE, MLA_ROPE, MLA_V_DIM = 64, 32, 64
MLA_SCALE = (MLA_NOPE + MLA_ROPE) ** -0.5
SWA_HEADS, SWA_KV_HEADS, SWA_HEAD_DIM, SWA_WINDOW = 16, 4, 64, 128
NAT_HEADS, NAT_HEAD_DIM, NAT_WIN_R, NAT_WIN_C = 16, 64, 8, 16

LANES = 128
COND_ROWS = 16
CTX_MOD_ROW = 8
VMEM_LIMIT = 48 * 1024 * 1024

MOD_COL_TILE = 1536
TOK_TILE = 512
WIDE_TOK_TILE = 1024
TOK_CHAIN = 256
MLP_TOK_TILE = 1024
MLP_FF_TILE = 1024
MLP_VMEM_LIMIT = 56 * 1024 * 1024
Q_TILE = 256
LATENT_Q_ROWS = 512
HEADS_PER_STEP = 16
NAT_TILE_ROWS = Q_TILE // GRID_W
NAT_KEY_ROWS = NAT_WIN_R + NAT_TILE_ROWS


def _params(*sem):
    return pltpu.CompilerParams(dimension_semantics=sem, vmem_limit_bytes=VMEM_LIMIT)


def _full(shape):
    nd = len(shape)
    return pl.BlockSpec(shape, lambda *_: (0,) * nd)


def _rms(x, g):
    return x * lax.rsqrt(jnp.mean(x * x, axis=-1, keepdims=True) + NORM_EPS) * g


def _mod(m, i):
    return m[:, i * D_MODEL:(i + 1) * D_MODEL]


def _premod(x, g, m, sub):
    return _rms(x, g * (1.0 + _mod(m, 3 * sub + 1))) + _mod(m, 3 * sub)


def _rope(x, cos, sin_signed, quarter):
    n = x.shape[-1]
    lane = lax.broadcasted_iota(jnp.int32, x.shape, 1)
    first = ((lane // quarter) % 2) == 0
    partner = jnp.where(first, pltpu.roll(x, n - quarter, 1), pltpu.roll(x, quarter, 1))
    return x * cos + partner * sin_signed


def _dot(a, b):
    return jnp.dot(a, b, preferred_element_type=F32)


def _dot_nt(a, b):
    return lax.dot_general(a, b, (((1,), (1,)), ((), ())), preferred_element_type=F32)


def _with_ones(v):
    return jnp.concatenate([v, jnp.ones_like(v)], axis=1)


def _attend(q, segs, sink=None):
    logits = []
    for seg in segs:
        k, bias, transposed = seg[0], seg[2], len(seg) > 3 and seg[3]
        s = _dot(q, k) if transposed else _dot_nt(q, k)
        logits.append(s if bias is None else s + bias)
    m = logits[0].max(axis=-1, keepdims=True)
    for s in logits[1:]:
        m = jnp.maximum(m, s.max(axis=-1, keepdims=True))
    if sink is not None:
        m = jnp.maximum(m, sink)
    acc = None
    for s, seg in zip(logits, segs):
        p = jnp.exp2(s - m).astype(BF16)
        pv = _dot_nt(p, seg[1]) if len(seg) > 3 and seg[3] else _dot(p, seg[1])
        acc = pv if acc is None else acc + pv
    den = acc[:, LANES:LANES + 1]
    if sink is not None:
        den = den + jnp.exp2(sink - m)
    return acc[:, :LANES] / den


def _attend_group(qs, ks, v1s, sink_logits=None):
    tq = qs[0].shape[0]
    s = jnp.concatenate([_dot_nt(q, k) for q, k in zip(qs, ks)], axis=0)
    if sink_logits is not None:
        s = jnp.concatenate([s, sink_logits], axis=1)
        zeros = jnp.zeros((LANES, LANES), BF16)
        tail = jnp.concatenate([zeros, jnp.ones_like(zeros)], axis=1)
        v1s = [jnp.concatenate([v1, tail], axis=0) for v1 in v1s]
    p = jnp.exp2(s - s.max(axis=-1, keepdims=True)).astype(BF16)
    acc = jnp.concatenate([_dot(p[i * tq:(i + 1) * tq], v1) for i, v1 in enumerate(v1s)], axis=0)
    return acc[:, :LANES] / acc[:, LANES:LANES + 1]


def _split_pair(q):
    low = lax.broadcasted_iota(jnp.int32, q.shape, 1) < (LANES // 2)
    zero = jnp.zeros_like(q)
    return [jnp.where(low, q, zero), jnp.where(low, zero, q)]


def _merge_pair(o0, o1):
    low = lax.broadcasted_iota(jnp.int32, o0.shape, 1) < (LANES // 2)
    return jnp.where(low, o0, o1).astype(BF16)


def _attend_pair(q, segs, sinks=None):
    outs = []
    for j, qj in enumerate(_split_pair(q)):
        segs_j = [(s[0], s[1], None if s[2] is None else s[2][j]) + tuple(s[3:]) for s in segs]
        outs.append(_attend(qj, segs_j, None if sinks is None else sinks[j]))
    return _merge_pair(*outs)


def _mods_kernel(cond_ref, w_ref, b_ref, o_ref):
    cnd = cond_ref[...]
    act = cnd * jax.nn.sigmoid(cnd)
    o_ref[...] = _dot(act.astype(BF16), w_ref[...].astype(BF16)) + b_ref[...]


def _modulation(cond, ada_w, ada_b):
    tn = MOD_COL_TILE
    n = N_MOD * D_MODEL
    return pl.pallas_call(
        _mods_kernel,
        grid=(DEPTH, n // tn),
        in_specs=[
            _full((COND_ROWS, D_MODEL)),
            pl.BlockSpec((None, D_MODEL, tn), lambda l, j: (l, 0, j)),
            pl.BlockSpec((None, 1, tn), lambda l, j: (l, 0, j)),
        ],
        out_specs=pl.BlockSpec((None, COND_ROWS, tn), lambda l, j: (l, 0, j)),
        out_shape=jax.ShapeDtypeStruct((DEPTH, COND_ROWS, n), F32),
        compiler_params=_params("arbitrary", "arbitrary"),
        name="adaln_mods",
    )(cond, ada_w, ada_b.reshape(DEPTH, 1, n))


def _proj_a_kernel(latent, x_ref, m_ref, g_ref, w_ref, qn_ref, kn_ref, *rest):
    if latent:
        cos_ref, sin_ref, q_ref, k_ref, v_ref = rest
    else:
        q_ref, k_ref, v_ref, ks_ref, vs_ref = rest
    hd = ATTN_HEAD_DIM
    n_qk = ATTN_HEADS + ATTN_KV_HEADS
    h = _premod(x_ref[...], g_ref[0:1, :], m_ref[...], 0).astype(BF16)
    w = w_ref[...].astype(BF16)
    for c in range(x_ref.shape[0] // TOK_CHAIN):
        rows = slice(c * TOK_CHAIN, (c + 1) * TOK_CHAIN)
        z = _dot(h[rows], w)
        ys = [_rms(z[:, i * hd:(i + 1) * hd], qn_ref[...] if i < ATTN_HEADS else kn_ref[...]) for i in range(n_qk)]
        if not latent:
            for j in range(ATTN_KV_HEADS):
                ks_ref[rows, j * hd:(j + 1) * hd] = ys[ATTN_HEADS + j]
        else:
            ys = [y * cos_ref[rows, :] + pltpu.roll(y, hd // 2, 1) * sin_ref[rows, :] for y in ys]
        for i in range(ATTN_HEADS):
            q_ref[rows, i * hd:(i + 1) * hd] = (ys[i] * (hd ** -0.5 * LOG2E)).astype(BF16)
        for j in range(ATTN_KV_HEADS):
            k_ref[rows, j * hd:(j + 1) * hd] = ys[ATTN_HEADS + j].astype(BF16)
        v = z[:, n_qk * hd:]
        v_ref[rows, :] = v.astype(BF16)
        if not latent:
            vs_ref[rows, :] = v


def _proj_b_kernel(latent, x_ref, m_ref, g_ref, win_ref, qn_ref, kvn_ref, wuq_ref, wukv_ref, *rest):
    if latent:
        cos_ref, sin_ref, qn_o, qr_o, kn_o, v_o, kp_o = rest
    else:
        qn_o, qr_o, kn_o, v_o, kp_o, ckv_s, kpe_s = rest
    nq, nkv = MLA_Q_LORA, MLA_KV_LORA
    nope_w = MLA_HEADS * MLA_NOPE
    scale = MLA_SCALE * LOG2E
    h = _premod(x_ref[...], g_ref[0:1, :], m_ref[...], 0)
    z = _dot(h.astype(BF16), win_ref[...])
    cq = _rms(z[:, :nq], qn_ref[...])
    ckv = _rms(z[:, nq:nq + nkv], kvn_ref[...])
    kp = z[:, nq + nkv:]
    q = _dot(cq.astype(BF16), wuq_ref[...])
    kv = _dot(ckv.astype(BF16), wukv_ref[...])
    if not latent:
        ckv_s[...] = ckv
        _store_transposed(kpe_s, kp, MLA_ROPE)
    qn_o[...] = (q[:, :nope_w] * scale).astype(BF16)
    for i in range(MLA_HEADS * MLA_ROPE // LANES):
        qr = q[:, nope_w + i * LANES:nope_w + (i + 1) * LANES]
        if latent:
            qr = _rope(qr, cos_ref[...], sin_ref[...], MLA_ROPE // 4)
        qr_o[:, i * LANES:(i + 1) * LANES] = (qr * scale).astype(BF16)
    if latent:
        kp = _rope(kp, cos_ref[...], sin_ref[...], MLA_ROPE // 4)
    kp_o[...] = kp.astype(BF16)
    kn_o[...] = kv[:, :nope_w].astype(BF16)
    v_o[...] = kv[:, nope_w:].astype(BF16)


def _proj_c_kernel(latent, x_ref, m_ref, g_ref, w_ref, *rest):
    if latent:
        cos_ref, sin_ref, q_o, kd_o, vd_o = rest
    else:
        q_o, kd_o, vd_o, ks_o, vs_o = rest
    qw = SWA_HEADS * SWA_HEAD_DIM
    kw = SWA_KV_HEADS * SWA_HEAD_DIM
    h = _premod(x_ref[...], g_ref[0:1, :], m_ref[...], 0)
    z = _dot(h.astype(BF16), w_ref[...])
    off = qw
    if not latent:
        _store_transposed(ks_o, z[:, qw:qw + kw], kw)
        _store_transposed(vs_o, z[:, qw + kw:qw + 2 * kw], kw)
        off = qw + 2 * kw
    scale = SWA_HEAD_DIM ** -0.5 * LOG2E
    for i in range(qw // LANES):
        y = z[:, i * LANES:(i + 1) * LANES]
        if latent:
            y = _rope(y, cos_ref[...], sin_ref[...], SWA_HEAD_DIM // 4)
        q_o[:, i * LANES:(i + 1) * LANES] = (y * scale).astype(BF16)
    for i in range(2 * kw // LANES):
        y = z[:, off + i * LANES:off + (i + 1) * LANES]
        if latent:
            y = _rope(y, cos_ref[...], sin_ref[...], SWA_HEAD_DIM // 4)
        kd_o[:, i * LANES:(i + 1) * LANES] = y.astype(BF16)
    vd_o[...] = z[:, off + 2 * kw:].astype(BF16)


def _proj_d_kernel(latent, x_ref, m_ref, g_ref, w_ref, *rest):
    if latent:
        q_o, k_o, v_o = rest
    else:
        q_o, k_o, v_o, ks_o, vs_o = rest
    hw = NAT_HEADS * NAT_HEAD_DIM
    h = _premod(x_ref[...], g_ref[0:1, :], m_ref[...], 0)
    z = _dot(h.astype(BF16), w_ref[...].astype(BF16))
    q_o[...] = (z[:, :hw] * (NAT_HEAD_DIM ** -0.5 * LOG2E)).astype(BF16)
    k_o[...] = z[:, hw:2 * hw].astype(BF16)
    v_o[...] = z[:, 2 * hw:].astype(BF16)
    if not latent:
        _store_transposed(ks_o, z[:, hw:2 * hw], hw)
        _store_transposed(vs_o, z[:, 2 * hw:], hw)


def _store_transposed(ref, val, width):
    n, _, seq = ref.shape
    for i in range(n):
        ref[i] = val[i * seq:(i + 1) * seq, :].T[:width, :]


def _run_proj(kernel, latent, x, mods_l, g, consts, tables, outs, name, state_seq=None,
              attend=None, attend_inputs=(), qkv=(), tile=TOK_TILE):
    bx, s, _ = x.shape
    tm = tile
    row = (lambda b, j: (b, 0, 0)) if latent else (lambda b, j: (CTX_MOD_ROW, 0, 0))
    in_specs = [
        pl.BlockSpec((None, tm, D_MODEL), lambda b, j: (b, j, 0)),
        pl.BlockSpec((None, 1, N_MOD * D_MODEL), row),
        _full(g.shape),
    ] + [_full(c.shape) for c in consts]
    in_specs += [pl.BlockSpec((tm, LANES), lambda b, j: (j, 0)) for _ in tables]
    in_specs += [_full(a.shape) for a in attend_inputs]
    body = functools.partial(kernel, latent)
    if attend is not None:
        n_proj_in = 3 + len(consts) + len(tables)
        n_att_in = len(attend_inputs)
        proj_body = body
        outs = [(D_MODEL, BF16)] + list(outs)

        def body(*refs):
            ins, att_ins = refs[:n_proj_in], refs[n_proj_in:n_proj_in + n_att_in]
            o_ref = refs[n_proj_in + n_att_in]
            states, scratch = refs[n_proj_in + n_att_in + 1:-len(qkv)], refs[-len(qkv):]
            proj_body(*ins, *scratch, *states)
            for i in range(tm // state_seq):
                rows = slice(i * state_seq, (i + 1) * state_seq)
                attend(*att_ins, *[r.at[rows] for r in scratch], o_ref.at[rows])

    out_specs, out_shape = [], []
    for o in outs:
        if len(o) == 2:
            out_specs.append(pl.BlockSpec((None, tm, o[0]), lambda b, j: (b, j, 0)))
            out_shape.append(jax.ShapeDtypeStruct((bx, s, o[0]), o[1]))
        else:
            per_tile = tm // state_seq
            out_specs.append(pl.BlockSpec((per_tile, o[0], state_seq),
                                          lambda b, j: (b * (s // tm) + j, 0, 0)))
            out_shape.append(jax.ShapeDtypeStruct((bx * s // state_seq, o[0], state_seq), o[1]))
    return pl.pallas_call(
        body,
        grid=(bx, s // tm),
        in_specs=in_specs,
        out_specs=out_specs,
        out_shape=out_shape,
        scratch_shapes=[pltpu.VMEM((tm, w), dt) for w, dt in qkv],
        compiler_params=_params("arbitrary", "arbitrary"),
        name=name,
    )(x, mods_l, g, *consts, *tables, *attend_inputs)


def _attn_a_kernel(has_ctx, q_ref, k_ref, v_ref, *rest):
    if has_ctx:
        kc_ref, vc_ref, o_ref = rest
    else:
        (o_ref,) = rest
    group = ATTN_HEADS // ATTN_KV_HEADS
    head_cols = [slice(h * LANES, (h + 1) * LANES) for h in range(ATTN_HEADS)]
    kv_cols = [slice(j * LANES, (j + 1) * LANES) for j in range(ATTN_KV_HEADS)]
    if not has_ctx:
        tq = q_ref.shape[0]
        qs = [jnp.concatenate([q_ref[:, c] for c in head_cols[j * group:(j + 1) * group]], axis=0)
              for j in range(ATTN_KV_HEADS)]
        out = _attend_group(qs, [k_ref[:, c] for c in kv_cols], [_with_ones(v_ref[:, c]) for c in kv_cols])
        for h, c in enumerate(head_cols):
            o_ref[:, c] = out[h * tq:(h + 1) * tq].astype(BF16)
        return
    for j, kc in enumerate(kv_cols):
        segs = [(k_ref[:, kc], _with_ones(v_ref[:, kc]), None),
                (kc_ref[:, kc], _with_ones(vc_ref[:, kc]), None)]
        for r in range(q_ref.shape[0] // Q_TILE):
            rows = slice(r * Q_TILE, (r + 1) * Q_TILE)
            for c in head_cols[j * group:(j + 1) * group]:
                o_ref[rows, c] = _attend(q_ref[rows, c], segs).astype(BF16)


def _attention_a(q, k, v, kc=None, vc=None):
    b, s, qw = q.shape
    kl, kw = k.shape[1:]
    has_ctx = kc is not None
    tq = min(LATENT_Q_ROWS if has_ctx else Q_TILE, s)
    in_specs = [
        pl.BlockSpec((None, tq, qw), lambda bi, t: (bi, t, 0)),
        pl.BlockSpec((None, kl, kw), lambda bi, t: (bi, 0, 0)),
        pl.BlockSpec((None, kl, kw), lambda bi, t: (bi, 0, 0)),
    ]
    args = [q, k, v]
    if has_ctx:
        cl = kc.shape[1]
        in_specs += [pl.BlockSpec((None, cl, kw), lambda bi, t: (bi, 0, 0))] * 2
        args += [kc, vc]
    return pl.pallas_call(
        functools.partial(_attn_a_kernel, has_ctx),
        grid=(b, s // tq),
        in_specs=in_specs,
        out_specs=pl.BlockSpec((None, tq, qw), lambda bi, t: (bi, t, 0)),
        out_shape=jax.ShapeDtypeStruct(q.shape, BF16),
        compiler_params=_params("arbitrary", "arbitrary"),
        name="attn_gqa_latent" if has_ctx else "attn_gqa_context",
    )(*args)


def _attn_b_kernel(has_ctx, qn_ref, qr_ref, kn_ref, kp_ref, v_ref, *rest):
    if has_ctx:
        knc_ref, kpc_ref, vc_ref, o_ref = rest
    else:
        (o_ref,) = rest
    tq = min(Q_TILE, qn_ref.shape[0])
    lane = lax.broadcasted_iota(jnp.int32, (tq, LANES), 1)
    low = lane < (LANES // 2)
    quarter = lane // MLA_ROPE
    zero = jnp.zeros((tq, LANES), BF16)
    kp = kp_ref[...]
    kpc = kpc_ref[...] if has_ctx else None
    pairs_per_rope = LANES // MLA_ROPE // 2
    n_pairs = HEADS_PER_STEP // 2
    pair_cols = [slice(p * LANES, (p + 1) * LANES) for p in range(n_pairs)]

    def pair_lhs(p, rows=slice(None)):
        qn = qn_ref[rows, pair_cols[p]]
        rg = p // pairs_per_rope
        qr = qr_ref[rows, rg * LANES:(rg + 1) * LANES]
        return [jnp.concatenate([jnp.where(low if j == 0 else ~low, qn, zero),
                                 jnp.where(quarter == 2 * (p % pairs_per_rope) + j, qr, zero)], axis=1)
                for j in range(2)]

    if not has_ctx:
        qs = [jnp.concatenate(pair_lhs(p), axis=0) for p in range(n_pairs)]
        out = _attend_group(qs, [jnp.concatenate([kn_ref[:, c], kp], axis=1) for c in pair_cols],
                            [_with_ones(v_ref[:, c]) for c in pair_cols])
        for p, c in enumerate(pair_cols):
            o_ref[:, c] = _merge_pair(out[2 * p * tq:(2 * p + 1) * tq], out[(2 * p + 1) * tq:(2 * p + 2) * tq])
        return
    for p, cols in enumerate(pair_cols):
        segs = [(jnp.concatenate([kn_ref[:, cols], kp], axis=1), _with_ones(v_ref[:, cols]), None),
                (jnp.concatenate([knc_ref[:, cols], kpc], axis=1), _with_ones(vc_ref[:, cols]), None)]
        for r in range(qn_ref.shape[0] // tq):
            rows = slice(r * tq, (r + 1) * tq)
            o_ref[rows, cols] = _merge_pair(*[_attend(lhs, segs) for lhs in pair_lhs(p, rows)])


def _attention_b(qn, qr, kn, kp, v, knc=None, kpc=None, vc=None):
    b, s, _ = qn.shape
    kl = kn.shape[1]
    has_ctx = knc is not None
    tq = min(LATENT_Q_ROWS if has_ctx else Q_TILE, s)
    gw = HEADS_PER_STEP * MLA_NOPE
    blk = lambda rows, w, fn: pl.BlockSpec((None, rows, w), fn)
    in_specs = [
        blk(tq, gw, lambda bi, g, t: (bi, t, g)),
        blk(tq, HEADS_PER_STEP * MLA_ROPE, lambda bi, g, t: (bi, t, g)),
        blk(kl, gw, lambda bi, g, t: (bi, 0, g)),
        blk(kl, LANES, lambda bi, g, t: (bi, 0, 0)),
        blk(kl, gw, lambda bi, g, t: (bi, 0, g)),
    ]
    args = [qn, qr, kn, kp, v]
    if has_ctx:
        cl = knc.shape[1]
        in_specs += [
            blk(cl, gw, lambda bi, g, t: (bi, 0, g)),
            blk(cl, LANES, lambda bi, g, t: (bi, 0, 0)),
            blk(cl, gw, lambda bi, g, t: (bi, 0, g)),
        ]
        args += [knc, kpc, vc]
    return pl.pallas_call(
        functools.partial(_attn_b_kernel, has_ctx),
        grid=(b, MLA_HEADS // HEADS_PER_STEP, s // tq),
        in_specs=in_specs,
        out_specs=blk(tq, gw, lambda bi, g, t: (bi, t, g)),
        out_shape=jax.ShapeDtypeStruct(qn.shape, BF16),
        compiler_params=_params("arbitrary", "arbitrary", "arbitrary"),
        name="attn_mla_latent" if has_ctx else "attn_mla_context",
    )(*args)


def _attn_c_kernel(latent, sink_ref, q_ref, kd_ref, vd_ref, *rest):
    if latent:
        kc_ref, vc_ref, o_ref = rest
    else:
        (o_ref,) = rest
    tq = min(Q_TILE, q_ref.shape[0])
    group = SWA_HEADS // SWA_KV_HEADS
    kv_per_step = HEADS_PER_STEP // group
    head0 = pl.program_id(1) * HEADS_PER_STEP
    windows = []
    if latent:
        s_len = kd_ref.shape[0]
        win = tq + 2 * SWA_WINDOW
        for r in range(q_ref.shape[0] // tq):
            t0 = pl.program_id(2) * q_ref.shape[0] + r * tq
            ws = pl.multiple_of(jnp.clip(t0 - SWA_WINDOW, 0, s_len - win), LANES)
            qpos = t0 + lax.broadcasted_iota(jnp.int32, (tq, win), 0)
            kpos = ws + lax.broadcasted_iota(jnp.int32, (tq, win), 1)
            bias = jnp.where(jnp.abs(qpos - kpos) <= SWA_WINDOW, 0.0, NEG_INF).astype(F32)
            windows.append((slice(r * tq, (r + 1) * tq), ws, bias))
    pairs_per_kv = group // 2
    pair_cols = [slice(p * LANES, (p + 1) * LANES) for p in range(HEADS_PER_STEP // 2)]
    kv_cols = [slice(j * LANES, (j + 1) * LANES) for j in range(kv_per_step)]
    if not latent:
        qs = [jnp.concatenate([h for c in pair_cols[j * pairs_per_kv:(j + 1) * pairs_per_kv]
                               for h in _split_pair(q_ref[:, c])], axis=0) for j in range(kv_per_step)]
        out = _attend_group(qs, [kd_ref[:, c] for c in kv_cols], [_with_ones(vd_ref[:, c]) for c in kv_cols],
                            sink_ref[...])
        for p, c in enumerate(pair_cols):
            o_ref[:, c] = _merge_pair(out[2 * p * tq:(2 * p + 1) * tq], out[(2 * p + 1) * tq:(2 * p + 2) * tq])
        return
    sinks = [sink_ref[head0 + h] * LOG2E for h in range(HEADS_PER_STEP)]
    hd = SWA_HEAD_DIM
    for j, kc in enumerate(kv_cols):
        kct = kc_ref[j * hd:(j + 1) * hd, :].astype(BF16)
        vct = vc_ref[j * hd:(j + 1) * hd, :].astype(BF16)
        ctx_seg = (jnp.concatenate([kct, kct], axis=0),
                   jnp.concatenate([vct, vct, jnp.ones((LANES, vct.shape[1]), BF16)], axis=0), None, True)
        for rows, ws, bias in windows:
            segs = [(kd_ref[pl.ds(ws, win), kc], _with_ones(vd_ref[pl.ds(ws, win), kc]), (bias, bias)), ctx_seg]
            for p in range(j * pairs_per_kv, (j + 1) * pairs_per_kv):
                o_ref[rows, pair_cols[p]] = _attend_pair(q_ref[rows, pair_cols[p]], segs, sinks[2 * p:2 * p + 2])


def _sink_logit_block(sink, tq):
    rows = jnp.repeat(sink * LOG2E, tq)[:, None]
    return jnp.where(jnp.arange(LANES)[None, :] == 0, rows, NEG_INF)


def _attention_c(sink, q, kd, vd, kc, vc):
    b, s, _ = q.shape
    kl = kd.shape[1]
    tq = min(LATENT_Q_ROWS, s)
    gw = HEADS_PER_STEP * SWA_HEAD_DIM
    kvw = 2 * SWA_HEAD_DIM * HEADS_PER_STEP // (SWA_HEADS // SWA_KV_HEADS)
    cw, cl = kc.shape[1] * HEADS_PER_STEP // SWA_HEADS, kc.shape[2]
    return pl.pallas_call(
        functools.partial(_attn_c_kernel, True),
        grid=(b, SWA_HEADS // HEADS_PER_STEP, s // tq),
        in_specs=[
            pl.BlockSpec(memory_space=pltpu.SMEM),
            pl.BlockSpec((None, tq, gw), lambda bi, g, t: (bi, t, g)),
            pl.BlockSpec((None, kl, kvw), lambda bi, g, t: (bi, 0, g)),
            pl.BlockSpec((None, kl, kvw), lambda bi, g, t: (bi, 0, g)),
            pl.BlockSpec((None, cw, cl), lambda bi, g, t: (bi, g, 0)),
            pl.BlockSpec((None, cw, cl), lambda bi, g, t: (bi, g, 0)),
        ],
        out_specs=pl.BlockSpec((None, tq, gw), lambda bi, g, t: (bi, t, g)),
        out_shape=jax.ShapeDtypeStruct(q.shape, BF16),
        compiler_params=_params("arbitrary", "arbitrary", "arbitrary"),
        name="attn_swa_latent",
    )(sink, q, kd, vd, kc, vc)


def _attn_d_ctx_kernel(q_ref, k_ref, v_ref, o_ref):
    tq = q_ref.shape[0]
    pair_cols = [slice(p * LANES, (p + 1) * LANES) for p in range(HEADS_PER_STEP // 2)]
    qs = [jnp.concatenate(_split_pair(q_ref[:, c]), axis=0) for c in pair_cols]
    out = _attend_group(qs, [k_ref[:, c] for c in pair_cols], [_with_ones(v_ref[:, c]) for c in pair_cols])
    for p, c in enumerate(pair_cols):
        o_ref[:, c] = _merge_pair(out[2 * p * tq:(2 * p + 1) * tq], out[(2 * p + 1) * tq:(2 * p + 2) * tq])


def _nat_row0(r, rows):
    return min(max(r - NAT_WIN_R // 2, 0), rows - NAT_WIN_R)


def _nat_tile_key_rows(tile, rows):
    first, last = tile * NAT_TILE_ROWS, (tile + 1) * NAT_TILE_ROWS - 1
    need = _nat_row0(last, rows) + NAT_WIN_R - _nat_row0(first, rows)
    return need + need % 2


def _nat_tile_start(tile, rows):
    return min(_nat_row0(tile * NAT_TILE_ROWS, rows), rows - _nat_tile_key_rows(tile, rows))


def _attn_d_lat_kernel(q_ref, k_ref, v_ref, kc_ref, vc_ref, bias_ref, o_ref):
    rows = k_ref.shape[0] // GRID_W
    tile = pl.program_id(0)
    ones = jnp.ones((LANES, kc_ref.shape[1]), BF16)

    def body(key_rows, start_row):
        slab = key_rows * GRID_W
        start = pl.multiple_of(start_row * GRID_W, GRID_W)
        for p in range(HEADS_PER_STEP // 2):
            cols = slice(p * LANES, (p + 1) * LANES)
            segs = [(k_ref[pl.ds(start, slab), cols], _with_ones(v_ref[pl.ds(start, slab), cols]),
                     (bias_ref[2 * p, :, :slab], bias_ref[2 * p + 1, :, :slab])),
                    (kc_ref[cols, :].astype(BF16),
                     jnp.concatenate([vc_ref[cols, :].astype(BF16), ones], axis=0), None, True)]
            o_ref[:, cols] = _attend_pair(q_ref[:, cols], segs)

    tiles = range(rows // NAT_TILE_ROWS)
    for key_rows in sorted({_nat_tile_key_rows(t, rows) for t in tiles}):
        members = [t for t in tiles if _nat_tile_key_rows(t, rows) == key_rows]
        cond = functools.reduce(jnp.logical_or, [tile == t for t in members])
        start_row = functools.reduce(lambda acc, t: jnp.where(tile == t, _nat_tile_start(t, rows), acc),
                                     members, jnp.int32(0))
        pl.when(cond)(functools.partial(body, key_rows, start_row))


def _attention_d_lat(q, k, v, kc, vc, bias):
    b, s, _ = q.shape
    cl = kc.shape[2]
    tq = NAT_TILE_ROWS * GRID_W
    slab = NAT_KEY_ROWS * GRID_W
    gw = HEADS_PER_STEP * NAT_HEAD_DIM
    return pl.pallas_call(
        _attn_d_lat_kernel,
        grid=(s // tq, NAT_HEADS // HEADS_PER_STEP, b),
        in_specs=[
            pl.BlockSpec((None, tq, gw), lambda t, g, bi: (bi, t, g)),
            pl.BlockSpec((None, s, gw), lambda t, g, bi: (bi, 0, g)),
            pl.BlockSpec((None, s, gw), lambda t, g, bi: (bi, 0, g)),
            pl.BlockSpec((None, gw, cl), lambda t, g, bi: (bi, g, 0)),
            pl.BlockSpec((None, gw, cl), lambda t, g, bi: (bi, g, 0)),
            pl.BlockSpec((HEADS_PER_STEP, None, tq, slab), lambda t, g, bi: (g, t, 0, 0)),
        ],
        out_specs=pl.BlockSpec((None, tq, gw), lambda t, g, bi: (bi, t, g)),
        out_shape=jax.ShapeDtypeStruct(q.shape, BF16),
        compiler_params=_params("arbitrary", "arbitrary", "arbitrary"),
        name="attn_nat_latent",
    )(q, k, v, kc, vc, bias)


def _mla_expand_kernel(c_ref, w_ref, kn_ref, v_ref):
    kv = _dot(c_ref[...].astype(BF16), w_ref[...])
    half = kv.shape[1] // 2
    kn_ref[...] = kv[:, :half].astype(BF16)
    v_ref[...] = kv[:, half:].astype(BF16)


def _mla_expand(ckv, w_ukv):
    b, l, c = ckv.shape
    n = w_ukv.shape[1] // 2
    out = pl.BlockSpec((None, l, n), lambda bi: (bi, 0, 0))
    return pl.pallas_call(
        _mla_expand_kernel,
        grid=(b,),
        in_specs=[pl.BlockSpec((None, l, c), lambda bi: (bi, 0, 0)), _full(w_ukv.shape)],
        out_specs=[out, out],
        out_shape=[jax.ShapeDtypeStruct((b, l, n), BF16)] * 2,
        compiler_params=_params("arbitrary"),
        name="mla_expand_cache",
    )(ckv, w_ukv)


def _finish_kernel(o_ref, x_ref, m_ref, g_ref, wo_ref, w1_ref, w2_ref, y_ref, x1_ref, h2_ref):
    f = pl.program_id(2)
    last = pl.num_programs(2) - 1
    chains = [slice(c * TOK_CHAIN, (c + 1) * TOK_CHAIN) for c in range(y_ref.shape[0] // TOK_CHAIN)]

    def mlp_part(h, w1, w2):
        u = jnp.square(jnp.maximum(_dot(h, w1), 0.0)).astype(BF16)
        return _dot(u, w2)

    @pl.when(f == 0)
    def _():
        m = m_ref[...]
        wo = wo_ref[...].astype(BF16)
        for rows in chains:
            a = _dot(o_ref[rows, :], wo)
            x1 = x_ref[rows, :] + _rms(a, _mod(m, 2) * g_ref[1:2, :])
            x1_ref[rows, :] = x1
            h2_ref[rows, :] = _premod(x1, g_ref[2:3, :], m, 1).astype(BF16)
        y_ref[...] = mlp_part(h2_ref[...], w1_ref[...].astype(BF16), w2_ref[...].astype(BF16))

    @pl.when((f > 0) & (f < last))
    def _():
        y_ref[...] += mlp_part(h2_ref[...], w1_ref[...].astype(BF16), w2_ref[...].astype(BF16))

    @pl.when(f == last)
    def _():
        gain = _mod(m_ref[...], 5) * g_ref[3:4, :]
        w1, w2 = w1_ref[...].astype(BF16), w2_ref[...].astype(BF16)
        for rows in chains:
            y = y_ref[rows, :] + mlp_part(h2_ref[rows, :], w1, w2)
            y_ref[rows, :] = x1_ref[rows, :] + _rms(y, gain)


def _finish_layer(o, x, mods_l, g, wo, w1, w2, layer, latent):
    bx, s, _ = x.shape
    tm, tf = MLP_TOK_TILE, MLP_FF_TILE
    row = (lambda b, j, f: (b, 0, 0)) if latent else (lambda b, j, f: (CTX_MOD_ROW, 0, 0))
    tok = lambda w: pl.BlockSpec((None, tm, w), lambda b, j, f: (b, j, 0))
    return pl.pallas_call(
        _finish_kernel,
        grid=(bx, s // tm, D_FF // tf),
        in_specs=[tok(o.shape[-1]), tok(D_MODEL),
                  pl.BlockSpec((None, 1, N_MOD * D_MODEL), row), _full(g.shape), _full(wo.shape),
                  pl.BlockSpec((None, D_MODEL, tf), lambda b, j, f: (layer, 0, f)),
                  pl.BlockSpec((None, tf, D_MODEL), lambda b, j, f: (layer, f, 0))],
        out_specs=tok(D_MODEL),
        out_shape=jax.ShapeDtypeStruct(x.shape, F32),
        scratch_shapes=[pltpu.VMEM((tm, D_MODEL), F32), pltpu.VMEM((tm, D_MODEL), BF16)],
        compiler_params=pltpu.CompilerParams(dimension_semantics=("arbitrary",) * 3,
                                             vmem_limit_bytes=MLP_VMEM_LIMIT),
        name="out_proj_mlp",
    )(o, x, mods_l, g, wo, w1, w2)


def _rope_tables(s, dim):
    quarter = dim // 4
    t = jnp.arange(s)
    pos = jnp.stack([t // GRID_W, t % GRID_W], axis=-1).astype(F32)
    inv = ROPE_THETA ** (-jnp.arange(quarter, dtype=F32) / quarter)
    ang = pos[:, :, None] * inv
    cos = jnp.broadcast_to(jnp.cos(ang)[:, :, None, :], (s, 2, 2, quarter)).reshape(s, dim)
    sign = jnp.array([-1.0, 1.0], F32)[None, None, :, None]
    sin = (jnp.sin(ang)[:, :, None, :] * sign).reshape(s, dim)
    reps = LANES // dim
    return jnp.tile(cos, (1, reps)), jnp.tile(sin, (1, reps))


def _dup_heads(w, heads, dim):
    lead = w.shape[:-1]
    w = w.reshape(lead + (heads, 1, dim))
    return jnp.broadcast_to(w, lead + (heads, 2, dim)).reshape(lead + (heads * 2 * dim,))


def _nat_dense_bias(rpb, rows):
    heads = rpb.shape[0]
    c = np.arange(GRID_W)
    c0 = np.clip(c - NAT_WIN_C // 2, 0, GRID_W - NAT_WIN_C)
    in_c = (c[None, :] >= c0[:, None]) & (c[None, :] < c0[:, None] + NAT_WIN_C)
    dc = c[None, :] - c[:, None] + NAT_WIN_C - 1
    onehot = (dc[None] == np.arange(2 * NAT_WIN_C - 1)[:, None, None]) & in_c[None]
    toe = jnp.einsum("had,dck->hack", rpb * LOG2E, jnp.asarray(onehot, F32),
                     precision=lax.Precision.HIGHEST)
    toe = jnp.where(jnp.asarray(in_c)[None, None], toe, NEG_INF)
    pad = jnp.full((heads, 1, GRID_W, GRID_W), NEG_INF, F32)
    ext = jnp.concatenate([pad, toe, pad], axis=1)
    pairs = jnp.concatenate([ext[:, :-1], ext[:, 1:]], axis=-1)
    tiles = rows // NAT_TILE_ROWS
    tq, slab = NAT_TILE_ROWS * GRID_W, NAT_KEY_ROWS * GRID_W
    n_off = 2 * NAT_WIN_R
    return pl.pallas_call(
        functools.partial(_nat_bias_kernel, rows),
        grid=(heads,),
        in_specs=[pl.BlockSpec((None, n_off, GRID_W, LANES), lambda h: (h, 0, 0, 0))],
        out_specs=pl.BlockSpec((None, tiles, tq, slab), lambda h: (h, 0, 0, 0)),
        out_shape=jax.ShapeDtypeStruct((heads, tiles, tq, slab), F32),
        compiler_params=_params("arbitrary"),
        name="nat_bias_expand",
    )(pairs)


def _nat_bias_kernel(rows, pairs_ref, o_ref):
    low = lax.broadcasted_iota(jnp.int32, (GRID_W, LANES), 1) < GRID_W
    masked = jnp.full((GRID_W, LANES), NEG_INF, F32)
    for r in range(rows):
        tile, i = divmod(r, NAT_TILE_ROWS)
        r0 = _nat_row0(r, rows)
        ws = _nat_tile_start(tile, rows)
        for jb in range(NAT_KEY_ROWS // 2):
            kr = ws + 2 * jb
            ok_lo, ok_hi = r0 <= kr < r0 + NAT_WIN_R, r0 <= kr + 1 < r0 + NAT_WIN_R
            blk = masked
            if ok_lo or ok_hi:
                blk = pairs_ref[kr - r + NAT_WIN_R]
                if not ok_lo:
                    blk = jnp.where(low, NEG_INF, blk)
                elif not ok_hi:
                    blk = jnp.where(low, blk, NEG_INF)
            o_ref[tile, i * GRID_W:(i + 1) * GRID_W, jb * LANES:(jb + 1) * LANES] = blk


def kernel(x_prompt, x_sample, cache_l0_k, cache_l0_v, cache_l1_ckv, cache_l1_kpe, cache_l2_k, cache_l2_v, cache_l3_k, cache_l3_v, c, c_ctx, ada_w, ada_b, norm_g, mlp_w1, mlp_w2, attn_w_qkv, attn_q_norm, attn_k_norm, attn_w_o, mla_w_in, mla_q_norm, mla_kv_norm, mla_w_uq, mla_w_ukv, mla_w_o, swa_w_qkv, swa_sink, swa_w_o, nat_w_qkv, nat_rpb, nat_w_o):
    nb, seq, d = x_prompt.shape
    db, dseq, _ = x_sample.shape
    past = cache_l0_k.shape[1]
    ctx_b = nb * seq // dseq
    xp = x_prompt.reshape(ctx_b, dseq, d)
    xs = x_sample

    cond = jnp.zeros((COND_ROWS, d), F32).at[:db].set(c).at[CTX_MOD_ROW].set(c_ctx)
    mods = _modulation(cond, ada_w, ada_b).reshape(DEPTH, COND_ROWS, 1, N_MOD * d)

    row = lambda v: v.reshape(1, -1)

    def head_major(cache):
        b_, l_, h_, dh = cache.shape
        return cache.transpose(0, 2, 3, 1).reshape(b_, h_ * dh, l_)

    def token_major(state, heads):
        b_, w_, l_ = state.shape
        return state.reshape(b_, heads, w_ // heads, l_).transpose(0, 3, 1, 2)

    def finish(o, x, layer, wo, latent):
        return _finish_layer(o, x, mods[layer], norm_g[layer], wo, mlp_w1, mlp_w2, layer, latent)

    g = norm_g[0]
    w = attn_w_qkv
    wo = attn_w_o
    consts = [w, row(attn_q_norm), row(attn_k_norm)]
    kvw = ATTN_KV_HEADS * ATTN_HEAD_DIM
    o, l0_k, l0_v = _run_proj(
        _proj_a_kernel, False, xp, mods[0], g, consts, [], [(kvw, F32), (kvw, F32)], "proj_attn_gqa_context",
        seq, attend=functools.partial(_attn_a_kernel, False), qkv=[(d, BF16), (kvw, BF16), (kvw, BF16)])
    xp = finish(o, xp, 0, wo, False)
    perm = np.arange(ATTN_HEAD_DIM).reshape(2, 2, ATTN_HEAD_DIM // 4).transpose(1, 0, 2).reshape(-1)
    qk_heads = ATTN_HEADS + ATTN_KV_HEADS
    cols = (np.arange(qk_heads)[:, None] * ATTN_HEAD_DIM + perm[None, :]).reshape(-1)
    w_lat = jnp.concatenate([attn_w_qkv[:, cols], attn_w_qkv[:, qk_heads * ATTN_HEAD_DIM:]], axis=1).astype(BF16)
    consts_lat = [w_lat, row(attn_q_norm[perm]), row(attn_k_norm[perm])]
    tables = [t[:, perm] for t in _rope_tables(dseq, ATTN_HEAD_DIM)]
    q, k, v = _run_proj(_proj_a_kernel, True, xs, mods[0], g, consts_lat, tables,
                        [(d, BF16), (kvw, BF16), (kvw, BF16)], "proj_gqa_latent")
    o = _attention_a(q, k, v, cache_l0_k[..., perm].reshape(db, past, kvw).astype(BF16),
                     cache_l0_v.reshape(db, past, kvw).astype(BF16))
    xs = finish(o, xs, 0, wo, True)
    new_l0 = (l0_k.reshape(nb, seq, ATTN_KV_HEADS, ATTN_HEAD_DIM),
              l0_v.reshape(nb, seq, ATTN_KV_HEADS, ATTN_HEAD_DIM))

    g = norm_g[1]
    nq, nkv = MLA_Q_LORA, MLA_KV_LORA
    w_in = jnp.concatenate([mla_w_in[:, :nq + nkv]] + [mla_w_in[:, nq + nkv:]] * (LANES // MLA_ROPE),
                           axis=1).astype(BF16)
    wuq = mla_w_uq.reshape(nq, MLA_HEADS, MLA_NOPE + MLA_ROPE)
    wuq = jnp.concatenate([wuq[:, :, :MLA_NOPE].reshape(nq, -1), wuq[:, :, MLA_NOPE:].reshape(nq, -1)],
                          axis=1).astype(BF16)
    wukv = mla_w_ukv.reshape(nkv, MLA_HEADS, MLA_NOPE + MLA_V_DIM)
    wukv = jnp.concatenate([wukv[:, :, :MLA_NOPE].reshape(nkv, -1), wukv[:, :, MLA_NOPE:].reshape(nkv, -1)],
                           axis=1).astype(BF16)
    wo = mla_w_o
    consts = [w_in, row(mla_q_norm), row(mla_kv_norm), wuq, wukv]
    hw = MLA_HEADS * MLA_NOPE
    rw = MLA_HEADS * MLA_ROPE
    outs = [(hw, BF16), (rw, BF16), (hw, BF16), (hw, BF16), (LANES, BF16)]
    o, l1_ckv, l1_kpe = _run_proj(
        _proj_b_kernel, False, xp, mods[1], g, consts, [], [(nkv, F32), (MLA_ROPE, F32, "T")],
        "proj_attn_mla_context", seq, qkv=outs,
        attend=lambda qn, qr, kn, v, kp, o_ref: _attn_b_kernel(False, qn, qr, kn, kp, v, o_ref))
    xp = finish(o, xp, 1, wo, False)
    tables = list(_rope_tables(dseq, MLA_ROPE))
    qn, qr, kn, v, kp = _run_proj(_proj_b_kernel, True, xs, mods[1], g, consts, tables, outs,
                                  "proj_mla_latent", tile=WIDE_TOK_TILE)
    knc, vc = _mla_expand(cache_l1_ckv, wukv)
    kpc = jnp.tile(cache_l1_kpe, (1, 1, LANES // MLA_ROPE)).astype(BF16)
    o = _attention_b(qn, qr, kn, kp, v, knc, kpc, vc)
    xs = finish(o, xs, 1, wo, True)
    new_l1 = (l1_ckv.reshape(nb, seq, nkv), l1_kpe.transpose(0, 2, 1))

    g = norm_g[2]
    qw = SWA_HEADS * SWA_HEAD_DIM
    kw = SWA_KV_HEADS * SWA_HEAD_DIM
    wq, wk, wv = swa_w_qkv[:, :qw], swa_w_qkv[:, qw:qw + kw], swa_w_qkv[:, qw + kw:]
    wkd = _dup_heads(wk, SWA_KV_HEADS, SWA_HEAD_DIM)
    wvd = _dup_heads(wv, SWA_KV_HEADS, SWA_HEAD_DIM)
    w_ctx = jnp.concatenate([wq, wk, wv, wkd, wvd], axis=1).astype(BF16)
    w_lat = jnp.concatenate([wq, wkd, wvd], axis=1).astype(BF16)
    wo = swa_w_o
    o, l2_k, l2_v = _run_proj(
        _proj_c_kernel, False, xp, mods[2], g, [w_ctx], [], [(kw, F32, "T"), (kw, F32, "T")],
        "proj_attn_swa_context", seq, attend=functools.partial(_attn_c_kernel, False),
        attend_inputs=[_sink_logit_block(swa_sink, seq)], qkv=[(qw, BF16), (2 * kw, BF16), (2 * kw, BF16)])
    xp = finish(o, xp, 2, wo, False)
    tables = list(_rope_tables(dseq, SWA_HEAD_DIM))
    q, kd, vd = _run_proj(_proj_c_kernel, True, xs, mods[2], g, [w_lat], tables,
                          [(qw, BF16), (2 * kw, BF16), (2 * kw, BF16)], "proj_swa_latent", tile=WIDE_TOK_TILE)
    o = _attention_c(swa_sink, q, kd, vd, head_major(cache_l2_k), head_major(cache_l2_v))
    xs = finish(o, xs, 2, wo, True)
    new_l2 = (token_major(l2_k, SWA_KV_HEADS), token_major(l2_v, SWA_KV_HEADS))

    g = norm_g[3]
    hw = NAT_HEADS * NAT_HEAD_DIM
    w = nat_w_qkv
    wo = nat_w_o
    o, l3_k, l3_v = _run_proj(
        _proj_d_kernel, False, xp, mods[3], g, [w], [], [(hw, F32, "T"), (hw, F32, "T")],
        "proj_attn_nat_context", seq, attend=_attn_d_ctx_kernel, qkv=[(hw, BF16)] * 3)
    xp = finish(o, xp, 3, wo, False)
    q, k, v = _run_proj(_proj_d_kernel, True, xs, mods[3], g, [w], [],
                        [(hw, BF16), (hw, BF16), (hw, BF16)], "proj_nat_latent")
    o = _attention_d_lat(q, k, v, head_major(cache_l3_k), head_major(cache_l3_v),
                         _nat_dense_bias(nat_rpb, dseq // GRID_W))
    xs = finish(o, xs, 3, wo, True)
    new_l3 = (token_major(l3_k, NAT_HEADS), token_major(l3_v, NAT_HEADS))

    return (xp.reshape(nb, seq, d), xs) + new_l0 + new_l1 + new_l2 + new_l3
```

```python
import functools

import numpy as np

import jax
import jax.numpy as jnp
from jax import lax
from jax.experimental import pallas as pl
from jax.experimental.pallas import tpu as pltpu

F32 = jnp.float32
BF16 = jnp.bfloat16

D_MODEL = 1024
DEPTH = 4
N_MOD = 6
D_FF = 4 * D_MODEL
GRID_W = 64
ROPE_THETA = 10000.0
NORM_EPS = 1e-6
NEG_INF = -1e30
LOG2E = 1.4426950408889634

ATTN_HEADS, ATTN_KV_HEADS, ATTN_HEAD_DIM = 8, 2, 128
MLA_HEADS, MLA_Q_LORA, MLA_KV_LORA = 16, 384, 256
MLA_NOPE, MLA_ROPE, MLA_V_DIM = 64, 32, 64
MLA_SCALE = (MLA_NOPE + MLA_ROPE) ** -0.5
SWA_HEADS, SWA_KV_HEADS, SWA_HEAD_DIM, SWA_WINDOW = 16, 4, 64, 128
NAT_HEADS, NAT_HEAD_DIM, NAT_WIN_R, NAT_WIN_C = 16, 64, 8, 16

LANES = 128
COND_ROWS = 16
CTX_MOD_ROW = 8
VMEM_LIMIT = 48 * 1024 * 1024

MOD_COL_TILE = 1536
TOK_TILE = 512
WIDE_TOK_TILE = 1024
TOK_CHAIN = 256
MLP_TOK_TILE = 1024
MLP_FF_TILE = 1024
MLP_VMEM_LIMIT = 56 * 1024 * 1024
Q_TILE = 256
LATENT_Q_ROWS = 512
HEADS_PER_STEP = 16
NAT_TILE_ROWS = Q_TILE // GRID_W
NAT_KEY_ROWS = NAT_WIN_R + NAT_TILE_ROWS


def _params(*sem):
    return pltpu.CompilerParams(dimension_semantics=sem, vmem_limit_bytes=VMEM_LIMIT)


def _full(shape):
    nd = len(shape)
    return pl.BlockSpec(shape, lambda *_: (0,) * nd)


def _rms(x, g):
    return x * lax.rsqrt(jnp.mean(x * x, axis=-1, keepdims=True) + NORM_EPS) * g


def _mod(m, i):
    return m[:, i * D_MODEL:(i + 1) * D_MODEL]


def _premod(x, g, m, sub):
    return _rms(x, g * (1.0 + _mod(m, 3 * sub + 1))) + _mod(m, 3 * sub)


def _rope(x, cos, sin_signed, quarter):
    n = x.shape[-1]
    lane = lax.broadcasted_iota(jnp.int32, x.shape, 1)
    first = ((lane // quarter) % 2) == 0
    partner = jnp.where(first, pltpu.roll(x, n - quarter, 1), pltpu.roll(x, quarter, 1))
    return x * cos + partner * sin_signed


def _dot(a, b):
    return jnp.dot(a, b, preferred_element_type=F32)


def _dot_nt(a, b):
    return lax.dot_general(a, b, (((1,), (1,)), ((), ())), preferred_element_type=F32)


def _with_ones(v):
    return jnp.concatenate([v, jnp.ones_like(v)], axis=1)


def _attend(q, segs, sink=None):
    logits = []
    for seg in segs:
        k, bias, transposed = seg[0], seg[2], len(seg) > 3 and seg[3]
        s = _dot(q, k) if transposed else _dot_nt(q, k)
        logits.append(s if bias is None else s + bias)
    m = logits[0].max(axis=-1, keepdims=True)
    for s in logits[1:]:
        m = jnp.maximum(m, s.max(axis=-1, keepdims=True))
    if sink is not None:
        m = jnp.maximum(m, sink)
    acc = None
    for s, seg in zip(logits, segs):
        p = jnp.exp2(s - m).astype(BF16)
        pv = _dot_nt(p, seg[1]) if len(seg) > 3 and seg[3] else _dot(p, seg[1])
        acc = pv if acc is None else acc + pv
    den = acc[:, LANES:LANES + 1]
    if sink is not None:
        den = den + jnp.exp2(sink - m)
    return acc[:, :LANES] / den


def _attend_group(qs, ks, v1s, sink_logits=None):
    tq = qs[0].shape[0]
    s = jnp.concatenate([_dot_nt(q, k) for q, k in zip(qs, ks)], axis=0)
    if sink_logits is not None:
        s = jnp.concatenate([s, sink_logits], axis=1)
        zeros = jnp.zeros((LANES, LANES), BF16)
        tail = jnp.concatenate([zeros, jnp.ones_like(zeros)], axis=1)
        v1s = [jnp.concatenate([v1, tail], axis=0) for v1 in v1s]
    p = jnp.exp2(s - s.max(axis=-1, keepdims=True)).astype(BF16)
    acc = jnp.concatenate([_dot(p[i * tq:(i + 1) * tq], v1) for i, v1 in enumerate(v1s)], axis=0)
    return acc[:, :LANES] / acc[:, LANES:LANES + 1]


def _split_pair(q):
    low = lax.broadcasted_iota(jnp.int32, q.shape, 1) < (LANES // 2)
    zero = jnp.zeros_like(q)
    return [jnp.where(low, q, zero), jnp.where(low, zero, q)]


def _merge_pair(o0, o1):
    low = lax.broadcasted_iota(jnp.int32, o0.shape, 1) < (LANES // 2)
    return jnp.where(low, o0, o1).astype(BF16)


def _attend_pair(q, segs, sinks=None):
    outs = []
    for j, qj in enumerate(_split_pair(q)):
        segs_j = [(s[0], s[1], None if s[2] is None else s[2][j]) + tuple(s[3:]) for s in segs]
        outs.append(_attend(qj, segs_j, None if sinks is None else sinks[j]))
    return _merge_pair(*outs)


def _mods_kernel(cond_ref, w_ref, b_ref, o_ref):
    cnd = cond_ref[...]
    act = cnd * jax.nn.sigmoid(cnd)
    o_ref[...] = _dot(act.astype(BF16), w_ref[...].astype(BF16)) + b_ref[...]


def _modulation(cond, ada_w, ada_b):
    tn = MOD_COL_TILE
    n = N_MOD * D_MODEL
    return pl.pallas_call(
        _mods_kernel,
        grid=(DEPTH, n // tn),
        in_specs=[
            _full((COND_ROWS, D_MODEL)),
            pl.BlockSpec((None, D_MODEL, tn), lambda l, j: (l, 0, j)),
            pl.BlockSpec((None, 1, tn), lambda l, j: (l, 0, j)),
        ],
        out_specs=pl.BlockSpec((None, COND_ROWS, tn), lambda l, j: (l, 0, j)),
        out_shape=jax.ShapeDtypeStruct((DEPTH, COND_ROWS, n), F32),
        compiler_params=_params("arbitrary", "arbitrary"),
        name="adaln_mods",
    )(cond, ada_w, ada_b.reshape(DEPTH, 1, n))


def _proj_a_kernel(latent, x_ref, m_ref, g_ref, w_ref, qn_ref, kn_ref, *rest):
    if latent:
        cos_ref, sin_ref, q_ref, k_ref, v_ref = rest
    else:
        q_ref, k_ref, v_ref, ks_ref, vs_ref = rest
    hd = ATTN_HEAD_DIM
    n_qk = ATTN_HEADS + ATTN_KV_HEADS
    h = _premod(x_ref[...], g_ref[0:1, :], m_ref[...], 0).astype(BF16)
    w = w_ref[...].astype(BF16)
    for c in range(x_ref.shape[0] // TOK_CHAIN):
        rows = slice(c * TOK_CHAIN, (c + 1) * TOK_CHAIN)
        z = _dot(h[rows], w)
        ys = [_rms(z[:, i * hd:(i + 1) * hd], qn_ref[...] if i < ATTN_HEADS else kn_ref[...]) for i in range(n_qk)]
        if not latent:
            for j in range(ATTN_KV_HEADS):
                state_rows = pl.ds(ATTN_KV_HEADS * rows.start + j, TOK_CHAIN, stride=ATTN_KV_HEADS)
                ks_ref[state_rows, :] = ys[ATTN_HEADS + j]
                vs_ref[state_rows, :] = z[:, (n_qk + j) * hd:(n_qk + j + 1) * hd]
        else:
            ys = [y * cos_ref[rows, :] + pltpu.roll(y, hd // 2, 1) * sin_ref[rows, :] for y in ys]
        for i in range(ATTN_HEADS):
            q_ref[rows, i * hd:(i + 1) * hd] = (ys[i] * (hd ** -0.5 * LOG2E)).astype(BF16)
        for j in range(ATTN_KV_HEADS):
            k_ref[rows, j * hd:(j + 1) * hd] = ys[ATTN_HEADS + j].astype(BF16)
        v_ref[rows, :] = z[:, n_qk * hd:].astype(BF16)


def _proj_b_kernel(latent, x_ref, m_ref, g_ref, win_ref, qn_ref, kvn_ref, wuq_ref, wukv_ref, *rest):
    if latent:
        cos_ref, sin_ref, qn_o, qr_o, kn_o, v_o, kp_o = rest
    else:
        qn_o, qr_o, kn_o, v_o, kp_o, ckv_s, kpe_s = rest
    nq, nkv = MLA_Q_LORA, MLA_KV_LORA
    nope_w = MLA_HEADS * MLA_NOPE
    scale = MLA_SCALE * LOG2E
    h = _premod(x_ref[...], g_ref[0:1, :], m_ref[...], 0)
    z = _dot(h.astype(BF16), win_ref[...])
    cq = _rms(z[:, :nq], qn_ref[...])
    ckv = _rms(z[:, nq:nq + nkv], kvn_ref[...])
    kp = z[:, nq + nkv:]
    q = _dot(cq.astype(BF16), wuq_ref[...])
    kv = _dot(ckv.astype(BF16), wukv_ref[...])
    if not latent:
        ckv_s[...] = ckv
        _store_transposed(kpe_s, kp, MLA_ROPE)
    qn_o[...] = (q[:, :nope_w] * scale).astype(BF16)
    for i in range(MLA_HEADS * MLA_ROPE // LANES):
        qr = q[:, nope_w + i * LANES:nope_w + (i + 1) * LANES]
        if latent:
            qr = _rope(qr, cos_ref[...], sin_ref[...], MLA_ROPE // 4)
        qr_o[:, i * LANES:(i + 1) * LANES] = (qr * scale).astype(BF16)
    if latent:
        kp = _rope(kp, cos_ref[...], sin_ref[...], MLA_ROPE // 4)
    kp_o[...] = kp.astype(BF16)
    kn_o[...] = kv[:, :nope_w].astype(BF16)
    v_o[...] = kv[:, nope_w:].astype(BF16)


def _proj_c_kernel(latent, x_ref, m_ref, g_ref, w_ref, *rest):
    if latent:
        cos_ref, sin_ref, q_o, kd_o, vd_o = rest
    else:
        q_o, kd_o, vd_o, ks_o, vs_o = rest
    qw = SWA_HEADS * SWA_HEAD_DIM
    kw = SWA_KV_HEADS * SWA_HEAD_DIM
    h = _premod(x_ref[...], g_ref[0:1, :], m_ref[...], 0)
    z = _dot(h.astype(BF16), w_ref[...])
    off = qw
    if not latent:
        _store_transposed(ks_o, z[:, qw:qw + kw], kw)
        _store_transposed(vs_o, z[:, qw + kw:qw + 2 * kw], kw)
        off = qw + 2 * kw
    scale = SWA_HEAD_DIM ** -0.5 * LOG2E
    for i in range(qw // LANES):
        y = z[:, i * LANES:(i + 1) * LANES]
        if latent:
            y = _rope(y, cos_ref[...], sin_ref[...], SWA_HEAD_DIM // 4)
        q_o[:, i * LANES:(i + 1) * LANES] = (y * scale).astype(BF16)
    for i in range(2 * kw // LANES):
        y = z[:, off + i * LANES:off + (i + 1) * LANES]
        if latent:
            y = _rope(y, cos_ref[...], sin_ref[...], SWA_HEAD_DIM // 4)
        kd_o[:, i * LANES:(i + 1) * LANES] = y.astype(BF16)
    vd_o[...] = z[:, off + 2 * kw:].astype(BF16)


def _proj_d_kernel(latent, x_ref, m_ref, g_ref, w_ref, *rest):
    if latent:
        q_o, k_o, v_o = rest
    else:
        q_o, k_o, v_o, ks_o, vs_o = rest
    hw = NAT_HEADS * NAT_HEAD_DIM
    h = _premod(x_ref[...], g_ref[0:1, :], m_ref[...], 0)
    z = _dot(h.astype(BF16), w_ref[...].astype(BF16))
    q_o[...] = (z[:, :hw] * (NAT_HEAD_DIM ** -0.5 * LOG2E)).astype(BF16)
    k_o[...] = z[:, hw:2 * hw].astype(BF16)
    v_o[...] = z[:, 2 * hw:].astype(BF16)
    if not latent:
        _store_transposed(ks_o, z[:, hw:2 * hw], hw)
        _store_transposed(vs_o, z[:, 2 * hw:], hw)


def _store_transposed(ref, val, width):
    n, _, seq = ref.shape
    for i in range(n):
        ref[i] = val[i * seq:(i + 1) * seq, :].T[:width, :]


def _run_proj(kernel, latent, x, mods_l, g, consts, tables, outs, name, state_seq=None,
              attend=None, attend_inputs=(), qkv=(), tile=TOK_TILE):
    bx, s, _ = x.shape
    tm = tile
    row = (lambda b, j: (b, 0, 0)) if latent else (lambda b, j: (CTX_MOD_ROW, 0, 0))
    in_specs = [
        pl.BlockSpec((None, tm, D_MODEL), lambda b, j: (b, j, 0)),
        pl.BlockSpec((None, 1, N_MOD * D_MODEL), row),
        _full(g.shape),
    ] + [_full(c.shape) for c in consts]
    in_specs += [pl.BlockSpec((tm, LANES), lambda b, j: (j, 0)) for _ in tables]
    in_specs += [_full(a.shape) for a in attend_inputs]
    body = functools.partial(kernel, latent)
    if attend is not None:
        n_proj_in = 3 + len(consts) + len(tables)
        n_att_in = len(attend_inputs)
        proj_body = body
        outs = [(D_MODEL, BF16)] + list(outs)

        def body(*refs):
            ins, att_ins = refs[:n_proj_in], refs[n_proj_in:n_proj_in + n_att_in]
            o_ref = refs[n_proj_in + n_att_in]
            states, scratch = refs[n_proj_in + n_att_in + 1:-len(qkv)], refs[-len(qkv):]
            proj_body(*ins, *scratch, *states)
            for i in range(tm // state_seq):
                rows = slice(i * state_seq, (i + 1) * state_seq)
                attend(*att_ins, *[r.at[rows] for r in scratch], o_ref.at[rows])

    out_specs, out_shape = [], []
    for o in outs:
        if len(o) == 2:
            out_specs.append(pl.BlockSpec((None, tm, o[0]), lambda b, j: (b, j, 0)))
            out_shape.append(jax.ShapeDtypeStruct((bx, s, o[0]), o[1]))
        elif o[2] == "I":
            out_specs.append(pl.BlockSpec((None, tm * o[3], o[0]), lambda b, j: (b, j, 0)))
            out_shape.append(jax.ShapeDtypeStruct((bx, s * o[3], o[0]), o[1]))
        else:
            per_tile = tm // state_seq
            out_specs.append(pl.BlockSpec((per_tile, o[0], state_seq),
                                          lambda b, j: (b * (s // tm) + j, 0, 0)))
            out_shape.append(jax.ShapeDtypeStruct((bx * s // state_seq, o[0], state_seq), o[1]))
    return pl.pallas_call(
        body,
        grid=(bx, s // tm),
        in_specs=in_specs,
        out_specs=out_specs,
        out_shape=out_shape,
        scratch_shapes=[pltpu.VMEM((tm, w), dt) for w, dt in qkv],
        compiler_params=_params("arbitrary", "arbitrary"),
        name=name,
    )(x, mods_l, g, *consts, *tables, *attend_inputs)


def _attn_a_kernel(has_ctx, q_ref, k_ref, v_ref, *rest):
    if has_ctx:
        kc_ref, vc_ref, perm_ref, o_ref = rest
    else:
        (o_ref,) = rest
    group = ATTN_HEADS // ATTN_KV_HEADS
    head_cols = [slice(h * LANES, (h + 1) * LANES) for h in range(ATTN_HEADS)]
    kv_cols = [slice(j * LANES, (j + 1) * LANES) for j in range(ATTN_KV_HEADS)]
    if not has_ctx:
        tq = q_ref.shape[0]
        qs = [jnp.concatenate([q_ref[:, c] for c in head_cols[j * group:(j + 1) * group]], axis=0)
              for j in range(ATTN_KV_HEADS)]
        out = _attend_group(qs, [k_ref[:, c] for c in kv_cols], [_with_ones(v_ref[:, c]) for c in kv_cols])
        for h, c in enumerate(head_cols):
            o_ref[:, c] = out[h * tq:(h + 1) * tq].astype(BF16)
        return
    ctx_len = kc_ref.shape[0] // ATTN_KV_HEADS
    for j, kc in enumerate(kv_cols):
        head_rows = pl.ds(j, ctx_len, stride=ATTN_KV_HEADS)
        k_ctx = _dot(kc_ref[head_rows, :].astype(BF16), perm_ref[...]).astype(BF16)
        segs = [(k_ref[:, kc], _with_ones(v_ref[:, kc]), None),
                (k_ctx, _with_ones(vc_ref[head_rows, :].astype(BF16)), None)]
        for r in range(q_ref.shape[0] // Q_TILE):
            rows = slice(r * Q_TILE, (r + 1) * Q_TILE)
            for c in head_cols[j * group:(j + 1) * group]:
                o_ref[rows, c] = _attend(q_ref[rows, c], segs).astype(BF16)


def _attention_a(q, k, v, kc, vc, perm):
    b, s, qw = q.shape
    kl, kw = k.shape[1:]
    tq = min(LATENT_Q_ROWS, s)
    cache_spec = pl.BlockSpec((None,) + kc.shape[1:], lambda bi, t: (bi, 0, 0))
    return pl.pallas_call(
        functools.partial(_attn_a_kernel, True),
        grid=(b, s // tq),
        in_specs=[
            pl.BlockSpec((None, tq, qw), lambda bi, t: (bi, t, 0)),
            pl.BlockSpec((None, kl, kw), lambda bi, t: (bi, 0, 0)),
            pl.BlockSpec((None, kl, kw), lambda bi, t: (bi, 0, 0)),
            cache_spec, cache_spec, _full(perm.shape),
        ],
        out_specs=pl.BlockSpec((None, tq, qw), lambda bi, t: (bi, t, 0)),
        out_shape=jax.ShapeDtypeStruct(q.shape, BF16),
        compiler_params=_params("arbitrary", "arbitrary"),
        name="attn_gqa_latent",
    )(q, k, v, kc, vc, perm)


def _attn_b_kernel(has_ctx, qn_ref, qr_ref, kn_ref, kp_ref, v_ref, *rest):
    if has_ctx:
        knc_ref, kpc_ref, vc_ref, o_ref = rest
    else:
        (o_ref,) = rest
    tq = min(Q_TILE, qn_ref.shape[0])
    lane = lax.broadcasted_iota(jnp.int32, (tq, LANES), 1)
    low = lane < (LANES // 2)
    quarter = lane // MLA_ROPE
    zero = jnp.zeros((tq, LANES), BF16)
    kp = kp_ref[...]
    kpc = kpc_ref[...] if has_ctx else None
    pairs_per_rope = LANES // MLA_ROPE // 2
    n_pairs = HEADS_PER_STEP // 2
    pair_cols = [slice(p * LANES, (p + 1) * LANES) for p in range(n_pairs)]

    def pair_lhs(p, rows=slice(None)):
        qn = qn_ref[rows, pair_cols[p]]
        rg = p // pairs_per_rope
        qr = qr_ref[rows, rg * LANES:(rg + 1) * LANES]
        return [jnp.concatenate([jnp.where(low if j == 0 else ~low, qn, zero),
                                 jnp.where(quarter == 2 * (p % pairs_per_rope) + j, qr, zero)], axis=1)
                for j in range(2)]

    if not has_ctx:
        qs = [jnp.concatenate(pair_lhs(p), axis=0) for p in range(n_pairs)]
        out = _attend_group(qs, [jnp.concatenate([kn_ref[:, c], kp], axis=1) for c in pair_cols],
                            [_with_ones(v_ref[:, c]) for c in pair_cols])
        for p, c in enumerate(pair_cols):
            o_ref[:, c] = _merge_pair(out[2 * p * tq:(2 * p + 1) * tq], out[(2 * p + 1) * tq:(2 * p + 2) * tq])
        return
    for p, cols in enumerate(pair_cols):
        segs = [(jnp.concatenate([kn_ref[:, cols], kp], axis=1), _with_ones(v_ref[:, cols]), None),
                (jnp.concatenate([knc_ref[:, cols], kpc], axis=1), _with_ones(vc_ref[:, cols]), None)]
        for r in range(qn_ref.shape[0] // tq):
            rows = slice(r * tq, (r + 1) * tq)
            o_ref[rows, cols] = _merge_pair(*[_attend(lhs, segs) for lhs in pair_lhs(p, rows)])


def _attention_b(qn, qr, kn, kp, v, knc=None, kpc=None, vc=None):
    b, s, _ = qn.shape
    kl = kn.shape[1]
    has_ctx = knc is not None
    tq = min(LATENT_Q_ROWS if has_ctx else Q_TILE, s)
    gw = HEADS_PER_STEP * MLA_NOPE
    blk = lambda rows, w, fn: pl.BlockSpec((None, rows, w), fn)
    in_specs = [
        blk(tq, gw, lambda bi, g, t: (bi, t, g)),
        blk(tq, HEADS_PER_STEP * MLA_ROPE, lambda bi, g, t: (bi, t, g)),
        blk(kl, gw, lambda bi, g, t: (bi, 0, g)),
        blk(kl, LANES, lambda bi, g, t: (bi, 0, 0)),
        blk(kl, gw, lambda bi, g, t: (bi, 0, g)),
    ]
    args = [qn, qr, kn, kp, v]
    if has_ctx:
        cl = knc.shape[1]
        in_specs += [
            blk(cl, gw, lambda bi, g, t: (bi, 0, g)),
            blk(cl, LANES, lambda bi, g, t: (bi, 0, 0)),
            blk(cl, gw, lambda bi, g, t: (bi, 0, g)),
        ]
        args += [knc, kpc, vc]
    return pl.pallas_call(
        functools.partial(_attn_b_kernel, has_ctx),
        grid=(b, MLA_HEADS // HEADS_PER_STEP, s // tq),
        in_specs=in_specs,
        out_specs=blk(tq, gw, lambda bi, g, t: (bi, t, g)),
        out_shape=jax.ShapeDtypeStruct(qn.shape, BF16),
        compiler_params=_params("arbitrary", "arbitrary", "arbitrary"),
        name="attn_mla_latent" if has_ctx else "attn_mla_context",
    )(*args)


def _attn_c_kernel(latent, sink_ref, q_ref, kd_ref, vd_ref, *rest):
    if latent:
        kc_ref, vc_ref, o_ref = rest
    else:
        (o_ref,) = rest
    tq = min(Q_TILE, q_ref.shape[0])
    group = SWA_HEADS // SWA_KV_HEADS
    kv_per_step = HEADS_PER_STEP // group
    head0 = pl.program_id(1) * HEADS_PER_STEP
    windows = []
    if latent:
        s_len = kd_ref.shape[0]
        win = tq + 2 * SWA_WINDOW
        for r in range(q_ref.shape[0] // tq):
            t0 = pl.program_id(2) * q_ref.shape[0] + r * tq
            ws = pl.multiple_of(jnp.clip(t0 - SWA_WINDOW, 0, s_len - win), LANES)
            qpos = t0 + lax.broadcasted_iota(jnp.int32, (tq, win), 0)
            kpos = ws + lax.broadcasted_iota(jnp.int32, (tq, win), 1)
            bias = jnp.where(jnp.abs(qpos - kpos) <= SWA_WINDOW, 0.0, NEG_INF).astype(F32)
            windows.append((slice(r * tq, (r + 1) * tq), ws, bias))
    pairs_per_kv = group // 2
    pair_cols = [slice(p * LANES, (p + 1) * LANES) for p in range(HEADS_PER_STEP // 2)]
    kv_cols = [slice(j * LANES, (j + 1) * LANES) for j in range(kv_per_step)]
    if not latent:
        qs = [jnp.concatenate([h for c in pair_cols[j * pairs_per_kv:(j + 1) * pairs_per_kv]
                               for h in _split_pair(q_ref[:, c])], axis=0) for j in range(kv_per_step)]
        out = _attend_group(qs, [kd_ref[:, c] for c in kv_cols], [_with_ones(vd_ref[:, c]) for c in kv_cols],
                            sink_ref[...])
        for p, c in enumerate(pair_cols):
            o_ref[:, c] = _merge_pair(out[2 * p * tq:(2 * p + 1) * tq], out[(2 * p + 1) * tq:(2 * p + 2) * tq])
        return
    sinks = [sink_ref[head0 + h] * LOG2E for h in range(HEADS_PER_STEP)]
    hd = SWA_HEAD_DIM
    for j, kc in enumerate(kv_cols):
        kct = kc_ref[j * hd:(j + 1) * hd, :].astype(BF16)
        vct = vc_ref[j * hd:(j + 1) * hd, :].astype(BF16)
        ctx_seg = (jnp.concatenate([kct, kct], axis=0),
                   jnp.concatenate([vct, vct, jnp.ones((LANES, vct.shape[1]), BF16)], axis=0), None, True)
        for rows, ws, bias in windows:
            segs = [(kd_ref[pl.ds(ws, win), kc], _with_ones(vd_ref[pl.ds(ws, win), kc]), (bias, bias)), ctx_seg]
            for p in range(j * pairs_per_kv, (j + 1) * pairs_per_kv):
                o_ref[rows, pair_cols[p]] = _attend_pair(q_ref[rows, pair_cols[p]], segs, sinks[2 * p:2 * p + 2])


def _sink_logit_block(sink, tq):
    rows = jnp.repeat(sink * LOG2E, tq)[:, None]
    return jnp.where(jnp.arange(LANES)[None, :] == 0, rows, NEG_INF)


def _attention_c(sink, q, kd, vd, kc, vc):
    b, s, _ = q.shape
    kl = kd.shape[1]
    tq = min(LATENT_Q_ROWS, s)
    gw = HEADS_PER_STEP * SWA_HEAD_DIM
    kvw = 2 * SWA_HEAD_DIM * HEADS_PER_STEP // (SWA_HEADS // SWA_KV_HEADS)
    cw, cl = kc.shape[1] * HEADS_PER_STEP // SWA_HEADS, kc.shape[2]
    return pl.pallas_call(
        functools.partial(_attn_c_kernel, True),
        grid=(b, SWA_HEADS // HEADS_PER_STEP, s // tq),
        in_specs=[
            pl.BlockSpec(memory_space=pltpu.SMEM),
            pl.BlockSpec((None, tq, gw), lambda bi, g, t: (bi, t, g)),
            pl.BlockSpec((None, kl, kvw), lambda bi, g, t: (bi, 0, g)),
            pl.BlockSpec((None, kl, kvw), lambda bi, g, t: (bi, 0, g)),
            pl.BlockSpec((None, cw, cl), lambda bi, g, t: (bi, g, 0)),
            pl.BlockSpec((None, cw, cl), lambda bi, g, t: (bi, g, 0)),
        ],
        out_specs=pl.BlockSpec((None, tq, gw), lambda bi, g, t: (bi, t, g)),
        out_shape=jax.ShapeDtypeStruct(q.shape, BF16),
        compiler_params=_params("arbitrary", "arbitrary", "arbitrary"),
        name="attn_swa_latent",
    )(sink, q, kd, vd, kc, vc)


def _attn_d_ctx_kernel(q_ref, k_ref, v_ref, o_ref):
    tq = q_ref.shape[0]
    pair_cols = [slice(p * LANES, (p + 1) * LANES) for p in range(HEADS_PER_STEP // 2)]
    qs = [jnp.concatenate(_split_pair(q_ref[:, c]), axis=0) for c in pair_cols]
    out = _attend_group(qs, [k_ref[:, c] for c in pair_cols], [_with_ones(v_ref[:, c]) for c in pair_cols])
    for p, c in enumerate(pair_cols):
        o_ref[:, c] = _merge_pair(out[2 * p * tq:(2 * p + 1) * tq], out[(2 * p + 1) * tq:(2 * p + 2) * tq])


def _nat_row0(r, rows):
    return min(max(r - NAT_WIN_R // 2, 0), rows - NAT_WIN_R)


def _nat_tile_key_rows(tile, rows):
    first, last = tile * NAT_TILE_ROWS, (tile + 1) * NAT_TILE_ROWS - 1
    need = _nat_row0(last, rows) + NAT_WIN_R - _nat_row0(first, rows)
    return need + need % 2


def _nat_tile_start(tile, rows):
    return min(_nat_row0(tile * NAT_TILE_ROWS, rows), rows - _nat_tile_key_rows(tile, rows))


def _attn_d_lat_kernel(q_ref, k_ref, v_ref, kc_ref, vc_ref, bias_ref, o_ref):
    rows = k_ref.shape[0] // GRID_W
    tile = pl.program_id(0)
    ones = jnp.ones((LANES, kc_ref.shape[1]), BF16)

    def body(key_rows, start_row):
        slab = key_rows * GRID_W
        start = pl.multiple_of(start_row * GRID_W, GRID_W)
        for p in range(HEADS_PER_STEP // 2):
            cols = slice(p * LANES, (p + 1) * LANES)
            segs = [(k_ref[pl.ds(start, slab), cols], _with_ones(v_ref[pl.ds(start, slab), cols]),
                     (bias_ref[2 * p, :, :slab], bias_ref[2 * p + 1, :, :slab])),
                    (kc_ref[cols, :].astype(BF16),
                     jnp.concatenate([vc_ref[cols, :].astype(BF16), ones], axis=0), None, True)]
            o_ref[:, cols] = _attend_pair(q_ref[:, cols], segs)

    tiles = range(rows // NAT_TILE_ROWS)
    for key_rows in sorted({_nat_tile_key_rows(t, rows) for t in tiles}):
        members = [t for t in tiles if _nat_tile_key_rows(t, rows) == key_rows]
        cond = functools.reduce(jnp.logical_or, [tile == t for t in members])
        start_row = functools.reduce(lambda acc, t: jnp.where(tile == t, _nat_tile_start(t, rows), acc),
                                     members, jnp.int32(0))
        pl.when(cond)(functools.partial(body, key_rows, start_row))


def _attention_d_lat(q, k, v, kc, vc, bias):
    b, s, _ = q.shape
    cl = kc.shape[2]
    tq = NAT_TILE_ROWS * GRID_W
    slab = NAT_KEY_ROWS * GRID_W
    gw = HEADS_PER_STEP * NAT_HEAD_DIM
    return pl.pallas_call(
        _attn_d_lat_kernel,
        grid=(s // tq, NAT_HEADS // HEADS_PER_STEP, b),
        in_specs=[
            pl.BlockSpec((None, tq, gw), lambda t, g, bi: (bi, t, g)),
            pl.BlockSpec((None, s, gw), lambda t, g, bi: (bi, 0, g)),
            pl.BlockSpec((None, s, gw), lambda t, g, bi: (bi, 0, g)),
            pl.BlockSpec((None, gw, cl), lambda t, g, bi: (bi, g, 0)),
            pl.BlockSpec((None, gw, cl), lambda t, g, bi: (bi, g, 0)),
            pl.BlockSpec((HEADS_PER_STEP, None, tq, slab), lambda t, g, bi: (g, t, 0, 0)),
        ],
        out_specs=pl.BlockSpec((None, tq, gw), lambda t, g, bi: (bi, t, g)),
        out_shape=jax.ShapeDtypeStruct(q.shape, BF16),
        compiler_params=_params("arbitrary", "arbitrary", "arbitrary"),
        name="attn_nat_latent",
    )(q, k, v, kc, vc, bias)


def _mla_expand_kernel(c_ref, w_ref, kn_ref, v_ref):
    kv = _dot(c_ref[...].astype(BF16), w_ref[...])
    half = kv.shape[1] // 2
    kn_ref[...] = kv[:, :half].astype(BF16)
    v_ref[...] = kv[:, half:].astype(BF16)


def _mla_expand(ckv, w_ukv):
    b, l, c = ckv.shape
    n = w_ukv.shape[1] // 2
    out = pl.BlockSpec((None, l, n), lambda bi: (bi, 0, 0))
    return pl.pallas_call(
        _mla_expand_kernel,
        grid=(b,),
        in_specs=[pl.BlockSpec((None, l, c), lambda bi: (bi, 0, 0)), _full(w_ukv.shape)],
        out_specs=[out, out],
        out_shape=[jax.ShapeDtypeStruct((b, l, n), BF16)] * 2,
        compiler_params=_params("arbitrary"),
        name="mla_expand_cache",
    )(ckv, w_ukv)


def _finish_kernel(o_ref, x_ref, m_ref, g_ref, wo_ref, w1_ref, w2_ref, y_ref, x1_ref, h2_ref):
    f = pl.program_id(2)
    last = pl.num_programs(2) - 1
    chains = [slice(c * TOK_CHAIN, (c + 1) * TOK_CHAIN) for c in range(y_ref.shape[0] // TOK_CHAIN)]

    def mlp_part(h, w1, w2):
        u = jnp.square(jnp.maximum(_dot(h, w1), 0.0)).astype(BF16)
        return _dot(u, w2)

    @pl.when(f == 0)
    def _():
        m = m_ref[...]
        wo = wo_ref[...].astype(BF16)
        for rows in chains:
            a = _dot(o_ref[rows, :], wo)
            x1 = x_ref[rows, :] + _rms(a, _mod(m, 2) * g_ref[1:2, :])
            x1_ref[rows, :] = x1
            h2_ref[rows, :] = _premod(x1, g_ref[2:3, :], m, 1).astype(BF16)
        y_ref[...] = mlp_part(h2_ref[...], w1_ref[...].astype(BF16), w2_ref[...].astype(BF16))

    @pl.when((f > 0) & (f < last))
    def _():
        y_ref[...] += mlp_part(h2_ref[...], w1_ref[...].astype(BF16), w2_ref[...].astype(BF16))

    @pl.when(f == last)
    def _():
        gain = _mod(m_ref[...], 5) * g_ref[3:4, :]
        w1, w2 = w1_ref[...].astype(BF16), w2_ref[...].astype(BF16)
        for rows in chains:
            y = y_ref[rows, :] + mlp_part(h2_ref[rows, :], w1, w2)
            y_ref[rows, :] = x1_ref[rows, :] + _rms(y, gain)


def _finish_layer(o, x, mods_l, g, wo, w1, w2, layer, latent):
    bx, s, _ = x.shape
    tm, tf = MLP_TOK_TILE, MLP_FF_TILE
    row = (lambda b, j, f: (b, 0, 0)) if latent else (lambda b, j, f: (CTX_MOD_ROW, 0, 0))
    tok = lambda w: pl.BlockSpec((None, tm, w), lambda b, j, f: (b, j, 0))
    return pl.pallas_call(
        _finish_kernel,
        grid=(bx, s // tm, D_FF // tf),
        in_specs=[tok(o.shape[-1]), tok(D_MODEL),
                  pl.BlockSpec((None, 1, N_MOD * D_MODEL), row), _full(g.shape), _full(wo.shape),
                  pl.BlockSpec((None, D_MODEL, tf), lambda b, j, f: (layer, 0, f)),
                  pl.BlockSpec((None, tf, D_MODEL), lambda b, j, f: (layer, f, 0))],
        out_specs=tok(D_MODEL),
        out_shape=jax.ShapeDtypeStruct(x.shape, F32),
        scratch_shapes=[pltpu.VMEM((tm, D_MODEL), F32), pltpu.VMEM((tm, D_MODEL), BF16)],
        compiler_params=pltpu.CompilerParams(dimension_semantics=("arbitrary",) * 3,
                                             vmem_limit_bytes=MLP_VMEM_LIMIT),
        name="out_proj_mlp",
    )(o, x, mods_l, g, wo, w1, w2)


def _rope_tables(s, dim):
    quarter = dim // 4
    t = jnp.arange(s)
    pos = jnp.stack([t // GRID_W, t % GRID_W], axis=-1).astype(F32)
    inv = ROPE_THETA ** (-jnp.arange(quarter, dtype=F32) / quarter)
    ang = pos[:, :, None] * inv
    cos = jnp.broadcast_to(jnp.cos(ang)[:, :, None, :], (s, 2, 2, quarter)).reshape(s, dim)
    sign = jnp.array([-1.0, 1.0], F32)[None, None, :, None]
    sin = (jnp.sin(ang)[:, :, None, :] * sign).reshape(s, dim)
    reps = LANES // dim
    return jnp.tile(cos, (1, reps)), jnp.tile(sin, (1, reps))


def _dup_heads(w, heads, dim):
    lead = w.shape[:-1]
    w = w.reshape(lead + (heads, 1, dim))
    return jnp.broadcast_to(w, lead + (heads, 2, dim)).reshape(lead + (heads * 2 * dim,))


def _nat_dense_bias(rpb, rows):
    heads = rpb.shape[0]
    c = np.arange(GRID_W)
    c0 = np.clip(c - NAT_WIN_C // 2, 0, GRID_W - NAT_WIN_C)
    in_c = (c[None, :] >= c0[:, None]) & (c[None, :] < c0[:, None] + NAT_WIN_C)
    dc = c[None, :] - c[:, None] + NAT_WIN_C - 1
    onehot = (dc[None] == np.arange(2 * NAT_WIN_C - 1)[:, None, None]) & in_c[None]
    toe = jnp.einsum("had,dck->hack", rpb * LOG2E, jnp.asarray(onehot, F32),
                     precision=lax.Precision.HIGHEST)
    toe = jnp.where(jnp.asarray(in_c)[None, None], toe, NEG_INF)
    pad = jnp.full((heads, 1, GRID_W, GRID_W), NEG_INF, F32)
    ext = jnp.concatenate([pad, toe, pad], axis=1)
    pairs = jnp.concatenate([ext[:, :-1], ext[:, 1:]], axis=-1)
    tiles = rows // NAT_TILE_ROWS
    tq, slab = NAT_TILE_ROWS * GRID_W, NAT_KEY_ROWS * GRID_W
    n_off = 2 * NAT_WIN_R
    return pl.pallas_call(
        functools.partial(_nat_bias_kernel, rows),
        grid=(heads,),
        in_specs=[pl.BlockSpec((None, n_off, GRID_W, LANES), lambda h: (h, 0, 0, 0))],
        out_specs=pl.BlockSpec((None, tiles, tq, slab), lambda h: (h, 0, 0, 0)),
        out_shape=jax.ShapeDtypeStruct((heads, tiles, tq, slab), F32),
        compiler_params=_params("arbitrary"),
        name="nat_bias_expand",
    )(pairs)


def _nat_bias_kernel(rows, pairs_ref, o_ref):
    low = lax.broadcasted_iota(jnp.int32, (GRID_W, LANES), 1) < GRID_W
    masked = jnp.full((GRID_W, LANES), NEG_INF, F32)
    for r in range(rows):
        tile, i = divmod(r, NAT_TILE_ROWS)
        r0 = _nat_row0(r, rows)
        ws = _nat_tile_start(tile, rows)
        for jb in range(NAT_KEY_ROWS // 2):
            kr = ws + 2 * jb
            ok_lo, ok_hi = r0 <= kr < r0 + NAT_WIN_R, r0 <= kr + 1 < r0 + NAT_WIN_R
            blk = masked
            if ok_lo or ok_hi:
                blk = pairs_ref[kr - r + NAT_WIN_R]
                if not ok_lo:
                    blk = jnp.where(low, NEG_INF, blk)
                elif not ok_hi:
                    blk = jnp.where(low, blk, NEG_INF)
            o_ref[tile, i * GRID_W:(i + 1) * GRID_W, jb * LANES:(jb + 1) * LANES] = blk


def kernel(x_prompt, x_sample, cache_l0_k, cache_l0_v, cache_l1_ckv, cache_l1_kpe, cache_l2_k, cache_l2_v, cache_l3_k, cache_l3_v, c, c_ctx, ada_w, ada_b, norm_g, mlp_w1, mlp_w2, attn_w_qkv, attn_q_norm, attn_k_norm, attn_w_o, mla_w_in, mla_q_norm, mla_kv_norm, mla_w_uq, mla_w_ukv, mla_w_o, swa_w_qkv, swa_sink, swa_w_o, nat_w_qkv, nat_rpb, nat_w_o):
    nb, seq, d = x_prompt.shape
    db, dseq, _ = x_sample.shape
    past = cache_l0_k.shape[1]
    ctx_b = nb * seq // dseq
    xp = x_prompt.reshape(ctx_b, dseq, d)
    xs = x_sample

    cond = jnp.zeros((COND_ROWS, d), F32).at[:db].set(c).at[CTX_MOD_ROW].set(c_ctx)
    mods = _modulation(cond, ada_w, ada_b).reshape(DEPTH, COND_ROWS, 1, N_MOD * d)

    row = lambda v: v.reshape(1, -1)

    def head_major(cache):
        b_, l_, h_, dh = cache.shape
        return cache.transpose(0, 2, 3, 1).reshape(b_, h_ * dh, l_)

    def token_major(state, heads):
        b_, w_, l_ = state.shape
        return state.reshape(b_, heads, w_ // heads, l_).transpose(0, 3, 1, 2)

    def finish(o, x, layer, wo, latent):
        return _finish_layer(o, x, mods[layer], norm_g[layer], wo, mlp_w1, mlp_w2, layer, latent)

    g = norm_g[0]
    w = attn_w_qkv
    wo = attn_w_o
    consts = [w, row(attn_q_norm), row(attn_k_norm)]
    kvw = ATTN_KV_HEADS * ATTN_HEAD_DIM
    o, l0_k, l0_v = _run_proj(
        _proj_a_kernel, False, xp, mods[0], g, consts, [],
        [(ATTN_HEAD_DIM, F32, "I", ATTN_KV_HEADS)] * 2, "proj_attn_gqa_context",
        seq, attend=functools.partial(_attn_a_kernel, False), qkv=[(d, BF16), (kvw, BF16), (kvw, BF16)])
    xp = finish(o, xp, 0, wo, False)
    quarter = ATTN_HEAD_DIM // 4
    perm = np.arange(ATTN_HEAD_DIM).reshape(2, 2, quarter).transpose(1, 0, 2).reshape(-1)
    qk_w = (ATTN_HEADS + ATTN_KV_HEADS) * ATTN_HEAD_DIM
    w_qk = attn_w_qkv[:, :qk_w].reshape(d, -1, 2, 2, quarter).transpose(0, 1, 3, 2, 4).reshape(d, qk_w)
    w_lat = jnp.concatenate([w_qk, attn_w_qkv[:, qk_w:]], axis=1).astype(BF16)
    consts_lat = [w_lat, row(attn_q_norm[perm]), row(attn_k_norm[perm])]
    tables = [t[:, perm] for t in _rope_tables(dseq, ATTN_HEAD_DIM)]
    q, k, v = _run_proj(_proj_a_kernel, True, xs, mods[0], g, consts_lat, tables,
                        [(d, BF16), (kvw, BF16), (kvw, BF16)], "proj_gqa_latent")
    perm_matrix = np.zeros((ATTN_HEAD_DIM, ATTN_HEAD_DIM), np.float32)
    perm_matrix[perm, np.arange(ATTN_HEAD_DIM)] = 1.0
    o = _attention_a(q, k, v, cache_l0_k.reshape(db, past * ATTN_KV_HEADS, ATTN_HEAD_DIM),
                     cache_l0_v.reshape(db, past * ATTN_KV_HEADS, ATTN_HEAD_DIM), jnp.asarray(perm_matrix, BF16))
    xs = finish(o, xs, 0, wo, True)
    new_l0 = (l0_k.reshape(nb, seq, ATTN_KV_HEADS, ATTN_HEAD_DIM),
              l0_v.reshape(nb, seq, ATTN_KV_HEADS, ATTN_HEAD_DIM))

    g = norm_g[1]
    nq, nkv = MLA_Q_LORA, MLA_KV_LORA
    w_in = jnp.concatenate([mla_w_in[:, :nq + nkv]] + [mla_w_in[:, nq + nkv:]] * (LANES // MLA_ROPE),
                           axis=1).astype(BF16)
    wuq = mla_w_uq.reshape(nq, MLA_HEADS, MLA_NOPE + MLA_ROPE)
    wuq = jnp.concatenate([wuq[:, :, :MLA_NOPE].reshape(nq, -1), wuq[:, :, MLA_NOPE:].reshape(nq, -1)],
                          axis=1).astype(BF16)
    wukv = mla_w_ukv.reshape(nkv, MLA_HEADS, MLA_NOPE + MLA_V_DIM)
    wukv = jnp.concatenate([wukv[:, :, :MLA_NOPE].reshape(nkv, -1), wukv[:, :, MLA_NOPE:].reshape(nkv, -1)],
                           axis=1).astype(BF16)
    wo = mla_w_o
    consts = [w_in, row(mla_q_norm), row(mla_kv_norm), wuq, wukv]
    hw = MLA_HEADS * MLA_NOPE
    rw = MLA_HEADS * MLA_ROPE
    outs = [(hw, BF16), (rw, BF16), (hw, BF16), (hw, BF16), (LANES, BF16)]
    o, l1_ckv, l1_kpe = _run_proj(
        _proj_b_kernel, False, xp, mods[1], g, consts, [], [(nkv, F32), (MLA_ROPE, F32, "T")],
        "proj_attn_mla_context", seq, qkv=outs,
        attend=lambda qn, qr, kn, v, kp, o_ref: _attn_b_kernel(False, qn, qr, kn, kp, v, o_ref))
    xp = finish(o, xp, 1, wo, False)
    tables = list(_rope_tables(dseq, MLA_ROPE))
    qn, qr, kn, v, kp = _run_proj(_proj_b_kernel, True, xs, mods[1], g, consts, tables, outs,
                                  "proj_mla_latent", tile=WIDE_TOK_TILE)
    knc, vc = _mla_expand(cache_l1_ckv, wukv)
    kpc = jnp.tile(cache_l1_kpe, (1, 1, LANES // MLA_ROPE)).astype(BF16)
    o = _attention_b(qn, qr, kn, kp, v, knc, kpc, vc)
    xs = finish(o, xs, 1, wo, True)
    new_l1 = (l1_ckv.reshape(nb, seq, nkv), l1_kpe.transpose(0, 2, 1))

    g = norm_g[2]
    qw = SWA_HEADS * SWA_HEAD_DIM
    kw = SWA_KV_HEADS * SWA_HEAD_DIM
    wq, wk, wv = swa_w_qkv[:, :qw], swa_w_qkv[:, qw:qw + kw], swa_w_qkv[:, qw + kw:]
    wkd = _dup_heads(wk, SWA_KV_HEADS, SWA_HEAD_DIM)
    wvd = _dup_heads(wv, SWA_KV_HEADS, SWA_HEAD_DIM)
    w_ctx = jnp.concatenate([wq, wk, wv, wkd, wvd], axis=1).astype(BF16)
    w_lat = jnp.concatenate([wq, wkd, wvd], axis=1).astype(BF16)
    wo = swa_w_o
    o, l2_k, l2_v = _run_proj(
        _proj_c_kernel, False, xp, mods[2], g, [w_ctx], [], [(kw, F32, "T"), (kw, F32, "T")],
        "proj_attn_swa_context", seq, attend=functools.partial(_attn_c_kernel, False),
        attend_inputs=[_sink_logit_block(swa_sink, seq)], qkv=[(qw, BF16), (2 * kw, BF16), (2 * kw, BF16)])
    xp = finish(o, xp, 2, wo, False)
    tables = list(_rope_tables(dseq, SWA_HEAD_DIM))
    q, kd, vd = _run_proj(_proj_c_kernel, True, xs, mods[2], g, [w_lat], tables,
                          [(qw, BF16), (2 * kw, BF16), (2 * kw, BF16)], "proj_swa_latent", tile=WIDE_TOK_TILE)
    o = _attention_c(swa_sink, q, kd, vd, head_major(cache_l2_k), head_major(cache_l2_v))
    xs = finish(o, xs, 2, wo, True)
    new_l2 = (token_major(l2_k, SWA_KV_HEADS), token_major(l2_v, SWA_KV_HEADS))

    g = norm_g[3]
    hw = NAT_HEADS * NAT_HEAD_DIM
    w = nat_w_qkv
    wo = nat_w_o
    o, l3_k, l3_v = _run_proj(
        _proj_d_kernel, False, xp, mods[3], g, [w], [], [(hw, F32, "T"), (hw, F32, "T")],
        "proj_attn_nat_context", seq, attend=_attn_d_ctx_kernel, qkv=[(hw, BF16)] * 3)
    xp = finish(o, xp, 3, wo, False)
    q, k, v = _run_proj(_proj_d_kernel, True, xs, mods[3], g, [w], [],
                        [(hw, BF16), (hw, BF16), (hw, BF16)], "proj_nat_latent")
    o = _attention_d_lat(q, k, v, head_major(cache_l3_k), head_major(cache_l3_v),
                         _nat_dense_bias(nat_rpb, dseq // GRID_W))
    xs = finish(o, xs, 3, wo, True)
    new_l3 = (token_major(l3_k, NAT_HEADS), token_major(l3_v, NAT_HEADS))

    return (xp.reshape(nb, seq, d), xs) + new_l0 + new_l1 + new_l2 + new_l3
```

```python
import functools

import numpy as np

import jax
import jax.numpy as jnp
from jax import lax
from jax.experimental import pallas as pl
from jax.experimental.pallas import tpu as pltpu

F32 = jnp.float32
BF16 = jnp.bfloat16

D_MODEL = 1024
DEPTH = 4
N_MOD = 6
D_FF = 4 * D_MODEL
GRID_W = 64
ROPE_THETA = 10000.0
NORM_EPS = 1e-6
NEG_INF = -1e30
LOG2E = 1.4426950408889634

ATTN_HEADS, ATTN_KV_HEADS, ATTN_HEAD_DIM = 8, 2, 128
MLA_HEADS, MLA_Q_LORA, MLA_KV_LORA = 16, 384, 256
MLA_NOPE, MLA_ROPE, MLA_V_DIM = 64, 32, 64
MLA_SCALE = (MLA_NOPE + MLA_ROPE) ** -0.5
SWA_HEADS, SWA_KV_HEADS, SWA_HEAD_DIM, SWA_WINDOW = 16, 4, 64, 128
NAT_HEADS, NAT_HEAD_DIM, NAT_WIN_R, NAT_WIN_C = 16, 64, 8, 16

LANES = 128
COND_ROWS = 16
CTX_MOD_ROW = 8
VMEM_LIMIT = 48 * 1024 * 1024

MOD_COL_TILE = 1536
TOK_TILE = 512
WIDE_TOK_TILE = 1024
TOK_CHAIN = 256
MLP_TOK_TILE = 1024
MLP_FF_TILE = 1024
MLP_VMEM_LIMIT = 56 * 1024 * 1024
Q_TILE = 256
LATENT_Q_ROWS = 512
HEADS_PER_STEP = 16
NAT_TILE_ROWS = Q_TILE // GRID_W
NAT_KEY_ROWS = NAT_WIN_R + NAT_TILE_ROWS


def _params(*sem):
    return pltpu.CompilerParams(dimension_semantics=sem, vmem_limit_bytes=VMEM_LIMIT)


def _full(shape):
    nd = len(shape)
    return pl.BlockSpec(shape, lambda *_: (0,) * nd)


def _rms(x, g):
    return x * lax.rsqrt(jnp.mean(x * x, axis=-1, keepdims=True) + NORM_EPS) * g


def _mod(m, i):
    return m[:, i * D_MODEL:(i + 1) * D_MODEL]


def _premod(x, g, m, sub):
    return _rms(x, g * (1.0 + _mod(m, 3 * sub + 1))) + _mod(m, 3 * sub)


def _rope(x, cos, sin_signed, quarter):
    n = x.shape[-1]
    lane = lax.broadcasted_iota(jnp.int32, x.shape, 1)
    first = ((lane // quarter) % 2) == 0
    partner = jnp.where(first, pltpu.roll(x, n - quarter, 1), pltpu.roll(x, quarter, 1))
    return x * cos + partner * sin_signed


def _dot(a, b):
    return jnp.dot(a, b, preferred_element_type=F32)


def _dot_nt(a, b):
    return lax.dot_general(a, b, (((1,), (1,)), ((), ())), preferred_element_type=F32)


def _with_ones(v):
    return jnp.concatenate([v, jnp.ones_like(v)], axis=1)


def _attend(q, segs, sink=None):
    logits = []
    for seg in segs:
        k, bias, transposed = seg[0], seg[2], len(seg) > 3 and seg[3]
        s = _dot(q, k) if transposed else _dot_nt(q, k)
        logits.append(s if bias is None else s + bias)
    m = logits[0].max(axis=-1, keepdims=True)
    for s in logits[1:]:
        m = jnp.maximum(m, s.max(axis=-1, keepdims=True))
    if sink is not None:
        m = jnp.maximum(m, sink)
    acc = None
    for s, seg in zip(logits, segs):
        p = jnp.exp2(s - m).astype(BF16)
        pv = _dot_nt(p, seg[1]) if len(seg) > 3 and seg[3] else _dot(p, seg[1])
        acc = pv if acc is None else acc + pv
    den = acc[:, LANES:LANES + 1]
    if sink is not None:
        den = den + jnp.exp2(sink - m)
    return acc[:, :LANES] / den


def _attend_group(qs, ks, v1s, sink_logits=None):
    tq = qs[0].shape[0]
    s = jnp.concatenate([_dot_nt(q, k) for q, k in zip(qs, ks)], axis=0)
    if sink_logits is not None:
        s = jnp.concatenate([s, sink_logits], axis=1)
        zeros = jnp.zeros((LANES, LANES), BF16)
        tail = jnp.concatenate([zeros, jnp.ones_like(zeros)], axis=1)
        v1s = [jnp.concatenate([v1, tail], axis=0) for v1 in v1s]
    p = jnp.exp2(s - s.max(axis=-1, keepdims=True)).astype(BF16)
    acc = jnp.concatenate([_dot(p[i * tq:(i + 1) * tq], v1) for i, v1 in enumerate(v1s)], axis=0)
    return acc[:, :LANES] / acc[:, LANES:LANES + 1]


def _split_pair(q):
    low = lax.broadcasted_iota(jnp.int32, q.shape, 1) < (LANES // 2)
    zero = jnp.zeros_like(q)
    return [jnp.where(low, q, zero), jnp.where(low, zero, q)]


def _merge_pair(o0, o1):
    low = lax.broadcasted_iota(jnp.int32, o0.shape, 1) < (LANES // 2)
    return jnp.where(low, o0, o1).astype(BF16)


def _attend_pair(q, segs, sinks=None):
    outs = []
    for j, qj in enumerate(_split_pair(q)):
        segs_j = [(s[0], s[1], None if s[2] is None else s[2][j]) + tuple(s[3:]) for s in segs]
        outs.append(_attend(qj, segs_j, None if sinks is None else sinks[j]))
    return _merge_pair(*outs)


def _mods_kernel(cond_ref, w_ref, b_ref, o_ref):
    cnd = cond_ref[...]
    act = cnd * jax.nn.sigmoid(cnd)
    o_ref[...] = _dot(act.astype(BF16), w_ref[...].astype(BF16)) + b_ref[...]


def _modulation(cond, ada_w, ada_b):
    tn = MOD_COL_TILE
    n = N_MOD * D_MODEL
    return pl.pallas_call(
        _mods_kernel,
        grid=(DEPTH, n // tn),
        in_specs=[
            _full((COND_ROWS, D_MODEL)),
            pl.BlockSpec((None, D_MODEL, tn), lambda l, j: (l, 0, j)),
            pl.BlockSpec((None, 1, tn), lambda l, j: (l, 0, j)),
        ],
        out_specs=pl.BlockSpec((None, COND_ROWS, tn), lambda l, j: (l, 0, j)),
        out_shape=jax.ShapeDtypeStruct((DEPTH, COND_ROWS, n), F32),
        compiler_params=_params("arbitrary", "arbitrary"),
        name="adaln_mods",
    )(cond, ada_w, ada_b.reshape(DEPTH, 1, n))


def _proj_a_kernel(latent, x_ref, m_ref, g_ref, w_ref, qn_ref, kn_ref, *rest):
    if latent:
        cos_ref, sin_ref, q_ref, k_ref, v_ref = rest
    else:
        q_ref, k_ref, v_ref, ks_ref, vs_ref = rest
    hd = ATTN_HEAD_DIM
    n_qk = ATTN_HEADS + ATTN_KV_HEADS
    h = _premod(x_ref[...], g_ref[0:1, :], m_ref[...], 0).astype(BF16)
    w = w_ref[...].astype(BF16)
    for c in range(x_ref.shape[0] // TOK_CHAIN):
        rows = slice(c * TOK_CHAIN, (c + 1) * TOK_CHAIN)
        z = _dot(h[rows], w)
        ys = [_rms(z[:, i * hd:(i + 1) * hd], qn_ref[...] if i < ATTN_HEADS else kn_ref[...]) for i in range(n_qk)]
        if not latent:
            for j in range(ATTN_KV_HEADS):
                state_rows = pl.ds(ATTN_KV_HEADS * rows.start + j, TOK_CHAIN, stride=ATTN_KV_HEADS)
                ks_ref[state_rows, :] = ys[ATTN_HEADS + j]
                vs_ref[state_rows, :] = z[:, (n_qk + j) * hd:(n_qk + j + 1) * hd]
        else:
            ys = [y * cos_ref[rows, :] + pltpu.roll(y, hd // 2, 1) * sin_ref[rows, :] for y in ys]
        for i in range(ATTN_HEADS):
            q_ref[rows, i * hd:(i + 1) * hd] = (ys[i] * (hd ** -0.5 * LOG2E)).astype(BF16)
        for j in range(ATTN_KV_HEADS):
            k_ref[rows, j * hd:(j + 1) * hd] = ys[ATTN_HEADS + j].astype(BF16)
        v_ref[rows, :] = z[:, n_qk * hd:].astype(BF16)


def _proj_b_kernel(latent, x_ref, m_ref, g_ref, win_ref, qn_ref, kvn_ref, wuq_ref, wukv_ref, *rest):
    if latent:
        cos_ref, sin_ref, qn_o, qr_o, kn_o, v_o, kp_o = rest
    else:
        qn_o, qr_o, kn_o, v_o, kp_o, ckv_s, kpe_s = rest
    nq, nkv = MLA_Q_LORA, MLA_KV_LORA
    nope_w = MLA_HEADS * MLA_NOPE
    scale = MLA_SCALE * LOG2E
    h = _premod(x_ref[...], g_ref[0:1, :], m_ref[...], 0)
    z = _dot(h.astype(BF16), win_ref[...])
    cq = _rms(z[:, :nq], qn_ref[...])
    ckv = _rms(z[:, nq:nq + nkv], kvn_ref[...])
    kp = z[:, nq + nkv:]
    q = _dot(cq.astype(BF16), wuq_ref[...])
    kv = _dot(ckv.astype(BF16), wukv_ref[...])
    if not latent:
        ckv_s[...] = ckv
        _store_transposed(kpe_s, kp, MLA_ROPE)
    qn_o[...] = (q[:, :nope_w] * scale).astype(BF16)
    for i in range(MLA_HEADS * MLA_ROPE // LANES):
        qr = q[:, nope_w + i * LANES:nope_w + (i + 1) * LANES]
        if latent:
            qr = _rope(qr, cos_ref[...], sin_ref[...], MLA_ROPE // 4)
        qr_o[:, i * LANES:(i + 1) * LANES] = (qr * scale).astype(BF16)
    if latent:
        kp = _rope(kp, cos_ref[...], sin_ref[...], MLA_ROPE // 4)
    kp_o[...] = kp.astype(BF16)
    kn_o[...] = kv[:, :nope_w].astype(BF16)
    v_o[...] = kv[:, nope_w:].astype(BF16)


def _proj_c_kernel(latent, x_ref, m_ref, g_ref, w_ref, *rest):
    if latent:
        cos_ref, sin_ref, q_o, kd_o, vd_o = rest
    else:
        q_o, kd_o, vd_o, ks_o, vs_o = rest
    qw = SWA_HEADS * SWA_HEAD_DIM
    kw = SWA_KV_HEADS * SWA_HEAD_DIM
    h = _premod(x_ref[...], g_ref[0:1, :], m_ref[...], 0)
    z = _dot(h.astype(BF16), w_ref[...])
    off = qw
    if not latent:
        _store_transposed(ks_o, z[:, qw:qw + kw], kw)
        _store_transposed(vs_o, z[:, qw + kw:qw + 2 * kw], kw)
        off = qw + 2 * kw
    scale = SWA_HEAD_DIM ** -0.5 * LOG2E
    for i in range(qw // LANES):
        y = z[:, i * LANES:(i + 1) * LANES]
        if latent:
            y = _rope(y, cos_ref[...], sin_ref[...], SWA_HEAD_DIM // 4)
        q_o[:, i * LANES:(i + 1) * LANES] = (y * scale).astype(BF16)
    for i in range(2 * kw // LANES):
        y = z[:, off + i * LANES:off + (i + 1) * LANES]
        if latent:
            y = _rope(y, cos_ref[...], sin_ref[...], SWA_HEAD_DIM // 4)
        kd_o[:, i * LANES:(i + 1) * LANES] = y.astype(BF16)
    vd_o[...] = z[:, off + 2 * kw:].astype(BF16)


def _proj_d_kernel(latent, x_ref, m_ref, g_ref, w_ref, *rest):
    if latent:
        q_o, k_o, v_o = rest
    else:
        q_o, k_o, v_o, ks_o, vs_o = rest
    hw = NAT_HEADS * NAT_HEAD_DIM
    h = _premod(x_ref[...], g_ref[0:1, :], m_ref[...], 0)
    z = _dot(h.astype(BF16), w_ref[...].astype(BF16))
    q_o[...] = (z[:, :hw] * (NAT_HEAD_DIM ** -0.5 * LOG2E)).astype(BF16)
    k_o[...] = z[:, hw:2 * hw].astype(BF16)
    v_o[...] = z[:, 2 * hw:].astype(BF16)
    if not latent:
        _store_transposed(ks_o, z[:, hw:2 * hw], hw)
        _store_transposed(vs_o, z[:, 2 * hw:], hw)


def _store_transposed(ref, val, width):
    n, _, seq = ref.shape
    for i in range(n):
        ref[i] = val[i * seq:(i + 1) * seq, :].T[:width, :]


def _run_proj(kernel, latent, x, mods_l, g, consts, tables, outs, name, state_seq=None,
              attend=None, attend_inputs=(), qkv=(), tile=TOK_TILE):
    bx, s, _ = x.shape
    tm = tile
    row = (lambda b, j: (b, 0, 0)) if latent else (lambda b, j: (CTX_MOD_ROW, 0, 0))
    in_specs = [
        pl.BlockSpec((None, tm, D_MODEL), lambda b, j: (b, j, 0)),
        pl.BlockSpec((None, 1, N_MOD * D_MODEL), row),
        _full(g.shape),
    ] + [_full(c.shape) for c in consts]
    in_specs += [pl.BlockSpec((tm, LANES), lambda b, j: (j, 0)) for _ in tables]
    in_specs += [_full(a.shape) for a in attend_inputs]
    body = functools.partial(kernel, latent)
    if attend is not None:
        n_proj_in = 3 + len(consts) + len(tables)
        n_att_in = len(attend_inputs)
        proj_body = body
        outs = [(D_MODEL, BF16)] + list(outs)

        def body(*refs):
            ins, att_ins = refs[:n_proj_in], refs[n_proj_in:n_proj_in + n_att_in]
            o_ref = refs[n_proj_in + n_att_in]
            states, scratch = refs[n_proj_in + n_att_in + 1:-len(qkv)], refs[-len(qkv):]
            proj_body(*ins, *scratch, *states)
            for i in range(tm // state_seq):
                rows = slice(i * state_seq, (i + 1) * state_seq)
                attend(*att_ins, *[r.at[rows] for r in scratch], o_ref.at[rows])

    out_specs, out_shape = [], []
    for o in outs:
        if len(o) == 2:
            out_specs.append(pl.BlockSpec((None, tm, o[0]), lambda b, j: (b, j, 0)))
            out_shape.append(jax.ShapeDtypeStruct((bx, s, o[0]), o[1]))
        elif o[2] == "I":
            out_specs.append(pl.BlockSpec((None, tm * o[3], o[0]), lambda b, j: (b, j, 0)))
            out_shape.append(jax.ShapeDtypeStruct((bx, s * o[3], o[0]), o[1]))
        else:
            per_tile = tm // state_seq
            out_specs.append(pl.BlockSpec((per_tile, o[0], state_seq),
                                          lambda b, j: (b * (s // tm) + j, 0, 0)))
            out_shape.append(jax.ShapeDtypeStruct((bx * s // state_seq, o[0], state_seq), o[1]))
    return pl.pallas_call(
        body,
        grid=(bx, s // tm),
        in_specs=in_specs,
        out_specs=out_specs,
        out_shape=out_shape,
        scratch_shapes=[pltpu.VMEM((tm, w), dt) for w, dt in qkv],
        compiler_params=_params("arbitrary", "arbitrary"),
        name=name,
    )(x, mods_l, g, *consts, *tables, *attend_inputs)


def _attn_a_kernel(has_ctx, q_ref, k_ref, v_ref, *rest):
    if has_ctx:
        kc_ref, vc_ref, perm_ref, o_ref = rest
    else:
        (o_ref,) = rest
    group = ATTN_HEADS // ATTN_KV_HEADS
    head_cols = [slice(h * LANES, (h + 1) * LANES) for h in range(ATTN_HEADS)]
    kv_cols = [slice(j * LANES, (j + 1) * LANES) for j in range(ATTN_KV_HEADS)]
    if not has_ctx:
        tq = q_ref.shape[0]
        qs = [jnp.concatenate([q_ref[:, c] for c in head_cols[j * group:(j + 1) * group]], axis=0)
              for j in range(ATTN_KV_HEADS)]
        out = _attend_group(qs, [k_ref[:, c] for c in kv_cols], [_with_ones(v_ref[:, c]) for c in kv_cols])
        for h, c in enumerate(head_cols):
            o_ref[:, c] = out[h * tq:(h + 1) * tq].astype(BF16)
        return
    ctx_len = kc_ref.shape[0] // ATTN_KV_HEADS
    for j, kc in enumerate(kv_cols):
        head_rows = pl.ds(j, ctx_len, stride=ATTN_KV_HEADS)
        k_ctx = _dot(kc_ref[head_rows, :].astype(BF16), perm_ref[...]).astype(BF16)
        segs = [(k_ref[:, kc], _with_ones(v_ref[:, kc]), None),
                (k_ctx, _with_ones(vc_ref[head_rows, :].astype(BF16)), None)]
        for r in range(q_ref.shape[0] // Q_TILE):
            rows = slice(r * Q_TILE, (r + 1) * Q_TILE)
            for c in head_cols[j * group:(j + 1) * group]:
                o_ref[rows, c] = _attend(q_ref[rows, c], segs).astype(BF16)


def _attention_a(q, k, v, kc, vc, perm):
    b, s, qw = q.shape
    kl, kw = k.shape[1:]
    tq = min(LATENT_Q_ROWS, s)
    cache_spec = pl.BlockSpec((None,) + kc.shape[1:], lambda bi, t: (bi, 0, 0))
    return pl.pallas_call(
        functools.partial(_attn_a_kernel, True),
        grid=(b, s // tq),
        in_specs=[
            pl.BlockSpec((None, tq, qw), lambda bi, t: (bi, t, 0)),
            pl.BlockSpec((None, kl, kw), lambda bi, t: (bi, 0, 0)),
            pl.BlockSpec((None, kl, kw), lambda bi, t: (bi, 0, 0)),
            cache_spec, cache_spec, _full(perm.shape),
        ],
        out_specs=pl.BlockSpec((None, tq, qw), lambda bi, t: (bi, t, 0)),
        out_shape=jax.ShapeDtypeStruct(q.shape, BF16),
        compiler_params=_params("arbitrary", "arbitrary"),
        name="attn_gqa_latent",
    )(q, k, v, kc, vc, perm)


def _attn_b_kernel(has_ctx, qn_ref, qr_ref, kn_ref, kp_ref, v_ref, *rest):
    if has_ctx:
        knc_ref, kpc_ref, vc_ref, o_ref = rest
    else:
        (o_ref,) = rest
    tq = min(Q_TILE, qn_ref.shape[0])
    lane = lax.broadcasted_iota(jnp.int32, (tq, LANES), 1)
    low = lane < (LANES // 2)
    quarter = lane // MLA_ROPE
    zero = jnp.zeros((tq, LANES), BF16)
    kp = kp_ref[...]
    kpc = kpc_ref[...] if has_ctx else None
    pairs_per_rope = LANES // MLA_ROPE // 2
    n_pairs = HEADS_PER_STEP // 2
    pair_cols = [slice(p * LANES, (p + 1) * LANES) for p in range(n_pairs)]

    def pair_lhs(p, rows=slice(None)):
        qn = qn_ref[rows, pair_cols[p]]
        rg = p // pairs_per_rope
        qr = qr_ref[rows, rg * LANES:(rg + 1) * LANES]
        return [jnp.concatenate([jnp.where(low if j == 0 else ~low, qn, zero),
                                 jnp.where(quarter == 2 * (p % pairs_per_rope) + j, qr, zero)], axis=1)
                for j in range(2)]

    if not has_ctx:
        qs = [jnp.concatenate(pair_lhs(p), axis=0) for p in range(n_pairs)]
        out = _attend_group(qs, [jnp.concatenate([kn_ref[:, c], kp], axis=1) for c in pair_cols],
                            [_with_ones(v_ref[:, c]) for c in pair_cols])
        for p, c in enumerate(pair_cols):
            o_ref[:, c] = _merge_pair(out[2 * p * tq:(2 * p + 1) * tq], out[(2 * p + 1) * tq:(2 * p + 2) * tq])
        return
    for p, cols in enumerate(pair_cols):
        segs = [(jnp.concatenate([kn_ref[:, cols], kp], axis=1), _with_ones(v_ref[:, cols]), None),
                (jnp.concatenate([knc_ref[:, cols], kpc], axis=1), _with_ones(vc_ref[:, cols]), None)]
        for r in range(qn_ref.shape[0] // tq):
            rows = slice(r * tq, (r + 1) * tq)
            o_ref[rows, cols] = _merge_pair(*[_attend(lhs, segs) for lhs in pair_lhs(p, rows)])


def _attention_b(qn, qr, kn, kp, v, knc=None, kpc=None, vc=None):
    b, s, _ = qn.shape
    kl = kn.shape[1]
    has_ctx = knc is not None
    tq = min(LATENT_Q_ROWS if has_ctx else Q_TILE, s)
    gw = HEADS_PER_STEP * MLA_NOPE
    blk = lambda rows, w, fn: pl.BlockSpec((None, rows, w), fn)
    in_specs = [
        blk(tq, gw, lambda bi, g, t: (bi, t, g)),
        blk(tq, HEADS_PER_STEP * MLA_ROPE, lambda bi, g, t: (bi, t, g)),
        blk(kl, gw, lambda bi, g, t: (bi, 0, g)),
        blk(kl, LANES, lambda bi, g, t: (bi, 0, 0)),
        blk(kl, gw, lambda bi, g, t: (bi, 0, g)),
    ]
    args = [qn, qr, kn, kp, v]
    if has_ctx:
        cl = knc.shape[1]
        in_specs += [
            blk(cl, gw, lambda bi, g, t: (bi, 0, g)),
            blk(cl, LANES, lambda bi, g, t: (bi, 0, 0)),
            blk(cl, gw, lambda bi, g, t: (bi, 0, g)),
        ]
        args += [knc, kpc, vc]
    return pl.pallas_call(
        functools.partial(_attn_b_kernel, has_ctx),
        grid=(b, MLA_HEADS // HEADS_PER_STEP, s // tq),
        in_specs=in_specs,
        out_specs=blk(tq, gw, lambda bi, g, t: (bi, t, g)),
        out_shape=jax.ShapeDtypeStruct(qn.shape, BF16),
        compiler_params=_params("arbitrary", "arbitrary", "arbitrary"),
        name="attn_mla_latent" if has_ctx else "attn_mla_context",
    )(*args)


def _attn_c_kernel(latent, sink_ref, q_ref, kd_ref, vd_ref, *rest):
    if latent:
        kc_ref, vc_ref, o_ref = rest
    else:
        (o_ref,) = rest
    tq = min(Q_TILE, q_ref.shape[0])
    group = SWA_HEADS // SWA_KV_HEADS
    kv_per_step = HEADS_PER_STEP // group
    head0 = pl.program_id(1) * HEADS_PER_STEP
    windows = []
    if latent:
        s_len = kd_ref.shape[0]
        win = tq + 2 * SWA_WINDOW
        for r in range(q_ref.shape[0] // tq):
            t0 = pl.program_id(2) * q_ref.shape[0] + r * tq
            ws = pl.multiple_of(jnp.clip(t0 - SWA_WINDOW, 0, s_len - win), LANES)
            qpos = t0 + lax.broadcasted_iota(jnp.int32, (tq, win), 0)
            kpos = ws + lax.broadcasted_iota(jnp.int32, (tq, win), 1)
            bias = jnp.where(jnp.abs(qpos - kpos) <= SWA_WINDOW, 0.0, NEG_INF).astype(F32)
            windows.append((slice(r * tq, (r + 1) * tq), ws, bias))
    pairs_per_kv = group // 2
    pair_cols = [slice(p * LANES, (p + 1) * LANES) for p in range(HEADS_PER_STEP // 2)]
    kv_cols = [slice(j * LANES, (j + 1) * LANES) for j in range(kv_per_step)]
    if not latent:
        qs = [jnp.concatenate([h for c in pair_cols[j * pairs_per_kv:(j + 1) * pairs_per_kv]
                               for h in _split_pair(q_ref[:, c])], axis=0) for j in range(kv_per_step)]
        out = _attend_group(qs, [kd_ref[:, c] for c in kv_cols], [_with_ones(vd_ref[:, c]) for c in kv_cols],
                            sink_ref[...])
        for p, c in enumerate(pair_cols):
            o_ref[:, c] = _merge_pair(out[2 * p * tq:(2 * p + 1) * tq], out[(2 * p + 1) * tq:(2 * p + 2) * tq])
        return
    sinks = [sink_ref[head0 + h] * LOG2E for h in range(HEADS_PER_STEP)]
    hd = SWA_HEAD_DIM
    for j, kc in enumerate(kv_cols):
        kct = kc_ref[j * hd:(j + 1) * hd, :].astype(BF16)
        vct = vc_ref[j * hd:(j + 1) * hd, :].astype(BF16)
        ctx_seg = (jnp.concatenate([kct, kct], axis=0),
                   jnp.concatenate([vct, vct, jnp.ones((LANES, vct.shape[1]), BF16)], axis=0), None, True)
        for rows, ws, bias in windows:
            segs = [(kd_ref[pl.ds(ws, win), kc], _with_ones(vd_ref[pl.ds(ws, win), kc]), (bias, bias)), ctx_seg]
            for p in range(j * pairs_per_kv, (j + 1) * pairs_per_kv):
                o_ref[rows, pair_cols[p]] = _attend_pair(q_ref[rows, pair_cols[p]], segs, sinks[2 * p:2 * p + 2])


def _sink_logit_block(sink, tq):
    rows = jnp.repeat(sink * LOG2E, tq)[:, None]
    return jnp.where(jnp.arange(LANES)[None, :] == 0, rows, NEG_INF)


def _attention_c(sink, q, kd, vd, kc, vc):
    b, s, _ = q.shape
    kl = kd.shape[1]
    tq = min(LATENT_Q_ROWS, s)
    gw = HEADS_PER_STEP * SWA_HEAD_DIM
    kvw = 2 * SWA_HEAD_DIM * HEADS_PER_STEP // (SWA_HEADS // SWA_KV_HEADS)
    cw, cl = kc.shape[1] * HEADS_PER_STEP // SWA_HEADS, kc.shape[2]
    return pl.pallas_call(
        functools.partial(_attn_c_kernel, True),
        grid=(b, SWA_HEADS // HEADS_PER_STEP, s // tq),
        in_specs=[
            pl.BlockSpec(memory_space=pltpu.SMEM),
            pl.BlockSpec((None, tq, gw), lambda bi, g, t: (bi, t, g)),
            pl.BlockSpec((None, kl, kvw), lambda bi, g, t: (bi, 0, g)),
            pl.BlockSpec((None, kl, kvw), lambda bi, g, t: (bi, 0, g)),
            pl.BlockSpec((None, cw, cl), lambda bi, g, t: (bi, g, 0)),
            pl.BlockSpec((None, cw, cl), lambda bi, g, t: (bi, g, 0)),
        ],
        out_specs=pl.BlockSpec((None, tq, gw), lambda bi, g, t: (bi, t, g)),
        out_shape=jax.ShapeDtypeStruct(q.shape, BF16),
        compiler_params=_params("arbitrary", "arbitrary", "arbitrary"),
        name="attn_swa_latent",
    )(sink, q, kd, vd, kc, vc)


def _attn_d_ctx_kernel(q_ref, k_ref, v_ref, o_ref):
    tq = q_ref.shape[0]
    pair_cols = [slice(p * LANES, (p + 1) * LANES) for p in range(HEADS_PER_STEP // 2)]
    qs = [jnp.concatenate(_split_pair(q_ref[:, c]), axis=0) for c in pair_cols]
    out = _attend_group(qs, [k_ref[:, c] for c in pair_cols], [_with_ones(v_ref[:, c]) for c in pair_cols])
    for p, c in enumerate(pair_cols):
        o_ref[:, c] = _merge_pair(out[2 * p * tq:(2 * p + 1) * tq], out[(2 * p + 1) * tq:(2 * p + 2) * tq])


def _nat_row0(r, rows):
    return min(max(r - NAT_WIN_R // 2, 0), rows - NAT_WIN_R)


def _nat_tile_key_rows(tile, rows):
    first, last = tile * NAT_TILE_ROWS, (tile + 1) * NAT_TILE_ROWS - 1
    need = _nat_row0(last, rows) + NAT_WIN_R - _nat_row0(first, rows)
    return need + need % 2


def _nat_tile_start(tile, rows):
    return min(_nat_row0(tile * NAT_TILE_ROWS, rows), rows - _nat_tile_key_rows(tile, rows))


def _nat_tile_slots(rows):
    seen, slot_of, reps = {}, [], []
    for t in range(rows // NAT_TILE_ROWS):
        ws = _nat_tile_start(t, rows)
        sig = (_nat_tile_key_rows(t, rows),) + tuple(
            (r - ws, _nat_row0(r, rows) - ws) for r in range(t * NAT_TILE_ROWS, (t + 1) * NAT_TILE_ROWS))
        if sig not in seen:
            seen[sig] = len(reps)
            reps.append(t)
        slot_of.append(seen[sig])
    return slot_of, reps


def _attn_d_lat_kernel(q_ref, k_ref, v_ref, kc_ref, vc_ref, bias_ref, o_ref):
    rows = k_ref.shape[0] // GRID_W
    tile = pl.program_id(0)
    ones = jnp.ones((LANES, kc_ref.shape[1]), BF16)

    def body(key_rows, start_row):
        slab = key_rows * GRID_W
        start = pl.multiple_of(start_row * GRID_W, GRID_W)
        for p in range(HEADS_PER_STEP // 2):
            cols = slice(p * LANES, (p + 1) * LANES)
            segs = [(k_ref[pl.ds(start, slab), cols], _with_ones(v_ref[pl.ds(start, slab), cols]),
                     (bias_ref[2 * p, :, :slab], bias_ref[2 * p + 1, :, :slab])),
                    (kc_ref[cols, :].astype(BF16),
                     jnp.concatenate([vc_ref[cols, :].astype(BF16), ones], axis=0), None, True)]
            o_ref[:, cols] = _attend_pair(q_ref[:, cols], segs)

    tiles = range(rows // NAT_TILE_ROWS)
    for key_rows in sorted({_nat_tile_key_rows(t, rows) for t in tiles}):
        members = [t for t in tiles if _nat_tile_key_rows(t, rows) == key_rows]
        cond = functools.reduce(jnp.logical_or, [tile == t for t in members])
        start_row = functools.reduce(lambda acc, t: jnp.where(tile == t, _nat_tile_start(t, rows), acc),
                                     members, jnp.int32(0))
        pl.when(cond)(functools.partial(body, key_rows, start_row))


def _attention_d_lat(q, k, v, kc, vc, bias):
    b, s, _ = q.shape
    cl = kc.shape[2]
    tq = NAT_TILE_ROWS * GRID_W
    slab = NAT_KEY_ROWS * GRID_W
    gw = HEADS_PER_STEP * NAT_HEAD_DIM
    slot_of, _ = _nat_tile_slots(s // GRID_W)

    def slot(t):
        return sum(jnp.where(t == tile, sl, 0) for tile, sl in enumerate(slot_of))

    return pl.pallas_call(
        _attn_d_lat_kernel,
        grid=(s // tq, NAT_HEADS // HEADS_PER_STEP, b),
        in_specs=[
            pl.BlockSpec((None, tq, gw), lambda t, g, bi: (bi, t, g)),
            pl.BlockSpec((None, s, gw), lambda t, g, bi: (bi, 0, g)),
            pl.BlockSpec((None, s, gw), lambda t, g, bi: (bi, 0, g)),
            pl.BlockSpec((None, gw, cl), lambda t, g, bi: (bi, g, 0)),
            pl.BlockSpec((None, gw, cl), lambda t, g, bi: (bi, g, 0)),
            pl.BlockSpec((HEADS_PER_STEP, None, tq, slab), lambda t, g, bi: (g, slot(t), 0, 0)),
        ],
        out_specs=pl.BlockSpec((None, tq, gw), lambda t, g, bi: (bi, t, g)),
        out_shape=jax.ShapeDtypeStruct(q.shape, BF16),
        compiler_params=_params("arbitrary", "arbitrary", "arbitrary"),
        name="attn_nat_latent",
    )(q, k, v, kc, vc, bias)


def _mla_expand_kernel(c_ref, w_ref, kn_ref, v_ref):
    kv = _dot(c_ref[...].astype(BF16), w_ref[...])
    half = kv.shape[1] // 2
    kn_ref[...] = kv[:, :half].astype(BF16)
    v_ref[...] = kv[:, half:].astype(BF16)


def _mla_expand(ckv, w_ukv):
    b, l, c = ckv.shape
    n = w_ukv.shape[1] // 2
    out = pl.BlockSpec((None, l, n), lambda bi: (bi, 0, 0))
    return pl.pallas_call(
        _mla_expand_kernel,
        grid=(b,),
        in_specs=[pl.BlockSpec((None, l, c), lambda bi: (bi, 0, 0)), _full(w_ukv.shape)],
        out_specs=[out, out],
        out_shape=[jax.ShapeDtypeStruct((b, l, n), BF16)] * 2,
        compiler_params=_params("arbitrary"),
        name="mla_expand_cache",
    )(ckv, w_ukv)


def _finish_kernel(o_ref, x_ref, m_ref, g_ref, wo_ref, w1_ref, w2_ref, y_ref, x1_ref, h2_ref):
    f = pl.program_id(2)
    last = pl.num_programs(2) - 1
    chains = [slice(c * TOK_CHAIN, (c + 1) * TOK_CHAIN) for c in range(y_ref.shape[0] // TOK_CHAIN)]

    def mlp_part(h, w1, w2):
        u = jnp.square(jnp.maximum(_dot(h, w1), 0.0)).astype(BF16)
        return _dot(u, w2)

    @pl.when(f == 0)
    def _():
        m = m_ref[...]
        wo = wo_ref[...].astype(BF16)
        for rows in chains:
            a = _dot(o_ref[rows, :], wo)
            x1 = x_ref[rows, :] + _rms(a, _mod(m, 2) * g_ref[1:2, :])
            x1_ref[rows, :] = x1
            h2_ref[rows, :] = _premod(x1, g_ref[2:3, :], m, 1).astype(BF16)
        y_ref[...] = mlp_part(h2_ref[...], w1_ref[...].astype(BF16), w2_ref[...].astype(BF16))

    @pl.when((f > 0) & (f < last))
    def _():
        y_ref[...] += mlp_part(h2_ref[...], w1_ref[...].astype(BF16), w2_ref[...].astype(BF16))

    @pl.when(f == last)
    def _():
        gain = _mod(m_ref[...], 5) * g_ref[3:4, :]
        w1, w2 = w1_ref[...].astype(BF16), w2_ref[...].astype(BF16)
        for rows in chains:
            y = y_ref[rows, :] + mlp_part(h2_ref[rows, :], w1, w2)
            y_ref[rows, :] = x1_ref[rows, :] + _rms(y, gain)


def _finish_layer(o, x, mods_l, g, wo, w1, w2, layer, latent):
    bx, s, _ = x.shape
    tm, tf = MLP_TOK_TILE, MLP_FF_TILE
    row = (lambda b, j, f: (b, 0, 0)) if latent else (lambda b, j, f: (CTX_MOD_ROW, 0, 0))
    tok = lambda w: pl.BlockSpec((None, tm, w), lambda b, j, f: (b, j, 0))
    return pl.pallas_call(
        _finish_kernel,
        grid=(bx, s // tm, D_FF // tf),
        in_specs=[tok(o.shape[-1]), tok(D_MODEL),
                  pl.BlockSpec((None, 1, N_MOD * D_MODEL), row), _full(g.shape), _full(wo.shape),
                  pl.BlockSpec((None, D_MODEL, tf), lambda b, j, f: (layer, 0, f)),
                  pl.BlockSpec((None, tf, D_MODEL), lambda b, j, f: (layer, f, 0))],
        out_specs=tok(D_MODEL),
        out_shape=jax.ShapeDtypeStruct(x.shape, F32),
        scratch_shapes=[pltpu.VMEM((tm, D_MODEL), F32), pltpu.VMEM((tm, D_MODEL), BF16)],
        compiler_params=pltpu.CompilerParams(dimension_semantics=("arbitrary",) * 3,
                                             vmem_limit_bytes=MLP_VMEM_LIMIT),
        name="out_proj_mlp",
    )(o, x, mods_l, g, wo, w1, w2)


def _rope_tables(s, dim):
    quarter = dim // 4
    t = jnp.arange(s)
    pos = jnp.stack([t // GRID_W, t % GRID_W], axis=-1).astype(F32)
    inv = ROPE_THETA ** (-jnp.arange(quarter, dtype=F32) / quarter)
    ang = pos[:, :, None] * inv
    cos = jnp.broadcast_to(jnp.cos(ang)[:, :, None, :], (s, 2, 2, quarter)).reshape(s, dim)
    sign = jnp.array([-1.0, 1.0], F32)[None, None, :, None]
    sin = (jnp.sin(ang)[:, :, None, :] * sign).reshape(s, dim)
    reps = LANES // dim
    return jnp.tile(cos, (1, reps)), jnp.tile(sin, (1, reps))


def _dup_heads(w, heads, dim):
    lead = w.shape[:-1]
    w = w.reshape(lead + (heads, 1, dim))
    return jnp.broadcast_to(w, lead + (heads, 2, dim)).reshape(lead + (heads * 2 * dim,))


def _nat_dense_bias(rpb, rows):
    heads = rpb.shape[0]
    c = np.arange(GRID_W)
    c0 = np.clip(c - NAT_WIN_C // 2, 0, GRID_W - NAT_WIN_C)
    in_c = (c[None, :] >= c0[:, None]) & (c[None, :] < c0[:, None] + NAT_WIN_C)
    dc = c[None, :] - c[:, None] + NAT_WIN_C - 1
    onehot = (dc[None] == np.arange(2 * NAT_WIN_C - 1)[:, None, None]) & in_c[None]
    toe = jnp.einsum("had,dck->hack", rpb * LOG2E, jnp.asarray(onehot, F32),
                     precision=lax.Precision.HIGHEST)
    toe = jnp.where(jnp.asarray(in_c)[None, None], toe, NEG_INF)
    pad = jnp.full((heads, 1, GRID_W, GRID_W), NEG_INF, F32)
    ext = jnp.concatenate([pad, toe, pad], axis=1)
    pairs = jnp.concatenate([ext[:, :-1], ext[:, 1:]], axis=-1)
    slots = len(_nat_tile_slots(rows)[1])
    tq, slab = NAT_TILE_ROWS * GRID_W, NAT_KEY_ROWS * GRID_W
    n_off = 2 * NAT_WIN_R
    return pl.pallas_call(
        functools.partial(_nat_bias_kernel, rows),
        grid=(heads,),
        in_specs=[pl.BlockSpec((None, n_off, GRID_W, LANES), lambda h: (h, 0, 0, 0))],
        out_specs=pl.BlockSpec((None, slots, tq, slab), lambda h: (h, 0, 0, 0)),
        out_shape=jax.ShapeDtypeStruct((heads, slots, tq, slab), F32),
        compiler_params=_params("arbitrary"),
        name="nat_bias_expand",
    )(pairs)


def _nat_bias_kernel(rows, pairs_ref, o_ref):
    low = lax.broadcasted_iota(jnp.int32, (GRID_W, LANES), 1) < GRID_W
    masked = jnp.full((GRID_W, LANES), NEG_INF, F32)
    reps = _nat_tile_slots(rows)[1]
    for slot, i in ((s, i) for s in range(len(reps)) for i in range(NAT_TILE_ROWS)):
        tile = reps[slot]
        r = tile * NAT_TILE_ROWS + i
        r0 = _nat_row0(r, rows)
        ws = _nat_tile_start(tile, rows)
        for jb in range(NAT_KEY_ROWS // 2):
            kr = ws + 2 * jb
            ok_lo, ok_hi = r0 <= kr < r0 + NAT_WIN_R, r0 <= kr + 1 < r0 + NAT_WIN_R
            blk = masked
            if ok_lo or ok_hi:
                blk = pairs_ref[kr - r + NAT_WIN_R]
                if not ok_lo:
                    blk = jnp.where(low, NEG_INF, blk)
                elif not ok_hi:
                    blk = jnp.where(low, blk, NEG_INF)
            o_ref[slot, i * GRID_W:(i + 1) * GRID_W, jb * LANES:(jb + 1) * LANES] = blk


def kernel(x_prompt, x_sample, cache_l0_k, cache_l0_v, cache_l1_ckv, cache_l1_kpe, cache_l2_k, cache_l2_v, cache_l3_k, cache_l3_v, c, c_ctx, ada_w, ada_b, norm_g, mlp_w1, mlp_w2, attn_w_qkv, attn_q_norm, attn_k_norm, attn_w_o, mla_w_in, mla_q_norm, mla_kv_norm, mla_w_uq, mla_w_ukv, mla_w_o, swa_w_qkv, swa_sink, swa_w_o, nat_w_qkv, nat_rpb, nat_w_o):
    nb, seq, d = x_prompt.shape
    db, dseq, _ = x_sample.shape
    past = cache_l0_k.shape[1]
    ctx_b = nb * seq // dseq
    xp = x_prompt.reshape(ctx_b, dseq, d)
    xs = x_sample

    cond = jnp.zeros((COND_ROWS, d), F32).at[:db].set(c).at[CTX_MOD_ROW].set(c_ctx)
    mods = _modulation(cond, ada_w, ada_b).reshape(DEPTH, COND_ROWS, 1, N_MOD * d)

    row = lambda v: v.reshape(1, -1)

    def head_major(cache):
        b_, l_, h_, dh = cache.shape
        return cache.transpose(0, 2, 3, 1).reshape(b_, h_ * dh, l_)

    def token_major(state, heads):
        b_, w_, l_ = state.shape
        return state.reshape(b_, heads, w_ // heads, l_).transpose(0, 3, 1, 2)

    def finish(o, x, layer, wo, latent):
        return _finish_layer(o, x, mods[layer], norm_g[layer], wo, mlp_w1, mlp_w2, layer, latent)

    g = norm_g[0]
    w = attn_w_qkv
    wo = attn_w_o
    consts = [w, row(attn_q_norm), row(attn_k_norm)]
    kvw = ATTN_KV_HEADS * ATTN_HEAD_DIM
    o, l0_k, l0_v = _run_proj(
        _proj_a_kernel, False, xp, mods[0], g, consts, [],
        [(ATTN_HEAD_DIM, F32, "I", ATTN_KV_HEADS)] * 2, "proj_attn_gqa_context",
        seq, attend=functools.partial(_attn_a_kernel, False), qkv=[(d, BF16), (kvw, BF16), (kvw, BF16)])
    xp = finish(o, xp, 0, wo, False)
    quarter = ATTN_HEAD_DIM // 4
    perm = np.arange(ATTN_HEAD_DIM).reshape(2, 2, quarter).transpose(1, 0, 2).reshape(-1)
    qk_w = (ATTN_HEADS + ATTN_KV_HEADS) * ATTN_HEAD_DIM
    w_qk = attn_w_qkv[:, :qk_w].reshape(d, -1, 2, 2, quarter).transpose(0, 1, 3, 2, 4).reshape(d, qk_w)
    w_lat = jnp.concatenate([w_qk, attn_w_qkv[:, qk_w:]], axis=1).astype(BF16)
    consts_lat = [w_lat, row(attn_q_norm[perm]), row(attn_k_norm[perm])]
    tables = [t[:, perm] for t in _rope_tables(dseq, ATTN_HEAD_DIM)]
    q, k, v = _run_proj(_proj_a_kernel, True, xs, mods[0], g, consts_lat, tables,
                        [(d, BF16), (kvw, BF16), (kvw, BF16)], "proj_gqa_latent")
    perm_matrix = np.zeros((ATTN_HEAD_DIM, ATTN_HEAD_DIM), np.float32)
    perm_matrix[perm, np.arange(ATTN_HEAD_DIM)] = 1.0
    o = _attention_a(q, k, v, cache_l0_k.reshape(db, past * ATTN_KV_HEADS, ATTN_HEAD_DIM),
                     cache_l0_v.reshape(db, past * ATTN_KV_HEADS, ATTN_HEAD_DIM), jnp.asarray(perm_matrix, BF16))
    xs = finish(o, xs, 0, wo, True)
    new_l0 = (l0_k.reshape(nb, seq, ATTN_KV_HEADS, ATTN_HEAD_DIM),
              l0_v.reshape(nb, seq, ATTN_KV_HEADS, ATTN_HEAD_DIM))

    g = norm_g[1]
    nq, nkv = MLA_Q_LORA, MLA_KV_LORA
    w_in = jnp.concatenate([mla_w_in[:, :nq + nkv]] + [mla_w_in[:, nq + nkv:]] * (LANES // MLA_ROPE),
                           axis=1).astype(BF16)
    wuq = mla_w_uq.reshape(nq, MLA_HEADS, MLA_NOPE + MLA_ROPE)
    wuq = jnp.concatenate([wuq[:, :, :MLA_NOPE].reshape(nq, -1), wuq[:, :, MLA_NOPE:].reshape(nq, -1)],
                          axis=1).astype(BF16)
    wukv = mla_w_ukv.reshape(nkv, MLA_HEADS, MLA_NOPE + MLA_V_DIM)
    wukv = jnp.concatenate([wukv[:, :, :MLA_NOPE].reshape(nkv, -1), wukv[:, :, MLA_NOPE:].reshape(nkv, -1)],
                           axis=1).astype(BF16)
    wo = mla_w_o
    consts = [w_in, row(mla_q_norm), row(mla_kv_norm), wuq, wukv]
    hw = MLA_HEADS * MLA_NOPE
    rw = MLA_HEADS * MLA_ROPE
    outs = [(hw, BF16), (rw, BF16), (hw, BF16), (hw, BF16), (LANES, BF16)]
    o, l1_ckv, l1_kpe = _run_proj(
        _proj_b_kernel, False, xp, mods[1], g, consts, [], [(nkv, F32), (MLA_ROPE, F32, "T")],
        "proj_attn_mla_context", seq, qkv=outs,
        attend=lambda qn, qr, kn, v, kp, o_ref: _attn_b_kernel(False, qn, qr, kn, kp, v, o_ref))
    xp = finish(o, xp, 1, wo, False)
    tables = list(_rope_tables(dseq, MLA_ROPE))
    qn, qr, kn, v, kp = _run_proj(_proj_b_kernel, True, xs, mods[1], g, consts, tables, outs,
                                  "proj_mla_latent", tile=WIDE_TOK_TILE)
    knc, vc = _mla_expand(cache_l1_ckv, wukv)
    kpc = jnp.tile(cache_l1_kpe, (1, 1, LANES // MLA_ROPE)).astype(BF16)
    o = _attention_b(qn, qr, kn, kp, v, knc, kpc, vc)
    xs = finish(o, xs, 1, wo, True)
    new_l1 = (l1_ckv.reshape(nb, seq, nkv), l1_kpe.transpose(0, 2, 1))

    g = norm_g[2]
    qw = SWA_HEADS * SWA_HEAD_DIM
    kw = SWA_KV_HEADS * SWA_HEAD_DIM
    wq, wk, wv = swa_w_qkv[:, :qw], swa_w_qkv[:, qw:qw + kw], swa_w_qkv[:, qw + kw:]
    wkd = _dup_heads(wk, SWA_KV_HEADS, SWA_HEAD_DIM)
    wvd = _dup_heads(wv, SWA_KV_HEADS, SWA_HEAD_DIM)
    w_ctx = jnp.concatenate([wq, wk, wv, wkd, wvd], axis=1).astype(BF16)
    w_lat = jnp.concatenate([wq, wkd, wvd], axis=1).astype(BF16)
    wo = swa_w_o
    o, l2_k, l2_v = _run_proj(
        _proj_c_kernel, False, xp, mods[2], g, [w_ctx], [], [(kw, F32, "T"), (kw, F32, "T")],
        "proj_attn_swa_context", seq, attend=functools.partial(_attn_c_kernel, False),
        attend_inputs=[_sink_logit_block(swa_sink, seq)], qkv=[(qw, BF16), (2 * kw, BF16), (2 * kw, BF16)])
    xp = finish(o, xp, 2, wo, False)
    tables = list(_rope_tables(dseq, SWA_HEAD_DIM))
    q, kd, vd = _run_proj(_proj_c_kernel, True, xs, mods[2], g, [w_lat], tables,
                          [(qw, BF16), (2 * kw, BF16), (2 * kw, BF16)], "proj_swa_latent", tile=WIDE_TOK_TILE)
    o = _attention_c(swa_sink, q, kd, vd, head_major(cache_l2_k), head_major(cache_l2_v))
    xs = finish(o, xs, 2, wo, True)
    new_l2 = (token_major(l2_k, SWA_KV_HEADS), token_major(l2_v, SWA_KV_HEADS))

    g = norm_g[3]
    hw = NAT_HEADS * NAT_HEAD_DIM
    w = nat_w_qkv
    wo = nat_w_o
    o, l3_k, l3_v = _run_proj(
        _proj_d_kernel, False, xp, mods[3], g, [w], [], [(hw, F32, "T"), (hw, F32, "T")],
        "proj_attn_nat_context", seq, attend=_attn_d_ctx_kernel, qkv=[(hw, BF16)] * 3)
    xp = finish(o, xp, 3, wo, False)
    q, k, v = _run_proj(_proj_d_kernel, True, xs, mods[3], g, [w], [],
                        [(hw, BF16), (hw, BF16), (hw, BF16)], "proj_nat_latent")
    o = _attention_d_lat(q, k, v, head_major(cache_l3_k), head_major(cache_l3_v),
                         _nat_dense_bias(nat_rpb, dseq // GRID_W))
    xs = finish(o, xs, 3, wo, True)
    new_l3 = (token_major(l3_k, NAT_HEADS), token_major(l3_v, NAT_HEADS))

    return (xp.reshape(nb, seq, d), xs) + new_l0 + new_l1 + new_l2 + new_l3
```

```python
import functools

import numpy as np

import jax
import jax.numpy as jnp
from jax import lax
from jax.experimental import pallas as pl
from jax.experimental.pallas import tpu as pltpu

F32 = jnp.float32
BF16 = jnp.bfloat16

D_MODEL = 1024
DEPTH = 4
N_MOD = 6
D_FF = 4 * D_MODEL
GRID_W = 64
ROPE_THETA = 10000.0
NORM_EPS = 1e-6
NEG_INF = -1e30
LOG2E = 1.4426950408889634

ATTN_HEADS, ATTN_KV_HEADS, ATTN_HEAD_DIM = 8, 2, 128
MLA_HEADS, MLA_Q_LORA, MLA_KV_LORA = 16, 384, 256
MLA_NOPE, MLA_ROPE, MLA_V_DIM = 64, 32, 64
MLA_SCALE = (MLA_NOPE + MLA_ROPE) ** -0.5
SWA_HEADS, SWA_KV_HEADS, SWA_HEAD_DIM, SWA_WINDOW = 16, 4, 64, 128
NAT_HEADS, NAT_HEAD_DIM, NAT_WIN_R, NAT_WIN_C = 16, 64, 8, 16

LANES = 128
COND_ROWS = 16
CTX_MOD_ROW = 8
VMEM_LIMIT = 48 * 1024 * 1024

MOD_COL_TILE = 1536
TOK_TILE = 512
WIDE_TOK_TILE = 1024
TOK_CHAIN = 256
MLP_TOK_TILE = 1024
MLP_FF_TILE = 1024
MLP_VMEM_LIMIT = 56 * 1024 * 1024
Q_TILE = 256
LATENT_Q_ROWS = 512
HEADS_PER_STEP = 16
NAT_TILE_ROWS = Q_TILE // GRID_W
NAT_KEY_ROWS = NAT_WIN_R + NAT_TILE_ROWS


def _params(*sem):
    return pltpu.CompilerParams(dimension_semantics=sem, vmem_limit_bytes=VMEM_LIMIT)


def _full(shape):
    nd = len(shape)
    return pl.BlockSpec(shape, lambda *_: (0,) * nd)


def _rms(x, g):
    return x * lax.rsqrt(jnp.mean(x * x, axis=-1, keepdims=True) + NORM_EPS) * g


def _mod(m, i):
    return m[:, i * D_MODEL:(i + 1) * D_MODEL]


def _premod(x, g, m, sub):
    return _rms(x, g * (1.0 + _mod(m, 3 * sub + 1))) + _mod(m, 3 * sub)


def _rope(x, cos, sin_signed, quarter):
    n = x.shape[-1]
    lane = lax.broadcasted_iota(jnp.int32, x.shape, 1)
    first = ((lane // quarter) % 2) == 0
    partner = jnp.where(first, pltpu.roll(x, n - quarter, 1), pltpu.roll(x, quarter, 1))
    return x * cos + partner * sin_signed


def _dot(a, b):
    return jnp.dot(a, b, preferred_element_type=F32)


def _dot_nt(a, b):
    return lax.dot_general(a, b, (((1,), (1,)), ((), ())), preferred_element_type=F32)


def _with_ones(v):
    return jnp.concatenate([v, jnp.ones_like(v)], axis=1)


def _attend(q, segs, sink=None):
    logits = []
    for seg in segs:
        k, bias, transposed = seg[0], seg[2], len(seg) > 3 and seg[3]
        s = _dot(q, k) if transposed else _dot_nt(q, k)
        logits.append(s if bias is None else s + bias)
    m = logits[0].max(axis=-1, keepdims=True)
    for s in logits[1:]:
        m = jnp.maximum(m, s.max(axis=-1, keepdims=True))
    if sink is not None:
        m = jnp.maximum(m, sink)
    acc = None
    for s, seg in zip(logits, segs):
        p = jnp.exp2(s - m).astype(BF16)
        pv = _dot_nt(p, seg[1]) if len(seg) > 3 and seg[3] else _dot(p, seg[1])
        acc = pv if acc is None else acc + pv
    den = acc[:, LANES:LANES + 1]
    if sink is not None:
        den = den + jnp.exp2(sink - m)
    return acc[:, :LANES] / den


def _attend_group(qs, ks, v1s, sink_logits=None):
    tq = qs[0].shape[0]
    s = jnp.concatenate([_dot_nt(q, k) for q, k in zip(qs, ks)], axis=0)
    if sink_logits is not None:
        s = jnp.concatenate([s, sink_logits], axis=1)
        zeros = jnp.zeros((LANES, LANES), BF16)
        tail = jnp.concatenate([zeros, jnp.ones_like(zeros)], axis=1)
        v1s = [jnp.concatenate([v1, tail], axis=0) for v1 in v1s]
    p = jnp.exp2(s - s.max(axis=-1, keepdims=True)).astype(BF16)
    acc = jnp.concatenate([_dot(p[i * tq:(i + 1) * tq], v1) for i, v1 in enumerate(v1s)], axis=0)
    return acc[:, :LANES] / acc[:, LANES:LANES + 1]


def _split_pair(q):
    low = lax.broadcasted_iota(jnp.int32, q.shape, 1) < (LANES // 2)
    zero = jnp.zeros_like(q)
    return [jnp.where(low, q, zero), jnp.where(low, zero, q)]


def _merge_pair(o0, o1):
    low = lax.broadcasted_iota(jnp.int32, o0.shape, 1) < (LANES // 2)
    return jnp.where(low, o0, o1).astype(BF16)


def _attend_pair(q, segs, sinks=None):
    outs = []
    for j, qj in enumerate(_split_pair(q)):
        segs_j = [(s[0], s[1], None if s[2] is None else s[2][j]) + tuple(s[3:]) for s in segs]
        outs.append(_attend(qj, segs_j, None if sinks is None else sinks[j]))
    return _merge_pair(*outs)


def _mods_kernel(cond_ref, w_ref, b_ref, o_ref):
    cnd = cond_ref[...]
    act = cnd * jax.nn.sigmoid(cnd)
    o_ref[...] = _dot(act.astype(BF16), w_ref[...].astype(BF16)) + b_ref[...]


def _modulation(cond, ada_w, ada_b):
    tn = MOD_COL_TILE
    n = N_MOD * D_MODEL
    return pl.pallas_call(
        _mods_kernel,
        grid=(DEPTH, n // tn),
        in_specs=[
            _full((COND_ROWS, D_MODEL)),
            pl.BlockSpec((None, D_MODEL, tn), lambda l, j: (l, 0, j)),
            pl.BlockSpec((None, 1, tn), lambda l, j: (l, 0, j)),
        ],
        out_specs=pl.BlockSpec((None, COND_ROWS, tn), lambda l, j: (l, 0, j)),
        out_shape=jax.ShapeDtypeStruct((DEPTH, COND_ROWS, n), F32),
        compiler_params=_params("arbitrary", "arbitrary"),
        name="adaln_mods",
    )(cond, ada_w, ada_b.reshape(DEPTH, 1, n))


def _proj_a_kernel(latent, x_ref, m_ref, g_ref, w_ref, qn_ref, kn_ref, *rest):
    if latent:
        cos_ref, sin_ref, q_ref, k_ref, v_ref = rest
    else:
        q_ref, k_ref, v_ref, ks_ref, vs_ref = rest
    hd = ATTN_HEAD_DIM
    n_qk = ATTN_HEADS + ATTN_KV_HEADS
    h = _premod(x_ref[...], g_ref[0:1, :], m_ref[...], 0).astype(BF16)
    w = w_ref[...].astype(BF16)
    for c in range(x_ref.shape[0] // TOK_CHAIN):
        rows = slice(c * TOK_CHAIN, (c + 1) * TOK_CHAIN)
        z = _dot(h[rows], w)
        ys = [_rms(z[:, i * hd:(i + 1) * hd], qn_ref[...] if i < ATTN_HEADS else kn_ref[...]) for i in range(n_qk)]
        if not latent:
            for j in range(ATTN_KV_HEADS):
                state_rows = pl.ds(ATTN_KV_HEADS * rows.start + j, TOK_CHAIN, stride=ATTN_KV_HEADS)
                ks_ref[state_rows, :] = ys[ATTN_HEADS + j]
                vs_ref[state_rows, :] = z[:, (n_qk + j) * hd:(n_qk + j + 1) * hd]
        else:
            ys = [y * cos_ref[rows, :] + pltpu.roll(y, hd // 2, 1) * sin_ref[rows, :] for y in ys]
        for i in range(ATTN_HEADS):
            q_ref[rows, i * hd:(i + 1) * hd] = (ys[i] * (hd ** -0.5 * LOG2E)).astype(BF16)
        for j in range(ATTN_KV_HEADS):
            k_ref[rows, j * hd:(j + 1) * hd] = ys[ATTN_HEADS + j].astype(BF16)
        v_ref[rows, :] = z[:, n_qk * hd:].astype(BF16)


def _proj_b_kernel(latent, x_ref, m_ref, g_ref, win_ref, qn_ref, kvn_ref, wuq_ref, wukv_ref, *rest):
    if latent:
        cos_ref, sin_ref, qn_o, qr_o, kn_o, v_o, kp_o = rest
    else:
        qn_o, qr_o, kn_o, v_o, kp_o, ckv_s, kpe_s = rest
    nq, nkv = MLA_Q_LORA, MLA_KV_LORA
    nope_w = MLA_HEADS * MLA_NOPE
    scale = MLA_SCALE * LOG2E
    h = _premod(x_ref[...], g_ref[0:1, :], m_ref[...], 0)
    z = _dot(h.astype(BF16), win_ref[...])
    cq = _rms(z[:, :nq], qn_ref[...])
    ckv = _rms(z[:, nq:nq + nkv], kvn_ref[...])
    kp = z[:, nq + nkv:]
    q = _dot(cq.astype(BF16), wuq_ref[...])
    kv = _dot(ckv.astype(BF16), wukv_ref[...])
    if not latent:
        ckv_s[...] = ckv
        _store_transposed(kpe_s, kp, MLA_ROPE)
    qn_o[...] = (q[:, :nope_w] * scale).astype(BF16)
    for i in range(MLA_HEADS * MLA_ROPE // LANES):
        qr = q[:, nope_w + i * LANES:nope_w + (i + 1) * LANES]
        if latent:
            qr = _rope(qr, cos_ref[...], sin_ref[...], MLA_ROPE // 4)
        qr_o[:, i * LANES:(i + 1) * LANES] = (qr * scale).astype(BF16)
    if latent:
        kp = _rope(kp, cos_ref[...], sin_ref[...], MLA_ROPE // 4)
    kp_o[...] = kp.astype(BF16)
    kn_o[...] = kv[:, :nope_w].astype(BF16)
    v_o[...] = kv[:, nope_w:].astype(BF16)


def _proj_c_kernel(latent, x_ref, m_ref, g_ref, w_ref, *rest):
    if latent:
        cos_ref, sin_ref, q_o, kd_o, vd_o = rest
    else:
        q_o, kd_o, vd_o, ks_o, vs_o = rest
    qw = SWA_HEADS * SWA_HEAD_DIM
    kw = SWA_KV_HEADS * SWA_HEAD_DIM
    h = _premod(x_ref[...], g_ref[0:1, :], m_ref[...], 0)
    z = _dot(h.astype(BF16), w_ref[...])
    off = qw
    if not latent:
        _store_transposed(ks_o, z[:, qw:qw + kw], kw)
        _store_transposed(vs_o, z[:, qw + kw:qw + 2 * kw], kw)
        off = qw + 2 * kw
    scale = SWA_HEAD_DIM ** -0.5 * LOG2E
    for i in range(qw // LANES):
        y = z[:, i * LANES:(i + 1) * LANES]
        if latent:
            y = _rope(y, cos_ref[...], sin_ref[...], SWA_HEAD_DIM // 4)
        q_o[:, i * LANES:(i + 1) * LANES] = (y * scale).astype(BF16)
    for i in range(2 * kw // LANES):
        y = z[:, off + i * LANES:off + (i + 1) * LANES]
        if latent:
            y = _rope(y, cos_ref[...], sin_ref[...], SWA_HEAD_DIM // 4)
        kd_o[:, i * LANES:(i + 1) * LANES] = y.astype(BF16)
    vd_o[...] = z[:, off + 2 * kw:].astype(BF16)


def _proj_d_kernel(latent, x_ref, m_ref, g_ref, w_ref, *rest):
    if latent:
        q_o, k_o, v_o = rest
    else:
        q_o, k_o, v_o, ks_o, vs_o = rest
    hw = NAT_HEADS * NAT_HEAD_DIM
    h = _premod(x_ref[...], g_ref[0:1, :], m_ref[...], 0)
    z = _dot(h.astype(BF16), w_ref[...].astype(BF16))
    q_o[...] = (z[:, :hw] * (NAT_HEAD_DIM ** -0.5 * LOG2E)).astype(BF16)
    k_o[...] = z[:, hw:2 * hw].astype(BF16)
    v_o[...] = z[:, 2 * hw:].astype(BF16)
    if not latent:
        _store_transposed(ks_o, z[:, hw:2 * hw], hw)
        _store_transposed(vs_o, z[:, 2 * hw:], hw)


def _store_transposed(ref, val, width):
    n, _, seq = ref.shape
    for i in range(n):
        ref[i] = val[i * seq:(i + 1) * seq, :].T[:width, :]


def _run_proj(kernel, latent, x, mods_l, g, consts, tables, outs, name, state_seq=None,
              attend=None, attend_inputs=(), qkv=(), tile=TOK_TILE):
    bx, s, _ = x.shape
    tm = tile
    row = (lambda b, j: (b, 0, 0)) if latent else (lambda b, j: (CTX_MOD_ROW, 0, 0))
    in_specs = [
        pl.BlockSpec((None, tm, D_MODEL), lambda b, j: (b, j, 0)),
        pl.BlockSpec((None, 1, N_MOD * D_MODEL), row),
        _full(g.shape),
    ] + [_full(c.shape) for c in consts]
    in_specs += [pl.BlockSpec((tm, LANES), lambda b, j: (j, 0)) for _ in tables]
    in_specs += [_full(a.shape) for a in attend_inputs]
    body = functools.partial(kernel, latent)
    if attend is not None:
        n_proj_in = 3 + len(consts) + len(tables)
        n_att_in = len(attend_inputs)
        proj_body = body
        outs = [(D_MODEL, BF16)] + list(outs)

        def body(*refs):
            ins, att_ins = refs[:n_proj_in], refs[n_proj_in:n_proj_in + n_att_in]
            o_ref = refs[n_proj_in + n_att_in]
            states, scratch = refs[n_proj_in + n_att_in + 1:-len(qkv)], refs[-len(qkv):]
            proj_body(*ins, *scratch, *states)
            for i in range(tm // state_seq):
                rows = slice(i * state_seq, (i + 1) * state_seq)
                attend(*att_ins, *[r.at[rows] for r in scratch], o_ref.at[rows])

    out_specs, out_shape = [], []
    for o in outs:
        if len(o) == 2:
            out_specs.append(pl.BlockSpec((None, tm, o[0]), lambda b, j: (b, j, 0)))
            out_shape.append(jax.ShapeDtypeStruct((bx, s, o[0]), o[1]))
        elif o[2] == "I":
            out_specs.append(pl.BlockSpec((None, tm * o[3], o[0]), lambda b, j: (b, j, 0)))
            out_shape.append(jax.ShapeDtypeStruct((bx, s * o[3], o[0]), o[1]))
        else:
            per_tile = tm // state_seq
            out_specs.append(pl.BlockSpec((per_tile, o[0], state_seq),
                                          lambda b, j: (b * (s // tm) + j, 0, 0)))
            out_shape.append(jax.ShapeDtypeStruct((bx * s // state_seq, o[0], state_seq), o[1]))
    return pl.pallas_call(
        body,
        grid=(bx, s // tm),
        in_specs=in_specs,
        out_specs=out_specs,
        out_shape=out_shape,
        scratch_shapes=[pltpu.VMEM((tm, w), dt) for w, dt in qkv],
        compiler_params=pltpu.CompilerParams(
            dimension_semantics=("arbitrary", "arbitrary"), vmem_limit_bytes=VMEM_LIMIT,
            allow_input_fusion=[False] * 3 + [True] * len(consts) + [False] * (len(tables) + len(attend_inputs))),
        name=name,
    )(x, mods_l, g, *consts, *tables, *attend_inputs)


def _attn_a_kernel(has_ctx, q_ref, k_ref, v_ref, *rest):
    if has_ctx:
        kc_ref, vc_ref, perm_ref, o_ref = rest
    else:
        (o_ref,) = rest
    group = ATTN_HEADS // ATTN_KV_HEADS
    head_cols = [slice(h * LANES, (h + 1) * LANES) for h in range(ATTN_HEADS)]
    kv_cols = [slice(j * LANES, (j + 1) * LANES) for j in range(ATTN_KV_HEADS)]
    if not has_ctx:
        tq = q_ref.shape[0]
        qs = [jnp.concatenate([q_ref[:, c] for c in head_cols[j * group:(j + 1) * group]], axis=0)
              for j in range(ATTN_KV_HEADS)]
        out = _attend_group(qs, [k_ref[:, c] for c in kv_cols], [_with_ones(v_ref[:, c]) for c in kv_cols])
        for h, c in enumerate(head_cols):
            o_ref[:, c] = out[h * tq:(h + 1) * tq].astype(BF16)
        return
    ctx_len = kc_ref.shape[0] // ATTN_KV_HEADS
    for j, kc in enumerate(kv_cols):
        head_rows = pl.ds(j, ctx_len, stride=ATTN_KV_HEADS)
        k_ctx = _dot(kc_ref[head_rows, :].astype(BF16), perm_ref[...]).astype(BF16)
        segs = [(k_ref[:, kc], _with_ones(v_ref[:, kc]), None),
                (k_ctx, _with_ones(vc_ref[head_rows, :].astype(BF16)), None)]
        for r in range(q_ref.shape[0] // Q_TILE):
            rows = slice(r * Q_TILE, (r + 1) * Q_TILE)
            for c in head_cols[j * group:(j + 1) * group]:
                o_ref[rows, c] = _attend(q_ref[rows, c], segs).astype(BF16)


def _attention_a(q, k, v, kc, vc, perm):
    b, s, qw = q.shape
    kl, kw = k.shape[1:]
    tq = min(LATENT_Q_ROWS, s)
    cache_spec = pl.BlockSpec((None,) + kc.shape[1:], lambda bi, t: (bi, 0, 0))
    return pl.pallas_call(
        functools.partial(_attn_a_kernel, True),
        grid=(b, s // tq),
        in_specs=[
            pl.BlockSpec((None, tq, qw), lambda bi, t: (bi, t, 0)),
            pl.BlockSpec((None, kl, kw), lambda bi, t: (bi, 0, 0)),
            pl.BlockSpec((None, kl, kw), lambda bi, t: (bi, 0, 0)),
            cache_spec, cache_spec, _full(perm.shape),
        ],
        out_specs=pl.BlockSpec((None, tq, qw), lambda bi, t: (bi, t, 0)),
        out_shape=jax.ShapeDtypeStruct(q.shape, BF16),
        compiler_params=_params("arbitrary", "arbitrary"),
        name="attn_gqa_latent",
    )(q, k, v, kc, vc, perm)


def _attn_b_kernel(has_ctx, qn_ref, qr_ref, kn_ref, kp_ref, v_ref, *rest):
    if has_ctx:
        knc_ref, kpc_ref, vc_ref, o_ref = rest
    else:
        (o_ref,) = rest
    tq = min(Q_TILE, qn_ref.shape[0])
    lane = lax.broadcasted_iota(jnp.int32, (tq, LANES), 1)
    low = lane < (LANES // 2)
    quarter = lane // MLA_ROPE
    zero = jnp.zeros((tq, LANES), BF16)
    kp = kp_ref[...]
    kpc = kpc_ref[...] if has_ctx else None
    pairs_per_rope = LANES // MLA_ROPE // 2
    n_pairs = HEADS_PER_STEP // 2
    pair_cols = [slice(p * LANES, (p + 1) * LANES) for p in range(n_pairs)]

    def pair_lhs(p, rows=slice(None)):
        qn = qn_ref[rows, pair_cols[p]]
        rg = p // pairs_per_rope
        qr = qr_ref[rows, rg * LANES:(rg + 1) * LANES]
        return [jnp.concatenate([jnp.where(low if j == 0 else ~low, qn, zero),
                                 jnp.where(quarter == 2 * (p % pairs_per_rope) + j, qr, zero)], axis=1)
                for j in range(2)]

    if not has_ctx:
        qs = [jnp.concatenate(pair_lhs(p), axis=0) for p in range(n_pairs)]
        out = _attend_group(qs, [jnp.concatenate([kn_ref[:, c], kp], axis=1) for c in pair_cols],
                            [_with_ones(v_ref[:, c]) for c in pair_cols])
        for p, c in enumerate(pair_cols):
            o_ref[:, c] = _merge_pair(out[2 * p * tq:(2 * p + 1) * tq], out[(2 * p + 1) * tq:(2 * p + 2) * tq])
        return
    for p, cols in enumerate(pair_cols):
        segs = [(jnp.concatenate([kn_ref[:, cols], kp], axis=1), _with_ones(v_ref[:, cols]), None),
                (jnp.concatenate([knc_ref[:, cols], kpc], axis=1), _with_ones(vc_ref[:, cols]), None)]
        for r in range(qn_ref.shape[0] // tq):
            rows = slice(r * tq, (r + 1) * tq)
            o_ref[rows, cols] = _merge_pair(*[_attend(lhs, segs) for lhs in pair_lhs(p, rows)])


def _attention_b(qn, qr, kn, kp, v, knc=None, kpc=None, vc=None):
    b, s, _ = qn.shape
    kl = kn.shape[1]
    has_ctx = knc is not None
    tq = min(LATENT_Q_ROWS if has_ctx else Q_TILE, s)
    gw = HEADS_PER_STEP * MLA_NOPE
    blk = lambda rows, w, fn: pl.BlockSpec((None, rows, w), fn)
    in_specs = [
        blk(tq, gw, lambda bi, g, t: (bi, t, g)),
        blk(tq, HEADS_PER_STEP * MLA_ROPE, lambda bi, g, t: (bi, t, g)),
        blk(kl, gw, lambda bi, g, t: (bi, 0, g)),
        blk(kl, LANES, lambda bi, g, t: (bi, 0, 0)),
        blk(kl, gw, lambda bi, g, t: (bi, 0, g)),
    ]
    args = [qn, qr, kn, kp, v]
    if has_ctx:
        cl = knc.shape[1]
        in_specs += [
            blk(cl, gw, lambda bi, g, t: (bi, 0, g)),
            blk(cl, LANES, lambda bi, g, t: (bi, 0, 0)),
            blk(cl, gw, lambda bi, g, t: (bi, 0, g)),
        ]
        args += [knc, kpc, vc]
    return pl.pallas_call(
        functools.partial(_attn_b_kernel, has_ctx),
        grid=(b, MLA_HEADS // HEADS_PER_STEP, s // tq),
        in_specs=in_specs,
        out_specs=blk(tq, gw, lambda bi, g, t: (bi, t, g)),
        out_shape=jax.ShapeDtypeStruct(qn.shape, BF16),
        compiler_params=_params("arbitrary", "arbitrary", "arbitrary"),
        name="attn_mla_latent" if has_ctx else "attn_mla_context",
    )(*args)


def _attn_c_kernel(latent, sink_ref, q_ref, kd_ref, vd_ref, *rest):
    if latent:
        kc_ref, vc_ref, o_ref = rest
    else:
        (o_ref,) = rest
    tq = min(Q_TILE, q_ref.shape[0])
    group = SWA_HEADS // SWA_KV_HEADS
    kv_per_step = HEADS_PER_STEP // group
    head0 = pl.program_id(1) * HEADS_PER_STEP
    windows = []
    if latent:
        s_len = kd_ref.shape[0]
        win = tq + 2 * SWA_WINDOW
        for r in range(q_ref.shape[0] // tq):
            t0 = pl.program_id(2) * q_ref.shape[0] + r * tq
            ws = pl.multiple_of(jnp.clip(t0 - SWA_WINDOW, 0, s_len - win), LANES)
            qpos = t0 + lax.broadcasted_iota(jnp.int32, (tq, win), 0)
            kpos = ws + lax.broadcasted_iota(jnp.int32, (tq, win), 1)
            bias = jnp.where(jnp.abs(qpos - kpos) <= SWA_WINDOW, 0.0, NEG_INF).astype(F32)
            windows.append((slice(r * tq, (r + 1) * tq), ws, bias))
    pairs_per_kv = group // 2
    pair_cols = [slice(p * LANES, (p + 1) * LANES) for p in range(HEADS_PER_STEP // 2)]
    kv_cols = [slice(j * LANES, (j + 1) * LANES) for j in range(kv_per_step)]
    if not latent:
        qs = [jnp.concatenate([h for c in pair_cols[j * pairs_per_kv:(j + 1) * pairs_per_kv]
                               for h in _split_pair(q_ref[:, c])], axis=0) for j in range(kv_per_step)]
        out = _attend_group(qs, [kd_ref[:, c] for c in kv_cols], [_with_ones(vd_ref[:, c]) for c in kv_cols],
                            sink_ref[...])
        for p, c in enumerate(pair_cols):
            o_ref[:, c] = _merge_pair(out[2 * p * tq:(2 * p + 1) * tq], out[(2 * p + 1) * tq:(2 * p + 2) * tq])
        return
    sinks = [sink_ref[head0 + h] * LOG2E for h in range(HEADS_PER_STEP)]
    hd = SWA_HEAD_DIM
    for j, kc in enumerate(kv_cols):
        kct = kc_ref[j * hd:(j + 1) * hd, :].astype(BF16)
        vct = vc_ref[j * hd:(j + 1) * hd, :].astype(BF16)
        ctx_seg = (jnp.concatenate([kct, kct], axis=0),
                   jnp.concatenate([vct, vct, jnp.ones((LANES, vct.shape[1]), BF16)], axis=0), None, True)
        for rows, ws, bias in windows:
            segs = [(kd_ref[pl.ds(ws, win), kc], _with_ones(vd_ref[pl.ds(ws, win), kc]), (bias, bias)), ctx_seg]
            for p in range(j * pairs_per_kv, (j + 1) * pairs_per_kv):
                o_ref[rows, pair_cols[p]] = _attend_pair(q_ref[rows, pair_cols[p]], segs, sinks[2 * p:2 * p + 2])


def _sink_logit_block(sink, tq):
    rows = jnp.repeat(sink * LOG2E, tq)[:, None]
    return jnp.where(jnp.arange(LANES)[None, :] == 0, rows, NEG_INF)


def _attention_c(sink, q, kd, vd, kc, vc):
    b, s, _ = q.shape
    kl = kd.shape[1]
    tq = min(LATENT_Q_ROWS, s)
    gw = HEADS_PER_STEP * SWA_HEAD_DIM
    kvw = 2 * SWA_HEAD_DIM * HEADS_PER_STEP // (SWA_HEADS // SWA_KV_HEADS)
    cw, cl = kc.shape[1] * HEADS_PER_STEP // SWA_HEADS, kc.shape[2]
    return pl.pallas_call(
        functools.partial(_attn_c_kernel, True),
        grid=(b, SWA_HEADS // HEADS_PER_STEP, s // tq),
        in_specs=[
            pl.BlockSpec(memory_space=pltpu.SMEM),
            pl.BlockSpec((None, tq, gw), lambda bi, g, t: (bi, t, g)),
            pl.BlockSpec((None, kl, kvw), lambda bi, g, t: (bi, 0, g)),
            pl.BlockSpec((None, kl, kvw), lambda bi, g, t: (bi, 0, g)),
            pl.BlockSpec((None, cw, cl), lambda bi, g, t: (bi, g, 0)),
            pl.BlockSpec((None, cw, cl), lambda bi, g, t: (bi, g, 0)),
        ],
        out_specs=pl.BlockSpec((None, tq, gw), lambda bi, g, t: (bi, t, g)),
        out_shape=jax.ShapeDtypeStruct(q.shape, BF16),
        compiler_params=_params("arbitrary", "arbitrary", "arbitrary"),
        name="attn_swa_latent",
    )(sink, q, kd, vd, kc, vc)


def _attn_d_ctx_kernel(q_ref, k_ref, v_ref, o_ref):
    tq = q_ref.shape[0]
    pair_cols = [slice(p * LANES, (p + 1) * LANES) for p in range(HEADS_PER_STEP // 2)]
    qs = [jnp.concatenate(_split_pair(q_ref[:, c]), axis=0) for c in pair_cols]
    out = _attend_group(qs, [k_ref[:, c] for c in pair_cols], [_with_ones(v_ref[:, c]) for c in pair_cols])
    for p, c in enumerate(pair_cols):
        o_ref[:, c] = _merge_pair(out[2 * p * tq:(2 * p + 1) * tq], out[(2 * p + 1) * tq:(2 * p + 2) * tq])


def _nat_row0(r, rows):
    return min(max(r - NAT_WIN_R // 2, 0), rows - NAT_WIN_R)


def _nat_tile_key_rows(tile, rows):
    first, last = tile * NAT_TILE_ROWS, (tile + 1) * NAT_TILE_ROWS - 1
    need = _nat_row0(last, rows) + NAT_WIN_R - _nat_row0(first, rows)
    return need + need % 2


def _nat_tile_start(tile, rows):
    return min(_nat_row0(tile * NAT_TILE_ROWS, rows), rows - _nat_tile_key_rows(tile, rows))


def _nat_tile_slots(rows):
    seen, slot_of, reps = {}, [], []
    for t in range(rows // NAT_TILE_ROWS):
        ws = _nat_tile_start(t, rows)
        sig = (_nat_tile_key_rows(t, rows),) + tuple(
            (r - ws, _nat_row0(r, rows) - ws) for r in range(t * NAT_TILE_ROWS, (t + 1) * NAT_TILE_ROWS))
        if sig not in seen:
            seen[sig] = len(reps)
            reps.append(t)
        slot_of.append(seen[sig])
    return slot_of, reps


def _attn_d_lat_kernel(q_ref, k_ref, v_ref, kc_ref, vc_ref, bias_ref, o_ref):
    rows = k_ref.shape[0] // GRID_W
    tile = pl.program_id(0)
    ones = jnp.ones((LANES, kc_ref.shape[1]), BF16)

    def body(key_rows, start_row):
        slab = key_rows * GRID_W
        start = pl.multiple_of(start_row * GRID_W, GRID_W)
        for p in range(HEADS_PER_STEP // 2):
            cols = slice(p * LANES, (p + 1) * LANES)
            segs = [(k_ref[pl.ds(start, slab), cols], _with_ones(v_ref[pl.ds(start, slab), cols]),
                     (bias_ref[2 * p, :, :slab], bias_ref[2 * p + 1, :, :slab])),
                    (kc_ref[cols, :].astype(BF16),
                     jnp.concatenate([vc_ref[cols, :].astype(BF16), ones], axis=0), None, True)]
            o_ref[:, cols] = _attend_pair(q_ref[:, cols], segs)

    tiles = range(rows // NAT_TILE_ROWS)
    for key_rows in sorted({_nat_tile_key_rows(t, rows) for t in tiles}):
        members = [t for t in tiles if _nat_tile_key_rows(t, rows) == key_rows]
        cond = functools.reduce(jnp.logical_or, [tile == t for t in members])
        start_row = functools.reduce(lambda acc, t: jnp.where(tile == t, _nat_tile_start(t, rows), acc),
                                     members, jnp.int32(0))
        pl.when(cond)(functools.partial(body, key_rows, start_row))


def _attention_d_lat(q, k, v, kc, vc, bias):
    b, s, _ = q.shape
    cl = kc.shape[2]
    tq = NAT_TILE_ROWS * GRID_W
    slab = NAT_KEY_ROWS * GRID_W
    gw = HEADS_PER_STEP * NAT_HEAD_DIM
    slot_of, _ = _nat_tile_slots(s // GRID_W)

    def slot(t):
        return sum(jnp.where(t == tile, sl, 0) for tile, sl in enumerate(slot_of))

    return pl.pallas_call(
        _attn_d_lat_kernel,
        grid=(s // tq, NAT_HEADS // HEADS_PER_STEP, b),
        in_specs=[
            pl.BlockSpec((None, tq, gw), lambda t, g, bi: (bi, t, g)),
            pl.BlockSpec((None, s, gw), lambda t, g, bi: (bi, 0, g)),
            pl.BlockSpec((None, s, gw), lambda t, g, bi: (bi, 0, g)),
            pl.BlockSpec((None, gw, cl), lambda t, g, bi: (bi, g, 0)),
            pl.BlockSpec((None, gw, cl), lambda t, g, bi: (bi, g, 0)),
            pl.BlockSpec((HEADS_PER_STEP, None, tq, slab), lambda t, g, bi: (g, slot(t), 0, 0)),
        ],
        out_specs=pl.BlockSpec((None, tq, gw), lambda t, g, bi: (bi, t, g)),
        out_shape=jax.ShapeDtypeStruct(q.shape, BF16),
        compiler_params=_params("arbitrary", "arbitrary", "arbitrary"),
        name="attn_nat_latent",
    )(q, k, v, kc, vc, bias)


def _mla_expand_kernel(c_ref, w_ref, kn_ref, v_ref):
    kv = _dot(c_ref[...].astype(BF16), w_ref[...])
    half = kv.shape[1] // 2
    kn_ref[...] = kv[:, :half].astype(BF16)
    v_ref[...] = kv[:, half:].astype(BF16)


def _mla_expand(ckv, w_ukv):
    b, l, c = ckv.shape
    n = w_ukv.shape[1] // 2
    out = pl.BlockSpec((None, l, n), lambda bi: (bi, 0, 0))
    return pl.pallas_call(
        _mla_expand_kernel,
        grid=(b,),
        in_specs=[pl.BlockSpec((None, l, c), lambda bi: (bi, 0, 0)), _full(w_ukv.shape)],
        out_specs=[out, out],
        out_shape=[jax.ShapeDtypeStruct((b, l, n), BF16)] * 2,
        compiler_params=_params("arbitrary"),
        name="mla_expand_cache",
    )(ckv, w_ukv)


def _finish_kernel(o_ref, x_ref, m_ref, g_ref, wo_ref, w1_ref, w2_ref, y_ref, x1_ref, h2_ref):
    f = pl.program_id(2)
    last = pl.num_programs(2) - 1
    chains = [slice(c * TOK_CHAIN, (c + 1) * TOK_CHAIN) for c in range(y_ref.shape[0] // TOK_CHAIN)]

    def mlp_part(h, w1, w2):
        u = jnp.square(jnp.maximum(_dot(h, w1), 0.0)).astype(BF16)
        return _dot(u, w2)

    @pl.when(f == 0)
    def _():
        m = m_ref[...]
        wo = wo_ref[...].astype(BF16)
        for rows in chains:
            a = _dot(o_ref[rows, :], wo)
            x1 = x_ref[rows, :] + _rms(a, _mod(m, 2) * g_ref[1:2, :])
            x1_ref[rows, :] = x1
            h2_ref[rows, :] = _premod(x1, g_ref[2:3, :], m, 1).astype(BF16)
        y_ref[...] = mlp_part(h2_ref[...], w1_ref[...].astype(BF16), w2_ref[...].astype(BF16))

    @pl.when((f > 0) & (f < last))
    def _():
        y_ref[...] += mlp_part(h2_ref[...], w1_ref[...].astype(BF16), w2_ref[...].astype(BF16))

    @pl.when(f == last)
    def _():
        gain = _mod(m_ref[...], 5) * g_ref[3:4, :]
        w1, w2 = w1_ref[...].astype(BF16), w2_ref[...].astype(BF16)
        for rows in chains:
            y = y_ref[rows, :] + mlp_part(h2_ref[rows, :], w1, w2)
            y_ref[rows, :] = x1_ref[rows, :] + _rms(y, gain)


def _finish_layer(o, x, mods_l, g, wo, w1, w2, layer, latent):
    bx, s, _ = x.shape
    tm, tf = MLP_TOK_TILE, MLP_FF_TILE
    row = (lambda b, j, f: (b, 0, 0)) if latent else (lambda b, j, f: (CTX_MOD_ROW, 0, 0))
    tok = lambda w: pl.BlockSpec((None, tm, w), lambda b, j, f: (b, j, 0))
    return pl.pallas_call(
        _finish_kernel,
        grid=(bx, s // tm, D_FF // tf),
        in_specs=[tok(o.shape[-1]), tok(D_MODEL),
                  pl.BlockSpec((None, 1, N_MOD * D_MODEL), row), _full(g.shape), _full(wo.shape),
                  pl.BlockSpec((None, D_MODEL, tf), lambda b, j, f: (layer, 0, f)),
                  pl.BlockSpec((None, tf, D_MODEL), lambda b, j, f: (layer, f, 0))],
        out_specs=tok(D_MODEL),
        out_shape=jax.ShapeDtypeStruct(x.shape, F32),
        scratch_shapes=[pltpu.VMEM((tm, D_MODEL), F32), pltpu.VMEM((tm, D_MODEL), BF16)],
        compiler_params=pltpu.CompilerParams(dimension_semantics=("arbitrary",) * 3,
                                             vmem_limit_bytes=MLP_VMEM_LIMIT),
        name="out_proj_mlp",
    )(o, x, mods_l, g, wo, w1, w2)


def _rope_tables(s, dim):
    quarter = dim // 4
    t = jnp.arange(s)
    pos = jnp.stack([t // GRID_W, t % GRID_W], axis=-1).astype(F32)
    inv = ROPE_THETA ** (-jnp.arange(quarter, dtype=F32) / quarter)
    ang = pos[:, :, None] * inv
    cos = jnp.broadcast_to(jnp.cos(ang)[:, :, None, :], (s, 2, 2, quarter)).reshape(s, dim)
    sign = jnp.array([-1.0, 1.0], F32)[None, None, :, None]
    sin = (jnp.sin(ang)[:, :, None, :] * sign).reshape(s, dim)
    reps = LANES // dim
    return jnp.tile(cos, (1, reps)), jnp.tile(sin, (1, reps))


def _dup_heads(w, heads, dim):
    lead = w.shape[:-1]
    w = w.reshape(lead + (heads, 1, dim))
    return jnp.broadcast_to(w, lead + (heads, 2, dim)).reshape(lead + (heads * 2 * dim,))


def _nat_dense_bias(rpb, rows):
    heads = rpb.shape[0]
    c = np.arange(GRID_W)
    c0 = np.clip(c - NAT_WIN_C // 2, 0, GRID_W - NAT_WIN_C)
    in_c = (c[None, :] >= c0[:, None]) & (c[None, :] < c0[:, None] + NAT_WIN_C)
    dc = c[None, :] - c[:, None] + NAT_WIN_C - 1
    onehot = (dc[None] == np.arange(2 * NAT_WIN_C - 1)[:, None, None]) & in_c[None]
    toe = jnp.einsum("had,dck->hack", rpb * LOG2E, jnp.asarray(onehot, F32),
                     precision=lax.Precision.HIGHEST)
    toe = jnp.where(jnp.asarray(in_c)[None, None], toe, NEG_INF)
    pad = jnp.full((heads, 1, GRID_W, GRID_W), NEG_INF, F32)
    ext = jnp.concatenate([pad, toe, pad], axis=1)
    pairs = jnp.concatenate([ext[:, :-1], ext[:, 1:]], axis=-1)
    slots = len(_nat_tile_slots(rows)[1])
    tq, slab = NAT_TILE_ROWS * GRID_W, NAT_KEY_ROWS * GRID_W
    n_off = 2 * NAT_WIN_R
    return pl.pallas_call(
        functools.partial(_nat_bias_kernel, rows),
        grid=(heads,),
        in_specs=[pl.BlockSpec((None, n_off, GRID_W, LANES), lambda h: (h, 0, 0, 0))],
        out_specs=pl.BlockSpec((None, slots, tq, slab), lambda h: (h, 0, 0, 0)),
        out_shape=jax.ShapeDtypeStruct((heads, slots, tq, slab), F32),
        compiler_params=_params("arbitrary"),
        name="nat_bias_expand",
    )(pairs)


def _nat_bias_kernel(rows, pairs_ref, o_ref):
    low = lax.broadcasted_iota(jnp.int32, (GRID_W, LANES), 1) < GRID_W
    masked = jnp.full((GRID_W, LANES), NEG_INF, F32)
    reps = _nat_tile_slots(rows)[1]
    for slot, i in ((s, i) for s in range(len(reps)) for i in range(NAT_TILE_ROWS)):
        tile = reps[slot]
        r = tile * NAT_TILE_ROWS + i
        r0 = _nat_row0(r, rows)
        ws = _nat_tile_start(tile, rows)
        for jb in range(NAT_KEY_ROWS // 2):
            kr = ws + 2 * jb
            ok_lo, ok_hi = r0 <= kr < r0 + NAT_WIN_R, r0 <= kr + 1 < r0 + NAT_WIN_R
            blk = masked
            if ok_lo or ok_hi:
                blk = pairs_ref[kr - r + NAT_WIN_R]
                if not ok_lo:
                    blk = jnp.where(low, NEG_INF, blk)
                elif not ok_hi:
                    blk = jnp.where(low, blk, NEG_INF)
            o_ref[slot, i * GRID_W:(i + 1) * GRID_W, jb * LANES:(jb + 1) * LANES] = blk


def kernel(x_prompt, x_sample, cache_l0_k, cache_l0_v, cache_l1_ckv, cache_l1_kpe, cache_l2_k, cache_l2_v, cache_l3_k, cache_l3_v, c, c_ctx, ada_w, ada_b, norm_g, mlp_w1, mlp_w2, attn_w_qkv, attn_q_norm, attn_k_norm, attn_w_o, mla_w_in, mla_q_norm, mla_kv_norm, mla_w_uq, mla_w_ukv, mla_w_o, swa_w_qkv, swa_sink, swa_w_o, nat_w_qkv, nat_rpb, nat_w_o):
    nb, seq, d = x_prompt.shape
    db, dseq, _ = x_sample.shape
    past = cache_l0_k.shape[1]
    ctx_b = nb * seq // dseq
    xp = x_prompt.reshape(ctx_b, dseq, d)
    xs = x_sample

    cond = jnp.zeros((COND_ROWS, d), F32).at[:db].set(c).at[CTX_MOD_ROW].set(c_ctx)
    mods = _modulation(cond, ada_w, ada_b).reshape(DEPTH, COND_ROWS, 1, N_MOD * d)

    row = lambda v: v.reshape(1, -1)

    def head_major(cache):
        b_, l_, h_, dh = cache.shape
        return cache.transpose(0, 2, 3, 1).reshape(b_, h_ * dh, l_)

    def token_major(state, heads):
        b_, w_, l_ = state.shape
        return state.reshape(b_, heads, w_ // heads, l_).transpose(0, 3, 1, 2)

    def finish(o, x, layer, wo, latent):
        return _finish_layer(o, x, mods[layer], norm_g[layer], wo, mlp_w1, mlp_w2, layer, latent)

    g = norm_g[0]
    w = attn_w_qkv
    wo = attn_w_o
    consts = [w, row(attn_q_norm), row(attn_k_norm)]
    kvw = ATTN_KV_HEADS * ATTN_HEAD_DIM
    o, l0_k, l0_v = _run_proj(
        _proj_a_kernel, False, xp, mods[0], g, consts, [],
        [(ATTN_HEAD_DIM, F32, "I", ATTN_KV_HEADS)] * 2, "proj_attn_gqa_context",
        seq, attend=functools.partial(_attn_a_kernel, False), qkv=[(d, BF16), (kvw, BF16), (kvw, BF16)])
    xp = finish(o, xp, 0, wo, False)
    quarter = ATTN_HEAD_DIM // 4
    perm = np.arange(ATTN_HEAD_DIM).reshape(2, 2, quarter).transpose(1, 0, 2).reshape(-1)
    qk_w = (ATTN_HEADS + ATTN_KV_HEADS) * ATTN_HEAD_DIM
    w_qk = attn_w_qkv[:, :qk_w].reshape(d, -1, 2, 2, quarter).transpose(0, 1, 3, 2, 4).reshape(d, qk_w)
    w_lat = jnp.concatenate([w_qk, attn_w_qkv[:, qk_w:]], axis=1).astype(BF16)
    consts_lat = [w_lat, row(attn_q_norm[perm]), row(attn_k_norm[perm])]
    tables = [t[:, perm] for t in _rope_tables(dseq, ATTN_HEAD_DIM)]
    q, k, v = _run_proj(_proj_a_kernel, True, xs, mods[0], g, consts_lat, tables,
                        [(d, BF16), (kvw, BF16), (kvw, BF16)], "proj_gqa_latent")
    perm_matrix = np.zeros((ATTN_HEAD_DIM, ATTN_HEAD_DIM), np.float32)
    perm_matrix[perm, np.arange(ATTN_HEAD_DIM)] = 1.0
    o = _attention_a(q, k, v, cache_l0_k.reshape(db, past * ATTN_KV_HEADS, ATTN_HEAD_DIM),
                     cache_l0_v.reshape(db, past * ATTN_KV_HEADS, ATTN_HEAD_DIM), jnp.asarray(perm_matrix, BF16))
    xs = finish(o, xs, 0, wo, True)
    new_l0 = (l0_k.reshape(nb, seq, ATTN_KV_HEADS, ATTN_HEAD_DIM),
              l0_v.reshape(nb, seq, ATTN_KV_HEADS, ATTN_HEAD_DIM))

    g = norm_g[1]
    nq, nkv = MLA_Q_LORA, MLA_KV_LORA
    w_in = jnp.concatenate([mla_w_in[:, :nq + nkv]] + [mla_w_in[:, nq + nkv:]] * (LANES // MLA_ROPE),
                           axis=1).astype(BF16)
    wuq = mla_w_uq.reshape(nq, MLA_HEADS, MLA_NOPE + MLA_ROPE)
    wuq = jnp.concatenate([wuq[:, :, :MLA_NOPE].reshape(nq, -1), wuq[:, :, MLA_NOPE:].reshape(nq, -1)],
                          axis=1).astype(BF16)
    wukv = mla_w_ukv.reshape(nkv, MLA_HEADS, MLA_NOPE + MLA_V_DIM)
    wukv = jnp.concatenate([wukv[:, :, :MLA_NOPE].reshape(nkv, -1), wukv[:, :, MLA_NOPE:].reshape(nkv, -1)],
                           axis=1).astype(BF16)
    wo = mla_w_o
    consts = [w_in, row(mla_q_norm), row(mla_kv_norm), wuq, wukv]
    hw = MLA_HEADS * MLA_NOPE
    rw = MLA_HEADS * MLA_ROPE
    outs = [(hw, BF16), (rw, BF16), (hw, BF16), (hw, BF16), (LANES, BF16)]
    o, l1_ckv, l1_kpe = _run_proj(
        _proj_b_kernel, False, xp, mods[1], g, consts, [], [(nkv, F32), (MLA_ROPE, F32, "T")],
        "proj_attn_mla_context", seq, qkv=outs,
        attend=lambda qn, qr, kn, v, kp, o_ref: _attn_b_kernel(False, qn, qr, kn, kp, v, o_ref))
    xp = finish(o, xp, 1, wo, False)
    tables = list(_rope_tables(dseq, MLA_ROPE))
    qn, qr, kn, v, kp = _run_proj(_proj_b_kernel, True, xs, mods[1], g, consts, tables, outs,
                                  "proj_mla_latent", tile=WIDE_TOK_TILE)
    knc, vc = _mla_expand(cache_l1_ckv, wukv)
    kpc = jnp.tile(cache_l1_kpe, (1, 1, LANES // MLA_ROPE)).astype(BF16)
    o = _attention_b(qn, qr, kn, kp, v, knc, kpc, vc)
    xs = finish(o, xs, 1, wo, True)
    new_l1 = (l1_ckv.reshape(nb, seq, nkv), l1_kpe.transpose(0, 2, 1))

    g = norm_g[2]
    qw = SWA_HEADS * SWA_HEAD_DIM
    kw = SWA_KV_HEADS * SWA_HEAD_DIM
    wq, wk, wv = swa_w_qkv[:, :qw], swa_w_qkv[:, qw:qw + kw], swa_w_qkv[:, qw + kw:]
    wkd = _dup_heads(wk, SWA_KV_HEADS, SWA_HEAD_DIM)
    wvd = _dup_heads(wv, SWA_KV_HEADS, SWA_HEAD_DIM)
    w_ctx = jnp.concatenate([wq, wk, wv, wkd, wvd], axis=1).astype(BF16)
    w_lat = jnp.concatenate([wq, wkd, wvd], axis=1).astype(BF16)
    wo = swa_w_o
    o, l2_k, l2_v = _run_proj(
        _proj_c_kernel, False, xp, mods[2], g, [w_ctx], [], [(kw, F32, "T"), (kw, F32, "T")],
        "proj_attn_swa_context", seq, attend=functools.partial(_attn_c_kernel, False),
        attend_inputs=[_sink_logit_block(swa_sink, seq)], qkv=[(qw, BF16), (2 * kw, BF16), (2 * kw, BF16)])
    xp = finish(o, xp, 2, wo, False)
    tables = list(_rope_tables(dseq, SWA_HEAD_DIM))
    q, kd, vd = _run_proj(_proj_c_kernel, True, xs, mods[2], g, [w_lat], tables,
                          [(qw, BF16), (2 * kw, BF16), (2 * kw, BF16)], "proj_swa_latent", tile=WIDE_TOK_TILE)
    o = _attention_c(swa_sink, q, kd, vd, head_major(cache_l2_k), head_major(cache_l2_v))
    xs = finish(o, xs, 2, wo, True)
    new_l2 = (token_major(l2_k, SWA_KV_HEADS), token_major(l2_v, SWA_KV_HEADS))

    g = norm_g[3]
    hw = NAT_HEADS * NAT_HEAD_DIM
    w = nat_w_qkv
    wo = nat_w_o
    o, l3_k, l3_v = _run_proj(
        _proj_d_kernel, False, xp, mods[3], g, [w], [], [(hw, F32, "T"), (hw, F32, "T")],
        "proj_attn_nat_context", seq, attend=_attn_d_ctx_kernel, qkv=[(hw, BF16)] * 3)
    xp = finish(o, xp, 3, wo, False)
    q, k, v = _run_proj(_proj_d_kernel, True, xs, mods[3], g, [w], [],
                        [(hw, BF16), (hw, BF16), (hw, BF16)], "proj_nat_latent")
    o = _attention_d_lat(q, k, v, head_major(cache_l3_k), head_major(cache_l3_v),
                         _nat_dense_bias(nat_rpb, dseq // GRID_W))
    xs = finish(o, xs, 3, wo, True)
    new_l3 = (token_major(l3_k, NAT_HEADS), token_major(l3_v, NAT_HEADS))

    return (xp.reshape(nb, seq, d), xs) + new_l0 + new_l1 + new_l2 + new_l3
```
